```python
import jax
import jax.numpy as jnp
from jax import lax
import numpy as np


D_MODEL = 1024
BATCH = 8
SEQ = 4096
DEPTH = 2

HEAD_DIM = 64
Q_BLK = 128
ROPE_THETA = 10000.0
LN_EPS = 1e-5
RMS_EPS = 1e-6

NSA_HEADS = 6
NSA_KV_HEADS = 2
NSA_GROUP = NSA_HEADS // NSA_KV_HEADS
NSA_WIDTH = NSA_HEADS * HEAD_DIM
CMP_LEN = 32
CMP_STRIDE = 16
SEL_BLOCK = 64
SEL_TOP = 16
WINDOW = 512
FORCE_BONUS = 1e4

DSA_HEADS = 6
DSA_WIDTH = DSA_HEADS * HEAD_DIM
DSA_LATENT = 128
IDX_HEADS = 4
IDX_DIM = 32
IDX_TOPK_MAX = 256

SGU_GROUPS = 4
SGU_CHUNK = 128
SGU_WIDTH = SGU_GROUPS * HEAD_DIM

MIX_WIDTH = NSA_WIDTH + DSA_WIDTH + SGU_WIDTH

SPLIT_SIZES = (NSA_WIDTH, 6 * NSA_KV_HEADS * HEAD_DIM, 3 * NSA_HEADS, DSA_WIDTH, DSA_LATENT, IDX_HEADS * IDX_DIM, IDX_DIM, IDX_HEADS, 2 * SGU_WIDTH)
IN_WIDTH = sum(SPLIT_SIZES)

D_FF = 2816
N_EXPERTS = 8
TOP_K = 2
EXPERT_FF = 3584
N_DENSE = (DEPTH + 1) // 2
N_MOE = DEPTH // 2

ALPHA = (2 * DEPTH) ** 0.25
BETA = (8 * DEPTH) ** -0.25
POS_OFFSET_MAX = 1024

kernel_name = 'hybrid_nsa_dsa_sgu_deepnorm_adaln_moe'


def standardize(x):
    xf = x.astype(jnp.float32)
    mu = jnp.mean(xf, -1, keepdims=True)
    var = jnp.mean(jnp.square(xf - mu), -1, keepdims=True)
    return ((xf - mu) * lax.rsqrt(var + LN_EPS)).astype(x.dtype)


def layer_norm(x, g, b):
    return standardize(x) * g + b


def rms_norm(x, g):
    xf = x.astype(jnp.float32)
    y = xf * lax.rsqrt(jnp.mean(jnp.square(xf), -1, keepdims=True) + RMS_EPS)
    return (y * g).astype(x.dtype)


def modulate(x, shift, scale):
    return standardize(x) * (1.0 + scale) + shift


def rope_tables(positions, dim):
    inv = ROPE_THETA ** (-jnp.arange(0, dim, 2, dtype=jnp.float32) / dim)
    ang = positions.astype(jnp.float32)[..., None] * inv
    return jnp.cos(ang), jnp.sin(ang)


def apply_rope(x, cos, sin):
    x1, x2 = jnp.split(x, 2, axis=-1)
    c = cos[:, :, None, :]
    s = sin[:, :, None, :]
    return jnp.concatenate([x1 * c - x2 * s, x2 * c + x1 * s], axis=-1).astype(x.dtype)


def masked_softmax(scores, mask):
    s = jnp.where(mask, scores.astype(jnp.float32), -jnp.inf)
    m = jnp.max(s, axis=-1, keepdims=True)
    m = jnp.where(jnp.isfinite(m), m, 0.0)
    p = jnp.exp(s - m)
    return p / jnp.maximum(jnp.sum(p, axis=-1, keepdims=True), 1e-30)


def nsa_mixer(q, kv, gate_logits, cos, sin, cmp_pos, cmp_w1, cmp_w2):
    B, S = q.shape[0], q.shape[1]
    G, R, dh = NSA_KV_HEADS, NSA_GROUP, HEAD_DIM
    n_cmp = (S - CMP_LEN) // CMP_STRIDE + 1
    n_blk = S // SEL_BLOCK
    n_sel = min(SEL_TOP, n_blk)
    n_q = S // Q_BLK
    scale = dh ** -0.5

    kv = kv.reshape(B, S, 6, G, dh)
    k_sel = apply_rope(kv[:, :, 2], cos, sin)
    v_sel = kv[:, :, 3]
    k_win = apply_rope(kv[:, :, 4], cos, sin)
    v_win = kv[:, :, 5]
    q_rot = apply_rope(q, cos, sin)
    gates = jax.nn.sigmoid(gate_logits.astype(jnp.float32)).astype(q.dtype)
    gates = gates.reshape(B, S, G, R, 3)

    win_idx = jnp.arange(n_cmp)[:, None] * CMP_STRIDE + jnp.arange(CMP_LEN)[None, :]

    def compress(x_tok, j):
        blocks = x_tok[:, win_idx] + cmp_pos[j][:, None, :]
        hid = jax.nn.gelu(jnp.einsum('bnlgd,lde->bnge', blocks, cmp_w1[j]))
        return jnp.einsum('bnge,ef->bngf', hid, cmp_w2[j])

    k_cmp = compress(kv[:, :, 0], 0)
    v_cmp = compress(kv[:, :, 1], 1)
    cmp_start = jnp.arange(n_cmp) * CMP_STRIDE
    cmp_end = cmp_start + CMP_LEN - 1
    blk_start = jnp.arange(n_blk) * SEL_BLOCK
    overlap = ((cmp_start[:, None] < blk_start[None, :] + SEL_BLOCK)
               & (cmp_start[:, None] + CMP_LEN > blk_start[None, :])).astype(jnp.float32)

    k_sel_blk = k_sel.reshape(B, n_blk, SEL_BLOCK, G, dh).transpose(0, 3, 1, 2, 4)
    v_sel_blk = v_sel.reshape(B, n_blk, SEL_BLOCK, G, dh).transpose(0, 3, 1, 2, 4)
    gather_blocks = jax.vmap(jax.vmap(lambda kb, ib: kb[ib]))

    pad = jnp.zeros((B, WINDOW, G, dh), k_win.dtype)
    k_win_pad = jnp.concatenate([pad, k_win], axis=1)
    v_win_pad = jnp.concatenate([pad, v_win], axis=1)
    blk_ids = jnp.arange(n_blk)

    def block(qb):
        t0 = qb * Q_BLK
        t = t0 + jnp.arange(Q_BLK)
        qc = lax.dynamic_slice_in_dim(q, t0, Q_BLK, 1).reshape(B, Q_BLK, G, R, dh)
        qr = lax.dynamic_slice_in_dim(q_rot, t0, Q_BLK, 1).reshape(B, Q_BLK, G, R, dh)

        s_c = jnp.einsum('bqgrd,bngd->bqgrn', qc, k_cmp) * scale
        p_c = masked_softmax(s_c, (cmp_end[None, :] <= t[:, None])[None, :, None, None, :])
        o_c = jnp.einsum('bqgrn,bngd->bqgrd', p_c.astype(v_cmp.dtype), v_cmp)

        imp = jnp.einsum('bqgrn,nj->bqgj', p_c, overlap)
        cur = t // SEL_BLOCK
        valid = blk_ids[None, :] <= cur[:, None]
        forced = ((blk_ids[None, :] == 0) | (blk_ids[None, :] == cur[:, None])
                  | (blk_ids[None, :] == cur[:, None] - 1)).astype(jnp.float32)
        score = jnp.where(valid[None, :, None, :], imp + FORCE_BONUS * forced[None, :, None, :], -jnp.inf)
        _, idx = lax.top_k(score, n_sel)
        idx_bg = idx.transpose(0, 2, 1, 3)
        k_g = gather_blocks(k_sel_blk, idx_bg)
        v_g = gather_blocks(v_sel_blk, idx_bg)
        key_pos = idx_bg[..., None] * SEL_BLOCK + jnp.arange(SEL_BLOCK)
        m_s = (key_pos <= t[None, None, :, None, None]).transpose(0, 2, 1, 3, 4)
        m_s = m_s.reshape(B, Q_BLK, G, 1, n_sel * SEL_BLOCK)
        s_s = jnp.einsum('bqgrd,bgqnsd->bqgrns', qr, k_g) * scale
        p_s = masked_softmax(s_s.reshape(B, Q_BLK, G, R, n_sel * SEL_BLOCK), m_s)
        p_s = p_s.reshape(B, Q_BLK, G, R, n_sel, SEL_BLOCK)
        o_s = jnp.einsum('bqgrns,bgqnsd->bqgrd', p_s.astype(v_g.dtype), v_g)

        k_w = lax.dynamic_slice_in_dim(k_win_pad, t0, WINDOW + Q_BLK, 1)
        v_w = lax.dynamic_slice_in_dim(v_win_pad, t0, WINDOW + Q_BLK, 1)
        s_pos = t0 - WINDOW + jnp.arange(WINDOW + Q_BLK)
        diff = t[:, None] - s_pos[None, :]
        m_w = (diff >= 0) & (diff < WINDOW) & (s_pos[None, :] >= 0)
        s_w = jnp.einsum('bqgrd,bsgd->bqgrs', qr, k_w) * scale
        p_w = masked_softmax(s_w, m_w[None, :, None, None, :])
        o_w = jnp.einsum('bqgrs,bsgd->bqgrd', p_w.astype(v_w.dtype), v_w)

        g = lax.dynamic_slice_in_dim(gates, t0, Q_BLK, 1)
        o = g[..., 0:1] * o_c + g[..., 1:2] * o_s + g[..., 2:3] * o_w
        return o.reshape(B, Q_BLK, NSA_WIDTH)

    out = lax.map(block, jnp.arange(n_q))
    return out.transpose(1, 0, 2, 3).reshape(B, S, NSA_WIDTH)


def dsa_mixer(q, c_kv, iq, ik, iw, cos, sin, cos_i, sin_i, kv_norm_g, w_uk, w_uv):
    B, S = q.shape[0], q.shape[1]
    k_top = min(IDX_TOPK_MAX, S // 4)
    n_q = S // Q_BLK
    scale = HEAD_DIM ** -0.5
    ckv = rms_norm(c_kv, kv_norm_g)
    k = apply_rope((ckv @ w_uk)[:, :, None, :], cos, sin)[:, :, 0]
    v = ckv @ w_uv
    q = apply_rope(q, cos, sin)
    iq = apply_rope(iq.reshape(B, S, IDX_HEADS, IDX_DIM), cos_i, sin_i)
    ik = apply_rope(ik[:, :, None, :], cos_i, sin_i)[:, :, 0]
    iw = iw * (IDX_HEADS ** -0.5)
    key_ids = jnp.arange(S)
    gather_tok = jax.vmap(lambda kb, ib: kb[ib])

    def block(qb):
        t0 = qb * Q_BLK
        t = t0 + jnp.arange(Q_BLK)
        iq_b = lax.dynamic_slice_in_dim(iq, t0, Q_BLK, 1)
        iw_b = lax.dynamic_slice_in_dim(iw, t0, Q_BLK, 1)
        logits = jnp.einsum('bqhe,bse->bqhs', iq_b, ik)
        i_score = jnp.einsum('bqh,bqhs->bqs', iw_b, jax.nn.relu(logits)).astype(jnp.float32)
        causal = key_ids[None, :] <= t[:, None]
        i_score = jnp.where(causal[None], i_score, -jnp.inf)
        _, idx = lax.top_k(i_score, k_top)
        k_g = gather_tok(k, idx)
        v_g = gather_tok(v, idx)
        q_b = lax.dynamic_slice_in_dim(q, t0, Q_BLK, 1)
        s = jnp.einsum('bqhd,bqkd->bqhk', q_b, k_g) * scale
        p = masked_softmax(s, (idx <= t[None, :, None])[:, :, None, :])
        o = jnp.einsum('bqhk,bqkd->bqhd', p.astype(v_g.dtype), v_g)
        return o.reshape(B, Q_BLK, DSA_WIDTH)

    out = lax.map(block, jnp.arange(n_q))
    return out.transpose(1, 0, 2, 3).reshape(B, S, DSA_WIDTH)


def sgu_mixer(z, norm_g, norm_b, w_s, b_s):
    B, S = z.shape[0], z.shape[1]
    z = jax.nn.gelu(z)
    u, v = jnp.split(z, 2, axis=-1)
    v = layer_norm(v.reshape(B, S, SGU_GROUPS, HEAD_DIM), norm_g, norm_b)
    v = v.reshape(B, S // SGU_CHUNK, SGU_CHUNK, SGU_GROUPS, HEAD_DIM)
    causal = jnp.tril(jnp.ones((SGU_CHUNK, SGU_CHUNK), w_s.dtype))
    s = jnp.einsum('gts,bnsgd->bntgd', w_s * causal, v) + jnp.swapaxes(b_s, 0, 1)[None, None, :, :, None]
    return u * s.reshape(B, S, SGU_WIDTH)


def swiglu(h, w_gate, w_up, w_down):
    return (jax.nn.silu(h @ w_gate) * (h @ w_up)) @ w_down


def moe_swiglu(h, w_router, b_router, w_gate, w_up, w_down):
    logits = (h @ w_router + b_router).astype(jnp.float32)
    top_val, top_idx = lax.top_k(logits, TOP_K)
    top_w = jax.nn.softmax(top_val, axis=-1)
    gates = jnp.sum(jax.nn.one_hot(top_idx, N_EXPERTS, dtype=jnp.float32) * top_w[..., None], axis=-2)
    gates = gates.astype(h.dtype)
    out = jnp.zeros_like(h)
    for e in range(N_EXPERTS):
        out = out + gates[..., e:e + 1] * swiglu(h, w_gate[e], w_up[e], w_down[e])
    return out


def setup_inputs(seed: int = 0) -> dict:
    key = jax.random.key(seed)
    ks = iter(jax.random.split(key, 40))
    D, dh = D_MODEL, HEAD_DIM

    def nrm(shape, scale):
        return jax.random.normal(next(ks), shape, jnp.float32) * scale

    x = nrm((BATCH, SEQ, D), 1.0)
    c = nrm((BATCH, D), 1.0)
    offs = jax.random.randint(next(ks), (BATCH, 1), 0, POS_OFFSET_MAX, dtype=jnp.int32)
    positions = (jnp.arange(SEQ, dtype=jnp.int32)[None, :] + offs).astype(jnp.int32)
    return {
        'x': x,
        'c': c,
        'positions': positions,
        'w_ada': nrm((DEPTH, D, 6 * D), D ** -0.5),
        'b_ada': nrm((DEPTH, 6 * D), 0.02),
        'w_in': nrm((DEPTH, D, IN_WIDTH), D ** -0.5),
        'nsa_cmp_pos': nrm((DEPTH, 2, CMP_LEN, dh), 0.5),
        'nsa_cmp_w1': nrm((DEPTH, 2, CMP_LEN, dh, dh), (CMP_LEN * dh) ** -0.5),
        'nsa_cmp_w2': nrm((DEPTH, 2, dh, dh), dh ** -0.5),
        'dsa_kv_norm': 1.0 + nrm((DEPTH, DSA_LATENT), 0.02),
        'dsa_w_uk': nrm((DEPTH, DSA_LATENT, dh), DSA_LATENT ** -0.5),
        'dsa_w_uv': nrm((DEPTH, DSA_LATENT, dh), DSA_LATENT ** -0.5),
        'sgu_norm_g': 1.0 + nrm((DEPTH, SGU_GROUPS, dh), 0.02),
        'sgu_norm_b': nrm((DEPTH, SGU_GROUPS, dh), 0.02),
        'sgu_w': nrm((DEPTH, SGU_GROUPS, SGU_CHUNK, SGU_CHUNK), SGU_CHUNK ** -0.5),
        'sgu_b': 1.0 + nrm((DEPTH, SGU_GROUPS, SGU_CHUNK), 0.02),
        'w_out': nrm((DEPTH, MIX_WIDTH, D), BETA * MIX_WIDTH ** -0.5),
        'ln1_g': 1.0 + nrm((DEPTH, D), 0.02),
        'ln1_b': nrm((DEPTH, D), 0.02),
        'ln2_g': 1.0 + nrm((DEPTH, D), 0.02),
        'ln2_b': nrm((DEPTH, D), 0.02),
        'ffn_w_gate': nrm((N_DENSE, D, D_FF), D ** -0.5),
        'ffn_w_up': nrm((N_DENSE, D, D_FF), D ** -0.5),
        'ffn_w_down': nrm((N_DENSE, D_FF, D), BETA * D_FF ** -0.5),
        'moe_w_router': nrm((N_MOE, D, N_EXPERTS), D ** -0.5),
        'moe_b_router': nrm((N_MOE, N_EXPERTS), 0.01),
        'moe_w_gate': nrm((N_MOE, N_EXPERTS, D, EXPERT_FF), D ** -0.5),
        'moe_w_up': nrm((N_MOE, N_EXPERTS, D, EXPERT_FF), D ** -0.5),
        'moe_w_down': nrm((N_MOE, N_EXPERTS, EXPERT_FF, D), BETA * EXPERT_FF ** -0.5),
    }


def reference(x, c, positions, w_ada, b_ada, w_in, nsa_cmp_pos, nsa_cmp_w1, nsa_cmp_w2,
              dsa_kv_norm, dsa_w_uk, dsa_w_uv, sgu_norm_g, sgu_norm_b, sgu_w, sgu_b,
              w_out, ln1_g, ln1_b, ln2_g, ln2_b, ffn_w_gate, ffn_w_up, ffn_w_down,
              moe_w_router, moe_b_router, moe_w_gate, moe_w_up, moe_w_down):
    B, S = x.shape[0], x.shape[1]
    cos_h, sin_h = rope_tables(positions, HEAD_DIM)
    cos_i, sin_i = rope_tables(positions, IDX_DIM)
    split_points = np.cumsum(SPLIT_SIZES)[:-1].tolist()
    c_act = jax.nn.silu(c)
    for layer in range(DEPTH):
        mod = (c_act @ w_ada[layer] + b_ada[layer])[:, None, :]
        shift1, scale1, gate1, shift2, scale2, gate2 = jnp.split(mod, 6, axis=-1)

        h = modulate(x, shift1, scale1)
        z = h @ w_in[layer]
        nsa_q, nsa_kv, nsa_g, dsa_q, dsa_ckv, idx_q, idx_k, idx_w, sgu_z = jnp.split(z, split_points, axis=-1)
        o_a = nsa_mixer(nsa_q.reshape(B, S, NSA_HEADS, HEAD_DIM), nsa_kv, nsa_g, cos_h, sin_h,
                        nsa_cmp_pos[layer], nsa_cmp_w1[layer], nsa_cmp_w2[layer])
        o_b = dsa_mixer(dsa_q.reshape(B, S, DSA_HEADS, HEAD_DIM), dsa_ckv, idx_q, idx_k, idx_w,
                        cos_h, sin_h, cos_i, sin_i, dsa_kv_norm[layer], dsa_w_uk[layer], dsa_w_uv[layer])
        o_c = sgu_mixer(sgu_z, sgu_norm_g[layer], sgu_norm_b[layer], sgu_w[layer], sgu_b[layer])
        mix = jnp.concatenate([o_a, o_b, o_c], axis=-1) @ w_out[layer]
        x = layer_norm(ALPHA * x + gate1 * mix, ln1_g[layer], ln1_b[layer])

        h = modulate(x, shift2, scale2)
        if layer % 2 == 0:
            j = layer // 2
            f = swiglu(h, ffn_w_gate[j], ffn_w_up[j], ffn_w_down[j])
        else:
            j = layer // 2
            f = moe_swiglu(h, moe_w_router[j], moe_b_router[j], moe_w_gate[j], moe_w_up[j], moe_w_down[j])
        x = layer_norm(ALPHA * x + gate2 * f, ln2_g[layer], ln2_b[layer])
    return x
```

```python
import functools

import numpy as np
import jax
import jax.numpy as jnp
from jax import lax
from jax.experimental import pallas as pl
from jax.experimental.pallas import tpu as pltpu

F32 = jnp.float32
BF16 = jnp.bfloat16
I32 = jnp.int32

HEAD_DIM = 64
Q_BLK = 128
ROPE_THETA = 10000.0
LN_EPS = 1e-5
RMS_EPS = 1e-6

NSA_HEADS = 6
NSA_KV_HEADS = 2
NSA_GROUP = NSA_HEADS // NSA_KV_HEADS
NSA_WIDTH = NSA_HEADS * HEAD_DIM
CMP_LEN = 32
CMP_STRIDE = 16
SEL_BLOCK = 64
SEL_TOP = 16
WINDOW = 512
FORCE_BONUS = 1e4

DSA_HEADS = 6
DSA_WIDTH = DSA_HEADS * HEAD_DIM
DSA_LATENT = 128
IDX_HEADS = 4
IDX_DIM = 32
IDX_TOPK_MAX = 256

SGU_GROUPS = 4
SGU_CHUNK = 128
SGU_WIDTH = SGU_GROUPS * HEAD_DIM

N_EXPERTS = 8
TOP_K = 2

LANES = 128
VMEM_LIMIT = 56 * 1024 * 1024
NEG = -1e30
INT_MIN = -(2 ** 31)

C_NQ = 0
C_KV = 384
C_DQ = 1152
C_CKV = 1536
C_IQ = 1664
C_SGU = 1792
C_SMALL = 2304
C_MAIN = 2432
C_NQ_SW = 2432
C_KSEL_SW = 2816
C_KWIN_SW = 2944
C_DQ_SW = 3072
C_IQ_SW = 3456
C_SMALL_SW = 3584
C_TOTAL = 3712
SMALL_IW = 32
SMALL_GATE = 36

SEL_KC = 256


def _cparams(sem):
    return pltpu.CompilerParams(dimension_semantics=sem, vmem_limit_bytes=VMEM_LIMIT)


def _dot(a, b):
    return jnp.dot(a.astype(BF16), b.astype(BF16), preferred_element_type=F32)


def _dot_nt(a, b):
    return lax.dot_general(a.astype(BF16), b.astype(BF16), (((1,), (1,)), ((), ())),
                           preferred_element_type=F32)


def _dot_split(a_exact, b):
    b_hi = b.astype(BF16)
    b_lo = (b - b_hi.astype(F32)).astype(BF16)
    return (jnp.dot(a_exact, b_hi, preferred_element_type=F32)
            + jnp.dot(a_exact, b_lo, preferred_element_type=F32))


def _gelu(x):
    return 0.5 * x * (1.0 + jnp.tanh(0.7978845608028654 * (x + 0.044715 * (x * x * x))))


def _silu(x):
    return x * (1.0 / (1.0 + jnp.exp(-x)))


def _sigmoid(x):
    return 1.0 / (1.0 + jnp.exp(-x))


def _standardize(x):
    mu = jnp.mean(x, axis=-1, keepdims=True)
    xc = x - mu
    var = jnp.mean(xc * xc, axis=-1, keepdims=True)
    return xc * lax.rsqrt(var + LN_EPS)


def _adaln_kernel(c_ref, w_ref, b_ref, o_ref):
    c = c_ref[...]
    o_ref[...] = jnp.dot(_silu(c), w_ref[...], preferred_element_type=F32,
                         precision=lax.Precision.HIGHEST) + b_ref[...]


def _adaln(c, w_ada, b_ada):
    depth, d, n = w_ada.shape
    b = c.shape[0]
    tn = 512
    return pl.pallas_call(
        _adaln_kernel,
        grid=(depth, n // tn),
        in_specs=[pl.BlockSpec((b, d), lambda l, j: (0, 0)),
                  pl.BlockSpec((None, d, tn), lambda l, j: (l, 0, j)),
                  pl.BlockSpec((None, 1, tn), lambda l, j: (l, 0, j))],
        out_specs=pl.BlockSpec((None, b, tn), lambda l, j: (l, 0, j)),
        out_shape=jax.ShapeDtypeStruct((depth, b, n), F32),
        compiler_params=_cparams(("arbitrary", "arbitrary")),
        name="adaln",
    )(c, w_ada, b_ada.reshape(depth, 1, n))


def _inproj_kernel(x_ref, mod_ref, cos_ref, sin_ref, cosi_ref, sini_ref, w_ref, wkv_ref, kvg_ref,
                   nq_ref, nqr_ref, kvcmp_ref, ksel_ref, vselT_ref, kwin_ref, vwinT_ref,
                   dq_ref, dk_ref, dvT_ref, iq_ref, small_ref, sgu_ref):
    tm = x_ref.shape[0]
    shift = mod_ref[0:1, :]
    scale = mod_ref[1:2, :]
    hb = (_standardize(x_ref[...]) * (1.0 + scale) + shift).astype(BF16)

    def proj(c0, n):
        return jnp.dot(hb, w_ref[:, c0:c0 + n], preferred_element_type=F32)

    cos = cos_ref[...]
    sin = sin_ref[...]
    cos3 = jnp.concatenate([cos, cos, cos], axis=1)
    sin3 = jnp.concatenate([sin, sin, sin], axis=1)
    lane = lax.broadcasted_iota(I32, (1, LANES), 1)

    zq = proj(C_NQ, 384)
    nq_ref[...] = zq.astype(BF16)
    nqr_ref[...] = (zq * cos3 + proj(C_NQ_SW, 384) * sin3).astype(BF16)

    kvcmp_ref[...] = proj(C_KV, 256)
    ksel_ref[...] = (proj(C_KV + 256, 128) * cos + proj(C_KSEL_SW, 128) * sin).astype(BF16)
    vsel_t = proj(C_KV + 384, 128).T.astype(BF16)
    for j in range(tm // SEL_KC):
        vselT_ref[j] = vsel_t[:, j * SEL_KC:(j + 1) * SEL_KC]
    kwin_ref[...] = (proj(C_KV + 512, 128) * cos + proj(C_KWIN_SW, 128) * sin).astype(BF16)
    vwin_t = proj(C_KV + 640, 128).T.astype(BF16)
    for j in range(tm // Q_BLK):
        vwinT_ref[j] = vwin_t[:, j * Q_BLK:(j + 1) * Q_BLK]

    dq_ref[...] = (proj(C_DQ, 384) * cos3 + proj(C_DQ_SW, 384) * sin3).astype(BF16)

    ckv = proj(C_CKV, 128)
    ckv = ckv * lax.rsqrt(jnp.mean(ckv * ckv, axis=-1, keepdims=True) + RMS_EPS) * kvg_ref[...]
    kd = jnp.dot(ckv.astype(BF16), wkv_ref[...], preferred_element_type=F32)
    first = lane < HEAD_DIM
    dkv = kd[:, :128] * jnp.where(first, cos, 1.0) + kd[:, 128:] * jnp.where(first, sin, 0.0)
    dk_ref[...] = dkv.astype(BF16)
    dkv_t = dkv.T.astype(BF16)
    for j in range(tm // SEL_KC):
        dvT_ref[j] = dkv_t[:, j * SEL_KC:(j + 1) * SEL_KC]

    cosi = cosi_ref[...]
    sini = sini_ref[...]
    iq_ref[...] = (proj(C_IQ, 128) * cosi + proj(C_IQ_SW, 128) * sini).astype(BF16)
    isk = lane < IDX_DIM
    small_ref[...] = (proj(C_SMALL, 128) * jnp.where(isk, cosi, 1.0)
                      + proj(C_SMALL_SW, 128) * jnp.where(isk, sini, 0.0))
    sgu_ref[...] = proj(C_SGU, 512)


def _inproj(xf, mod, tabs, w_ext, wkv, kvg, batch, seq):
    t, d = xf.shape
    tm = 512
    per_b = seq // tm
    row = lambda n: pl.BlockSpec((tm, n), lambda i: (i, 0))
    tr256 = pl.BlockSpec((None, tm // SEL_KC, 128, SEL_KC), lambda i: (i // per_b, i % per_b, 0, 0))
    tr128 = pl.BlockSpec((None, tm // Q_BLK, 128, Q_BLK), lambda i: (i // per_b, i % per_b, 0, 0))
    full = lambda a: pl.BlockSpec(a.shape, lambda i: (0,) * a.ndim)
    out_shape = (
        jax.ShapeDtypeStruct((t, 384), BF16),
        jax.ShapeDtypeStruct((t, 384), BF16),
        jax.ShapeDtypeStruct((t, 256), F32),
        jax.ShapeDtypeStruct((t, 128), BF16),
        jax.ShapeDtypeStruct((batch, seq // SEL_KC, 128, SEL_KC), BF16),
        jax.ShapeDtypeStruct((t, 128), BF16),
        jax.ShapeDtypeStruct((batch, seq // Q_BLK, 128, Q_BLK), BF16),
        jax.ShapeDtypeStruct((t, 384), BF16),
        jax.ShapeDtypeStruct((t, 128), BF16),
        jax.ShapeDtypeStruct((batch, seq // SEL_KC, 128, SEL_KC), BF16),
        jax.ShapeDtypeStruct((t, 128), BF16),
        jax.ShapeDtypeStruct((t, 128), F32),
        jax.ShapeDtypeStruct((t, 512), F32),
    )
    out_specs = (row(384), row(384), row(256), row(128), tr256, row(128), tr128,
                 row(384), row(128), tr256, row(128), row(128), row(512))
    return pl.pallas_call(
        _inproj_kernel,
        grid=(t // tm,),
        in_specs=[row(d),
                  pl.BlockSpec((None, 6, d), lambda i: (i // per_b, 0, 0)),
                  row(128), row(128), row(128), row(128),
                  full(w_ext), full(wkv), full(kvg)],
        out_specs=out_specs,
        out_shape=out_shape,
        compiler_params=_cparams(("parallel",)),
        name="inproj",
    )(xf, mod, *tabs, w_ext, wkv, kvg)


def _compress_kernel(h_ref, plo_ref, phi_ref, wlo_ref, whi_ref, w2_ref, kc_ref, vct_ref):
    h = h_ref[...]
    a = _dot(h + plo_ref[...], wlo_ref[...])
    b = _dot(h + phi_ref[...], whi_ref[...])
    nh = h.shape[0]
    pre = a + pltpu.roll(b, nh - 1, 0)
    cmp = _dot(_gelu(pre), w2_ref[...])
    kc_ref[...] = cmp[:, :128].astype(BF16)
    vct_ref[...] = cmp[:, 128:].T.astype(BF16)


def _compress(kvcmp_h, plo, phi, wlo, whi, w2):
    batch, nh, width = kvcmp_h.shape
    full = lambda a: pl.BlockSpec(a.shape, lambda b: (0,) * a.ndim)
    return pl.pallas_call(
        _compress_kernel,
        grid=(batch,),
        in_specs=[pl.BlockSpec((None, nh, width), lambda b: (b, 0, 0)),
                  full(plo), full(phi), full(wlo), full(whi), full(w2)],
        out_specs=(pl.BlockSpec((None, nh, 128), lambda b: (b, 0, 0)),
                   pl.BlockSpec((None, 128, nh), lambda b: (b, 0, 0))),
        out_shape=(jax.ShapeDtypeStruct((batch, nh, 128), BF16),
                   jax.ShapeDtypeStruct((batch, 128, nh), BF16)),
        compiler_params=_cparams(("parallel",)),
        name="nsa_compress",
    )(kvcmp_h, plo, phi, wlo, whi, w2)


def _softmax_cols(s, mask):
    m = jnp.max(jnp.where(mask, s, NEG), axis=0, keepdims=True)
    p = jnp.where(mask, jnp.exp(s - m), 0.0)
    return p / jnp.maximum(jnp.sum(p, axis=0, keepdims=True), 1e-30)


def _flash_step(s, maskf, vt, m_ref, l_ref, acc_ref):
    on = maskf > 0.5
    m_old = m_ref[...]
    m_new = jnp.maximum(m_old, jnp.max(jnp.where(on, s, NEG), axis=0, keepdims=True))
    alpha = jnp.exp(m_old - m_new)
    p = jnp.where(on, jnp.exp(s - m_new), 0.0)
    l_ref[...] = alpha * l_ref[...] + jnp.sum(p, axis=0, keepdims=True)
    acc_ref[...] = alpha * acc_ref[...] + _dot(vt, p)
    m_ref[...] = m_new


def _nsa_kernel(nq_ref, nqr_ref, small_ref, kc_ref, vct_ref, ksel_ref, vselT_ref, kwin_ref,
                vwinT_ref, ovt_ref, o_ref, sc_ref, sel_ref, m_ref, l_ref, acc_ref,
                *, n_blk, n_sel, n_cmp, seq):
    i = pl.program_id(1)
    t0 = i * Q_BLK
    scale = HEAD_DIM ** -0.5
    tq = t0 + lax.broadcasted_iota(I32, (1, Q_BLK), 1)
    tq3 = jnp.concatenate([tq, tq, tq], axis=1)
    small_t = small_ref[...].T
    nq = nq_ref[...]
    nqr = nqr_ref[...]
    nh = kc_ref.shape[0]
    heads = []
    for g in range(NSA_KV_HEADS):
        lo, hi = g * HEAD_DIM, (g + 1) * HEAD_DIM
        hs = [g * NSA_GROUP + r for r in range(NSA_GROUP)]
        qc = jnp.concatenate([nq[:, h * HEAD_DIM:(h + 1) * HEAD_DIM] for h in hs], axis=0)
        qr = jnp.concatenate([nqr[:, h * HEAD_DIM:(h + 1) * HEAD_DIM] for h in hs], axis=0)

        s_c = _dot_nt(kc_ref[:, lo:hi], qc) * scale
        n_io = lax.broadcasted_iota(I32, (nh, 1), 0)
        m_c = (n_io * CMP_STRIDE + (CMP_LEN - 1) <= tq3) & (n_io < n_cmp)
        p_c = _softmax_cols(s_c, m_c)
        o_c = _dot(vct_ref[lo:hi, :], p_c)

        p_sum = p_c[:, 0:Q_BLK] + p_c[:, Q_BLK:2 * Q_BLK] + p_c[:, 2 * Q_BLK:3 * Q_BLK]
        imp = _dot_split(ovt_ref[...], p_sum)
        j_io = lax.broadcasted_iota(I32, (n_blk, 1), 0)
        cur = tq >> 6
        valid = j_io <= cur
        forced = (j_io == 0) | (j_io == cur) | (j_io == cur - 1)
        score = jnp.where(valid, imp + jnp.where(forced, FORCE_BONUS, 0.0), -jnp.inf)
        sc_ref[...] = score
        rank = jnp.zeros((n_blk, Q_BLK), F32)
        for b in range(n_blk):
            row = sc_ref[b:b + 1, :]
            beats = (row > score) | ((row == score) & (j_io > b))
            rank = rank + jnp.where(beats, 1.0, 0.0)
        sel_ref[...] = jnp.where((rank < n_sel) & valid, 1.0, 0.0)

        m_ref[...] = jnp.full(m_ref.shape, NEG, F32)
        l_ref[...] = jnp.zeros(l_ref.shape, F32)
        acc_ref[...] = jnp.zeros(acc_ref.shape, F32)
        bpc = SEL_KC // SEL_BLOCK

        def sel_step(c, carry):
            k0 = pl.multiple_of(c * SEL_KC, SEL_KC)
            s = _dot_nt(ksel_ref[pl.ds(k0, SEL_KC), lo:hi], qr) * scale
            rows = [jnp.broadcast_to(sel_ref[pl.ds(c * bpc + u, 1), :], (SEL_BLOCK, Q_BLK))
                    for u in range(bpc)]
            key = k0 + lax.broadcasted_iota(I32, (SEL_KC, 1), 0)
            mk = jnp.where(key <= tq, jnp.concatenate(rows, axis=0), 0.0)
            mk3 = jnp.concatenate([mk, mk, mk], axis=1)
            _flash_step(s, mk3, vselT_ref[c, lo:hi, :], m_ref, l_ref, acc_ref)
            return carry

        lax.fori_loop(0, (t0 + Q_BLK + SEL_KC - 1) // SEL_KC, sel_step, 0)
        o_s = acc_ref[...] / jnp.maximum(l_ref[...], 1e-30)

        nband = WINDOW // Q_BLK + 1
        cb = jnp.maximum(i - WINDOW // Q_BLK, 0)
        b0 = pl.multiple_of(cb * Q_BLK, Q_BLK)
        s_w = _dot_nt(kwin_ref[pl.ds(b0, nband * Q_BLK), lo:hi], qr) * scale
        key = b0 + lax.broadcasted_iota(I32, (nband * Q_BLK, 1), 0)
        diff = tq3 - key
        p_w = _softmax_cols(s_w, (diff >= 0) & (diff < WINDOW))
        o_w = jnp.zeros((HEAD_DIM, NSA_GROUP * Q_BLK), F32)
        for u in range(nband):
            o_w = o_w + _dot(vwinT_ref[cb + u, lo:hi, :], p_w[u * Q_BLK:(u + 1) * Q_BLK, :])

        for r in range(NSA_GROUP):
            gi = SMALL_GATE + (g * NSA_GROUP + r) * 3
            gt = _sigmoid(small_t[gi:gi + 3, :])
            cs = slice(r * Q_BLK, (r + 1) * Q_BLK)
            heads.append(gt[0:1, :] * o_c[:, cs] + gt[1:2, :] * o_s[:, cs] + gt[2:3, :] * o_w[:, cs])
    o_ref[...] = jnp.concatenate(heads, axis=0).T.astype(BF16)


def _nsa(nq, nqr, small, kc, vct, ksel, vselT, kwin, vwinT, ovt, batch, seq):
    n_q = seq // Q_BLK
    n_blk = seq // SEL_BLOCK
    n_cmp = (seq - CMP_LEN) // CMP_STRIDE + 1
    nh = kc.shape[1]
    blk = lambda n: pl.BlockSpec((Q_BLK, n), lambda b, i: (b * n_q + i, 0))
    perb2 = lambda r, c: pl.BlockSpec((None, r, c), lambda b, i: (b, 0, 0))
    perb_rows = pl.BlockSpec((seq, 128), lambda b, i: (b, 0))
    kern = functools.partial(_nsa_kernel, n_blk=n_blk, n_sel=min(SEL_TOP, n_blk), n_cmp=n_cmp, seq=seq)
    return pl.pallas_call(
        kern,
        grid=(batch, n_q),
        in_specs=[blk(384), blk(384), blk(128),
                  perb2(nh, 128), perb2(128, nh),
                  perb_rows,
                  pl.BlockSpec((None, seq // SEL_KC, 128, SEL_KC), lambda b, i: (b, 0, 0, 0)),
                  perb_rows,
                  pl.BlockSpec((None, seq // Q_BLK, 128, Q_BLK), lambda b, i: (b, 0, 0, 0)),
                  pl.BlockSpec(ovt.shape, lambda b, i: (0, 0))],
        out_specs=blk(384),
        out_shape=jax.ShapeDtypeStruct((batch * seq, 384), BF16),
        scratch_shapes=[pltpu.VMEM((n_blk, Q_BLK), F32), pltpu.VMEM((n_blk, Q_BLK), F32),
                        pltpu.VMEM((1, 384), F32), pltpu.VMEM((1, 384), F32),
                        pltpu.VMEM((HEAD_DIM, 384), F32)],
        compiler_params=_cparams(("parallel", "arbitrary")),
        name="nsa_attn",
    )(nq, nqr, small, kc, vct, ksel, vselT, kwin, vwinT, ovt)


def _dsa_kernel(dq_ref, iq_ref, small_ref, dk_ref, dvT_ref, ltri_ref, o_ref,
                ord_ref, m_ref, l_ref, acc_ref, *, k_top):
    i = pl.program_id(1)
    t0 = i * Q_BLK
    kc = SEL_KC
    n_ch = (t0 + Q_BLK + kc - 1) // kc
    scale = HEAD_DIM ** -0.5
    tq = t0 + lax.broadcasted_iota(I32, (1, Q_BLK), 1)

    small_t = small_ref[pl.ds(pl.multiple_of(t0, Q_BLK), Q_BLK), :].T
    w_rows = [small_t[SMALL_IW + h:SMALL_IW + h + 1, :] * (IDX_HEADS ** -0.5) for h in range(IDX_HEADS)]
    iq = iq_ref[...]
    iqs = jnp.concatenate([iq[:, h * IDX_DIM:(h + 1) * IDX_DIM] for h in range(IDX_HEADS)], axis=0)

    def score_step(c, carry):
        k0 = pl.multiple_of(c * kc, kc)
        ik = small_ref[pl.ds(k0, kc), :][:, 0:IDX_DIM]
        lg = _dot_nt(ik, iqs)
        sc = jnp.zeros((kc, Q_BLK), F32)
        for h in range(IDX_HEADS):
            sc = sc + w_rows[h] * jnp.maximum(lg[:, h * Q_BLK:(h + 1) * Q_BLK], 0.0)
        sc = jnp.where(sc == 0.0, 0.0, sc)
        bits = lax.bitcast_convert_type(sc, I32)
        ordv = bits ^ ((bits >> 31) & 0x7FFFFFFF)
        key = k0 + lax.broadcasted_iota(I32, (kc, 1), 0)
        ord_ref[pl.ds(k0, kc), :] = jnp.where(key <= tq, ordv, INT_MIN)
        return carry

    lax.fori_loop(0, n_ch, score_step, 0)

    def count(pred):
        def body(c, acc):
            k0 = pl.multiple_of(c * kc, kc)
            hit = jnp.where(pred(ord_ref[pl.ds(k0, kc), :]), 1, 0).astype(I32)
            return acc + jnp.sum(hit.reshape(kc // 8, 8, Q_BLK), axis=0)
        part = lax.fori_loop(0, n_ch, body, jnp.zeros((8, Q_BLK), I32))
        return jnp.sum(part, axis=0, keepdims=True)

    prefix = jnp.zeros((1, Q_BLK), I32)
    for bit in range(31, -1, -1):
        bitv = np.int32(INT_MIN) if bit == 31 else np.int32(1 << bit)
        cand = (prefix | bitv) ^ np.int32(INT_MIN)
        cnt = count(lambda o, cand=cand: o >= cand)
        prefix = jnp.where(cnt >= k_top, prefix | bitv, prefix)
    thr = prefix ^ np.int32(INT_MIN)
    need = (k_top - count(lambda o: o > thr)).astype(F32)

    q = dq_ref[...]
    qs = jnp.concatenate([q[:, h * HEAD_DIM:(h + 1) * HEAD_DIM] for h in range(DSA_HEADS)], axis=0)
    m_ref[...] = jnp.full(m_ref.shape, NEG, F32)
    l_ref[...] = jnp.zeros(l_ref.shape, F32)
    acc_ref[...] = jnp.zeros(acc_ref.shape, F32)

    def attn_step(c, seen):
        k0 = pl.multiple_of(c * kc, kc)
        o = ord_ref[pl.ds(k0, kc), :]
        eq = jnp.where(o == thr, 1.0, 0.0)
        before = jnp.dot(ltri_ref[...], eq.astype(BF16), preferred_element_type=F32) + seen
        key = k0 + lax.broadcasted_iota(I32, (kc, 1), 0)
        take = (o > thr) | ((eq > 0.5) & (before < need))
        mk = jnp.where(take & (key <= tq), 1.0, 0.0)
        mk6 = jnp.concatenate([mk] * DSA_HEADS, axis=1)
        s = _dot_nt(dk_ref[pl.ds(k0, kc), 0:HEAD_DIM], qs) * scale
        _flash_step(s, mk6, dvT_ref[c, HEAD_DIM:2 * HEAD_DIM, :], m_ref, l_ref, acc_ref)
        return seen + jnp.sum(eq, axis=0, keepdims=True)

    lax.fori_loop(0, n_ch, attn_step, jnp.zeros((1, Q_BLK), F32))
    o_t = acc_ref[...] / jnp.maximum(l_ref[...], 1e-30)
    heads = [o_t[:, h * Q_BLK:(h + 1) * Q_BLK] for h in range(DSA_HEADS)]
    o_ref[...] = jnp.concatenate(heads, axis=0).T.astype(BF16)


def _dsa(dq, iq, small, dk, dvT, ltri, batch, seq):
    n_q = seq // Q_BLK
    blk = lambda n: pl.BlockSpec((Q_BLK, n), lambda b, i: (b * n_q + i, 0))
    perb_rows = pl.BlockSpec((seq, 128), lambda b, i: (b, 0))
    kern = functools.partial(_dsa_kernel, k_top=min(IDX_TOPK_MAX, seq // 4))
    return pl.pallas_call(
        kern,
        grid=(batch, n_q),
        in_specs=[blk(384), blk(128), perb_rows, perb_rows,
                  pl.BlockSpec((None, seq // SEL_KC, 128, SEL_KC), lambda b, i: (b, 0, 0, 0)),
                  pl.BlockSpec(ltri.shape, lambda b, i: (0, 0))],
        out_specs=blk(384),
        out_shape=jax.ShapeDtypeStruct((batch * seq, 384), BF16),
        scratch_shapes=[pltpu.VMEM((seq, Q_BLK), I32),
                        pltpu.VMEM((1, DSA_HEADS * Q_BLK), F32), pltpu.VMEM((1, DSA_HEADS * Q_BLK), F32),
                        pltpu.VMEM((HEAD_DIM, DSA_HEADS * Q_BLK), F32)],
        compiler_params=_cparams(("parallel", "arbitrary")),
        name="dsa_attn",
    )(dq, iq, small, dk, dvT, ltri)


def _sgu_kernel(z_ref, g_ref, b_ref, w_ref, bs_ref, o_ref):
    z = _gelu(z_ref[...])
    row = lax.broadcasted_iota(I32, (SGU_CHUNK, SGU_CHUNK), 0)
    col = lax.broadcasted_iota(I32, (SGU_CHUNK, SGU_CHUNK), 1)
    outs = []
    for g in range(SGU_GROUPS):
        lo, hi = g * HEAD_DIM, (g + 1) * HEAD_DIM
        u = z[:, lo:hi]
        v = _standardize(z[:, SGU_WIDTH + lo:SGU_WIDTH + hi]) * g_ref[:, lo:hi] + b_ref[:, lo:hi]
        w = jnp.where(row >= col, w_ref[g], 0.0)
        outs.append(u * (_dot(w, v) + bs_ref[:, lo:hi]))
    o_ref[...] = jnp.concatenate(outs, axis=1).astype(BF16)


def _sgu(z, g, b, w, bs):
    t = z.shape[0]
    n_c = bs.shape[0] // SGU_CHUNK
    full = lambda a: pl.BlockSpec(a.shape, lambda i: (0,) * a.ndim)
    return pl.pallas_call(
        _sgu_kernel,
        grid=(t // SGU_CHUNK,),
        in_specs=[pl.BlockSpec((SGU_CHUNK, 2 * SGU_WIDTH), lambda i: (i, 0)),
                  full(g), full(b), full(w), full(bs)],
        out_specs=pl.BlockSpec((SGU_CHUNK, SGU_WIDTH), lambda i: (i, 0)),
        out_shape=jax.ShapeDtypeStruct((t, SGU_WIDTH), BF16),
        compiler_params=_cparams(("parallel",)),
        name="sgu",
    )(z, g, b, w, bs)


def _outproj_kernel(oa_ref, ob_ref, oc_ref, x_ref, mod_ref, w_ref, g_ref, b_ref, o_ref, *, alpha):
    mix = (jnp.dot(oa_ref[...], w_ref[0:384, :], preferred_element_type=F32)
           + jnp.dot(ob_ref[...], w_ref[384:768, :], preferred_element_type=F32)
           + jnp.dot(oc_ref[...], w_ref[768:1024, :], preferred_element_type=F32))
    y = alpha * x_ref[...] + mod_ref[2:3, :] * mix
    o_ref[...] = _standardize(y) * g_ref[...] + b_ref[...]


def _outproj(oa, ob, oc, xf, mod, w, g, b, seq, alpha):
    t, d = xf.shape
    tm = 512
    per_b = seq // tm
    row = lambda n: pl.BlockSpec((tm, n), lambda i: (i, 0))
    full = lambda a: pl.BlockSpec(a.shape, lambda i: (0,) * a.ndim)
    return pl.pallas_call(
        functools.partial(_outproj_kernel, alpha=alpha),
        grid=(t // tm,),
        in_specs=[row(384), row(384), row(256), row(d),
                  pl.BlockSpec((None, 6, d), lambda i: (i // per_b, 0, 0)),
                  full(w), full(g), full(b)],
        out_specs=row(d),
        out_shape=jax.ShapeDtypeStruct((t, d), F32),
        compiler_params=_cparams(("parallel",)),
        name="outproj",
    )(oa, ob, oc, xf, mod, w, g, b)


def _ffn_kernel(x_ref, mod_ref, wg_ref, wu_ref, wd_ref, g_ref, b_ref, o_ref, acc_ref, *, alpha, fc):
    x = x_ref[...]
    hb = (_standardize(x) * (1.0 + mod_ref[4:5, :]) + mod_ref[3:4, :]).astype(BF16)
    d_ff = wg_ref.shape[1]
    for j in range(d_ff // fc):
        cs = slice(j * fc, (j + 1) * fc)
        a = (_silu(jnp.dot(hb, wg_ref[:, cs], preferred_element_type=F32))
             * jnp.dot(hb, wu_ref[:, cs], preferred_element_type=F32)).astype(BF16)
        part = jnp.dot(a, wd_ref[cs, :], preferred_element_type=F32)
        if j == 0:
            acc_ref[...] = part
        else:
            acc_ref[...] += part
    y = alpha * x + mod_ref[5:6, :] * acc_ref[...]
    o_ref[...] = _standardize(y) * g_ref[...] + b_ref[...]


def _ffn(xf, mod, wg, wu, wd, g, b, seq, alpha):
    t, d = xf.shape
    tm = 512
    per_b = seq // tm
    row = pl.BlockSpec((tm, d), lambda i: (i, 0))
    once = lambda a: pl.BlockSpec(a.shape, lambda i: (0,) * a.ndim, pipeline_mode=pl.Buffered(1))
    return pl.pallas_call(
        functools.partial(_ffn_kernel, alpha=alpha, fc=256),
        grid=(t // tm,),
        in_specs=[row, pl.BlockSpec((None, 6, d), lambda i: (i // per_b, 0, 0)),
                  once(wg), once(wu), once(wd), once(g), once(b)],
        out_specs=row,
        out_shape=jax.ShapeDtypeStruct((t, d), F32),
        scratch_shapes=[pltpu.VMEM((tm, d), F32)],
        compiler_params=_cparams(("parallel",)),
        name="ffn",
    )(xf, mod, wg, wu, wd, g, b)


def _moe_kernel(x_ref, mod_ref, wr_ref, br_ref, wg_ref, wu_ref, wd_ref, g_ref, b_ref, o_ref,
                h_ref, gate_ref, acc_ref, *, alpha):
    e = pl.program_id(1)
    f = pl.program_id(2)
    tm = x_ref.shape[0]

    @pl.when((e == 0) & (f == 0))
    def _():
        h = _standardize(x_ref[...]) * (1.0 + mod_ref[4:5, :]) + mod_ref[3:4, :]
        h_ref[...] = h.astype(BF16)
        lane = lax.broadcasted_iota(I32, (1, LANES), 1)
        logits = jnp.dot(h, wr_ref[...], preferred_element_type=F32,
                         precision=lax.Precision.HIGHEST) + br_ref[...]
        lg = jnp.where(lane < N_EXPERTS, logits, -jnp.inf)
        v0 = jnp.max(lg, axis=-1, keepdims=True)
        lane_f = lane.astype(F32)
        i0 = jnp.min(jnp.where(lg == v0, lane_f, float(LANES)), axis=-1, keepdims=True)
        lg1 = jnp.where(lane_f == i0, -jnp.inf, lg)
        v1 = jnp.max(lg1, axis=-1, keepdims=True)
        i1 = jnp.min(jnp.where(lg1 == v1, lane_f, float(LANES)), axis=-1, keepdims=True)
        e1 = jnp.exp(v1 - v0)
        den = 1.0 + e1
        gates = jnp.where(lane_f == i0, 1.0 / den, 0.0) + jnp.where(lane_f == i1, e1 / den, 0.0)
        for k in range(N_EXPERTS):
            col = jnp.sum(jnp.where(lane == k, gates, 0.0), axis=-1, keepdims=True)
            gate_ref[k] = jnp.broadcast_to(col, (tm, LANES))
        acc_ref[...] = jnp.zeros(acc_ref.shape, F32)

    hb = h_ref[...]
    a = (_silu(jnp.dot(hb, wg_ref[...], preferred_element_type=F32))
         * jnp.dot(hb, wu_ref[...], preferred_element_type=F32))
    gb = gate_ref[e]
    fc = a.shape[1]
    a = (a * jnp.concatenate([gb] * (fc // LANES), axis=1)).astype(BF16)
    acc_ref[...] += jnp.dot(a, wd_ref[...], preferred_element_type=F32)

    @pl.when((e == pl.num_programs(1) - 1) & (f == pl.num_programs(2) - 1))
    def _():
        y = alpha * x_ref[...] + mod_ref[5:6, :] * acc_ref[...]
        o_ref[...] = _standardize(y) * g_ref[...] + b_ref[...]


def _moe(xf, mod, wr, br, wg, wu, wd, g, b, seq, alpha):
    t, d = xf.shape
    n_e, _, d_ff = wg.shape
    tm = 1024
    fc = 512
    per_b = seq // tm
    row = pl.BlockSpec((tm, d), lambda i, e, f: (i, 0))
    full = lambda a: pl.BlockSpec(a.shape, lambda i, e, f: (0,) * a.ndim)
    return pl.pallas_call(
        functools.partial(_moe_kernel, alpha=alpha),
        grid=(t // tm, n_e, d_ff // fc),
        in_specs=[row, pl.BlockSpec((None, 6, d), lambda i, e, f: (i // per_b, 0, 0)),
                  full(wr), full(br),
                  pl.BlockSpec((None, d, fc), lambda i, e, f: (e, 0, f)),
                  pl.BlockSpec((None, d, fc), lambda i, e, f: (e, 0, f)),
                  pl.BlockSpec((None, fc, d), lambda i, e, f: (e, f, 0)),
                  full(g), full(b)],
        out_specs=row,
        out_shape=jax.ShapeDtypeStruct((t, d), F32),
        scratch_shapes=[pltpu.VMEM((tm, d), BF16), pltpu.VMEM((n_e, tm, LANES), F32),
                        pltpu.VMEM((tm, d), F32)],
        compiler_params=_cparams(("parallel", "arbitrary", "arbitrary")),
        name="moe",
    )(xf, mod, wr, br, wg, wu, wd, g, b)


def _inproj_columns():
    sizes = (NSA_WIDTH, 6 * NSA_KV_HEADS * HEAD_DIM, 3 * NSA_HEADS, DSA_WIDTH, DSA_LATENT,
             IDX_HEADS * IDX_DIM, IDX_DIM, IDX_HEADS, 2 * SGU_WIDTH)
    starts = np.concatenate([[0], np.cumsum(sizes)])
    o_nq, o_kv, o_g, o_dq, o_ckv, o_iq, o_ik, o_iw, o_sgu = starts[:-1]
    perm = np.full((C_TOTAL,), -1, np.int64)

    def swap(base, n, dim):
        idx = np.arange(n)
        return base + (idx // dim) * dim + (idx % dim + dim // 2) % dim

    perm[C_NQ:C_NQ + 384] = o_nq + np.arange(384)
    perm[C_KV:C_KV + 768] = o_kv + np.arange(768)
    perm[C_DQ:C_DQ + 384] = o_dq + np.arange(384)
    perm[C_CKV:C_CKV + 128] = o_ckv + np.arange(128)
    perm[C_IQ:C_IQ + 128] = o_iq + np.arange(128)
    perm[C_SGU:C_SGU + 512] = o_sgu + np.arange(512)
    perm[C_SMALL:C_SMALL + IDX_DIM] = o_ik + np.arange(IDX_DIM)
    perm[C_SMALL + SMALL_IW:C_SMALL + SMALL_IW + IDX_HEADS] = o_iw + np.arange(IDX_HEADS)
    perm[C_SMALL + SMALL_GATE:C_SMALL + SMALL_GATE + 3 * NSA_HEADS] = o_g + np.arange(3 * NSA_HEADS)
    perm[C_NQ_SW:C_NQ_SW + 384] = swap(o_nq, 384, HEAD_DIM)
    perm[C_KSEL_SW:C_KSEL_SW + 128] = swap(o_kv + 256, 128, HEAD_DIM)
    perm[C_KWIN_SW:C_KWIN_SW + 128] = swap(o_kv + 512, 128, HEAD_DIM)
    perm[C_DQ_SW:C_DQ_SW + 384] = swap(o_dq, 384, HEAD_DIM)
    perm[C_IQ_SW:C_IQ_SW + 128] = swap(o_iq, 128, IDX_DIM)
    perm[C_SMALL_SW:C_SMALL_SW + IDX_DIM] = swap(o_ik, IDX_DIM, IDX_DIM)
    return perm, int(starts[-1])


def _swap_halves(w):
    half = w.shape[-1] // 2
    return jnp.concatenate([w[..., half:], w[..., :half]], axis=-1)


def _compress_weights(pos, w1, w2):
    half = CMP_LEN // 2
    eye_g = jnp.eye(NSA_KV_HEADS, dtype=F32)
    eye_j = jnp.eye(2, dtype=F32)

    def big(w1_half):
        w = jnp.einsum('jlde,jk,gh->ljgdkhe', w1_half, eye_j, eye_g)
        return w.reshape(half * 2 * NSA_KV_HEADS * HEAD_DIM, 2 * NSA_KV_HEADS * HEAD_DIM)

    def posrow(p_half):
        p = jnp.broadcast_to(p_half.transpose(1, 0, 2)[:, :, None, :], (half, 2, NSA_KV_HEADS, HEAD_DIM))
        return p.reshape(1, -1)

    w2big = jnp.einsum('jef,jk,gh->jgekhf', w2, eye_j, eye_g).reshape(256, 256)
    return (posrow(pos[:, :half]), posrow(pos[:, half:]),
            big(w1[:, :half]).astype(BF16), big(w1[:, half:]).astype(BF16), w2big.astype(BF16))


def _rope_tables(positions):
    def tab(dim, reps):
        inv = ROPE_THETA ** (-jnp.arange(0, dim, 2, dtype=F32) / dim)
        ang = positions.astype(F32)[..., None] * inv
        cos, sin = jnp.cos(ang), jnp.sin(ang)
        cos = jnp.tile(jnp.concatenate([cos, cos], axis=-1), (1, 1, reps))
        sin = jnp.tile(jnp.concatenate([-sin, sin], axis=-1), (1, 1, reps))
        return cos.reshape(-1, LANES), sin.reshape(-1, LANES)
    cos_h, sin_h = tab(HEAD_DIM, LANES // HEAD_DIM)
    cos_i, sin_i = tab(IDX_DIM, LANES // IDX_DIM)
    return cos_h, sin_h, cos_i, sin_i


def kernel(x, c, positions, w_ada, b_ada, w_in, nsa_cmp_pos, nsa_cmp_w1, nsa_cmp_w2, dsa_kv_norm, dsa_w_uk, dsa_w_uv, sgu_norm_g, sgu_norm_b, sgu_w, sgu_b, w_out, ln1_g, ln1_b, ln2_g, ln2_b, ffn_w_gate, ffn_w_up, ffn_w_down, moe_w_router, moe_b_router, moe_w_gate, moe_w_up, moe_w_down):
    batch, seq, d = x.shape
    depth = w_ada.shape[0]
    t = batch * seq
    alpha = (2 * depth) ** 0.25
    assert seq % 512 == 0 and seq >= WINDOW + Q_BLK

    tabs = _rope_tables(positions)
    mod_all = _adaln(c, w_ada, b_ada).reshape(depth, batch, 6, d)
    perm, in_width = _inproj_columns()
    perm = jnp.asarray(np.where(perm < 0, in_width, perm), I32)

    n_blk = seq // SEL_BLOCK
    n_half = seq // CMP_STRIDE
    cmp_start = np.arange(n_half)[None, :] * CMP_STRIDE
    blk_start = np.arange(n_blk)[:, None] * SEL_BLOCK
    ovt = jnp.asarray((cmp_start < blk_start + SEL_BLOCK) & (cmp_start + CMP_LEN > blk_start), BF16)
    ltri = jnp.asarray(np.tril(np.ones((SEL_KC, SEL_KC), np.float32), -1), BF16)

    xf = x.reshape(t, d)
    for layer in range(depth):
        mod = mod_all[layer]
        w_pad = jnp.concatenate([w_in[layer], jnp.zeros((d, 1), F32)], axis=1)
        w_ext = jnp.take(w_pad, perm, axis=1).astype(BF16)
        wkv = jnp.concatenate([dsa_w_uk[layer], dsa_w_uv[layer], _swap_halves(dsa_w_uk[layer]),
                               jnp.zeros((DSA_LATENT, HEAD_DIM), F32)], axis=1).astype(BF16)
        (nq, nqr, kvcmp, ksel, vselT, kwin, vwinT, dq, dk, dvT, iq, small, sgu_z) = _inproj(
            xf, mod, tabs, w_ext, wkv, dsa_kv_norm[layer].reshape(1, -1), batch, seq)

        plo, phi, wlo, whi, w2big = _compress_weights(nsa_cmp_pos[layer], nsa_cmp_w1[layer], nsa_cmp_w2[layer])
        kc, vct = _compress(kvcmp.reshape(batch, n_half, CMP_STRIDE * 256), plo, phi, wlo, whi, w2big)
        o_a = _nsa(nq, nqr, small, kc, vct, ksel, vselT, kwin, vwinT, ovt, batch, seq)
        o_b = _dsa(dq, iq, small, dk, dvT, ltri, batch, seq)
        bs = jnp.repeat(sgu_b[layer].T, HEAD_DIM, axis=1)
        o_c = _sgu(sgu_z, sgu_norm_g[layer].reshape(1, -1), sgu_norm_b[layer].reshape(1, -1),
                   sgu_w[layer], bs)
        xf = _outproj(o_a, o_b, o_c, xf, mod, w_out[layer].astype(BF16),
                      ln1_g[layer].reshape(1, -1), ln1_b[layer].reshape(1, -1), seq, alpha)

        j = layer // 2
        g2, b2 = ln2_g[layer].reshape(1, -1), ln2_b[layer].reshape(1, -1)
        if layer % 2 == 0:
            xf = _ffn(xf, mod, ffn_w_gate[j].astype(BF16), ffn_w_up[j].astype(BF16),
                      ffn_w_down[j].astype(BF16), g2, b2, seq, alpha)
        else:
            wr = jnp.pad(moe_w_router[j], ((0, 0), (0, LANES - N_EXPERTS)))
            br = jnp.pad(moe_b_router[j], (0, LANES - N_EXPERTS)).reshape(1, -1)
            xf = _moe(xf, mod, wr, br, moe_w_gate[j].astype(BF16), moe_w_up[j].astype(BF16),
                      moe_w_down[j].astype(BF16), g2, b2, seq, alpha)
    return xf.reshape(batch, seq, d)
```

```python
import functools

import numpy as np
import jax
import jax.numpy as jnp
from jax import lax
from jax.experimental import pallas as pl
from jax.experimental.pallas import tpu as pltpu

F32 = jnp.float32
BF16 = jnp.bfloat16
I32 = jnp.int32
I16 = jnp.int16

HEAD_DIM = 64
Q_BLK = 128
ROPE_THETA = 10000.0
LN_EPS = 1e-5
RMS_EPS = 1e-6

NSA_HEADS = 6
NSA_KV_HEADS = 2
NSA_GROUP = NSA_HEADS // NSA_KV_HEADS
NSA_WIDTH = NSA_HEADS * HEAD_DIM
CMP_LEN = 32
CMP_STRIDE = 16
SEL_BLOCK = 64
SEL_TOP = 16
WINDOW = 512
FORCE_BONUS = 1e4

DSA_HEADS = 6
DSA_WIDTH = DSA_HEADS * HEAD_DIM
DSA_LATENT = 128
IDX_HEADS = 4
IDX_DIM = 32
IDX_TOPK_MAX = 256

SGU_GROUPS = 4
SGU_CHUNK = 128
SGU_WIDTH = SGU_GROUPS * HEAD_DIM

N_EXPERTS = 8
TOP_K = 2

LANES = 128
VMEM_LIMIT = 56 * 1024 * 1024
NEG = -1e30
INT_MIN = -(2 ** 31)

C_NQ = 0
C_KV = 384
C_DQ = 1152
C_CKV = 1536
C_IQ = 1664
C_SGU = 1792
C_SMALL = 2304
C_MAIN = 2432
C_NQ_SW = 2432
C_KSEL_SW = 2816
C_KWIN_SW = 2944
C_DQ_SW = 3072
C_IQ_SW = 3456
C_SMALL_SW = 3584
C_TOTAL = 3712
SMALL_IW = 32
SMALL_GATE = 36

SEL_KC = 512
Q_SCALE = HEAD_DIM ** -0.5 * 1.4426950408889634
INT16_MIN = -(2 ** 15)


def _cparams(sem):
    return pltpu.CompilerParams(dimension_semantics=sem, vmem_limit_bytes=VMEM_LIMIT)


def _dot(a, b):
    return jnp.dot(a.astype(BF16), b.astype(BF16), preferred_element_type=F32)


def _dot_nt(a, b):
    return lax.dot_general(a.astype(BF16), b.astype(BF16), (((1,), (1,)), ((), ())),
                           preferred_element_type=F32)


def _dot_split(a_exact, b):
    b_hi = b.astype(BF16)
    b_lo = (b - b_hi.astype(F32)).astype(BF16)
    return (jnp.dot(a_exact, b_hi, preferred_element_type=F32)
            + jnp.dot(a_exact, b_lo, preferred_element_type=F32))


def _gelu(x):
    return 0.5 * x * (1.0 + jnp.tanh(0.7978845608028654 * (x + 0.044715 * (x * x * x))))


def _silu(x):
    return x * (1.0 / (1.0 + jnp.exp(-x)))


def _sigmoid(x):
    return 1.0 / (1.0 + jnp.exp(-x))


def _standardize(x):
    mu = jnp.mean(x, axis=-1, keepdims=True)
    xc = x - mu
    var = jnp.mean(xc * xc, axis=-1, keepdims=True)
    return xc * lax.rsqrt(var + LN_EPS)


def _adaln_kernel(c_ref, w_ref, b_ref, o_ref):
    c = c_ref[...]
    o_ref[...] = jnp.dot(_silu(c), w_ref[...], preferred_element_type=F32,
                         precision=lax.Precision.HIGHEST) + b_ref[...]


def _adaln(c, w_ada, b_ada):
    depth, d, n = w_ada.shape
    b = c.shape[0]
    tn = 512
    return pl.pallas_call(
        _adaln_kernel,
        grid=(depth, n // tn),
        in_specs=[pl.BlockSpec((b, d), lambda l, j: (0, 0)),
                  pl.BlockSpec((None, d, tn), lambda l, j: (l, 0, j)),
                  pl.BlockSpec((None, 1, tn), lambda l, j: (l, 0, j))],
        out_specs=pl.BlockSpec((None, b, tn), lambda l, j: (l, 0, j)),
        out_shape=jax.ShapeDtypeStruct((depth, b, n), F32),
        compiler_params=_cparams(("arbitrary", "arbitrary")),
        name="adaln",
    )(c, w_ada, b_ada.reshape(depth, 1, n))


def _inproj_kernel(x_ref, mod_ref, cos_ref, sin_ref, cosi_ref, sini_ref, w_ref, wkv_ref, kvg_ref,
                   nq_ref, nqr_ref, kvcmp_ref, ksel_ref, vselT_ref, kwin_ref, vwinT_ref,
                   dq_ref, dk_ref, dvT_ref, iq_ref, small_ref, sgu_ref):
    tm = x_ref.shape[0]
    shift = mod_ref[0:1, :]
    scale = mod_ref[1:2, :]
    hb = (_standardize(x_ref[...]) * (1.0 + scale) + shift).astype(BF16)

    def proj(c0, n):
        return jnp.dot(hb, w_ref[:, c0:c0 + n], preferred_element_type=F32)

    cos = cos_ref[...]
    sin = sin_ref[...]
    cos3 = jnp.concatenate([cos, cos, cos], axis=1)
    sin3 = jnp.concatenate([sin, sin, sin], axis=1)
    lane = lax.broadcasted_iota(I32, (1, LANES), 1)

    zq = proj(C_NQ, 384)
    nq_ref[...] = (zq * Q_SCALE).astype(BF16)
    nqr_ref[...] = ((zq * cos3 + proj(C_NQ_SW, 384) * sin3) * Q_SCALE).astype(BF16)

    kvcmp_ref[...] = proj(C_KV, 256)
    ksel_ref[...] = (proj(C_KV + 256, 128) * cos + proj(C_KSEL_SW, 128) * sin).astype(BF16)
    ones_t = jnp.ones((HEAD_DIM, tm), F32)
    vsel_t = proj(C_KV + 384, 128).T
    vsel_x = jnp.concatenate([vsel_t[:HEAD_DIM], ones_t, vsel_t[HEAD_DIM:], ones_t], axis=0).astype(BF16)
    for j in range(tm // SEL_KC):
        vselT_ref[j] = vsel_x[:, j * SEL_KC:(j + 1) * SEL_KC]
    kwin_ref[...] = (proj(C_KV + 512, 128) * cos + proj(C_KWIN_SW, 128) * sin).astype(BF16)
    vwin_t = proj(C_KV + 640, 128).T.astype(BF16)
    for j in range(tm // Q_BLK):
        vwinT_ref[j] = vwin_t[:, j * Q_BLK:(j + 1) * Q_BLK]

    dq_ref[...] = ((proj(C_DQ, 384) * cos3 + proj(C_DQ_SW, 384) * sin3) * Q_SCALE).astype(BF16)

    ckv = proj(C_CKV, 128)
    ckv = ckv * lax.rsqrt(jnp.mean(ckv * ckv, axis=-1, keepdims=True) + RMS_EPS) * kvg_ref[...]
    kd = jnp.dot(ckv.astype(BF16), wkv_ref[...], preferred_element_type=F32)
    first = lane < HEAD_DIM
    dkv = kd[:, :128] * jnp.where(first, cos, 1.0) + kd[:, 128:] * jnp.where(first, sin, 0.0)
    dk_ref[...] = dkv.astype(BF16)
    dv_x = jnp.concatenate([dkv.T[HEAD_DIM:], ones_t], axis=0).astype(BF16)
    for j in range(tm // SEL_KC):
        dvT_ref[j] = dv_x[:, j * SEL_KC:(j + 1) * SEL_KC]

    cosi = cosi_ref[...]
    sini = sini_ref[...]
    iq_ref[...] = (proj(C_IQ, 128) * cosi + proj(C_IQ_SW, 128) * sini).astype(BF16)
    isk = lane < IDX_DIM
    small_ref[...] = (proj(C_SMALL, 128) * jnp.where(isk, cosi, 1.0)
                      + proj(C_SMALL_SW, 128) * jnp.where(isk, sini, 0.0))
    sgu_ref[...] = proj(C_SGU, 512)


def _inproj(xf, mod, tabs, w_ext, wkv, kvg, batch, seq):
    t, d = xf.shape
    tm = 512
    per_b = seq // tm
    row = lambda n: pl.BlockSpec((tm, n), lambda i: (i, 0))
    trk = lambda r: pl.BlockSpec((None, tm // SEL_KC, r, SEL_KC), lambda i: (i // per_b, i % per_b, 0, 0))
    tr128 = pl.BlockSpec((None, tm // Q_BLK, 128, Q_BLK), lambda i: (i // per_b, i % per_b, 0, 0))
    full = lambda a: pl.BlockSpec(a.shape, lambda i: (0,) * a.ndim)
    out_shape = (
        jax.ShapeDtypeStruct((t, 384), BF16),
        jax.ShapeDtypeStruct((t, 384), BF16),
        jax.ShapeDtypeStruct((t, 256), F32),
        jax.ShapeDtypeStruct((t, 128), BF16),
        jax.ShapeDtypeStruct((batch, seq // SEL_KC, 256, SEL_KC), BF16),
        jax.ShapeDtypeStruct((t, 128), BF16),
        jax.ShapeDtypeStruct((batch, seq // Q_BLK, 128, Q_BLK), BF16),
        jax.ShapeDtypeStruct((t, 384), BF16),
        jax.ShapeDtypeStruct((t, 128), BF16),
        jax.ShapeDtypeStruct((batch, seq // SEL_KC, 128, SEL_KC), BF16),
        jax.ShapeDtypeStruct((t, 128), BF16),
        jax.ShapeDtypeStruct((t, 128), F32),
        jax.ShapeDtypeStruct((t, 512), F32),
    )
    out_specs = (row(384), row(384), row(256), row(128), trk(256), row(128), tr128,
                 row(384), row(128), trk(128), row(128), row(128), row(512))
    return pl.pallas_call(
        _inproj_kernel,
        grid=(t // tm,),
        in_specs=[row(d),
                  pl.BlockSpec((None, 6, d), lambda i: (i // per_b, 0, 0)),
                  row(128), row(128), row(128), row(128),
                  full(w_ext), full(wkv), full(kvg)],
        out_specs=out_specs,
        out_shape=out_shape,
        compiler_params=_cparams(("parallel",)),
        name="inproj",
    )(xf, mod, *tabs, w_ext, wkv, kvg)


def _compress_kernel(h_ref, plo_ref, phi_ref, wlo_ref, whi_ref, w2_ref, kc_ref, vct_ref):
    h = h_ref[...]
    a = _dot(h + plo_ref[...], wlo_ref[...])
    b = _dot(h + phi_ref[...], whi_ref[...])
    nh = h.shape[0]
    pre = a + pltpu.roll(b, nh - 1, 0)
    cmp = _dot(_gelu(pre), w2_ref[...])
    kc_ref[...] = cmp[:, :128].astype(BF16)
    vct_ref[...] = cmp[:, 128:].T.astype(BF16)


def _compress(kvcmp_h, plo, phi, wlo, whi, w2):
    batch, nh, width = kvcmp_h.shape
    full = lambda a: pl.BlockSpec(a.shape, lambda b: (0,) * a.ndim)
    return pl.pallas_call(
        _compress_kernel,
        grid=(batch,),
        in_specs=[pl.BlockSpec((None, nh, width), lambda b: (b, 0, 0)),
                  full(plo), full(phi), full(wlo), full(whi), full(w2)],
        out_specs=(pl.BlockSpec((None, nh, 128), lambda b: (b, 0, 0)),
                   pl.BlockSpec((None, 128, nh), lambda b: (b, 0, 0))),
        out_shape=(jax.ShapeDtypeStruct((batch, nh, 128), BF16),
                   jax.ShapeDtypeStruct((batch, 128, nh), BF16)),
        compiler_params=_cparams(("parallel",)),
        name="nsa_compress",
    )(kvcmp_h, plo, phi, wlo, whi, w2)


def _softmax_cols(s, mask):
    m = jnp.max(jnp.where(mask, s, NEG), axis=0, keepdims=True)
    p = jnp.where(mask, jnp.exp2(s - m), 0.0)
    return p / jnp.maximum(jnp.sum(p, axis=0, keepdims=True), 1e-30)


def _flash_step(s, on, vt_ones, m_ref, acc_ref):
    m_new, acc_new = _flash_update(s, on, vt_ones, m_ref[...], acc_ref[...])
    acc_ref[...] = acc_new
    m_ref[...] = m_new


def _flash_update(s, on, vt_ones, m_old, acc_old):
    n_grp = s.shape[1] // Q_BLK
    sm = [jnp.where(on, s[:, j * Q_BLK:(j + 1) * Q_BLK], NEG) for j in range(n_grp)]
    m_new = jnp.maximum(m_old, jnp.concatenate([jnp.max(x, axis=0, keepdims=True) for x in sm], axis=1))
    p = jnp.concatenate([jnp.exp2(sm[j] - m_new[:, j * Q_BLK:(j + 1) * Q_BLK]).astype(BF16)
                         for j in range(n_grp)], axis=1)
    acc_new = jnp.exp2(m_old - m_new) * acc_old + jnp.dot(vt_ones, p, preferred_element_type=F32)
    return m_new, acc_new


def _nsa_kernel(nq_ref, nqr_ref, small_ref, kc_ref, vct_ref, ksel_ref, vselT_ref, kwin_ref,
                vwinT_ref, ovt_ref, o_ref, sc_ref, lim_ref, oc_ref, m_ref, acc_ref,
                *, n_blk, n_sel, n_cmp):
    i = pl.program_id(1)
    t0 = i * Q_BLK
    tq = t0 + lax.broadcasted_iota(I32, (1, Q_BLK), 1)
    tq3 = jnp.concatenate([tq, tq, tq], axis=1)
    nq = nq_ref[...]
    nqr = nqr_ref[...]
    nh = kc_ref.shape[0]

    def stack_heads(q, g):
        hs = [g * NSA_GROUP + r for r in range(NSA_GROUP)]
        return jnp.concatenate([q[:, h * HEAD_DIM:(h + 1) * HEAD_DIM] for h in hs], axis=0)

    qrs = [stack_heads(nqr, g) for g in range(NSA_KV_HEADS)]
    for g in range(NSA_KV_HEADS):
        lo, hi = g * HEAD_DIM, (g + 1) * HEAD_DIM
        s_c = _dot_nt(kc_ref[:, lo:hi], stack_heads(nq, g))
        n_io = lax.broadcasted_iota(I32, (nh, 1), 0)
        m_c = (n_io * CMP_STRIDE + (CMP_LEN - 1) <= tq3) & (n_io < n_cmp)
        p_c = _softmax_cols(s_c, m_c)
        oc_ref[g] = _dot(vct_ref[lo:hi, :], p_c)

        p_sum = p_c[:, 0:Q_BLK] + p_c[:, Q_BLK:2 * Q_BLK] + p_c[:, 2 * Q_BLK:3 * Q_BLK]
        imp = _dot_split(ovt_ref[...], p_sum)
        j_io = lax.broadcasted_iota(I32, (n_blk, 1), 0)
        cur = tq >> 6
        valid = j_io <= cur
        forced = (j_io == 0) | (j_io == cur) | (j_io == cur - 1)
        score = jnp.where(valid, imp + jnp.where(forced, FORCE_BONUS, 0.0), -jnp.inf)
        sc_ref[...] = score
        rank = jnp.zeros((n_blk, Q_BLK), F32)
        for b in range(n_blk):
            row = sc_ref[b:b + 1, :]
            beats = (row > score) | ((row == score) & (j_io > b))
            rank = rank + jnp.where(beats, 1.0, 0.0)
        lim_ref[g] = jnp.where((rank < n_sel) & valid, tq, -1)

    m_ref[...] = jnp.full(m_ref.shape, NEG, F32)
    acc_ref[...] = jnp.zeros(acc_ref.shape, F32)
    bpc = SEL_KC // SEL_BLOCK

    def sel_step(c, carry):
        k0 = pl.multiple_of(c * SEL_KC, SEL_KC)
        kch = ksel_ref[pl.ds(k0, SEL_KC), :]
        key = k0 + lax.broadcasted_iota(I32, (SEL_KC, 1), 0)
        for g in range(NSA_KV_HEADS):
            s = _dot_nt(kch[:, g * HEAD_DIM:(g + 1) * HEAD_DIM], qrs[g])
            rows = [jnp.broadcast_to(lim_ref[g, pl.ds(c * bpc + u, 1), :], (SEL_BLOCK, Q_BLK))
                    for u in range(bpc)]
            on = key <= jnp.concatenate(rows, axis=0)
            _flash_step(s, on, vselT_ref[c, 2 * g * HEAD_DIM:2 * (g + 1) * HEAD_DIM, :],
                        m_ref.at[g], acc_ref.at[g])
        return carry

    lax.fori_loop(0, (t0 + Q_BLK + SEL_KC - 1) // SEL_KC, sel_step, 0)

    small_t = small_ref[...].T
    heads = []
    for g in range(NSA_KV_HEADS):
        lo, hi = g * HEAD_DIM, (g + 1) * HEAD_DIM
        acc = acc_ref[g]
        o_s = acc[:HEAD_DIM] / jnp.maximum(acc[HEAD_DIM:HEAD_DIM + 1], 1e-30)
        o_c = oc_ref[g]

        nband = WINDOW // Q_BLK + 1
        cb = jnp.maximum(i - WINDOW // Q_BLK, 0)
        b0 = pl.multiple_of(cb * Q_BLK, Q_BLK)
        s_w = _dot_nt(kwin_ref[pl.ds(b0, nband * Q_BLK), lo:hi], qrs[g])
        key = b0 + lax.broadcasted_iota(I32, (nband * Q_BLK, 1), 0)
        diff = tq3 - key
        p_w = _softmax_cols(s_w, (diff >= 0) & (diff < WINDOW))
        o_w = jnp.zeros((HEAD_DIM, NSA_GROUP * Q_BLK), F32)
        for u in range(nband):
            o_w = o_w + _dot(vwinT_ref[cb + u, lo:hi, :], p_w[u * Q_BLK:(u + 1) * Q_BLK, :])

        for r in range(NSA_GROUP):
            gi = SMALL_GATE + (g * NSA_GROUP + r) * 3
            gt = _sigmoid(small_t[gi:gi + 3, :])
            cs = slice(r * Q_BLK, (r + 1) * Q_BLK)
            heads.append(gt[0:1, :] * o_c[:, cs] + gt[1:2, :] * o_s[:, cs] + gt[2:3, :] * o_w[:, cs])
    o_ref[...] = jnp.concatenate(heads, axis=0).T.astype(BF16)


def _nsa(nq, nqr, small, kc, vct, ksel, vselT, kwin, vwinT, ovt, batch, seq):
    n_q = seq // Q_BLK
    n_blk = seq // SEL_BLOCK
    n_cmp = (seq - CMP_LEN) // CMP_STRIDE + 1
    nh = kc.shape[1]
    blk = lambda n: pl.BlockSpec((Q_BLK, n), lambda b, i: (b * n_q + i, 0))
    perb2 = lambda r, c: pl.BlockSpec((None, r, c), lambda b, i: (b, 0, 0))
    perb_rows = pl.BlockSpec((seq, 128), lambda b, i: (b, 0))
    kern = functools.partial(_nsa_kernel, n_blk=n_blk, n_sel=min(SEL_TOP, n_blk), n_cmp=n_cmp)
    n_col = NSA_GROUP * Q_BLK
    return pl.pallas_call(
        kern,
        grid=(batch, n_q),
        in_specs=[blk(384), blk(384), blk(128),
                  perb2(nh, 128), perb2(128, nh),
                  perb_rows,
                  pl.BlockSpec((None, seq // SEL_KC, 256, SEL_KC), lambda b, i: (b, 0, 0, 0)),
                  perb_rows,
                  pl.BlockSpec((None, seq // Q_BLK, 128, Q_BLK), lambda b, i: (b, 0, 0, 0)),
                  pl.BlockSpec(ovt.shape, lambda b, i: (0, 0))],
        out_specs=blk(384),
        out_shape=jax.ShapeDtypeStruct((batch * seq, 384), BF16),
        scratch_shapes=[pltpu.VMEM((n_blk, Q_BLK), F32),
                        pltpu.VMEM((NSA_KV_HEADS, n_blk, Q_BLK), I32),
                        pltpu.VMEM((NSA_KV_HEADS, HEAD_DIM, n_col), F32),
                        pltpu.VMEM((NSA_KV_HEADS, 1, n_col), F32),
                        pltpu.VMEM((NSA_KV_HEADS, 2 * HEAD_DIM, n_col), F32)],
        compiler_params=_cparams(("parallel", "arbitrary")),
        name="nsa_attn",
    )(nq, nqr, small, kc, vct, ksel, vselT, kwin, vwinT, ovt)


def _dsa_kernel(dq_ref, iq_ref, small_ref, dk_ref, dvT_ref, ltri_ref, o_ref,
                ord_ref, hi_ref, lo_ref, m_ref, acc_ref, *, k_top):
    i = pl.program_id(1)
    t0 = i * Q_BLK
    kc = SEL_KC
    n_ch = (t0 + Q_BLK + kc - 1) // kc
    tq = t0 + lax.broadcasted_iota(I32, (1, Q_BLK), 1)

    small_t = small_ref[pl.ds(pl.multiple_of(t0, Q_BLK), Q_BLK), :].T
    w_rows = [small_t[SMALL_IW + h:SMALL_IW + h + 1, :] * (IDX_HEADS ** -0.5) for h in range(IDX_HEADS)]
    iq = iq_ref[...]
    iqs = jnp.concatenate([iq[:, h * IDX_DIM:(h + 1) * IDX_DIM] for h in range(IDX_HEADS)], axis=0)

    def score_step(c, carry):
        k0 = pl.multiple_of(c * kc, kc)
        ik = small_ref[pl.ds(k0, kc), :][:, 0:IDX_DIM]
        lg = _dot_nt(ik, iqs)
        sc = jnp.zeros((kc, Q_BLK), F32)
        for h in range(IDX_HEADS):
            sc = sc + w_rows[h] * jnp.maximum(lg[:, h * Q_BLK:(h + 1) * Q_BLK], 0.0)
        sc = jnp.where(sc == 0.0, 0.0, sc)
        bits = lax.bitcast_convert_type(sc, I32)
        ordv = bits ^ ((bits >> 31) & 0x7FFFFFFF)
        key = k0 + lax.broadcasted_iota(I32, (kc, 1), 0)
        ordv = jnp.where(key <= tq, ordv, INT_MIN)
        ord_ref[pl.ds(k0, kc), :] = ordv
        hi_ref[pl.ds(k0, kc), :] = (ordv >> 16).astype(I16)
        return carry

    lax.fori_loop(0, n_ch, score_step, 0)

    rows16 = 16

    def count16(ref, hit_fn):
        def body(c, acc):
            v = ref[pl.ds(pl.multiple_of(c * kc, kc), kc), :]
            hits = [jnp.where(hit_fn(v[j * rows16:(j + 1) * rows16, :]), jnp.int16(1), jnp.int16(0))
                    for j in range(kc // rows16)]
            while len(hits) > 1:
                hits = [a + b for a, b in zip(hits[0::2], hits[1::2])]
            return acc + hits[0]
        part = lax.fori_loop(0, n_ch, body, jnp.zeros((rows16, Q_BLK), I16))
        return jnp.sum(part.astype(I32), axis=0, keepdims=True)

    def pack16(row):
        return jnp.broadcast_to(row, (rows16, Q_BLK)).astype(I16)

    def radix16(ref, k_need):
        prefix = jnp.zeros((1, Q_BLK), I32)
        for bit in range(15, -1, -1):
            cand = pack16((prefix | (1 << bit)) + INT16_MIN)
            cnt = count16(ref, lambda v, cand=cand: v >= cand)
            prefix = jnp.where(cnt >= k_need, prefix | (1 << bit), prefix)
        return prefix

    thr_hi = radix16(hi_ref, k_top) + INT16_MIN
    thr_hi16 = pack16(thr_hi)
    above = count16(hi_ref, lambda v: v > thr_hi16)

    def low_step(c, carry):
        k0 = pl.multiple_of(c * kc, kc)
        o = ord_ref[pl.ds(k0, kc), :]
        low = (o & 0xFFFF) + INT16_MIN
        lo_ref[pl.ds(k0, kc), :] = jnp.where((o >> 16) == thr_hi, low, INT16_MIN).astype(I16)
        return carry

    lax.fori_loop(0, n_ch, low_step, 0)
    thr = (thr_hi << 16) | radix16(lo_ref, k_top - above)

    def count_gt_ge(c, acc):
        k0 = pl.multiple_of(c * kc, kc)
        o = ord_ref[pl.ds(k0, kc), :]
        gt = jnp.where(o > thr, 1, 0).astype(I32).reshape(kc // 8, 8, Q_BLK)
        ge = jnp.where(o >= thr, 1, 0).astype(I32).reshape(kc // 8, 8, Q_BLK)
        return acc[0] + jnp.sum(gt, axis=0), acc[1] + jnp.sum(ge, axis=0)

    zero8 = jnp.zeros((8, Q_BLK), I32)
    gt8, ge8 = lax.fori_loop(0, n_ch, count_gt_ge, (zero8, zero8))
    n_gt = jnp.sum(gt8, axis=0, keepdims=True)
    n_eq = jnp.sum(ge8, axis=0, keepdims=True) - n_gt
    short = thr == INT_MIN
    need = jnp.where(short, 0, k_top - n_gt)
    thr_all = jnp.where(short, INT_MIN + 1, thr)
    no_cut = jnp.min(jnp.where(short | (n_eq == need), 1.0, 0.0)) > 0.5

    q = dq_ref[...]
    qs = jnp.concatenate([q[:, h * HEAD_DIM:(h + 1) * HEAD_DIM] for h in range(DSA_HEADS)], axis=0)
    m_ref[...] = jnp.full(m_ref.shape, NEG, F32)
    acc_ref[...] = jnp.zeros(acc_ref.shape, F32)

    def sweep(mask_fn):
        def attn_step(c, seen):
            k0 = pl.multiple_of(c * kc, kc)
            on, seen = mask_fn(ord_ref[pl.ds(k0, kc), :], seen)
            kch = dk_ref[pl.ds(k0, kc), 0:HEAD_DIM]
            vt = dvT_ref[c]
            m_old = m_ref[...]
            acc_old = acc_ref[...]
            pairs = [slice(2 * j * Q_BLK, 2 * (j + 1) * Q_BLK) for j in range(DSA_HEADS // 2)]
            ss = [_dot_nt(kch, qs[cols]) for cols in pairs]
            outs = [_flash_update(s, on, vt, m_old[:, cols], acc_old[:, cols]) for s, cols in zip(ss, pairs)]
            m_ref[...] = jnp.concatenate([o[0] for o in outs], axis=1)
            acc_ref[...] = jnp.concatenate([o[1] for o in outs], axis=1)
            return seen
        lax.fori_loop(0, n_ch, attn_step, jnp.zeros((1, Q_BLK), F32))

    @pl.when(no_cut)
    def _():
        sweep(lambda o, seen: (o >= thr_all, seen))

    @pl.when(jnp.logical_not(no_cut))
    def _():
        need_f = need.astype(F32)

        def cut_mask(o, seen):
            eq = jnp.where(o == thr, 1.0, 0.0)
            before = jnp.dot(ltri_ref[...], eq.astype(BF16), preferred_element_type=F32) + seen
            on = (o > thr) | ((eq > 0.5) & (before < need_f))
            return on, seen + jnp.sum(eq, axis=0, keepdims=True)

        sweep(cut_mask)

    acc = acc_ref[...]
    o_t = acc[:HEAD_DIM] / jnp.maximum(acc[HEAD_DIM:HEAD_DIM + 1], 1e-30)
    heads = [o_t[:, h * Q_BLK:(h + 1) * Q_BLK] for h in range(DSA_HEADS)]
    o_ref[...] = jnp.concatenate(heads, axis=0).T.astype(BF16)


def _dsa(dq, iq, small, dk, dvT, ltri, batch, seq):
    n_q = seq // Q_BLK
    blk = lambda n: pl.BlockSpec((Q_BLK, n), lambda b, i: (b * n_q + i, 0))
    perb_rows = pl.BlockSpec((seq, 128), lambda b, i: (b, 0))
    kern = functools.partial(_dsa_kernel, k_top=min(IDX_TOPK_MAX, seq // 4))
    return pl.pallas_call(
        kern,
        grid=(batch, n_q),
        in_specs=[blk(384), blk(128), perb_rows, perb_rows,
                  pl.BlockSpec((None, seq // SEL_KC, 128, SEL_KC), lambda b, i: (b, 0, 0, 0)),
                  pl.BlockSpec(ltri.shape, lambda b, i: (0, 0))],
        out_specs=blk(384),
        out_shape=jax.ShapeDtypeStruct((batch * seq, 384), BF16),
        scratch_shapes=[pltpu.VMEM((seq, Q_BLK), I32),
                        pltpu.VMEM((seq, Q_BLK), I16), pltpu.VMEM((seq, Q_BLK), I16),
                        pltpu.VMEM((1, DSA_HEADS * Q_BLK), F32),
                        pltpu.VMEM((2 * HEAD_DIM, DSA_HEADS * Q_BLK), F32)],
        compiler_params=_cparams(("parallel", "arbitrary")),
        name="dsa_attn",
    )(dq, iq, small, dk, dvT, ltri)


def _sgu_kernel(z_ref, g_ref, b_ref, w_ref, bs_ref, o_ref):
    z = _gelu(z_ref[...])
    row = lax.broadcasted_iota(I32, (SGU_CHUNK, SGU_CHUNK), 0)
    col = lax.broadcasted_iota(I32, (SGU_CHUNK, SGU_CHUNK), 1)
    outs = []
    for g in range(SGU_GROUPS):
        lo, hi = g * HEAD_DIM, (g + 1) * HEAD_DIM
        u = z[:, lo:hi]
        v = _standardize(z[:, SGU_WIDTH + lo:SGU_WIDTH + hi]) * g_ref[:, lo:hi] + b_ref[:, lo:hi]
        w = jnp.where(row >= col, w_ref[g], 0.0)
        outs.append(u * (_dot(w, v) + bs_ref[:, lo:hi]))
    o_ref[...] = jnp.concatenate(outs, axis=1).astype(BF16)


def _sgu(z, g, b, w, bs):
    t = z.shape[0]
    n_c = bs.shape[0] // SGU_CHUNK
    full = lambda a: pl.BlockSpec(a.shape, lambda i: (0,) * a.ndim)
    return pl.pallas_call(
        _sgu_kernel,
        grid=(t // SGU_CHUNK,),
        in_specs=[pl.BlockSpec((SGU_CHUNK, 2 * SGU_WIDTH), lambda i: (i, 0)),
                  full(g), full(b), full(w), full(bs)],
        out_specs=pl.BlockSpec((SGU_CHUNK, SGU_WIDTH), lambda i: (i, 0)),
        out_shape=jax.ShapeDtypeStruct((t, SGU_WIDTH), BF16),
        compiler_params=_cparams(("parallel",)),
        name="sgu",
    )(z, g, b, w, bs)


def _outproj_kernel(oa_ref, ob_ref, oc_ref, x_ref, mod_ref, w_ref, g_ref, b_ref, o_ref, *, alpha):
    mix = (jnp.dot(oa_ref[...], w_ref[0:384, :], preferred_element_type=F32)
           + jnp.dot(ob_ref[...], w_ref[384:768, :], preferred_element_type=F32)
           + jnp.dot(oc_ref[...], w_ref[768:1024, :], preferred_element_type=F32))
    y = alpha * x_ref[...] + mod_ref[2:3, :] * mix
    o_ref[...] = _standardize(y) * g_ref[...] + b_ref[...]


def _outproj(oa, ob, oc, xf, mod, w, g, b, seq, alpha):
    t, d = xf.shape
    tm = 512
    per_b = seq // tm
    row = lambda n: pl.BlockSpec((tm, n), lambda i: (i, 0))
    full = lambda a: pl.BlockSpec(a.shape, lambda i: (0,) * a.ndim)
    return pl.pallas_call(
        functools.partial(_outproj_kernel, alpha=alpha),
        grid=(t // tm,),
        in_specs=[row(384), row(384), row(256), row(d),
                  pl.BlockSpec((None, 6, d), lambda i: (i // per_b, 0, 0)),
                  full(w), full(g), full(b)],
        out_specs=row(d),
        out_shape=jax.ShapeDtypeStruct((t, d), F32),
        compiler_params=_cparams(("parallel",)),
        name="outproj",
    )(oa, ob, oc, xf, mod, w, g, b)


def _ffn_kernel(x_ref, mod_ref, wg_ref, wu_ref, wd_ref, g_ref, b_ref, o_ref, acc_ref, *, alpha, fc):
    x = x_ref[...]
    hb = (_standardize(x) * (1.0 + mod_ref[4:5, :]) + mod_ref[3:4, :]).astype(BF16)
    d_ff = wg_ref.shape[1]
    for j in range(d_ff // fc):
        cs = slice(j * fc, (j + 1) * fc)
        a = (_silu(jnp.dot(hb, wg_ref[:, cs], preferred_element_type=F32))
             * jnp.dot(hb, wu_ref[:, cs], preferred_element_type=F32)).astype(BF16)
        part = jnp.dot(a, wd_ref[cs, :], preferred_element_type=F32)
        if j == 0:
            acc_ref[...] = part
        else:
            acc_ref[...] += part
    y = alpha * x + mod_ref[5:6, :] * acc_ref[...]
    o_ref[...] = _standardize(y) * g_ref[...] + b_ref[...]


def _ffn(xf, mod, wg, wu, wd, g, b, seq, alpha):
    t, d = xf.shape
    tm = 512
    per_b = seq // tm
    row = pl.BlockSpec((tm, d), lambda i: (i, 0))
    once = lambda a: pl.BlockSpec(a.shape, lambda i: (0,) * a.ndim, pipeline_mode=pl.Buffered(1))
    return pl.pallas_call(
        functools.partial(_ffn_kernel, alpha=alpha, fc=256),
        grid=(t // tm,),
        in_specs=[row, pl.BlockSpec((None, 6, d), lambda i: (i // per_b, 0, 0)),
                  once(wg), once(wu), once(wd), once(g), once(b)],
        out_specs=row,
        out_shape=jax.ShapeDtypeStruct((t, d), F32),
        scratch_shapes=[pltpu.VMEM((tm, d), F32)],
        compiler_params=_cparams(("parallel",)),
        name="ffn",
    )(xf, mod, wg, wu, wd, g, b)


def _moe_kernel(x_ref, mod_ref, wr_ref, br_ref, wg_ref, wu_ref, wd_ref, g_ref, b_ref, o_ref,
                h_ref, gate_ref, acc_ref, *, alpha):
    e = pl.program_id(1)
    f = pl.program_id(2)
    tm = x_ref.shape[0]

    @pl.when((e == 0) & (f == 0))
    def _():
        h = _standardize(x_ref[...]) * (1.0 + mod_ref[4:5, :]) + mod_ref[3:4, :]
        h_ref[...] = h.astype(BF16)
        lane = lax.broadcasted_iota(I32, (1, LANES), 1)
        logits = jnp.dot(h, wr_ref[...], preferred_element_type=F32,
                         precision=lax.Precision.HIGHEST) + br_ref[...]
        lg = jnp.where(lane < N_EXPERTS, logits, -jnp.inf)
        v0 = jnp.max(lg, axis=-1, keepdims=True)
        lane_f = lane.astype(F32)
        i0 = jnp.min(jnp.where(lg == v0, lane_f, float(LANES)), axis=-1, keepdims=True)
        lg1 = jnp.where(lane_f == i0, -jnp.inf, lg)
        v1 = jnp.max(lg1, axis=-1, keepdims=True)
        i1 = jnp.min(jnp.where(lg1 == v1, lane_f, float(LANES)), axis=-1, keepdims=True)
        e1 = jnp.exp(v1 - v0)
        den = 1.0 + e1
        gates = jnp.where(lane_f == i0, 1.0 / den, 0.0) + jnp.where(lane_f == i1, e1 / den, 0.0)
        for k in range(N_EXPERTS):
            col = jnp.sum(jnp.where(lane == k, gates, 0.0), axis=-1, keepdims=True)
            gate_ref[k] = jnp.broadcast_to(col, (tm, LANES))
        acc_ref[...] = jnp.zeros(acc_ref.shape, F32)

    hb = h_ref[...]
    a = (_silu(jnp.dot(hb, wg_ref[...], preferred_element_type=F32))
         * jnp.dot(hb, wu_ref[...], preferred_element_type=F32))
    gb = gate_ref[e]
    fc = a.shape[1]
    a = (a * jnp.concatenate([gb] * (fc // LANES), axis=1)).astype(BF16)
    acc_ref[...] += jnp.dot(a, wd_ref[...], preferred_element_type=F32)

    @pl.when((e == pl.num_programs(1) - 1) & (f == pl.num_programs(2) - 1))
    def _():
        y = alpha * x_ref[...] + mod_ref[5:6, :] * acc_ref[...]
        o_ref[...] = _standardize(y) * g_ref[...] + b_ref[...]


def _moe(xf, mod, wr, br, wg, wu, wd, g, b, seq, alpha):
    t, d = xf.shape
    n_e, _, d_ff = wg.shape
    tm = 1024
    fc = 512
    per_b = seq // tm
    row = pl.BlockSpec((tm, d), lambda i, e, f: (i, 0))
    full = lambda a: pl.BlockSpec(a.shape, lambda i, e, f: (0,) * a.ndim)
    return pl.pallas_call(
        functools.partial(_moe_kernel, alpha=alpha),
        grid=(t // tm, n_e, d_ff // fc),
        in_specs=[row, pl.BlockSpec((None, 6, d), lambda i, e, f: (i // per_b, 0, 0)),
                  full(wr), full(br),
                  pl.BlockSpec((None, d, fc), lambda i, e, f: (e, 0, f)),
                  pl.BlockSpec((None, d, fc), lambda i, e, f: (e, 0, f)),
                  pl.BlockSpec((None, fc, d), lambda i, e, f: (e, f, 0)),
                  full(g), full(b)],
        out_specs=row,
        out_shape=jax.ShapeDtypeStruct((t, d), F32),
        scratch_shapes=[pltpu.VMEM((tm, d), BF16), pltpu.VMEM((n_e, tm, LANES), F32),
                        pltpu.VMEM((tm, d), F32)],
        compiler_params=_cparams(("parallel", "arbitrary", "arbitrary")),
        name="moe",
    )(xf, mod, wr, br, wg, wu, wd, g, b)


def _inproj_columns():
    sizes = (NSA_WIDTH, 6 * NSA_KV_HEADS * HEAD_DIM, 3 * NSA_HEADS, DSA_WIDTH, DSA_LATENT,
             IDX_HEADS * IDX_DIM, IDX_DIM, IDX_HEADS, 2 * SGU_WIDTH)
    starts = np.concatenate([[0], np.cumsum(sizes)])
    o_nq, o_kv, o_g, o_dq, o_ckv, o_iq, o_ik, o_iw, o_sgu = starts[:-1]
    perm = np.full((C_TOTAL,), -1, np.int64)

    def swap(base, n, dim):
        idx = np.arange(n)
        return base + (idx // dim) * dim + (idx % dim + dim // 2) % dim

    perm[C_NQ:C_NQ + 384] = o_nq + np.arange(384)
    perm[C_KV:C_KV + 768] = o_kv + np.arange(768)
    perm[C_DQ:C_DQ + 384] = o_dq + np.arange(384)
    perm[C_CKV:C_CKV + 128] = o_ckv + np.arange(128)
    perm[C_IQ:C_IQ + 128] = o_iq + np.arange(128)
    perm[C_SGU:C_SGU + 512] = o_sgu + np.arange(512)
    perm[C_SMALL:C_SMALL + IDX_DIM] = o_ik + np.arange(IDX_DIM)
    perm[C_SMALL + SMALL_IW:C_SMALL + SMALL_IW + IDX_HEADS] = o_iw + np.arange(IDX_HEADS)
    perm[C_SMALL + SMALL_GATE:C_SMALL + SMALL_GATE + 3 * NSA_HEADS] = o_g + np.arange(3 * NSA_HEADS)
    perm[C_NQ_SW:C_NQ_SW + 384] = swap(o_nq, 384, HEAD_DIM)
    perm[C_KSEL_SW:C_KSEL_SW + 128] = swap(o_kv + 256, 128, HEAD_DIM)
    perm[C_KWIN_SW:C_KWIN_SW + 128] = swap(o_kv + 512, 128, HEAD_DIM)
    perm[C_DQ_SW:C_DQ_SW + 384] = swap(o_dq, 384, HEAD_DIM)
    perm[C_IQ_SW:C_IQ_SW + 128] = swap(o_iq, 128, IDX_DIM)
    perm[C_SMALL_SW:C_SMALL_SW + IDX_DIM] = swap(o_ik, IDX_DIM, IDX_DIM)
    return perm, int(starts[-1])


def _swap_halves(w):
    half = w.shape[-1] // 2
    return jnp.concatenate([w[..., half:], w[..., :half]], axis=-1)


def _compress_weights(pos, w1, w2):
    half = CMP_LEN // 2
    eye_g = jnp.eye(NSA_KV_HEADS, dtype=F32)
    eye_j = jnp.eye(2, dtype=F32)

    def big(w1_half):
        w = jnp.einsum('jlde,jk,gh->ljgdkhe', w1_half, eye_j, eye_g)
        return w.reshape(half * 2 * NSA_KV_HEADS * HEAD_DIM, 2 * NSA_KV_HEADS * HEAD_DIM)

    def posrow(p_half):
        p = jnp.broadcast_to(p_half.transpose(1, 0, 2)[:, :, None, :], (half, 2, NSA_KV_HEADS, HEAD_DIM))
        return p.reshape(1, -1)

    w2big = jnp.einsum('jef,jk,gh->jgekhf', w2, eye_j, eye_g).reshape(256, 256)
    return (posrow(pos[:, :half]), posrow(pos[:, half:]),
            big(w1[:, :half]).astype(BF16), big(w1[:, half:]).astype(BF16), w2big.astype(BF16))


def _rope_tables(positions):
    def tab(dim, reps):
        inv = ROPE_THETA ** (-jnp.arange(0, dim, 2, dtype=F32) / dim)
        ang = positions.astype(F32)[..., None] * inv
        cos, sin = jnp.cos(ang), jnp.sin(ang)
        cos = jnp.tile(jnp.concatenate([cos, cos], axis=-1), (1, 1, reps))
        sin = jnp.tile(jnp.concatenate([-sin, sin], axis=-1), (1, 1, reps))
        return cos.reshape(-1, LANES), sin.reshape(-1, LANES)
    cos_h, sin_h = tab(HEAD_DIM, LANES // HEAD_DIM)
    cos_i, sin_i = tab(IDX_DIM, LANES // IDX_DIM)
    return cos_h, sin_h, cos_i, sin_i


def kernel(x, c, positions, w_ada, b_ada, w_in, nsa_cmp_pos, nsa_cmp_w1, nsa_cmp_w2, dsa_kv_norm, dsa_w_uk, dsa_w_uv, sgu_norm_g, sgu_norm_b, sgu_w, sgu_b, w_out, ln1_g, ln1_b, ln2_g, ln2_b, ffn_w_gate, ffn_w_up, ffn_w_down, moe_w_router, moe_b_router, moe_w_gate, moe_w_up, moe_w_down):
    batch, seq, d = x.shape
    depth = w_ada.shape[0]
    t = batch * seq
    alpha = (2 * depth) ** 0.25
    assert seq % 512 == 0 and seq >= WINDOW + Q_BLK

    tabs = _rope_tables(positions)
    mod_all = _adaln(c, w_ada, b_ada).reshape(depth, batch, 6, d)
    perm, in_width = _inproj_columns()
    perm = jnp.asarray(np.where(perm < 0, in_width, perm), I32)

    n_blk = seq // SEL_BLOCK
    n_half = seq // CMP_STRIDE
    cmp_start = np.arange(n_half)[None, :] * CMP_STRIDE
    blk_start = np.arange(n_blk)[:, None] * SEL_BLOCK
    ovt = jnp.asarray((cmp_start < blk_start + SEL_BLOCK) & (cmp_start + CMP_LEN > blk_start), BF16)
    ltri = jnp.asarray(np.tril(np.ones((SEL_KC, SEL_KC), np.float32), -1), BF16)

    xf = x.reshape(t, d)
    for layer in range(depth):
        mod = mod_all[layer]
        w_pad = jnp.concatenate([w_in[layer], jnp.zeros((d, 1), F32)], axis=1)
        w_ext = jnp.take(w_pad, perm, axis=1).astype(BF16)
        wkv = jnp.concatenate([dsa_w_uk[layer], dsa_w_uv[layer], _swap_halves(dsa_w_uk[layer]),
                               jnp.zeros((DSA_LATENT, HEAD_DIM), F32)], axis=1).astype(BF16)
        (nq, nqr, kvcmp, ksel, vselT, kwin, vwinT, dq, dk, dvT, iq, small, sgu_z) = _inproj(
            xf, mod, tabs, w_ext, wkv, dsa_kv_norm[layer].reshape(1, -1), batch, seq)

        plo, phi, wlo, whi, w2big = _compress_weights(nsa_cmp_pos[layer], nsa_cmp_w1[layer], nsa_cmp_w2[layer])
        kc, vct = _compress(kvcmp.reshape(batch, n_half, CMP_STRIDE * 256), plo, phi, wlo, whi, w2big)
        o_a = _nsa(nq, nqr, small, kc, vct, ksel, vselT, kwin, vwinT, ovt, batch, seq)
        o_b = _dsa(dq, iq, small, dk, dvT, ltri, batch, seq)
        bs = jnp.repeat(sgu_b[layer].T, HEAD_DIM, axis=1)
        o_c = _sgu(sgu_z, sgu_norm_g[layer].reshape(1, -1), sgu_norm_b[layer].reshape(1, -1),
                   sgu_w[layer], bs)
        xf = _outproj(o_a, o_b, o_c, xf, mod, w_out[layer].astype(BF16),
                      ln1_g[layer].reshape(1, -1), ln1_b[layer].reshape(1, -1), seq, alpha)

        j = layer // 2
        g2, b2 = ln2_g[layer].reshape(1, -1), ln2_b[layer].reshape(1, -1)
        if layer % 2 == 0:
            xf = _ffn(xf, mod, ffn_w_gate[j].astype(BF16), ffn_w_up[j].astype(BF16),
                      ffn_w_down[j].astype(BF16), g2, b2, seq, alpha)
        else:
            wr = jnp.pad(moe_w_router[j], ((0, 0), (0, LANES - N_EXPERTS)))
            br = jnp.pad(moe_b_router[j], (0, LANES - N_EXPERTS)).reshape(1, -1)
            xf = _moe(xf, mod, wr, br, moe_w_gate[j].astype(BF16), moe_w_up[j].astype(BF16),
                      moe_w_down[j].astype(BF16), g2, b2, seq, alpha)
    return xf.reshape(batch, seq, d)
```

```python
import functools

import numpy as np
import jax
import jax.numpy as jnp
from jax import lax
from jax.experimental import pallas as pl
from jax.experimental.pallas import tpu as pltpu

F32 = jnp.float32
BF16 = jnp.bfloat16
I32 = jnp.int32
I16 = jnp.int16

HEAD_DIM = 64
Q_BLK = 128
ROPE_THETA = 10000.0
LN_EPS = 1e-5
RMS_EPS = 1e-6

NSA_HEADS = 6
NSA_KV_HEADS = 2
NSA_GROUP = NSA_HEADS // NSA_KV_HEADS
NSA_WIDTH = NSA_HEADS * HEAD_DIM
CMP_LEN = 32
CMP_STRIDE = 16
SEL_BLOCK = 64
SEL_TOP = 16
WINDOW = 512
FORCE_BONUS = 1e4

DSA_HEADS = 6
DSA_WIDTH = DSA_HEADS * HEAD_DIM
DSA_LATENT = 128
IDX_HEADS = 4
IDX_DIM = 32
IDX_TOPK_MAX = 256

SGU_GROUPS = 4
SGU_CHUNK = 128
SGU_WIDTH = SGU_GROUPS * HEAD_DIM

N_EXPERTS = 8
TOP_K = 2

LANES = 128
VMEM_LIMIT = 56 * 1024 * 1024
NEG = -1e30
INT_MIN = -(2 ** 31)

C_NQ = 0
C_KV = 384
C_DQ = 1152
C_CKV = 1536
C_IQ = 1664
C_SGU = 1792
C_SMALL = 2304
C_MAIN = 2432
C_NQ_SW = 2432
C_KSEL_SW = 2816
C_KWIN_SW = 2944
C_DQ_SW = 3072
C_IQ_SW = 3456
C_SMALL_SW = 3584
C_TOTAL = 3712
SMALL_IW = 32
SMALL_GATE = 36

SEL_KC = 512
Q_SCALE = HEAD_DIM ** -0.5 * 1.4426950408889634
INT16_MIN = -(2 ** 15)


def _cparams(sem):
    return pltpu.CompilerParams(dimension_semantics=sem, vmem_limit_bytes=VMEM_LIMIT)


def _dot(a, b):
    return jnp.dot(a.astype(BF16), b.astype(BF16), preferred_element_type=F32)


def _dot_nt(a, b):
    return lax.dot_general(a.astype(BF16), b.astype(BF16), (((1,), (1,)), ((), ())),
                           preferred_element_type=F32)


def _dot_split(a_exact, b):
    b_hi = b.astype(BF16)
    b_lo = (b - b_hi.astype(F32)).astype(BF16)
    return (jnp.dot(a_exact, b_hi, preferred_element_type=F32)
            + jnp.dot(a_exact, b_lo, preferred_element_type=F32))


def _gelu(x):
    return 0.5 * x * (1.0 + jnp.tanh(0.7978845608028654 * (x + 0.044715 * (x * x * x))))


def _silu(x):
    return x * (1.0 / (1.0 + jnp.exp(-x)))


def _sigmoid(x):
    return 1.0 / (1.0 + jnp.exp(-x))


def _standardize(x):
    mu = jnp.mean(x, axis=-1, keepdims=True)
    xc = x - mu
    var = jnp.mean(xc * xc, axis=-1, keepdims=True)
    return xc * lax.rsqrt(var + LN_EPS)


def _adaln_kernel(c_ref, w_ref, b_ref, o_ref):
    c = c_ref[...]
    o_ref[...] = jnp.dot(_silu(c), w_ref[...], preferred_element_type=F32,
                         precision=lax.Precision.HIGHEST) + b_ref[...]


def _adaln(c, w_ada, b_ada):
    depth, d, n = w_ada.shape
    b = c.shape[0]
    tn = 512
    return pl.pallas_call(
        _adaln_kernel,
        grid=(depth, n // tn),
        in_specs=[pl.BlockSpec((b, d), lambda l, j: (0, 0)),
                  pl.BlockSpec((None, d, tn), lambda l, j: (l, 0, j)),
                  pl.BlockSpec((None, 1, tn), lambda l, j: (l, 0, j))],
        out_specs=pl.BlockSpec((None, b, tn), lambda l, j: (l, 0, j)),
        out_shape=jax.ShapeDtypeStruct((depth, b, n), F32),
        compiler_params=_cparams(("arbitrary", "arbitrary")),
        name="adaln",
    )(c, w_ada, b_ada.reshape(depth, 1, n))


def _inproj_kernel(x_ref, mod_ref, cos_ref, sin_ref, cosi_ref, sini_ref, w_ref, wkv_ref, kvg_ref,
                   nq_ref, nqr_ref, kvcmp_ref, ksel_ref, vselT_ref, kwin_ref, vwinT_ref,
                   dq_ref, dk_ref, dvT_ref, iq_ref, small_ref, sgu_ref):
    tm = x_ref.shape[0]
    shift = mod_ref[0:1, :]
    scale = mod_ref[1:2, :]
    hb = (_standardize(x_ref[...]) * (1.0 + scale) + shift).astype(BF16)

    def proj(c0, n):
        return jnp.dot(hb, w_ref[:, c0:c0 + n], preferred_element_type=F32)

    cos = cos_ref[...]
    sin = sin_ref[...]
    cos3 = jnp.concatenate([cos, cos, cos], axis=1)
    sin3 = jnp.concatenate([sin, sin, sin], axis=1)
    lane = lax.broadcasted_iota(I32, (1, LANES), 1)

    zq = proj(C_NQ, 384)
    nq_ref[...] = (zq * Q_SCALE).astype(BF16)
    nqr_ref[...] = ((zq * cos3 + proj(C_NQ_SW, 384) * sin3) * Q_SCALE).astype(BF16)

    kvcmp_ref[...] = proj(C_KV, 256)
    ksel_ref[...] = (proj(C_KV + 256, 128) * cos + proj(C_KSEL_SW, 128) * sin).astype(BF16)
    ones_t = jnp.ones((HEAD_DIM, tm), F32)
    vsel_t = proj(C_KV + 384, 128).T
    vsel_x = jnp.concatenate([vsel_t[:HEAD_DIM], ones_t, vsel_t[HEAD_DIM:], ones_t], axis=0).astype(BF16)
    for j in range(tm // SEL_KC):
        vselT_ref[j] = vsel_x[:, j * SEL_KC:(j + 1) * SEL_KC]
    kwin_ref[...] = (proj(C_KV + 512, 128) * cos + proj(C_KWIN_SW, 128) * sin).astype(BF16)
    vwin_t = proj(C_KV + 640, 128).T.astype(BF16)
    for j in range(tm // Q_BLK):
        vwinT_ref[j] = vwin_t[:, j * Q_BLK:(j + 1) * Q_BLK]

    dq_ref[...] = ((proj(C_DQ, 384) * cos3 + proj(C_DQ_SW, 384) * sin3) * Q_SCALE).astype(BF16)

    ckv = proj(C_CKV, 128)
    ckv = ckv * lax.rsqrt(jnp.mean(ckv * ckv, axis=-1, keepdims=True) + RMS_EPS) * kvg_ref[...]
    kd = jnp.dot(ckv.astype(BF16), wkv_ref[...], preferred_element_type=F32)
    first = lane < HEAD_DIM
    dkv = kd[:, :128] * jnp.where(first, cos, 1.0) + kd[:, 128:] * jnp.where(first, sin, 0.0)
    dk_ref[...] = dkv.astype(BF16)
    dv_x = jnp.concatenate([dkv.T[HEAD_DIM:], ones_t], axis=0).astype(BF16)
    for j in range(tm // SEL_KC):
        dvT_ref[j] = dv_x[:, j * SEL_KC:(j + 1) * SEL_KC]

    cosi = cosi_ref[...]
    sini = sini_ref[...]
    iq_ref[...] = (proj(C_IQ, 128) * cosi + proj(C_IQ_SW, 128) * sini).astype(BF16)
    isk = lane < IDX_DIM
    small_ref[...] = (proj(C_SMALL, 128) * jnp.where(isk, cosi, 1.0)
                      + proj(C_SMALL_SW, 128) * jnp.where(isk, sini, 0.0))
    sgu_ref[...] = proj(C_SGU, 512)


def _inproj(xf, mod, tabs, w_ext, wkv, kvg, batch, seq):
    t, d = xf.shape
    tm = 512
    per_b = seq // tm
    row = lambda n: pl.BlockSpec((tm, n), lambda i: (i, 0))
    trk = lambda r: pl.BlockSpec((None, tm // SEL_KC, r, SEL_KC), lambda i: (i // per_b, i % per_b, 0, 0))
    tr128 = pl.BlockSpec((None, tm // Q_BLK, 128, Q_BLK), lambda i: (i // per_b, i % per_b, 0, 0))
    full = lambda a: pl.BlockSpec(a.shape, lambda i: (0,) * a.ndim)
    out_shape = (
        jax.ShapeDtypeStruct((t, 384), BF16),
        jax.ShapeDtypeStruct((t, 384), BF16),
        jax.ShapeDtypeStruct((t, 256), F32),
        jax.ShapeDtypeStruct((t, 128), BF16),
        jax.ShapeDtypeStruct((batch, seq // SEL_KC, 256, SEL_KC), BF16),
        jax.ShapeDtypeStruct((t, 128), BF16),
        jax.ShapeDtypeStruct((batch, seq // Q_BLK, 128, Q_BLK), BF16),
        jax.ShapeDtypeStruct((t, 384), BF16),
        jax.ShapeDtypeStruct((t, 128), BF16),
        jax.ShapeDtypeStruct((batch, seq // SEL_KC, 128, SEL_KC), BF16),
        jax.ShapeDtypeStruct((t, 128), BF16),
        jax.ShapeDtypeStruct((t, 128), F32),
        jax.ShapeDtypeStruct((t, 512), F32),
    )
    out_specs = (row(384), row(384), row(256), row(128), trk(256), row(128), tr128,
                 row(384), row(128), trk(128), row(128), row(128), row(512))
    return pl.pallas_call(
        _inproj_kernel,
        grid=(t // tm,),
        in_specs=[row(d),
                  pl.BlockSpec((None, 6, d), lambda i: (i // per_b, 0, 0)),
                  row(128), row(128), row(128), row(128),
                  full(w_ext), full(wkv), full(kvg)],
        out_specs=out_specs,
        out_shape=out_shape,
        compiler_params=_cparams(("parallel",)),
        name="inproj",
    )(xf, mod, *tabs, w_ext, wkv, kvg)


def _compress_kernel(h_ref, plo_ref, phi_ref, wlo_ref, whi_ref, w2_ref, kc_ref, vct_ref):
    h = h_ref[...]
    a = _dot(h + plo_ref[...], wlo_ref[...])
    b = _dot(h + phi_ref[...], whi_ref[...])
    nh = h.shape[0]
    pre = a + pltpu.roll(b, nh - 1, 0)
    cmp = _dot(_gelu(pre), w2_ref[...])
    kc_ref[...] = cmp[:, :128].astype(BF16)
    vct_ref[...] = cmp[:, 128:].T.astype(BF16)


def _compress(kvcmp_h, plo, phi, wlo, whi, w2):
    batch, nh, width = kvcmp_h.shape
    full = lambda a: pl.BlockSpec(a.shape, lambda b: (0,) * a.ndim)
    return pl.pallas_call(
        _compress_kernel,
        grid=(batch,),
        in_specs=[pl.BlockSpec((None, nh, width), lambda b: (b, 0, 0)),
                  full(plo), full(phi), full(wlo), full(whi), full(w2)],
        out_specs=(pl.BlockSpec((None, nh, 128), lambda b: (b, 0, 0)),
                   pl.BlockSpec((None, 128, nh), lambda b: (b, 0, 0))),
        out_shape=(jax.ShapeDtypeStruct((batch, nh, 128), BF16),
                   jax.ShapeDtypeStruct((batch, 128, nh), BF16)),
        compiler_params=_cparams(("parallel",)),
        name="nsa_compress",
    )(kvcmp_h, plo, phi, wlo, whi, w2)


def _softmax_cols(s, mask):
    m = jnp.max(jnp.where(mask, s, NEG), axis=0, keepdims=True)
    p = jnp.where(mask, jnp.exp2(s - m), 0.0)
    return p / jnp.maximum(jnp.sum(p, axis=0, keepdims=True), 1e-30)


def _flash_step(s, on, vt_ones, m_ref, acc_ref):
    m_new, acc_new = _flash_update(s, on, vt_ones, m_ref[...], acc_ref[...])
    acc_ref[...] = acc_new
    m_ref[...] = m_new


def _flash_update(s, on, vt_ones, m_old, acc_old):
    n_grp = s.shape[1] // Q_BLK
    sm = [jnp.where(on, s[:, j * Q_BLK:(j + 1) * Q_BLK], NEG) for j in range(n_grp)]
    m_new = jnp.maximum(m_old, jnp.concatenate([jnp.max(x, axis=0, keepdims=True) for x in sm], axis=1))
    p = jnp.concatenate([jnp.exp2(sm[j] - m_new[:, j * Q_BLK:(j + 1) * Q_BLK]).astype(BF16)
                         for j in range(n_grp)], axis=1)
    acc_new = jnp.exp2(m_old - m_new) * acc_old + jnp.dot(vt_ones, p, preferred_element_type=F32)
    return m_new, acc_new


def _nsa_kernel(nq_ref, nqr_ref, small_ref, kc_ref, vct_ref, ksel_ref, vselT_ref, kwin_ref,
                vwinT_ref, ovt_ref, o_ref, sc_ref, lim_ref, oc_ref, m_ref, acc_ref,
                *, n_blk, n_sel, n_cmp):
    i = pl.program_id(1)
    t0 = i * Q_BLK
    tq = t0 + lax.broadcasted_iota(I32, (1, Q_BLK), 1)
    tq3 = jnp.concatenate([tq, tq, tq], axis=1)
    nq = nq_ref[...]
    nqr = nqr_ref[...]
    nh = kc_ref.shape[0]

    def stack_heads(q, g):
        hs = [g * NSA_GROUP + r for r in range(NSA_GROUP)]
        return jnp.concatenate([q[:, h * HEAD_DIM:(h + 1) * HEAD_DIM] for h in hs], axis=0)

    qrs = [stack_heads(nqr, g) for g in range(NSA_KV_HEADS)]
    for g in range(NSA_KV_HEADS):
        lo, hi = g * HEAD_DIM, (g + 1) * HEAD_DIM
        s_c = _dot_nt(kc_ref[:, lo:hi], stack_heads(nq, g))
        n_io = lax.broadcasted_iota(I32, (nh, 1), 0)
        m_c = (n_io * CMP_STRIDE + (CMP_LEN - 1) <= tq3) & (n_io < n_cmp)
        p_c = _softmax_cols(s_c, m_c)
        oc_ref[g] = _dot(vct_ref[lo:hi, :], p_c)

        p_sum = p_c[:, 0:Q_BLK] + p_c[:, Q_BLK:2 * Q_BLK] + p_c[:, 2 * Q_BLK:3 * Q_BLK]
        imp = _dot_split(ovt_ref[...], p_sum)
        j_io = lax.broadcasted_iota(I32, (n_blk, 1), 0)
        cur = tq >> 6
        valid = j_io <= cur
        forced = (j_io == 0) | (j_io == cur) | (j_io == cur - 1)
        score = jnp.where(valid, imp + jnp.where(forced, FORCE_BONUS, 0.0), -jnp.inf)
        sc_ref[...] = score
        rank = jnp.zeros((n_blk, Q_BLK), F32)
        for b in range(n_blk):
            row = sc_ref[b:b + 1, :]
            beats = (row > score) | ((row == score) & (j_io > b))
            rank = rank + jnp.where(beats, 1.0, 0.0)
        lim_ref[g] = jnp.where((rank < n_sel) & valid, tq, -1)

    m_ref[...] = jnp.full(m_ref.shape, NEG, F32)
    acc_ref[...] = jnp.zeros(acc_ref.shape, F32)
    bpc = SEL_KC // SEL_BLOCK

    def sel_step(c, carry):
        k0 = pl.multiple_of(c * SEL_KC, SEL_KC)
        kch = ksel_ref[pl.ds(k0, SEL_KC), :]
        key = k0 + lax.broadcasted_iota(I32, (SEL_KC, 1), 0)
        for g in range(NSA_KV_HEADS):
            s = _dot_nt(kch[:, g * HEAD_DIM:(g + 1) * HEAD_DIM], qrs[g])
            rows = [jnp.broadcast_to(lim_ref[g, pl.ds(c * bpc + u, 1), :], (SEL_BLOCK, Q_BLK))
                    for u in range(bpc)]
            on = key <= jnp.concatenate(rows, axis=0)
            _flash_step(s, on, vselT_ref[c, 2 * g * HEAD_DIM:2 * (g + 1) * HEAD_DIM, :],
                        m_ref.at[g], acc_ref.at[g])
        return carry

    lax.fori_loop(0, (t0 + Q_BLK + SEL_KC - 1) // SEL_KC, sel_step, 0)

    small_t = small_ref[...].T
    heads = []
    for g in range(NSA_KV_HEADS):
        lo, hi = g * HEAD_DIM, (g + 1) * HEAD_DIM
        acc = acc_ref[g]
        o_s = acc[:HEAD_DIM] / jnp.maximum(acc[HEAD_DIM:HEAD_DIM + 1], 1e-30)
        o_c = oc_ref[g]

        nband = WINDOW // Q_BLK + 1
        cb = jnp.maximum(i - WINDOW // Q_BLK, 0)
        b0 = pl.multiple_of(cb * Q_BLK, Q_BLK)
        s_w = _dot_nt(kwin_ref[pl.ds(b0, nband * Q_BLK), lo:hi], qrs[g])
        key = b0 + lax.broadcasted_iota(I32, (nband * Q_BLK, 1), 0)
        diff = tq3 - key
        p_w = _softmax_cols(s_w, (diff >= 0) & (diff < WINDOW))
        o_w = jnp.zeros((HEAD_DIM, NSA_GROUP * Q_BLK), F32)
        for u in range(nband):
            o_w = o_w + _dot(vwinT_ref[cb + u, lo:hi, :], p_w[u * Q_BLK:(u + 1) * Q_BLK, :])

        for r in range(NSA_GROUP):
            gi = SMALL_GATE + (g * NSA_GROUP + r) * 3
            gt = _sigmoid(small_t[gi:gi + 3, :])
            cs = slice(r * Q_BLK, (r + 1) * Q_BLK)
            heads.append(gt[0:1, :] * o_c[:, cs] + gt[1:2, :] * o_s[:, cs] + gt[2:3, :] * o_w[:, cs])
    o_ref[...] = jnp.concatenate(heads, axis=0).T.astype(BF16)


def _nsa(nq, nqr, small, kc, vct, ksel, vselT, kwin, vwinT, ovt, batch, seq):
    n_q = seq // Q_BLK
    n_blk = seq // SEL_BLOCK
    n_cmp = (seq - CMP_LEN) // CMP_STRIDE + 1
    nh = kc.shape[1]
    blk = lambda n: pl.BlockSpec((Q_BLK, n), lambda b, i: (b * n_q + i, 0))
    perb2 = lambda r, c: pl.BlockSpec((None, r, c), lambda b, i: (b, 0, 0))
    perb_rows = pl.BlockSpec((seq, 128), lambda b, i: (b, 0))
    kern = functools.partial(_nsa_kernel, n_blk=n_blk, n_sel=min(SEL_TOP, n_blk), n_cmp=n_cmp)
    n_col = NSA_GROUP * Q_BLK
    return pl.pallas_call(
        kern,
        grid=(batch, n_q),
        in_specs=[blk(384), blk(384), blk(128),
                  perb2(nh, 128), perb2(128, nh),
                  perb_rows,
                  pl.BlockSpec((None, seq // SEL_KC, 256, SEL_KC), lambda b, i: (b, 0, 0, 0)),
                  perb_rows,
                  pl.BlockSpec((None, seq // Q_BLK, 128, Q_BLK), lambda b, i: (b, 0, 0, 0)),
                  pl.BlockSpec(ovt.shape, lambda b, i: (0, 0))],
        out_specs=blk(384),
        out_shape=jax.ShapeDtypeStruct((batch * seq, 384), BF16),
        scratch_shapes=[pltpu.VMEM((n_blk, Q_BLK), F32),
                        pltpu.VMEM((NSA_KV_HEADS, n_blk, Q_BLK), I32),
                        pltpu.VMEM((NSA_KV_HEADS, HEAD_DIM, n_col), F32),
                        pltpu.VMEM((NSA_KV_HEADS, 1, n_col), F32),
                        pltpu.VMEM((NSA_KV_HEADS, 2 * HEAD_DIM, n_col), F32)],
        compiler_params=_cparams(("parallel", "arbitrary")),
        name="nsa_attn",
    )(nq, nqr, small, kc, vct, ksel, vselT, kwin, vwinT, ovt)


def _dsa_kernel(dq_ref, iq_ref, small_ref, dk_ref, dvT_ref, ltri_ref, o_ref,
                ord_ref, hi_ref, lo_ref, m_ref, acc_ref, *, k_top):
    i = pl.program_id(1)
    t0 = i * Q_BLK
    kc = SEL_KC
    n_ch = (t0 + Q_BLK + kc - 1) // kc
    tq = t0 + lax.broadcasted_iota(I32, (1, Q_BLK), 1)

    small_t = small_ref[pl.ds(pl.multiple_of(t0, Q_BLK), Q_BLK), :].T
    w_rows = [small_t[SMALL_IW + h:SMALL_IW + h + 1, :] * (IDX_HEADS ** -0.5) for h in range(IDX_HEADS)]
    iq = iq_ref[...]
    iqs = jnp.concatenate([iq[:, h * IDX_DIM:(h + 1) * IDX_DIM] for h in range(IDX_HEADS)], axis=0)

    def score_step(c, carry):
        k0 = pl.multiple_of(c * kc, kc)
        ik = small_ref[pl.ds(k0, kc), :][:, 0:IDX_DIM]
        lg = _dot_nt(ik, iqs)
        sc = jnp.zeros((kc, Q_BLK), F32)
        for h in range(IDX_HEADS):
            sc = sc + w_rows[h] * jnp.maximum(lg[:, h * Q_BLK:(h + 1) * Q_BLK], 0.0)
        sc = jnp.where(sc == 0.0, 0.0, sc)
        bits = lax.bitcast_convert_type(sc, I32)
        ordv = bits ^ ((bits >> 31) & 0x7FFFFFFF)
        key = k0 + lax.broadcasted_iota(I32, (kc, 1), 0)
        ordv = jnp.where(key <= tq, ordv, INT_MIN)
        ord_ref[pl.ds(k0, kc), :] = ordv
        hi_ref[pl.ds(k0, kc), :] = (ordv >> 16).astype(I16)
        return carry

    lax.fori_loop(0, n_ch, score_step, 0)

    rows16 = 16

    def count16(ref, hit_fn):
        def body(c, acc):
            v = ref[pl.ds(pl.multiple_of(c * kc, kc), kc), :]
            hits = [jnp.where(hit_fn(v[j * rows16:(j + 1) * rows16, :]), jnp.int16(1), jnp.int16(0))
                    for j in range(kc // rows16)]
            while len(hits) > 1:
                hits = [a + b for a, b in zip(hits[0::2], hits[1::2])]
            return acc + hits[0]
        part = lax.fori_loop(0, n_ch, body, jnp.zeros((rows16, Q_BLK), I16))
        return jnp.sum(part.astype(I32), axis=0, keepdims=True)

    def pack16(row):
        return jnp.broadcast_to(row, (rows16, Q_BLK)).astype(I16)

    def radix16(ref, k_need):
        prefix = jnp.zeros((1, Q_BLK), I32)
        for bit in range(15, -1, -1):
            cand = pack16((prefix | (1 << bit)) + INT16_MIN)
            cnt = count16(ref, lambda v, cand=cand: v >= cand)
            prefix = jnp.where(cnt >= k_need, prefix | (1 << bit), prefix)
        return prefix

    thr_hi = radix16(hi_ref, k_top) + INT16_MIN
    thr_hi16 = pack16(thr_hi)
    above = count16(hi_ref, lambda v: v > thr_hi16)

    def low_step(c, carry):
        k0 = pl.multiple_of(c * kc, kc)
        o = ord_ref[pl.ds(k0, kc), :]
        low = (o & 0xFFFF) + INT16_MIN
        lo_ref[pl.ds(k0, kc), :] = jnp.where((o >> 16) == thr_hi, low, INT16_MIN).astype(I16)
        return carry

    lax.fori_loop(0, n_ch, low_step, 0)
    thr = (thr_hi << 16) | radix16(lo_ref, k_top - above)

    def count_gt_ge(c, acc):
        k0 = pl.multiple_of(c * kc, kc)
        o = ord_ref[pl.ds(k0, kc), :]
        gt = jnp.where(o > thr, 1, 0).astype(I32).reshape(kc // 8, 8, Q_BLK)
        ge = jnp.where(o >= thr, 1, 0).astype(I32).reshape(kc // 8, 8, Q_BLK)
        return acc[0] + jnp.sum(gt, axis=0), acc[1] + jnp.sum(ge, axis=0)

    zero8 = jnp.zeros((8, Q_BLK), I32)
    gt8, ge8 = lax.fori_loop(0, n_ch, count_gt_ge, (zero8, zero8))
    n_gt = jnp.sum(gt8, axis=0, keepdims=True)
    n_eq = jnp.sum(ge8, axis=0, keepdims=True) - n_gt
    short = thr == INT_MIN
    need = jnp.where(short, 0, k_top - n_gt)
    thr_all = jnp.where(short, INT_MIN + 1, thr)
    no_cut = jnp.min(jnp.where(short | (n_eq == need), 1.0, 0.0)) > 0.5

    q = dq_ref[...]
    qs = jnp.concatenate([q[:, h * HEAD_DIM:(h + 1) * HEAD_DIM] for h in range(DSA_HEADS)], axis=0)
    m_ref[...] = jnp.full(m_ref.shape, NEG, F32)
    acc_ref[...] = jnp.zeros(acc_ref.shape, F32)

    def sweep(mask_fn):
        def attn_step(c, seen):
            k0 = pl.multiple_of(c * kc, kc)
            on, seen = mask_fn(ord_ref[pl.ds(k0, kc), :], seen)
            kch = dk_ref[pl.ds(k0, kc), 0:HEAD_DIM]
            vt = dvT_ref[c]
            m_old = m_ref[...]
            acc_old = acc_ref[...]
            pairs = [slice(2 * j * Q_BLK, 2 * (j + 1) * Q_BLK) for j in range(DSA_HEADS // 2)]
            ss = [_dot_nt(kch, qs[cols]) for cols in pairs]
            outs = [_flash_update(s, on, vt, m_old[:, cols], acc_old[:, cols]) for s, cols in zip(ss, pairs)]
            m_ref[...] = jnp.concatenate([o[0] for o in outs], axis=1)
            acc_ref[...] = jnp.concatenate([o[1] for o in outs], axis=1)
            return seen
        lax.fori_loop(0, n_ch, attn_step, jnp.zeros((1, Q_BLK), F32))

    @pl.when(no_cut)
    def _():
        sweep(lambda o, seen: (o >= thr_all, seen))

    @pl.when(jnp.logical_not(no_cut))
    def _():
        need_f = need.astype(F32)

        def cut_mask(o, seen):
            eq = jnp.where(o == thr, 1.0, 0.0)
            before = jnp.dot(ltri_ref[...], eq.astype(BF16), preferred_element_type=F32) + seen
            on = (o > thr) | ((eq > 0.5) & (before < need_f))
            return on, seen + jnp.sum(eq, axis=0, keepdims=True)

        sweep(cut_mask)

    acc = acc_ref[...]
    o_t = acc[:HEAD_DIM] / jnp.maximum(acc[HEAD_DIM:HEAD_DIM + 1], 1e-30)
    heads = [o_t[:, h * Q_BLK:(h + 1) * Q_BLK] for h in range(DSA_HEADS)]
    o_ref[...] = jnp.concatenate(heads, axis=0).T.astype(BF16)


def _dsa(dq, iq, small, dk, dvT, ltri, batch, seq):
    n_q = seq // Q_BLK
    blk = lambda n: pl.BlockSpec((Q_BLK, n), lambda b, i: (b * n_q + i, 0))
    perb_rows = pl.BlockSpec((seq, 128), lambda b, i: (b, 0))
    kern = functools.partial(_dsa_kernel, k_top=min(IDX_TOPK_MAX, seq // 4))
    return pl.pallas_call(
        kern,
        grid=(batch, n_q),
        in_specs=[blk(384), blk(128), perb_rows, perb_rows,
                  pl.BlockSpec((None, seq // SEL_KC, 128, SEL_KC), lambda b, i: (b, 0, 0, 0)),
                  pl.BlockSpec(ltri.shape, lambda b, i: (0, 0))],
        out_specs=blk(384),
        out_shape=jax.ShapeDtypeStruct((batch * seq, 384), BF16),
        scratch_shapes=[pltpu.VMEM((seq, Q_BLK), I32),
                        pltpu.VMEM((seq, Q_BLK), I16), pltpu.VMEM((seq, Q_BLK), I16),
                        pltpu.VMEM((1, DSA_HEADS * Q_BLK), F32),
                        pltpu.VMEM((2 * HEAD_DIM, DSA_HEADS * Q_BLK), F32)],
        compiler_params=_cparams(("parallel", "arbitrary")),
        name="dsa_attn",
    )(dq, iq, small, dk, dvT, ltri)


def _sgu_kernel(z_ref, g_ref, b_ref, w_ref, bs_ref, o_ref):
    z = _gelu(z_ref[...])
    row = lax.broadcasted_iota(I32, (SGU_CHUNK, SGU_CHUNK), 0)
    col = lax.broadcasted_iota(I32, (SGU_CHUNK, SGU_CHUNK), 1)
    outs = []
    for g in range(SGU_GROUPS):
        lo, hi = g * HEAD_DIM, (g + 1) * HEAD_DIM
        u = z[:, lo:hi]
        v = _standardize(z[:, SGU_WIDTH + lo:SGU_WIDTH + hi]) * g_ref[:, lo:hi] + b_ref[:, lo:hi]
        w = jnp.where(row >= col, w_ref[g], 0.0)
        outs.append(u * (_dot(w, v) + bs_ref[:, lo:hi]))
    o_ref[...] = jnp.concatenate(outs, axis=1).astype(BF16)


def _sgu(z, g, b, w, bs):
    t = z.shape[0]
    n_c = bs.shape[0] // SGU_CHUNK
    full = lambda a: pl.BlockSpec(a.shape, lambda i: (0,) * a.ndim)
    return pl.pallas_call(
        _sgu_kernel,
        grid=(t // SGU_CHUNK,),
        in_specs=[pl.BlockSpec((SGU_CHUNK, 2 * SGU_WIDTH), lambda i: (i, 0)),
                  full(g), full(b), full(w), full(bs)],
        out_specs=pl.BlockSpec((SGU_CHUNK, SGU_WIDTH), lambda i: (i, 0)),
        out_shape=jax.ShapeDtypeStruct((t, SGU_WIDTH), BF16),
        compiler_params=_cparams(("parallel",)),
        name="sgu",
    )(z, g, b, w, bs)


def _outproj_kernel(oa_ref, ob_ref, oc_ref, x_ref, mod_ref, w_ref, g_ref, b_ref, o_ref, *, alpha):
    mix = (jnp.dot(oa_ref[...], w_ref[0:384, :], preferred_element_type=F32)
           + jnp.dot(ob_ref[...], w_ref[384:768, :], preferred_element_type=F32)
           + jnp.dot(oc_ref[...], w_ref[768:1024, :], preferred_element_type=F32))
    y = alpha * x_ref[...] + mod_ref[2:3, :] * mix
    o_ref[...] = _standardize(y) * g_ref[...] + b_ref[...]


def _outproj(oa, ob, oc, xf, mod, w, g, b, seq, alpha):
    t, d = xf.shape
    tm = 512
    per_b = seq // tm
    row = lambda n: pl.BlockSpec((tm, n), lambda i: (i, 0))
    full = lambda a: pl.BlockSpec(a.shape, lambda i: (0,) * a.ndim)
    return pl.pallas_call(
        functools.partial(_outproj_kernel, alpha=alpha),
        grid=(t // tm,),
        in_specs=[row(384), row(384), row(256), row(d),
                  pl.BlockSpec((None, 6, d), lambda i: (i // per_b, 0, 0)),
                  full(w), full(g), full(b)],
        out_specs=row(d),
        out_shape=jax.ShapeDtypeStruct((t, d), F32),
        compiler_params=_cparams(("parallel",)),
        name="outproj",
    )(oa, ob, oc, xf, mod, w, g, b)


def _ffn_kernel(x_ref, mod_ref, wg_ref, wu_ref, wd_ref, g_ref, b_ref, o_ref, acc_ref, *, alpha, fc):
    x = x_ref[...]
    hb = (_standardize(x) * (1.0 + mod_ref[4:5, :]) + mod_ref[3:4, :]).astype(BF16)
    d_ff = wg_ref.shape[1]
    for j in range(d_ff // fc):
        cs = slice(j * fc, (j + 1) * fc)
        a = (_silu(jnp.dot(hb, wg_ref[:, cs], preferred_element_type=F32))
             * jnp.dot(hb, wu_ref[:, cs], preferred_element_type=F32)).astype(BF16)
        part = jnp.dot(a, wd_ref[cs, :], preferred_element_type=F32)
        if j == 0:
            acc_ref[...] = part
        else:
            acc_ref[...] += part
    y = alpha * x + mod_ref[5:6, :] * acc_ref[...]
    o_ref[...] = _standardize(y) * g_ref[...] + b_ref[...]


def _ffn(xf, mod, wg, wu, wd, g, b, seq, alpha):
    t, d = xf.shape
    tm = 512
    per_b = seq // tm
    row = pl.BlockSpec((tm, d), lambda i: (i, 0))
    once = lambda a: pl.BlockSpec(a.shape, lambda i: (0,) * a.ndim, pipeline_mode=pl.Buffered(1))
    return pl.pallas_call(
        functools.partial(_ffn_kernel, alpha=alpha, fc=256),
        grid=(t // tm,),
        in_specs=[row, pl.BlockSpec((None, 6, d), lambda i: (i // per_b, 0, 0)),
                  once(wg), once(wu), once(wd), once(g), once(b)],
        out_specs=row,
        out_shape=jax.ShapeDtypeStruct((t, d), F32),
        scratch_shapes=[pltpu.VMEM((tm, d), F32)],
        compiler_params=_cparams(("parallel",)),
        name="ffn",
    )(xf, mod, wg, wu, wd, g, b)


MOE_TM = 1024
MOE_SUB = 128
MOE_BM = 512
MOE_FC = 512


def _route_kernel(x_ref, mod_ref, wr_ref, br_ref, utri_ref, h_ref, gate_ref, pos_ref, post_ref, cnt_ref):
    h = _standardize(x_ref[...]) * (1.0 + mod_ref[4:5, :]) + mod_ref[3:4, :]
    h_ref[...] = h.astype(BF16)
    lane = lax.broadcasted_iota(I32, (1, LANES), 1)
    logits = jnp.dot(h, wr_ref[...], preferred_element_type=F32,
                     precision=lax.Precision.HIGHEST) + br_ref[...]
    lg = jnp.where(lane < N_EXPERTS, logits, -jnp.inf)
    v0 = jnp.max(lg, axis=-1, keepdims=True)
    lane_f = lane.astype(F32)
    i0 = jnp.min(jnp.where(lg == v0, lane_f, float(LANES)), axis=-1, keepdims=True)
    lg1 = jnp.where(lane_f == i0, -jnp.inf, lg)
    v1 = jnp.max(lg1, axis=-1, keepdims=True)
    i1 = jnp.min(jnp.where(lg1 == v1, lane_f, float(LANES)), axis=-1, keepdims=True)
    e1 = jnp.exp(v1 - v0)
    den = 1.0 + e1
    gate_ref[...] = jnp.where(lane_f == i0, 1.0 / den, 0.0) + jnp.where(lane_f == i1, e1 / den, 0.0)
    sel_t = jnp.where((lane_f == i0) | (lane_f == i1), 1.0, 0.0).T
    rank_t = jnp.dot(sel_t.astype(BF16), utri_ref[...], preferred_element_type=F32)
    pos_t = jnp.where(sel_t > 0.5, rank_t, -1.0)
    post_ref[...] = pos_t[0:N_EXPERTS]
    pos_ref[...] = pos_t.T
    cnt_ref[...] = jnp.broadcast_to(jnp.sum(sel_t[0:N_EXPERTS], axis=1, keepdims=True), (N_EXPERTS, LANES))


def _route(xf, mod, wr, br, utri, seq):
    t, d = xf.shape
    tm = MOE_TM
    nt = t // tm
    per_b = seq // tm
    full = lambda a: pl.BlockSpec(a.shape, lambda i: (0,) * a.ndim)
    return pl.pallas_call(
        _route_kernel,
        grid=(nt,),
        in_specs=[pl.BlockSpec((tm, d), lambda i: (i, 0)),
                  pl.BlockSpec((None, 6, d), lambda i: (i // per_b, 0, 0)),
                  full(wr), full(br), full(utri)],
        out_specs=(pl.BlockSpec((tm, d), lambda i: (i, 0)),
                   pl.BlockSpec((tm, LANES), lambda i: (i, 0)),
                   pl.BlockSpec((tm, LANES), lambda i: (i, 0)),
                   pl.BlockSpec((None, N_EXPERTS, tm), lambda i: (i, 0, 0)),
                   pl.BlockSpec((None, N_EXPERTS, LANES), lambda i: (i, 0, 0))),
        out_shape=(jax.ShapeDtypeStruct((t, d), BF16),
                   jax.ShapeDtypeStruct((t, LANES), F32),
                   jax.ShapeDtypeStruct((t, LANES), F32),
                   jax.ShapeDtypeStruct((nt, N_EXPERTS, tm), F32),
                   jax.ShapeDtypeStruct((nt, N_EXPERTS, LANES), F32)),
        compiler_params=_cparams(("parallel",)),
        name="moe_route",
    )(xf, mod, wr, br, utri)


def _moe_schedule(cnt, n_blocks, k_max):
    nt, ne = cnt.shape
    per = MOE_BM // MOE_SUB
    nb = (cnt + MOE_SUB - 1) // MOE_SUB
    nbt = nb.T
    tot = jnp.sum(nbt, axis=1)
    reg = (tot + per - 1) // per * per
    reg_end = jnp.cumsum(reg)
    reg_start = reg_end - reg
    seg_end = jnp.cumsum(nbt, axis=1)
    seg_start = reg_start[:, None] + seg_end - nbt
    j = jnp.arange(n_blocks, dtype=I32)
    e_j = jnp.minimum(jnp.sum(reg_end[None, :] <= j[:, None], axis=1), ne - 1).astype(I32)
    q = j - reg_start[e_j]
    valid_j = q < tot[e_j]
    i_j = jnp.minimum(jnp.sum(seg_end[e_j] <= q[:, None], axis=1), nt - 1).astype(I32)
    s_j = jnp.where(valid_j, q - (seg_end[e_j, i_j] - nbt[e_j, i_j]), -2).astype(I32)
    step_e = e_j[::per]
    step_valid = valid_j[::per].astype(I32)
    cum = jnp.cumsum(nb, axis=1)
    n_tile = cum[:, -1]
    k = jnp.minimum(jnp.arange(k_max, dtype=I32)[None, :], n_tile[:, None] - 1)
    e_k = jnp.sum(cum[:, None, :] <= k[:, :, None], axis=2).astype(I32)
    tile = jnp.arange(nt, dtype=I32)[:, None]
    s_k = k - (jnp.take_along_axis(cum, e_k, axis=1) - jnp.take_along_axis(nb, e_k, axis=1))
    blk_k = seg_start[e_k, tile] + s_k
    valid_k = (jnp.arange(k_max, dtype=I32)[None, :] < n_tile[:, None]).astype(I32)
    return (i_j, e_j, s_j), (step_e, step_valid), (blk_k.astype(I32), e_k, s_k.astype(I32), valid_k)


def _dispatch_kernel(tile_ref, e_ref, s_ref, h_ref, post_ref, xs_ref):
    j = pl.program_id(0)
    row = post_ref[pl.ds(e_ref[j], 1), :]
    want = s_ref[j] * MOE_SUB + lax.broadcasted_iota(I32, (MOE_SUB, 1), 0)
    onehot = jnp.where(row == want.astype(F32), 1.0, 0.0).astype(BF16)
    xs_ref[...] = jnp.dot(onehot, h_ref[...], preferred_element_type=F32).astype(BF16)


def _dispatch(h, post, sched, n_blocks):
    t, d = h.shape
    tm = MOE_TM
    grid_spec = pltpu.PrefetchScalarGridSpec(
        num_scalar_prefetch=3,
        grid=(n_blocks,),
        in_specs=[pl.BlockSpec((tm, d), lambda j, ti, e, s: (ti[j], 0)),
                  pl.BlockSpec((None, N_EXPERTS, tm), lambda j, ti, e, s: (ti[j], 0, 0))],
        out_specs=pl.BlockSpec((MOE_SUB, d), lambda j, ti, e, s: (j, 0)),
    )
    return pl.pallas_call(
        _dispatch_kernel,
        grid_spec=grid_spec,
        out_shape=jax.ShapeDtypeStruct((n_blocks * MOE_SUB, d), BF16),
        compiler_params=_cparams(("arbitrary",)),
        name="moe_dispatch",
    )(*sched, h, post)


def _experts_kernel(e_ref, v_ref, xs_ref, wg_ref, wu_ref, wd_ref, ys_ref, acc_ref):
    jb = pl.program_id(0)
    f = pl.program_id(1)
    last = pl.num_programs(1) - 1

    @pl.when(v_ref[jb] > 0)
    def _():
        xb = xs_ref[...]
        a = (_silu(jnp.dot(xb, wg_ref[...], preferred_element_type=F32))
             * jnp.dot(xb, wu_ref[...], preferred_element_type=F32)).astype(BF16)
        part = jnp.dot(a, wd_ref[...], preferred_element_type=F32)

        @pl.when(f == 0)
        def _():
            acc_ref[...] = part

        @pl.when(f > 0)
        def _():
            acc_ref[...] += part

        @pl.when(f == last)
        def _():
            ys_ref[...] = acc_ref[...].astype(BF16)

    @pl.when((v_ref[jb] == 0) & (f == last))
    def _():
        ys_ref[...] = jnp.zeros(ys_ref.shape, BF16)


def _experts(xs, wg, wu, wd, sched):
    rows, d = xs.shape
    n_e, _, d_ff = wg.shape
    fc = MOE_FC
    n_f = d_ff // fc
    fsel = lambda jb, f, e, v: jnp.where(v[jb] > 0, f, n_f - 1)
    grid_spec = pltpu.PrefetchScalarGridSpec(
        num_scalar_prefetch=2,
        grid=(rows // MOE_BM, n_f),
        in_specs=[pl.BlockSpec((MOE_BM, d), lambda jb, f, e, v: (jb, 0)),
                  pl.BlockSpec((None, d, fc), lambda jb, f, e, v: (e[jb], 0, fsel(jb, f, e, v))),
                  pl.BlockSpec((None, d, fc), lambda jb, f, e, v: (e[jb], 0, fsel(jb, f, e, v))),
                  pl.BlockSpec((None, fc, d), lambda jb, f, e, v: (e[jb], fsel(jb, f, e, v), 0))],
        out_specs=pl.BlockSpec((MOE_BM, d), lambda jb, f, e, v: (jb, 0)),
        scratch_shapes=[pltpu.VMEM((MOE_BM, d), F32)],
    )
    return pl.pallas_call(
        _experts_kernel,
        grid_spec=grid_spec,
        out_shape=jax.ShapeDtypeStruct((rows, d), BF16),
        compiler_params=_cparams(("arbitrary", "arbitrary")),
        name="moe_experts",
    )(*sched, xs, wg, wu, wd)


def _combine_kernel(blk_ref, e_ref, s_ref, v_ref, ys_ref, pos_ref, gate_ref, x_ref, mod_ref, g_ref, b_ref,
                    o_ref, acc_ref, *, alpha):
    i = pl.program_id(0)
    k = pl.program_id(1)

    @pl.when(k == 0)
    def _():
        acc_ref[...] = jnp.zeros(acc_ref.shape, F32)

    @pl.when(v_ref[i, k] > 0)
    def _():
        lane = lax.broadcasted_iota(I32, (1, LANES), 1)
        mine = lane == e_ref[i, k]
        pcol = jnp.sum(jnp.where(mine, pos_ref[...], 0.0), axis=-1, keepdims=True)
        gcol = jnp.sum(jnp.where(mine, gate_ref[...], 0.0), axis=-1, keepdims=True)
        want = (s_ref[i, k] * MOE_SUB + lane).astype(F32)
        onehot = jnp.where(pcol == want, 1.0, 0.0).astype(BF16)
        acc_ref[...] += gcol * jnp.dot(onehot, ys_ref[...], preferred_element_type=F32)

    @pl.when(k == pl.num_programs(1) - 1)
    def _():
        y = alpha * x_ref[...] + mod_ref[5:6, :] * acc_ref[...]
        o_ref[...] = _standardize(y) * g_ref[...] + b_ref[...]


def _combine(ys, pos, gates, xf, mod, g, b, sched, seq, alpha):
    t, d = xf.shape
    tm = MOE_TM
    per_b = seq // tm
    k_max = sched[0].shape[1]
    row = lambda n: pl.BlockSpec((tm, n), lambda i, k, *_: (i, 0))
    full = lambda a: pl.BlockSpec(a.shape, lambda i, k, *_: (0,) * a.ndim)
    grid_spec = pltpu.PrefetchScalarGridSpec(
        num_scalar_prefetch=4,
        grid=(t // tm, k_max),
        in_specs=[pl.BlockSpec((MOE_SUB, d), lambda i, k, blk, e, s, v: (blk[i, k], 0)),
                  row(LANES), row(LANES), row(d),
                  pl.BlockSpec((None, 6, d), lambda i, k, *_: (i // per_b, 0, 0)),
                  full(g), full(b)],
        out_specs=row(d),
        scratch_shapes=[pltpu.VMEM((tm, d), F32)],
    )
    return pl.pallas_call(
        functools.partial(_combine_kernel, alpha=alpha),
        grid_spec=grid_spec,
        out_shape=jax.ShapeDtypeStruct((t, d), F32),
        compiler_params=_cparams(("arbitrary", "arbitrary")),
        name="moe_combine",
    )(*sched, ys, pos, gates, xf, mod, g, b)


def _moe(xf, mod, wr, br, wg, wu, wd, g, b, seq, alpha):
    t, d = xf.shape
    nt = t // MOE_TM
    per = MOE_BM // MOE_SUB
    n_blocks = TOP_K * t // MOE_SUB + nt * N_EXPERTS + N_EXPERTS * (per - 1)
    n_blocks = (n_blocks + per - 1) // per * per
    k_max = TOP_K * MOE_TM // MOE_SUB + N_EXPERTS
    utri = jnp.asarray(np.triu(np.ones((MOE_TM, MOE_TM), np.float32), 1), BF16)
    h, gates, pos, post, cnt = _route(xf, mod, wr, br, utri, seq)
    sched_d, sched_x, sched_c = _moe_schedule(cnt[:, :, 0].astype(I32), n_blocks, k_max)
    xs = _dispatch(h, post, sched_d, n_blocks)
    ys = _experts(xs, wg, wu, wd, sched_x)
    return _combine(ys, pos, gates, xf, mod, g, b, sched_c, seq, alpha)


def _inproj_columns():
    sizes = (NSA_WIDTH, 6 * NSA_KV_HEADS * HEAD_DIM, 3 * NSA_HEADS, DSA_WIDTH, DSA_LATENT,
             IDX_HEADS * IDX_DIM, IDX_DIM, IDX_HEADS, 2 * SGU_WIDTH)
    starts = np.concatenate([[0], np.cumsum(sizes)])
    o_nq, o_kv, o_g, o_dq, o_ckv, o_iq, o_ik, o_iw, o_sgu = starts[:-1]
    perm = np.full((C_TOTAL,), -1, np.int64)

    def swap(base, n, dim):
        idx = np.arange(n)
        return base + (idx // dim) * dim + (idx % dim + dim // 2) % dim

    perm[C_NQ:C_NQ + 384] = o_nq + np.arange(384)
    perm[C_KV:C_KV + 768] = o_kv + np.arange(768)
    perm[C_DQ:C_DQ + 384] = o_dq + np.arange(384)
    perm[C_CKV:C_CKV + 128] = o_ckv + np.arange(128)
    perm[C_IQ:C_IQ + 128] = o_iq + np.arange(128)
    perm[C_SGU:C_SGU + 512] = o_sgu + np.arange(512)
    perm[C_SMALL:C_SMALL + IDX_DIM] = o_ik + np.arange(IDX_DIM)
    perm[C_SMALL + SMALL_IW:C_SMALL + SMALL_IW + IDX_HEADS] = o_iw + np.arange(IDX_HEADS)
    perm[C_SMALL + SMALL_GATE:C_SMALL + SMALL_GATE + 3 * NSA_HEADS] = o_g + np.arange(3 * NSA_HEADS)
    perm[C_NQ_SW:C_NQ_SW + 384] = swap(o_nq, 384, HEAD_DIM)
    perm[C_KSEL_SW:C_KSEL_SW + 128] = swap(o_kv + 256, 128, HEAD_DIM)
    perm[C_KWIN_SW:C_KWIN_SW + 128] = swap(o_kv + 512, 128, HEAD_DIM)
    perm[C_DQ_SW:C_DQ_SW + 384] = swap(o_dq, 384, HEAD_DIM)
    perm[C_IQ_SW:C_IQ_SW + 128] = swap(o_iq, 128, IDX_DIM)
    perm[C_SMALL_SW:C_SMALL_SW + IDX_DIM] = swap(o_ik, IDX_DIM, IDX_DIM)
    return perm, int(starts[-1])


def _swap_halves(w):
    half = w.shape[-1] // 2
    return jnp.concatenate([w[..., half:], w[..., :half]], axis=-1)


def _compress_weights(pos, w1, w2):
    half = CMP_LEN // 2
    eye_g = jnp.eye(NSA_KV_HEADS, dtype=F32)
    eye_j = jnp.eye(2, dtype=F32)

    def big(w1_half):
        w = jnp.einsum('jlde,jk,gh->ljgdkhe', w1_half, eye_j, eye_g)
        return w.reshape(half * 2 * NSA_KV_HEADS * HEAD_DIM, 2 * NSA_KV_HEADS * HEAD_DIM)

    def posrow(p_half):
        p = jnp.broadcast_to(p_half.transpose(1, 0, 2)[:, :, None, :], (half, 2, NSA_KV_HEADS, HEAD_DIM))
        return p.reshape(1, -1)

    w2big = jnp.einsum('jef,jk,gh->jgekhf', w2, eye_j, eye_g).reshape(256, 256)
    return (posrow(pos[:, :half]), posrow(pos[:, half:]),
            big(w1[:, :half]).astype(BF16), big(w1[:, half:]).astype(BF16), w2big.astype(BF16))


def _rope_tables(positions):
    def tab(dim, reps):
        inv = ROPE_THETA ** (-jnp.arange(0, dim, 2, dtype=F32) / dim)
        ang = positions.astype(F32)[..., None] * inv
        cos, sin = jnp.cos(ang), jnp.sin(ang)
        cos = jnp.tile(jnp.concatenate([cos, cos], axis=-1), (1, 1, reps))
        sin = jnp.tile(jnp.concatenate([-sin, sin], axis=-1), (1, 1, reps))
        return cos.reshape(-1, LANES), sin.reshape(-1, LANES)
    cos_h, sin_h = tab(HEAD_DIM, LANES // HEAD_DIM)
    cos_i, sin_i = tab(IDX_DIM, LANES // IDX_DIM)
    return cos_h, sin_h, cos_i, sin_i


def kernel(x, c, positions, w_ada, b_ada, w_in, nsa_cmp_pos, nsa_cmp_w1, nsa_cmp_w2, dsa_kv_norm, dsa_w_uk, dsa_w_uv, sgu_norm_g, sgu_norm_b, sgu_w, sgu_b, w_out, ln1_g, ln1_b, ln2_g, ln2_b, ffn_w_gate, ffn_w_up, ffn_w_down, moe_w_router, moe_b_router, moe_w_gate, moe_w_up, moe_w_down):
    batch, seq, d = x.shape
    depth = w_ada.shape[0]
    t = batch * seq
    alpha = (2 * depth) ** 0.25
    assert seq % 512 == 0 and seq >= WINDOW + Q_BLK

    tabs = _rope_tables(positions)
    mod_all = _adaln(c, w_ada, b_ada).reshape(depth, batch, 6, d)
    perm, in_width = _inproj_columns()
    perm = jnp.asarray(np.where(perm < 0, in_width, perm), I32)

    n_blk = seq // SEL_BLOCK
    n_half = seq // CMP_STRIDE
    cmp_start = np.arange(n_half)[None, :] * CMP_STRIDE
    blk_start = np.arange(n_blk)[:, None] * SEL_BLOCK
    ovt = jnp.asarray((cmp_start < blk_start + SEL_BLOCK) & (cmp_start + CMP_LEN > blk_start), BF16)
    ltri = jnp.asarray(np.tril(np.ones((SEL_KC, SEL_KC), np.float32), -1), BF16)

    xf = x.reshape(t, d)
    for layer in range(depth):
        mod = mod_all[layer]
        w_pad = jnp.concatenate([w_in[layer], jnp.zeros((d, 1), F32)], axis=1)
        w_ext = jnp.take(w_pad, perm, axis=1).astype(BF16)
        wkv = jnp.concatenate([dsa_w_uk[layer], dsa_w_uv[layer], _swap_halves(dsa_w_uk[layer]),
                               jnp.zeros((DSA_LATENT, HEAD_DIM), F32)], axis=1).astype(BF16)
        (nq, nqr, kvcmp, ksel, vselT, kwin, vwinT, dq, dk, dvT, iq, small, sgu_z) = _inproj(
            xf, mod, tabs, w_ext, wkv, dsa_kv_norm[layer].reshape(1, -1), batch, seq)

        plo, phi, wlo, whi, w2big = _compress_weights(nsa_cmp_pos[layer], nsa_cmp_w1[layer], nsa_cmp_w2[layer])
        kc, vct = _compress(kvcmp.reshape(batch, n_half, CMP_STRIDE * 256), plo, phi, wlo, whi, w2big)
        o_a = _nsa(nq, nqr, small, kc, vct, ksel, vselT, kwin, vwinT, ovt, batch, seq)
        o_b = _dsa(dq, iq, small, dk, dvT, ltri, batch, seq)
        bs = jnp.repeat(sgu_b[layer].T, HEAD_DIM, axis=1)
        o_c = _sgu(sgu_z, sgu_norm_g[layer].reshape(1, -1), sgu_norm_b[layer].reshape(1, -1),
                   sgu_w[layer], bs)
        xf = _outproj(o_a, o_b, o_c, xf, mod, w_out[layer].astype(BF16),
                      ln1_g[layer].reshape(1, -1), ln1_b[layer].reshape(1, -1), seq, alpha)

        j = layer // 2
        g2, b2 = ln2_g[layer].reshape(1, -1), ln2_b[layer].reshape(1, -1)
        if layer % 2 == 0:
            xf = _ffn(xf, mod, ffn_w_gate[j].astype(BF16), ffn_w_up[j].astype(BF16),
                      ffn_w_down[j].astype(BF16), g2, b2, seq, alpha)
        else:
            wr = jnp.pad(moe_w_router[j], ((0, 0), (0, LANES - N_EXPERTS)))
            br = jnp.pad(moe_b_router[j], (0, LANES - N_EXPERTS)).reshape(1, -1)
            xf = _moe(xf, mod, wr, br, moe_w_gate[j].astype(BF16), moe_w_up[j].astype(BF16),
                      moe_w_down[j].astype(BF16), g2, b2, seq, alpha)
    return xf.reshape(batch, seq, d)
```

```python
import functools

import numpy as np
import jax
import jax.numpy as jnp
from jax import lax
from jax.experimental import pallas as pl
from jax.experimental.pallas import tpu as pltpu

F32 = jnp.float32
BF16 = jnp.bfloat16
I32 = jnp.int32

HEAD_DIM = 64
Q_BLK = 128
ROPE_THETA = 10000.0
LN_EPS = 1e-5
RMS_EPS = 1e-6

NSA_HEADS = 6
NSA_KV_HEADS = 2
NSA_GROUP = NSA_HEADS // NSA_KV_HEADS
NSA_WIDTH = NSA_HEADS * HEAD_DIM
CMP_LEN = 32
CMP_STRIDE = 16
SEL_BLOCK = 64
SEL_TOP = 16
WINDOW = 512
FORCE_BONUS = 1e4

DSA_HEADS = 6
DSA_WIDTH = DSA_HEADS * HEAD_DIM
DSA_LATENT = 128
IDX_HEADS = 4
IDX_DIM = 32
IDX_TOPK_MAX = 256

SGU_GROUPS = 4
SGU_CHUNK = 128
SGU_WIDTH = SGU_GROUPS * HEAD_DIM

N_EXPERTS = 8
TOP_K = 2

LANES = 128
VMEM_LIMIT = 56 * 1024 * 1024
NEG = -1e30
INT_MIN = -(2 ** 31)

C_NQ = 0
C_KV = 384
C_DQ = 1152
C_CKV = 1536
C_IQ = 1664
C_SGU = 1792
C_SMALL = 2304
C_MAIN = 2432
C_NQ_SW = 2432
C_KSEL_SW = 2816
C_KWIN_SW = 2944
C_DQ_SW = 3072
C_IQ_SW = 3456
C_SMALL_SW = 3584
C_TOTAL = 3712
SMALL_IW = 32
SMALL_GATE = 36

SEL_KC = 512
Q_SCALE = HEAD_DIM ** -0.5 * 1.4426950408889634


def _cparams(sem):
    return pltpu.CompilerParams(dimension_semantics=sem, vmem_limit_bytes=VMEM_LIMIT)


def _dot(a, b):
    return jnp.dot(a.astype(BF16), b.astype(BF16), preferred_element_type=F32)


def _dot_nt(a, b):
    return lax.dot_general(a.astype(BF16), b.astype(BF16), (((1,), (1,)), ((), ())),
                           preferred_element_type=F32)


def _dot_split(a_exact, b):
    b_hi = b.astype(BF16)
    b_lo = (b - b_hi.astype(F32)).astype(BF16)
    return (jnp.dot(a_exact, b_hi, preferred_element_type=F32)
            + jnp.dot(a_exact, b_lo, preferred_element_type=F32))


def _gelu(x):
    return 0.5 * x * (1.0 + jnp.tanh(0.7978845608028654 * (x + 0.044715 * (x * x * x))))


def _silu(x):
    return x * (1.0 / (1.0 + jnp.exp(-x)))


def _sigmoid(x):
    return 1.0 / (1.0 + jnp.exp(-x))


def _standardize(x):
    mu = jnp.mean(x, axis=-1, keepdims=True)
    xc = x - mu
    var = jnp.mean(xc * xc, axis=-1, keepdims=True)
    return xc * lax.rsqrt(var + LN_EPS)


def _adaln_kernel(c_ref, w_ref, b_ref, o_ref):
    c = c_ref[...]
    o_ref[...] = jnp.dot(_silu(c), w_ref[...], preferred_element_type=F32,
                         precision=lax.Precision.HIGHEST) + b_ref[...]


def _adaln(c, w_ada, b_ada):
    depth, d, n = w_ada.shape
    b = c.shape[0]
    tn = 512
    return pl.pallas_call(
        _adaln_kernel,
        grid=(depth, n // tn),
        in_specs=[pl.BlockSpec((b, d), lambda l, j: (0, 0)),
                  pl.BlockSpec((None, d, tn), lambda l, j: (l, 0, j)),
                  pl.BlockSpec((None, 1, tn), lambda l, j: (l, 0, j))],
        out_specs=pl.BlockSpec((None, b, tn), lambda l, j: (l, 0, j)),
        out_shape=jax.ShapeDtypeStruct((depth, b, n), F32),
        compiler_params=_cparams(("arbitrary", "arbitrary")),
        name="adaln",
    )(c, w_ada, b_ada.reshape(depth, 1, n))


def _inproj_kernel(x_ref, mod_ref, cos_ref, sin_ref, cosi_ref, sini_ref, w_ref, wkv_ref, kvg_ref,
                   nq_ref, nqr_ref, kvcmp_ref, ksel_ref, vselT_ref, kwin_ref, vwinT_ref,
                   dq_ref, dk_ref, dvT_ref, iq_ref, small_ref, sgu_ref):
    tm = x_ref.shape[0]
    shift = mod_ref[0:1, :]
    scale = mod_ref[1:2, :]
    hb = (_standardize(x_ref[...]) * (1.0 + scale) + shift).astype(BF16)

    def proj(c0, n):
        return jnp.dot(hb, w_ref[:, c0:c0 + n], preferred_element_type=F32)

    cos = cos_ref[...]
    sin = sin_ref[...]
    cos3 = jnp.concatenate([cos, cos, cos], axis=1)
    sin3 = jnp.concatenate([sin, sin, sin], axis=1)
    lane = lax.broadcasted_iota(I32, (1, LANES), 1)

    zq = proj(C_NQ, 384)
    nq_ref[...] = (zq * Q_SCALE).astype(BF16)
    nqr_ref[...] = ((zq * cos3 + proj(C_NQ_SW, 384) * sin3) * Q_SCALE).astype(BF16)

    kvcmp_ref[...] = proj(C_KV, 256)
    ksel_ref[...] = (proj(C_KV + 256, 128) * cos + proj(C_KSEL_SW, 128) * sin).astype(BF16)
    ones_t = jnp.ones((HEAD_DIM, tm), F32)
    vsel_t = proj(C_KV + 384, 128).T
    vsel_x = jnp.concatenate([vsel_t[:HEAD_DIM], ones_t, vsel_t[HEAD_DIM:], ones_t], axis=0).astype(BF16)
    for j in range(tm // SEL_KC):
        vselT_ref[j] = vsel_x[:, j * SEL_KC:(j + 1) * SEL_KC]
    kwin_ref[...] = (proj(C_KV + 512, 128) * cos + proj(C_KWIN_SW, 128) * sin).astype(BF16)
    vwin_t = proj(C_KV + 640, 128).T.astype(BF16)
    for j in range(tm // Q_BLK):
        vwinT_ref[j] = vwin_t[:, j * Q_BLK:(j + 1) * Q_BLK]

    dq_ref[...] = ((proj(C_DQ, 384) * cos3 + proj(C_DQ_SW, 384) * sin3) * Q_SCALE).astype(BF16)

    ckv = proj(C_CKV, 128)
    ckv = ckv * lax.rsqrt(jnp.mean(ckv * ckv, axis=-1, keepdims=True) + RMS_EPS) * kvg_ref[...]
    kd = jnp.dot(ckv.astype(BF16), wkv_ref[...], preferred_element_type=F32)
    first = lane < HEAD_DIM
    dkv = kd[:, :128] * jnp.where(first, cos, 1.0) + kd[:, 128:] * jnp.where(first, sin, 0.0)
    dk_ref[...] = dkv.astype(BF16)
    dv_x = jnp.concatenate([dkv.T[HEAD_DIM:], ones_t], axis=0).astype(BF16)
    for j in range(tm // SEL_KC):
        dvT_ref[j] = dv_x[:, j * SEL_KC:(j + 1) * SEL_KC]

    cosi = cosi_ref[...]
    sini = sini_ref[...]
    iq_ref[...] = (proj(C_IQ, 128) * cosi + proj(C_IQ_SW, 128) * sini).astype(BF16)
    isk = lane < IDX_DIM
    small_ref[...] = (proj(C_SMALL, 128) * jnp.where(isk, cosi, 1.0)
                      + proj(C_SMALL_SW, 128) * jnp.where(isk, sini, 0.0))
    sgu_ref[...] = proj(C_SGU, 512)


def _inproj(xf, mod, tabs, w_ext, wkv, kvg, batch, seq):
    t, d = xf.shape
    tm = 512
    per_b = seq // tm
    row = lambda n: pl.BlockSpec((tm, n), lambda i: (i, 0))
    trk = lambda r: pl.BlockSpec((None, tm // SEL_KC, r, SEL_KC), lambda i: (i // per_b, i % per_b, 0, 0))
    tr128 = pl.BlockSpec((None, tm // Q_BLK, 128, Q_BLK), lambda i: (i // per_b, i % per_b, 0, 0))
    full = lambda a: pl.BlockSpec(a.shape, lambda i: (0,) * a.ndim)
    out_shape = (
        jax.ShapeDtypeStruct((t, 384), BF16),
        jax.ShapeDtypeStruct((t, 384), BF16),
        jax.ShapeDtypeStruct((t, 256), F32),
        jax.ShapeDtypeStruct((t, 128), BF16),
        jax.ShapeDtypeStruct((batch, seq // SEL_KC, 256, SEL_KC), BF16),
        jax.ShapeDtypeStruct((t, 128), BF16),
        jax.ShapeDtypeStruct((batch, seq // Q_BLK, 128, Q_BLK), BF16),
        jax.ShapeDtypeStruct((t, 384), BF16),
        jax.ShapeDtypeStruct((t, 128), BF16),
        jax.ShapeDtypeStruct((batch, seq // SEL_KC, 128, SEL_KC), BF16),
        jax.ShapeDtypeStruct((t, 128), BF16),
        jax.ShapeDtypeStruct((t, 128), F32),
        jax.ShapeDtypeStruct((t, 512), F32),
    )
    out_specs = (row(384), row(384), row(256), row(128), trk(256), row(128), tr128,
                 row(384), row(128), trk(128), row(128), row(128), row(512))
    return pl.pallas_call(
        _inproj_kernel,
        grid=(t // tm,),
        in_specs=[row(d),
                  pl.BlockSpec((None, 6, d), lambda i: (i // per_b, 0, 0)),
                  row(128), row(128), row(128), row(128),
                  full(w_ext), full(wkv), full(kvg)],
        out_specs=out_specs,
        out_shape=out_shape,
        compiler_params=_cparams(("parallel",)),
        name="inproj",
    )(xf, mod, *tabs, w_ext, wkv, kvg)


def _compress_kernel(h_ref, plo_ref, phi_ref, wlo_ref, whi_ref, w2_ref, kc_ref, vct_ref):
    h = h_ref[...]
    a = _dot(h + plo_ref[...], wlo_ref[...])
    b = _dot(h + phi_ref[...], whi_ref[...])
    nh = h.shape[0]
    pre = a + pltpu.roll(b, nh - 1, 0)
    cmp = _dot(_gelu(pre), w2_ref[...])
    kc_ref[...] = cmp[:, :128].astype(BF16)
    vct_ref[...] = cmp[:, 128:].T.astype(BF16)


def _compress(kvcmp_h, plo, phi, wlo, whi, w2):
    batch, nh, width = kvcmp_h.shape
    full = lambda a: pl.BlockSpec(a.shape, lambda b: (0,) * a.ndim)
    return pl.pallas_call(
        _compress_kernel,
        grid=(batch,),
        in_specs=[pl.BlockSpec((None, nh, width), lambda b: (b, 0, 0)),
                  full(plo), full(phi), full(wlo), full(whi), full(w2)],
        out_specs=(pl.BlockSpec((None, nh, 128), lambda b: (b, 0, 0)),
                   pl.BlockSpec((None, 128, nh), lambda b: (b, 0, 0))),
        out_shape=(jax.ShapeDtypeStruct((batch, nh, 128), BF16),
                   jax.ShapeDtypeStruct((batch, 128, nh), BF16)),
        compiler_params=_cparams(("parallel",)),
        name="nsa_compress",
    )(kvcmp_h, plo, phi, wlo, whi, w2)


def _softmax_cols(s, mask):
    m = jnp.max(jnp.where(mask, s, NEG), axis=0, keepdims=True)
    p = jnp.where(mask, jnp.exp2(s - m), 0.0)
    return p / jnp.maximum(jnp.sum(p, axis=0, keepdims=True), 1e-30)


def _flash_step(s, on, vt_ones, m_ref, acc_ref):
    m_new, acc_new = _flash_update(s, on, vt_ones, m_ref[...], acc_ref[...])
    acc_ref[...] = acc_new
    m_ref[...] = m_new


def _flash_update(s, on, vt_ones, m_old, acc_old):
    n_grp = s.shape[1] // Q_BLK
    sm = [jnp.where(on, s[:, j * Q_BLK:(j + 1) * Q_BLK], NEG) for j in range(n_grp)]
    m_new = jnp.maximum(m_old, jnp.concatenate([jnp.max(x, axis=0, keepdims=True) for x in sm], axis=1))
    p = jnp.concatenate([jnp.exp2(sm[j] - m_new[:, j * Q_BLK:(j + 1) * Q_BLK]).astype(BF16)
                         for j in range(n_grp)], axis=1)
    acc_new = jnp.exp2(m_old - m_new) * acc_old + jnp.dot(vt_ones, p, preferred_element_type=F32)
    return m_new, acc_new


def _nsa_kernel(nq_ref, nqr_ref, small_ref, kc_ref, vct_ref, ksel_ref, vselT_ref, kwin_ref,
                vwinT_ref, ovt_ref, o_ref, sc_ref, lim_ref, oc_ref, m_ref, acc_ref, sa_ref, sb_ref,
                *, n_blk, n_sel, n_cmp):
    i = pl.program_id(1)
    t0 = i * Q_BLK
    tq = t0 + lax.broadcasted_iota(I32, (1, Q_BLK), 1)
    tq3 = jnp.concatenate([tq, tq, tq], axis=1)
    nq = nq_ref[...]
    nqr = nqr_ref[...]
    nh = kc_ref.shape[0]

    def stack_heads(q, g):
        hs = [g * NSA_GROUP + r for r in range(NSA_GROUP)]
        return jnp.concatenate([q[:, h * HEAD_DIM:(h + 1) * HEAD_DIM] for h in hs], axis=0)

    qrs = [stack_heads(nqr, g) for g in range(NSA_KV_HEADS)]
    for g in range(NSA_KV_HEADS):
        lo, hi = g * HEAD_DIM, (g + 1) * HEAD_DIM
        s_c = _dot_nt(kc_ref[:, lo:hi], stack_heads(nq, g))
        n_io = lax.broadcasted_iota(I32, (nh, 1), 0)
        m_c = (n_io * CMP_STRIDE + (CMP_LEN - 1) <= tq3) & (n_io < n_cmp)
        p_c = _softmax_cols(s_c, m_c)
        oc_ref[g] = _dot(vct_ref[lo:hi, :], p_c)

        p_sum = p_c[:, 0:Q_BLK] + p_c[:, Q_BLK:2 * Q_BLK] + p_c[:, 2 * Q_BLK:3 * Q_BLK]
        imp = _dot_split(ovt_ref[...], p_sum)
        j_io = lax.broadcasted_iota(I32, (n_blk, 1), 0)
        cur = tq >> 6
        valid = j_io <= cur
        forced = (j_io == 0) | (j_io == cur) | (j_io == cur - 1)
        score = jnp.where(valid, imp + jnp.where(forced, FORCE_BONUS, 0.0), -jnp.inf)
        sc_ref[...] = score
        rank = jnp.zeros((n_blk, Q_BLK), F32)
        for b in range(n_blk):
            row = sc_ref[b:b + 1, :]
            beats = (row > score) | ((row == score) & (j_io > b))
            rank = rank + jnp.where(beats, 1.0, 0.0)
        lim_ref[g] = jnp.where((rank < n_sel) & valid, tq, -1)

    m_ref[...] = jnp.full(m_ref.shape, NEG, F32)
    acc_ref[...] = jnp.zeros(acc_ref.shape, F32)
    bpc = SEL_KC // SEL_BLOCK

    n_ch = (t0 + Q_BLK + SEL_KC - 1) // SEL_KC

    def scores(c, g):
        k0 = pl.multiple_of(c * SEL_KC, SEL_KC)
        return _dot_nt(ksel_ref[pl.ds(k0, SEL_KC), g * HEAD_DIM:(g + 1) * HEAD_DIM], qrs[g])

    def sel_step(c, carry):
        key = c * SEL_KC + lax.broadcasted_iota(I32, (SEL_KC, 1), 0)

        def flash(s, g):
            rows = [jnp.broadcast_to(lim_ref[g, pl.ds(c * bpc + u, 1), :], (SEL_BLOCK, Q_BLK))
                    for u in range(bpc)]
            on = key <= jnp.concatenate(rows, axis=0)
            _flash_step(s, on, vselT_ref[c, 2 * g * HEAD_DIM:2 * (g + 1) * HEAD_DIM, :],
                        m_ref.at[g], acc_ref.at[g])

        sb_ref[...] = scores(c, 1)
        flash(sa_ref[...], 0)
        sa_ref[...] = scores(jnp.minimum(c + 1, n_ch - 1), 0)
        flash(sb_ref[...], 1)
        return carry

    sa_ref[...] = scores(0, 0)
    lax.fori_loop(0, n_ch, sel_step, 0)

    small_t = small_ref[...].T
    heads = []
    for g in range(NSA_KV_HEADS):
        lo, hi = g * HEAD_DIM, (g + 1) * HEAD_DIM
        acc = acc_ref[g]
        o_s = acc[:HEAD_DIM] / jnp.maximum(acc[HEAD_DIM:HEAD_DIM + 1], 1e-30)
        o_c = oc_ref[g]

        nband = WINDOW // Q_BLK + 1
        cb = jnp.maximum(i - WINDOW // Q_BLK, 0)
        b0 = pl.multiple_of(cb * Q_BLK, Q_BLK)
        s_w = _dot_nt(kwin_ref[pl.ds(b0, nband * Q_BLK), lo:hi], qrs[g])
        key = b0 + lax.broadcasted_iota(I32, (nband * Q_BLK, 1), 0)
        diff = tq3 - key
        p_w = _softmax_cols(s_w, (diff >= 0) & (diff < WINDOW))
        o_w = jnp.zeros((HEAD_DIM, NSA_GROUP * Q_BLK), F32)
        for u in range(nband):
            o_w = o_w + _dot(vwinT_ref[cb + u, lo:hi, :], p_w[u * Q_BLK:(u + 1) * Q_BLK, :])

        for r in range(NSA_GROUP):
            gi = SMALL_GATE + (g * NSA_GROUP + r) * 3
            gt = _sigmoid(small_t[gi:gi + 3, :])
            cs = slice(r * Q_BLK, (r + 1) * Q_BLK)
            heads.append(gt[0:1, :] * o_c[:, cs] + gt[1:2, :] * o_s[:, cs] + gt[2:3, :] * o_w[:, cs])
    o_ref[...] = jnp.concatenate(heads, axis=0).T.astype(BF16)


def _nsa(nq, nqr, small, kc, vct, ksel, vselT, kwin, vwinT, ovt, batch, seq):
    n_q = seq // Q_BLK
    n_blk = seq // SEL_BLOCK
    n_cmp = (seq - CMP_LEN) // CMP_STRIDE + 1
    nh = kc.shape[1]
    blk = lambda n: pl.BlockSpec((Q_BLK, n), lambda b, i: (b * n_q + i, 0))
    perb2 = lambda r, c: pl.BlockSpec((None, r, c), lambda b, i: (b, 0, 0))
    perb_rows = pl.BlockSpec((seq, 128), lambda b, i: (b, 0))
    kern = functools.partial(_nsa_kernel, n_blk=n_blk, n_sel=min(SEL_TOP, n_blk), n_cmp=n_cmp)
    n_col = NSA_GROUP * Q_BLK
    return pl.pallas_call(
        kern,
        grid=(batch, n_q),
        in_specs=[blk(384), blk(384), blk(128),
                  perb2(nh, 128), perb2(128, nh),
                  perb_rows,
                  pl.BlockSpec((None, seq // SEL_KC, 256, SEL_KC), lambda b, i: (b, 0, 0, 0)),
                  perb_rows,
                  pl.BlockSpec((None, seq // Q_BLK, 128, Q_BLK), lambda b, i: (b, 0, 0, 0)),
                  pl.BlockSpec(ovt.shape, lambda b, i: (0, 0))],
        out_specs=blk(384),
        out_shape=jax.ShapeDtypeStruct((batch * seq, 384), BF16),
        scratch_shapes=[pltpu.VMEM((n_blk, Q_BLK), F32),
                        pltpu.VMEM((NSA_KV_HEADS, n_blk, Q_BLK), I32),
                        pltpu.VMEM((NSA_KV_HEADS, HEAD_DIM, n_col), F32),
                        pltpu.VMEM((NSA_KV_HEADS, 1, n_col), F32),
                        pltpu.VMEM((NSA_KV_HEADS, 2 * HEAD_DIM, n_col), F32),
                        pltpu.VMEM((SEL_KC, n_col), F32), pltpu.VMEM((SEL_KC, n_col), F32)],
        compiler_params=_cparams(("parallel", "arbitrary")),
        name="nsa_attn",
    )(nq, nqr, small, kc, vct, ksel, vselT, kwin, vwinT, ovt)


PLANE_KEYS = 32 * 8


def _bit_transpose32(rows):
    rows = list(rows)
    j, mask = 16, 0x0000FFFF
    while j:
        m32 = np.array(mask, np.uint32).view(np.int32)
        k = 0
        while k < 32:
            t = (rows[k] ^ lax.shift_right_logical(rows[k + j], np.int32(j))) & m32
            rows[k] = rows[k] ^ t
            rows[k + j] = rows[k + j] ^ (t << j)
            k = (k + j + 1) & ~j
        j >>= 1
        mask = (mask ^ (mask << j)) & 0xFFFFFFFF
    return rows

def _dsa_kernel(dq_ref, iq_ref, small_ref, dk_ref, dvT_ref, ltri_ref, o_ref,
                ord_ref, plane_ref, m_ref, acc_ref, sa_ref, sb_ref, *, k_top):
    i = pl.program_id(1)
    t0 = i * Q_BLK
    kc = SEL_KC
    n_ch = (t0 + Q_BLK + kc - 1) // kc
    tq = t0 + lax.broadcasted_iota(I32, (1, Q_BLK), 1)

    small_t = small_ref[pl.ds(pl.multiple_of(t0, Q_BLK), Q_BLK), :].T
    w_rows = [small_t[SMALL_IW + h:SMALL_IW + h + 1, :] * (IDX_HEADS ** -0.5) for h in range(IDX_HEADS)]
    iq = iq_ref[...]
    iqs = jnp.concatenate([iq[:, h * IDX_DIM:(h + 1) * IDX_DIM] for h in range(IDX_HEADS)], axis=0)

    def score_step(c, carry):
        k0 = pl.multiple_of(c * kc, kc)
        ik = small_ref[pl.ds(k0, kc), :][:, 0:IDX_DIM]
        lg = _dot_nt(ik, iqs)
        sc = jnp.zeros((kc, Q_BLK), F32)
        for h in range(IDX_HEADS):
            sc = sc + w_rows[h] * jnp.maximum(lg[:, h * Q_BLK:(h + 1) * Q_BLK], 0.0)
        sc = jnp.where(sc == 0.0, 0.0, sc)
        bits = lax.bitcast_convert_type(sc, I32)
        ordv = bits ^ ((bits >> 31) & 0x7FFFFFFF)
        key = k0 + lax.broadcasted_iota(I32, (kc, 1), 0)
        ordv = jnp.where(key <= tq, ordv, INT_MIN)
        ord_ref[pl.ds(k0, kc), :] = ordv
        u = ordv ^ INT_MIN
        for g in range(kc // PLANE_KEYS):
            rows = [u[g * PLANE_KEYS + r * 8:g * PLANE_KEYS + (r + 1) * 8, :] for r in range(32)]
            cols = _bit_transpose32(rows)
            w0 = pl.multiple_of((c * (kc // PLANE_KEYS) + g) * 8, 8)
            for b in range(32):
                plane_ref[b, pl.ds(w0, 8), :] = cols[31 - b]
        return carry

    lax.fori_loop(0, n_ch, score_step, 0)

    n_words = plane_ref.shape[1] // 8
    alive = [jnp.broadcast_to(jnp.where(w < n_ch * (kc // PLANE_KEYS), -1, 0).astype(I32), (8, Q_BLK))
             for w in range(n_words)]
    k_rem = jnp.full((1, Q_BLK), k_top, I32)
    prefix = jnp.zeros((1, Q_BLK), I32)

    def popcount_rows(words):
        pcs = [lax.population_count(x) for x in words]
        while len(pcs) > 1:
            pcs = [a + b for a, b in zip(pcs[0::2], pcs[1::2])]
        return jnp.sum(pcs[0], axis=0, keepdims=True)

    for bit in range(31, -1, -1):
        ones = [alive[w] & plane_ref[bit, w * 8:(w + 1) * 8, :] for w in range(n_words)]
        cnt = popcount_rows(ones)
        take = cnt >= k_rem
        keep0 = jnp.broadcast_to(jnp.where(take, 0, -1).astype(I32), (8, Q_BLK))
        alive = [ones[w] ^ (alive[w] & keep0) for w in range(n_words)]
        k_rem = jnp.where(take, k_rem, k_rem - cnt)
        prefix = prefix | jnp.where(take, np.int32(INT_MIN) if bit == 31 else np.int32(1 << bit), 0)
    thr = prefix ^ np.int32(INT_MIN)
    n_eq = popcount_rows(alive)
    short = thr == INT_MIN
    need = jnp.where(short, 0, k_rem)
    thr_all = jnp.where(short, INT_MIN + 1, thr)
    no_cut = jnp.min(jnp.where(short | (n_eq == need), 1.0, 0.0)) > 0.5

    q = dq_ref[...]
    qs = jnp.concatenate([q[:, h * HEAD_DIM:(h + 1) * HEAD_DIM] for h in range(DSA_HEADS)], axis=0)
    m_ref[...] = jnp.full(m_ref.shape, NEG, F32)
    acc_ref[...] = jnp.zeros(acc_ref.shape, F32)

    def sweep(mask_fn):
        def attn_step(c, seen):
            k0 = pl.multiple_of(c * kc, kc)
            on, seen = mask_fn(ord_ref[pl.ds(k0, kc), :], seen)
            vt = dvT_ref[c]
            sb_ref[...] = scores(c, 1)
            _flash_step(sa_ref[...], on[:half], vt[:, :half], m_ref, acc_ref)
            sa_ref[...] = scores(jnp.minimum(c + 1, n_ch - 1), 0)
            _flash_step(sb_ref[...], on[half:], vt[:, half:], m_ref, acc_ref)
            return seen
        sa_ref[...] = scores(0, 0)
        lax.fori_loop(0, n_ch, attn_step, jnp.zeros((1, Q_BLK), F32))

    half = kc // 2

    def scores(c, sub):
        k0 = pl.multiple_of(c * kc + sub * half, half)
        return _dot_nt(dk_ref[pl.ds(k0, half), 0:HEAD_DIM], qs)

    @pl.when(no_cut)
    def _():
        sweep(lambda o, seen: (o >= thr_all, seen))

    @pl.when(jnp.logical_not(no_cut))
    def _():
        need_f = need.astype(F32)

        def cut_mask(o, seen):
            eq = jnp.where(o == thr, 1.0, 0.0)
            before = jnp.dot(ltri_ref[...], eq.astype(BF16), preferred_element_type=F32) + seen
            on = (o > thr) | ((eq > 0.5) & (before < need_f))
            return on, seen + jnp.sum(eq, axis=0, keepdims=True)

        sweep(cut_mask)

    acc = acc_ref[...]
    o_t = acc[:HEAD_DIM] / jnp.maximum(acc[HEAD_DIM:HEAD_DIM + 1], 1e-30)
    heads = [o_t[:, h * Q_BLK:(h + 1) * Q_BLK] for h in range(DSA_HEADS)]
    o_ref[...] = jnp.concatenate(heads, axis=0).T.astype(BF16)


def _dsa(dq, iq, small, dk, dvT, ltri, batch, seq):
    n_q = seq // Q_BLK
    blk = lambda n: pl.BlockSpec((Q_BLK, n), lambda b, i: (b * n_q + i, 0))
    perb_rows = pl.BlockSpec((seq, 128), lambda b, i: (b, 0))
    kern = functools.partial(_dsa_kernel, k_top=min(IDX_TOPK_MAX, seq // 4))
    return pl.pallas_call(
        kern,
        grid=(batch, n_q),
        in_specs=[blk(384), blk(128), perb_rows, perb_rows,
                  pl.BlockSpec((None, seq // SEL_KC, 128, SEL_KC), lambda b, i: (b, 0, 0, 0)),
                  pl.BlockSpec(ltri.shape, lambda b, i: (0, 0))],
        out_specs=blk(384),
        out_shape=jax.ShapeDtypeStruct((batch * seq, 384), BF16),
        scratch_shapes=[pltpu.VMEM((seq, Q_BLK), I32),
                        pltpu.VMEM((32, seq // 32, Q_BLK), I32),
                        pltpu.VMEM((1, DSA_HEADS * Q_BLK), F32),
                        pltpu.VMEM((2 * HEAD_DIM, DSA_HEADS * Q_BLK), F32),
                        pltpu.VMEM((SEL_KC // 2, DSA_HEADS * Q_BLK), F32),
                        pltpu.VMEM((SEL_KC // 2, DSA_HEADS * Q_BLK), F32)],
        compiler_params=_cparams(("parallel", "arbitrary")),
        name="dsa_attn",
    )(dq, iq, small, dk, dvT, ltri)


def _sgu_kernel(z_ref, g_ref, b_ref, w_ref, bs_ref, o_ref):
    z = _gelu(z_ref[...])
    row = lax.broadcasted_iota(I32, (SGU_CHUNK, SGU_CHUNK), 0)
    col = lax.broadcasted_iota(I32, (SGU_CHUNK, SGU_CHUNK), 1)
    outs = []
    for g in range(SGU_GROUPS):
        lo, hi = g * HEAD_DIM, (g + 1) * HEAD_DIM
        u = z[:, lo:hi]
        v = _standardize(z[:, SGU_WIDTH + lo:SGU_WIDTH + hi]) * g_ref[:, lo:hi] + b_ref[:, lo:hi]
        w = jnp.where(row >= col, w_ref[g], 0.0)
        outs.append(u * (_dot(w, v) + bs_ref[:, lo:hi]))
    o_ref[...] = jnp.concatenate(outs, axis=1).astype(BF16)


def _sgu(z, g, b, w, bs):
    t = z.shape[0]
    n_c = bs.shape[0] // SGU_CHUNK
    full = lambda a: pl.BlockSpec(a.shape, lambda i: (0,) * a.ndim)
    return pl.pallas_call(
        _sgu_kernel,
        grid=(t // SGU_CHUNK,),
        in_specs=[pl.BlockSpec((SGU_CHUNK, 2 * SGU_WIDTH), lambda i: (i, 0)),
                  full(g), full(b), full(w), full(bs)],
        out_specs=pl.BlockSpec((SGU_CHUNK, SGU_WIDTH), lambda i: (i, 0)),
        out_shape=jax.ShapeDtypeStruct((t, SGU_WIDTH), BF16),
        compiler_params=_cparams(("parallel",)),
        name="sgu",
    )(z, g, b, w, bs)


def _outproj_kernel(oa_ref, ob_ref, oc_ref, x_ref, mod_ref, w_ref, g_ref, b_ref, o_ref, *, alpha):
    mix = (jnp.dot(oa_ref[...], w_ref[0:384, :], preferred_element_type=F32)
           + jnp.dot(ob_ref[...], w_ref[384:768, :], preferred_element_type=F32)
           + jnp.dot(oc_ref[...], w_ref[768:1024, :], preferred_element_type=F32))
    y = alpha * x_ref[...] + mod_ref[2:3, :] * mix
    o_ref[...] = _standardize(y) * g_ref[...] + b_ref[...]


def _outproj(oa, ob, oc, xf, mod, w, g, b, seq, alpha):
    t, d = xf.shape
    tm = 512
    per_b = seq // tm
    row = lambda n: pl.BlockSpec((tm, n), lambda i: (i, 0))
    full = lambda a: pl.BlockSpec(a.shape, lambda i: (0,) * a.ndim)
    return pl.pallas_call(
        functools.partial(_outproj_kernel, alpha=alpha),
        grid=(t // tm,),
        in_specs=[row(384), row(384), row(256), row(d),
                  pl.BlockSpec((None, 6, d), lambda i: (i // per_b, 0, 0)),
                  full(w), full(g), full(b)],
        out_specs=row(d),
        out_shape=jax.ShapeDtypeStruct((t, d), F32),
        compiler_params=_cparams(("parallel",)),
        name="outproj",
    )(oa, ob, oc, xf, mod, w, g, b)


def _ffn_kernel(x_ref, mod_ref, wg_ref, wu_ref, wd_ref, g_ref, b_ref, o_ref, acc_ref, *, alpha, fc):
    x = x_ref[...]
    hb = (_standardize(x) * (1.0 + mod_ref[4:5, :]) + mod_ref[3:4, :]).astype(BF16)
    d_ff = wg_ref.shape[1]
    for j in range(d_ff // fc):
        cs = slice(j * fc, (j + 1) * fc)
        a = (_silu(jnp.dot(hb, wg_ref[:, cs], preferred_element_type=F32))
             * jnp.dot(hb, wu_ref[:, cs], preferred_element_type=F32)).astype(BF16)
        part = jnp.dot(a, wd_ref[cs, :], preferred_element_type=F32)
        if j == 0:
            acc_ref[...] = part
        else:
            acc_ref[...] += part
    y = alpha * x + mod_ref[5:6, :] * acc_ref[...]
    o_ref[...] = _standardize(y) * g_ref[...] + b_ref[...]


def _ffn(xf, mod, wg, wu, wd, g, b, seq, alpha):
    t, d = xf.shape
    tm = 512
    per_b = seq // tm
    row = pl.BlockSpec((tm, d), lambda i: (i, 0))
    once = lambda a: pl.BlockSpec(a.shape, lambda i: (0,) * a.ndim, pipeline_mode=pl.Buffered(1))
    return pl.pallas_call(
        functools.partial(_ffn_kernel, alpha=alpha, fc=256),
        grid=(t // tm,),
        in_specs=[row, pl.BlockSpec((None, 6, d), lambda i: (i // per_b, 0, 0)),
                  once(wg), once(wu), once(wd), once(g), once(b)],
        out_specs=row,
        out_shape=jax.ShapeDtypeStruct((t, d), F32),
        scratch_shapes=[pltpu.VMEM((tm, d), F32)],
        compiler_params=_cparams(("parallel",)),
        name="ffn",
    )(xf, mod, wg, wu, wd, g, b)


MOE_TM = 1024
MOE_SUB = 128
MOE_BM = 512
MOE_FC = 512


def _route_kernel(x_ref, mod_ref, wr_ref, br_ref, utri_ref, h_ref, gate_ref, pos_ref, post_ref, cnt_ref):
    h = _standardize(x_ref[...]) * (1.0 + mod_ref[4:5, :]) + mod_ref[3:4, :]
    h_ref[...] = h.astype(BF16)
    lane = lax.broadcasted_iota(I32, (1, LANES), 1)
    logits = jnp.dot(h, wr_ref[...], preferred_element_type=F32,
                     precision=lax.Precision.HIGHEST) + br_ref[...]
    lg = jnp.where(lane < N_EXPERTS, logits, -jnp.inf)
    v0 = jnp.max(lg, axis=-1, keepdims=True)
    lane_f = lane.astype(F32)
    i0 = jnp.min(jnp.where(lg == v0, lane_f, float(LANES)), axis=-1, keepdims=True)
    lg1 = jnp.where(lane_f == i0, -jnp.inf, lg)
    v1 = jnp.max(lg1, axis=-1, keepdims=True)
    i1 = jnp.min(jnp.where(lg1 == v1, lane_f, float(LANES)), axis=-1, keepdims=True)
    e1 = jnp.exp(v1 - v0)
    den = 1.0 + e1
    gate_ref[...] = jnp.where(lane_f == i0, 1.0 / den, 0.0) + jnp.where(lane_f == i1, e1 / den, 0.0)
    sel_t = jnp.where((lane_f == i0) | (lane_f == i1), 1.0, 0.0).T
    rank_t = jnp.dot(sel_t.astype(BF16), utri_ref[...], preferred_element_type=F32)
    pos_t = jnp.where(sel_t > 0.5, rank_t, -1.0)
    post_ref[...] = pos_t[0:N_EXPERTS]
    pos_ref[...] = pos_t.T
    cnt_ref[...] = jnp.broadcast_to(jnp.sum(sel_t[0:N_EXPERTS], axis=1, keepdims=True), (N_EXPERTS, LANES))


def _route(xf, mod, wr, br, utri, seq):
    t, d = xf.shape
    tm = MOE_TM
    nt = t // tm
    per_b = seq // tm
    full = lambda a: pl.BlockSpec(a.shape, lambda i: (0,) * a.ndim)
    return pl.pallas_call(
        _route_kernel,
        grid=(nt,),
        in_specs=[pl.BlockSpec((tm, d), lambda i: (i, 0)),
                  pl.BlockSpec((None, 6, d), lambda i: (i // per_b, 0, 0)),
                  full(wr), full(br), full(utri)],
        out_specs=(pl.BlockSpec((tm, d), lambda i: (i, 0)),
                   pl.BlockSpec((tm, LANES), lambda i: (i, 0)),
                   pl.BlockSpec((tm, LANES), lambda i: (i, 0)),
                   pl.BlockSpec((None, N_EXPERTS, tm), lambda i: (i, 0, 0)),
                   pl.BlockSpec((None, N_EXPERTS, LANES), lambda i: (i, 0, 0))),
        out_shape=(jax.ShapeDtypeStruct((t, d), BF16),
                   jax.ShapeDtypeStruct((t, LANES), F32),
                   jax.ShapeDtypeStruct((t, LANES), F32),
                   jax.ShapeDtypeStruct((nt, N_EXPERTS, tm), F32),
                   jax.ShapeDtypeStruct((nt, N_EXPERTS, LANES), F32)),
        compiler_params=_cparams(("parallel",)),
        name="moe_route",
    )(xf, mod, wr, br, utri)


def _moe_schedule(cnt, n_blocks, k_max):
    nt, ne = cnt.shape
    per = MOE_BM // MOE_SUB
    nb = (cnt + MOE_SUB - 1) // MOE_SUB
    nbt = nb.T
    tot = jnp.sum(nbt, axis=1)
    reg = (tot + per - 1) // per * per
    reg_end = jnp.cumsum(reg)
    reg_start = reg_end - reg
    seg_end = jnp.cumsum(nbt, axis=1)
    seg_start = reg_start[:, None] + seg_end - nbt
    j = jnp.arange(n_blocks, dtype=I32)
    e_j = jnp.minimum(jnp.sum(reg_end[None, :] <= j[:, None], axis=1), ne - 1).astype(I32)
    q = j - reg_start[e_j]
    valid_j = q < tot[e_j]
    i_j = jnp.minimum(jnp.sum(seg_end[e_j] <= q[:, None], axis=1), nt - 1).astype(I32)
    s_j = jnp.where(valid_j, q - (seg_end[e_j, i_j] - nbt[e_j, i_j]), -2).astype(I32)
    step_e = e_j[::per]
    step_valid = valid_j[::per].astype(I32)
    cum = jnp.cumsum(nb, axis=1)
    n_tile = cum[:, -1]
    k = jnp.minimum(jnp.arange(k_max, dtype=I32)[None, :], n_tile[:, None] - 1)
    e_k = jnp.sum(cum[:, None, :] <= k[:, :, None], axis=2).astype(I32)
    tile = jnp.arange(nt, dtype=I32)[:, None]
    s_k = k - (jnp.take_along_axis(cum, e_k, axis=1) - jnp.take_along_axis(nb, e_k, axis=1))
    blk_k = seg_start[e_k, tile] + s_k
    valid_k = (jnp.arange(k_max, dtype=I32)[None, :] < n_tile[:, None]).astype(I32)
    return (i_j, e_j, s_j), (step_e, step_valid), (blk_k.astype(I32), e_k, s_k.astype(I32), valid_k)


def _dispatch_kernel(tile_ref, e_ref, s_ref, h_ref, post_ref, xs_ref):
    j = pl.program_id(0)
    row = post_ref[pl.ds(e_ref[j], 1), :]
    want = s_ref[j] * MOE_SUB + lax.broadcasted_iota(I32, (MOE_SUB, 1), 0)
    onehot = jnp.where(row == want.astype(F32), 1.0, 0.0).astype(BF16)
    xs_ref[...] = jnp.dot(onehot, h_ref[...], preferred_element_type=F32).astype(BF16)


def _dispatch(h, post, sched, n_blocks):
    t, d = h.shape
    tm = MOE_TM
    grid_spec = pltpu.PrefetchScalarGridSpec(
        num_scalar_prefetch=3,
        grid=(n_blocks,),
        in_specs=[pl.BlockSpec((tm, d), lambda j, ti, e, s: (ti[j], 0)),
                  pl.BlockSpec((None, N_EXPERTS, tm), lambda j, ti, e, s: (ti[j], 0, 0))],
        out_specs=pl.BlockSpec((MOE_SUB, d), lambda j, ti, e, s: (j, 0)),
    )
    return pl.pallas_call(
        _dispatch_kernel,
        grid_spec=grid_spec,
        out_shape=jax.ShapeDtypeStruct((n_blocks * MOE_SUB, d), BF16),
        compiler_params=_cparams(("arbitrary",)),
        name="moe_dispatch",
    )(*sched, h, post)


def _experts_kernel(e_ref, v_ref, xs_ref, wg_ref, wu_ref, wd_ref, ys_ref, acc_ref):
    jb = pl.program_id(0)
    f = pl.program_id(1)
    last = pl.num_programs(1) - 1

    @pl.when(v_ref[jb] > 0)
    def _():
        xb = xs_ref[...]
        a = (_silu(jnp.dot(xb, wg_ref[...], preferred_element_type=F32))
             * jnp.dot(xb, wu_ref[...], preferred_element_type=F32)).astype(BF16)
        part = jnp.dot(a, wd_ref[...], preferred_element_type=F32)

        @pl.when(f == 0)
        def _():
            acc_ref[...] = part

        @pl.when(f > 0)
        def _():
            acc_ref[...] += part

        @pl.when(f == last)
        def _():
            ys_ref[...] = acc_ref[...].astype(BF16)

    @pl.when((v_ref[jb] == 0) & (f == last))
    def _():
        ys_ref[...] = jnp.zeros(ys_ref.shape, BF16)


def _experts(xs, wg, wu, wd, sched):
    rows, d = xs.shape
    n_e, _, d_ff = wg.shape
    fc = MOE_FC
    n_f = d_ff // fc
    fsel = lambda jb, f, e, v: jnp.where(v[jb] > 0, f, n_f - 1)
    grid_spec = pltpu.PrefetchScalarGridSpec(
        num_scalar_prefetch=2,
        grid=(rows // MOE_BM, n_f),
        in_specs=[pl.BlockSpec((MOE_BM, d), lambda jb, f, e, v: (jb, 0)),
                  pl.BlockSpec((None, d, fc), lambda jb, f, e, v: (e[jb], 0, fsel(jb, f, e, v))),
                  pl.BlockSpec((None, d, fc), lambda jb, f, e, v: (e[jb], 0, fsel(jb, f, e, v))),
                  pl.BlockSpec((None, fc, d), lambda jb, f, e, v: (e[jb], fsel(jb, f, e, v), 0))],
        out_specs=pl.BlockSpec((MOE_BM, d), lambda jb, f, e, v: (jb, 0)),
        scratch_shapes=[pltpu.VMEM((MOE_BM, d), F32)],
    )
    return pl.pallas_call(
        _experts_kernel,
        grid_spec=grid_spec,
        out_shape=jax.ShapeDtypeStruct((rows, d), BF16),
        compiler_params=_cparams(("arbitrary", "arbitrary")),
        name="moe_experts",
    )(*sched, xs, wg, wu, wd)


def _combine_kernel(blk_ref, e_ref, s_ref, v_ref, ys_ref, pos_ref, gate_ref, x_ref, mod_ref, g_ref, b_ref,
                    o_ref, acc_ref, *, alpha):
    i = pl.program_id(0)
    k = pl.program_id(1)

    @pl.when(k == 0)
    def _():
        acc_ref[...] = jnp.zeros(acc_ref.shape, F32)

    @pl.when(v_ref[i, k] > 0)
    def _():
        lane = lax.broadcasted_iota(I32, (1, LANES), 1)
        mine = lane == e_ref[i, k]
        pcol = jnp.sum(jnp.where(mine, pos_ref[...], 0.0), axis=-1, keepdims=True)
        gcol = jnp.sum(jnp.where(mine, gate_ref[...], 0.0), axis=-1, keepdims=True)
        want = (s_ref[i, k] * MOE_SUB + lane).astype(F32)
        onehot = jnp.where(pcol == want, 1.0, 0.0).astype(BF16)
        acc_ref[...] += gcol * jnp.dot(onehot, ys_ref[...], preferred_element_type=F32)

    @pl.when(k == pl.num_programs(1) - 1)
    def _():
        y = alpha * x_ref[...] + mod_ref[5:6, :] * acc_ref[...]
        o_ref[...] = _standardize(y) * g_ref[...] + b_ref[...]


def _combine(ys, pos, gates, xf, mod, g, b, sched, seq, alpha):
    t, d = xf.shape
    tm = MOE_TM
    per_b = seq // tm
    k_max = sched[0].shape[1]
    row = lambda n: pl.BlockSpec((tm, n), lambda i, k, *_: (i, 0))
    full = lambda a: pl.BlockSpec(a.shape, lambda i, k, *_: (0,) * a.ndim)
    grid_spec = pltpu.PrefetchScalarGridSpec(
        num_scalar_prefetch=4,
        grid=(t // tm, k_max),
        in_specs=[pl.BlockSpec((MOE_SUB, d), lambda i, k, blk, e, s, v: (blk[i, k], 0)),
                  row(LANES), row(LANES), row(d),
                  pl.BlockSpec((None, 6, d), lambda i, k, *_: (i // per_b, 0, 0)),
                  full(g), full(b)],
        out_specs=row(d),
        scratch_shapes=[pltpu.VMEM((tm, d), F32)],
    )
    return pl.pallas_call(
        functools.partial(_combine_kernel, alpha=alpha),
        grid_spec=grid_spec,
        out_shape=jax.ShapeDtypeStruct((t, d), F32),
        compiler_params=_cparams(("arbitrary", "arbitrary")),
        name="moe_combine",
    )(*sched, ys, pos, gates, xf, mod, g, b)


def _moe(xf, mod, wr, br, wg, wu, wd, g, b, seq, alpha):
    t, d = xf.shape
    nt = t // MOE_TM
    per = MOE_BM // MOE_SUB
    n_blocks = TOP_K * t // MOE_SUB + nt * N_EXPERTS + N_EXPERTS * (per - 1)
    n_blocks = (n_blocks + per - 1) // per * per
    k_max = TOP_K * MOE_TM // MOE_SUB + N_EXPERTS
    utri = jnp.asarray(np.triu(np.ones((MOE_TM, MOE_TM), np.float32), 1), BF16)
    h, gates, pos, post, cnt = _route(xf, mod, wr, br, utri, seq)
    sched_d, sched_x, sched_c = _moe_schedule(cnt[:, :, 0].astype(I32), n_blocks, k_max)
    xs = _dispatch(h, post, sched_d, n_blocks)
    ys = _experts(xs, wg, wu, wd, sched_x)
    return _combine(ys, pos, gates, xf, mod, g, b, sched_c, seq, alpha)


def _inproj_columns():
    sizes = (NSA_WIDTH, 6 * NSA_KV_HEADS * HEAD_DIM, 3 * NSA_HEADS, DSA_WIDTH, DSA_LATENT,
             IDX_HEADS * IDX_DIM, IDX_DIM, IDX_HEADS, 2 * SGU_WIDTH)
    starts = np.concatenate([[0], np.cumsum(sizes)])
    o_nq, o_kv, o_g, o_dq, o_ckv, o_iq, o_ik, o_iw, o_sgu = starts[:-1]
    perm = np.full((C_TOTAL,), -1, np.int64)

    def swap(base, n, dim):
        idx = np.arange(n)
        return base + (idx // dim) * dim + (idx % dim + dim // 2) % dim

    perm[C_NQ:C_NQ + 384] = o_nq + np.arange(384)
    perm[C_KV:C_KV + 768] = o_kv + np.arange(768)
    perm[C_DQ:C_DQ + 384] = o_dq + np.arange(384)
    perm[C_CKV:C_CKV + 128] = o_ckv + np.arange(128)
    perm[C_IQ:C_IQ + 128] = o_iq + np.arange(128)
    perm[C_SGU:C_SGU + 512] = o_sgu + np.arange(512)
    perm[C_SMALL:C_SMALL + IDX_DIM] = o_ik + np.arange(IDX_DIM)
    perm[C_SMALL + SMALL_IW:C_SMALL + SMALL_IW + IDX_HEADS] = o_iw + np.arange(IDX_HEADS)
    perm[C_SMALL + SMALL_GATE:C_SMALL + SMALL_GATE + 3 * NSA_HEADS] = o_g + np.arange(3 * NSA_HEADS)
    perm[C_NQ_SW:C_NQ_SW + 384] = swap(o_nq, 384, HEAD_DIM)
    perm[C_KSEL_SW:C_KSEL_SW + 128] = swap(o_kv + 256, 128, HEAD_DIM)
    perm[C_KWIN_SW:C_KWIN_SW + 128] = swap(o_kv + 512, 128, HEAD_DIM)
    perm[C_DQ_SW:C_DQ_SW + 384] = swap(o_dq, 384, HEAD_DIM)
    perm[C_IQ_SW:C_IQ_SW + 128] = swap(o_iq, 128, IDX_DIM)
    perm[C_SMALL_SW:C_SMALL_SW + IDX_DIM] = swap(o_ik, IDX_DIM, IDX_DIM)
    return perm, int(starts[-1])


def _swap_halves(w):
    half = w.shape[-1] // 2
    return jnp.concatenate([w[..., half:], w[..., :half]], axis=-1)


def _compress_weights(pos, w1, w2):
    half = CMP_LEN // 2
    eye_g = jnp.eye(NSA_KV_HEADS, dtype=F32)
    eye_j = jnp.eye(2, dtype=F32)

    def big(w1_half):
        w = jnp.einsum('jlde,jk,gh->ljgdkhe', w1_half, eye_j, eye_g)
        return w.reshape(half * 2 * NSA_KV_HEADS * HEAD_DIM, 2 * NSA_KV_HEADS * HEAD_DIM)

    def posrow(p_half):
        p = jnp.broadcast_to(p_half.transpose(1, 0, 2)[:, :, None, :], (half, 2, NSA_KV_HEADS, HEAD_DIM))
        return p.reshape(1, -1)

    w2big = jnp.einsum('jef,jk,gh->jgekhf', w2, eye_j, eye_g).reshape(256, 256)
    return (posrow(pos[:, :half]), posrow(pos[:, half:]),
            big(w1[:, :half]).astype(BF16), big(w1[:, half:]).astype(BF16), w2big.astype(BF16))


def _rope_tables(positions):
    def tab(dim, reps):
        inv = ROPE_THETA ** (-jnp.arange(0, dim, 2, dtype=F32) / dim)
        ang = positions.astype(F32)[..., None] * inv
        cos, sin = jnp.cos(ang), jnp.sin(ang)
        cos = jnp.tile(jnp.concatenate([cos, cos], axis=-1), (1, 1, reps))
        sin = jnp.tile(jnp.concatenate([-sin, sin], axis=-1), (1, 1, reps))
        return cos.reshape(-1, LANES), sin.reshape(-1, LANES)
    cos_h, sin_h = tab(HEAD_DIM, LANES // HEAD_DIM)
    cos_i, sin_i = tab(IDX_DIM, LANES // IDX_DIM)
    return cos_h, sin_h, cos_i, sin_i


def kernel(x, c, positions, w_ada, b_ada, w_in, nsa_cmp_pos, nsa_cmp_w1, nsa_cmp_w2, dsa_kv_norm, dsa_w_uk, dsa_w_uv, sgu_norm_g, sgu_norm_b, sgu_w, sgu_b, w_out, ln1_g, ln1_b, ln2_g, ln2_b, ffn_w_gate, ffn_w_up, ffn_w_down, moe_w_router, moe_b_router, moe_w_gate, moe_w_up, moe_w_down):
    batch, seq, d = x.shape
    depth = w_ada.shape[0]
    t = batch * seq
    alpha = (2 * depth) ** 0.25
    assert seq % 512 == 0 and seq >= WINDOW + Q_BLK

    tabs = _rope_tables(positions)
    mod_all = _adaln(c, w_ada, b_ada).reshape(depth, batch, 6, d)
    perm, in_width = _inproj_columns()
    perm = jnp.asarray(np.where(perm < 0, in_width, perm), I32)

    n_blk = seq // SEL_BLOCK
    n_half = seq // CMP_STRIDE
    cmp_start = np.arange(n_half)[None, :] * CMP_STRIDE
    blk_start = np.arange(n_blk)[:, None] * SEL_BLOCK
    ovt = jnp.asarray((cmp_start < blk_start + SEL_BLOCK) & (cmp_start + CMP_LEN > blk_start), BF16)
    ltri = jnp.asarray(np.tril(np.ones((SEL_KC, SEL_KC), np.float32), -1), BF16)

    xf = x.reshape(t, d)
    for layer in range(depth):
        mod = mod_all[layer]
        w_pad = jnp.concatenate([w_in[layer], jnp.zeros((d, 1), F32)], axis=1)
        w_ext = jnp.take(w_pad, perm, axis=1).astype(BF16)
        wkv = jnp.concatenate([dsa_w_uk[layer], dsa_w_uv[layer], _swap_halves(dsa_w_uk[layer]),
                               jnp.zeros((DSA_LATENT, HEAD_DIM), F32)], axis=1).astype(BF16)
        (nq, nqr, kvcmp, ksel, vselT, kwin, vwinT, dq, dk, dvT, iq, small, sgu_z) = _inproj(
            xf, mod, tabs, w_ext, wkv, dsa_kv_norm[layer].reshape(1, -1), batch, seq)

        plo, phi, wlo, whi, w2big = _compress_weights(nsa_cmp_pos[layer], nsa_cmp_w1[layer], nsa_cmp_w2[layer])
        kc, vct = _compress(kvcmp.reshape(batch, n_half, CMP_STRIDE * 256), plo, phi, wlo, whi, w2big)
        o_a = _nsa(nq, nqr, small, kc, vct, ksel, vselT, kwin, vwinT, ovt, batch, seq)
        o_b = _dsa(dq, iq, small, dk, dvT, ltri, batch, seq)
        bs = jnp.repeat(sgu_b[layer].T, HEAD_DIM, axis=1)
        o_c = _sgu(sgu_z, sgu_norm_g[layer].reshape(1, -1), sgu_norm_b[layer].reshape(1, -1),
                   sgu_w[layer], bs)
        xf = _outproj(o_a, o_b, o_c, xf, mod, w_out[layer].astype(BF16),
                      ln1_g[layer].reshape(1, -1), ln1_b[layer].reshape(1, -1), seq, alpha)

        j = layer // 2
        g2, b2 = ln2_g[layer].reshape(1, -1), ln2_b[layer].reshape(1, -1)
        if layer % 2 == 0:
            xf = _ffn(xf, mod, ffn_w_gate[j].astype(BF16), ffn_w_up[j].astype(BF16),
                      ffn_w_down[j].astype(BF16), g2, b2, seq, alpha)
        else:
            wr = jnp.pad(moe_w_router[j], ((0, 0), (0, LANES - N_EXPERTS)))
            br = jnp.pad(moe_b_router[j], (0, LANES - N_EXPERTS)).reshape(1, -1)
            xf = _moe(xf, mod, wr, br, moe_w_gate[j].astype(BF16), moe_w_up[j].astype(BF16),
                      moe_w_down[j].astype(BF16), g2, b2, seq, alpha)
    return xf.reshape(batch, seq, d)
```

```python
import functools

import numpy as np
import jax
import jax.numpy as jnp
from jax import lax
from jax.experimental import pallas as pl
from jax.experimental.pallas import tpu as pltpu

F32 = jnp.float32
BF16 = jnp.bfloat16
I32 = jnp.int32

HEAD_DIM = 64
Q_BLK = 128
ROPE_THETA = 10000.0
LN_EPS = 1e-5
RMS_EPS = 1e-6

NSA_HEADS = 6
NSA_KV_HEADS = 2
NSA_GROUP = NSA_HEADS // NSA_KV_HEADS
NSA_WIDTH = NSA_HEADS * HEAD_DIM
CMP_LEN = 32
CMP_STRIDE = 16
SEL_BLOCK = 64
SEL_TOP = 16
WINDOW = 512
FORCE_BONUS = 1e4

DSA_HEADS = 6
DSA_WIDTH = DSA_HEADS * HEAD_DIM
DSA_LATENT = 128
IDX_HEADS = 4
IDX_DIM = 32
IDX_TOPK_MAX = 256

SGU_GROUPS = 4
SGU_CHUNK = 128
SGU_WIDTH = SGU_GROUPS * HEAD_DIM

N_EXPERTS = 8
TOP_K = 2

LANES = 128
VMEM_LIMIT = 56 * 1024 * 1024
NEG = -1e30
INT_MIN = -(2 ** 31)

C_NQ = 0
C_KV = 384
C_DQ = 1152
C_CKV = 1536
C_IQ = 1664
C_SGU = 1792
C_SMALL = 2304
C_MAIN = 2432
C_NQ_SW = 2432
C_KSEL_SW = 2816
C_KWIN_SW = 2944
C_DQ_SW = 3072
C_IQ_SW = 3456
C_SMALL_SW = 3584
C_TOTAL = 3712
SMALL_IW = 32
SMALL_GATE = 36

SEL_KC = 512
Q_SCALE = HEAD_DIM ** -0.5 * 1.4426950408889634


def _cparams(sem):
    return pltpu.CompilerParams(dimension_semantics=sem, vmem_limit_bytes=VMEM_LIMIT)


def _dot(a, b):
    return jnp.dot(a.astype(BF16), b.astype(BF16), preferred_element_type=F32)


def _dot_nt(a, b):
    return lax.dot_general(a.astype(BF16), b.astype(BF16), (((1,), (1,)), ((), ())),
                           preferred_element_type=F32)


def _gelu(x):
    return 0.5 * x * (1.0 + jnp.tanh(0.7978845608028654 * (x + 0.044715 * (x * x * x))))


def _silu(x):
    return x * (1.0 / (1.0 + jnp.exp(-x)))


def _sigmoid(x):
    return 1.0 / (1.0 + jnp.exp(-x))


def _standardize(x):
    mu = jnp.mean(x, axis=-1, keepdims=True)
    xc = x - mu
    var = jnp.mean(xc * xc, axis=-1, keepdims=True)
    return xc * lax.rsqrt(var + LN_EPS)


def _adaln_kernel(c_ref, w_ref, b_ref, o_ref):
    c = c_ref[...]
    o_ref[...] = jnp.dot(_silu(c), w_ref[...], preferred_element_type=F32,
                         precision=lax.Precision.HIGHEST) + b_ref[...]


def _adaln(c, w_ada, b_ada):
    depth, d, n = w_ada.shape
    b = c.shape[0]
    tn = 512
    return pl.pallas_call(
        _adaln_kernel,
        grid=(depth, n // tn),
        in_specs=[pl.BlockSpec((b, d), lambda l, j: (0, 0)),
                  pl.BlockSpec((None, d, tn), lambda l, j: (l, 0, j)),
                  pl.BlockSpec((None, 1, tn), lambda l, j: (l, 0, j))],
        out_specs=pl.BlockSpec((None, b, tn), lambda l, j: (l, 0, j)),
        out_shape=jax.ShapeDtypeStruct((depth, b, n), F32),
        compiler_params=_cparams(("arbitrary", "arbitrary")),
        name="adaln",
    )(c, w_ada, b_ada.reshape(depth, 1, n))


def _inproj_kernel(x_ref, mod_ref, cos_ref, sin_ref, cosi_ref, sini_ref, w_ref, wkv_ref, kvg_ref,
                   nq_ref, nqr_ref, kvcmp_ref, ksel_ref, vselT_ref, kwin_ref, vwinT_ref,
                   dq_ref, dk_ref, dvT_ref, iq_ref, small_ref, sgu_ref):
    tm = x_ref.shape[0]
    shift = mod_ref[0:1, :]
    scale = mod_ref[1:2, :]
    hb = (_standardize(x_ref[...]) * (1.0 + scale) + shift).astype(BF16)

    def proj(c0, n):
        return jnp.dot(hb, w_ref[:, c0:c0 + n], preferred_element_type=F32)

    cos = cos_ref[...]
    sin = sin_ref[...]
    cos3 = jnp.concatenate([cos, cos, cos], axis=1)
    sin3 = jnp.concatenate([sin, sin, sin], axis=1)
    lane = lax.broadcasted_iota(I32, (1, LANES), 1)

    zq = proj(C_NQ, 384)
    nq_ref[...] = (zq * Q_SCALE).astype(BF16)
    nqr_ref[...] = ((zq * cos3 + proj(C_NQ_SW, 384) * sin3) * Q_SCALE).astype(BF16)

    kvcmp_ref[...] = proj(C_KV, 256)
    ksel_ref[...] = (proj(C_KV + 256, 128) * cos + proj(C_KSEL_SW, 128) * sin).astype(BF16)
    ones_t = jnp.ones((HEAD_DIM, tm), F32)
    vsel_t = proj(C_KV + 384, 128).T
    vsel_x = jnp.concatenate([vsel_t[:HEAD_DIM], ones_t, vsel_t[HEAD_DIM:], ones_t], axis=0).astype(BF16)
    for j in range(tm // SEL_KC):
        vselT_ref[j] = vsel_x[:, j * SEL_KC:(j + 1) * SEL_KC]
    kwin_ref[...] = (proj(C_KV + 512, 128) * cos + proj(C_KWIN_SW, 128) * sin).astype(BF16)
    vwin_t = proj(C_KV + 640, 128).T
    vwin_x = jnp.concatenate([vwin_t[:HEAD_DIM], ones_t, vwin_t[HEAD_DIM:], ones_t], axis=0).astype(BF16)
    for j in range(tm // Q_BLK):
        vwinT_ref[j] = vwin_x[:, j * Q_BLK:(j + 1) * Q_BLK]

    dq_ref[...] = ((proj(C_DQ, 384) * cos3 + proj(C_DQ_SW, 384) * sin3) * Q_SCALE).astype(BF16)

    ckv = proj(C_CKV, 128)
    ckv = ckv * lax.rsqrt(jnp.mean(ckv * ckv, axis=-1, keepdims=True) + RMS_EPS) * kvg_ref[...]
    kd = jnp.dot(ckv.astype(BF16), wkv_ref[...], preferred_element_type=F32)
    first = lane < HEAD_DIM
    dkv = kd[:, :128] * jnp.where(first, cos, 1.0) + kd[:, 128:] * jnp.where(first, sin, 0.0)
    dk_ref[...] = dkv.astype(BF16)
    dv_x = jnp.concatenate([dkv.T[HEAD_DIM:], ones_t], axis=0).astype(BF16)
    for j in range(tm // SEL_KC):
        dvT_ref[j] = dv_x[:, j * SEL_KC:(j + 1) * SEL_KC]

    cosi = cosi_ref[...]
    sini = sini_ref[...]
    iq_ref[...] = (proj(C_IQ, 128) * cosi + proj(C_IQ_SW, 128) * sini).astype(BF16)
    isk = lane < IDX_DIM
    small_ref[...] = (proj(C_SMALL, 128) * jnp.where(isk, cosi, 1.0)
                      + proj(C_SMALL_SW, 128) * jnp.where(isk, sini, 0.0))
    sgu_ref[...] = proj(C_SGU, 512)


def _inproj(xf, mod, tabs, w_ext, wkv, kvg, batch, seq):
    t, d = xf.shape
    tm = 512
    per_b = seq // tm
    row = lambda n: pl.BlockSpec((tm, n), lambda i: (i, 0))
    trk = lambda r: pl.BlockSpec((None, tm // SEL_KC, r, SEL_KC), lambda i: (i // per_b, i % per_b, 0, 0))
    tr128 = pl.BlockSpec((None, tm // Q_BLK, 256, Q_BLK), lambda i: (i // per_b, i % per_b, 0, 0))
    full = lambda a: pl.BlockSpec(a.shape, lambda i: (0,) * a.ndim)
    out_shape = (
        jax.ShapeDtypeStruct((t, 384), BF16),
        jax.ShapeDtypeStruct((t, 384), BF16),
        jax.ShapeDtypeStruct((t, 256), F32),
        jax.ShapeDtypeStruct((t, 128), BF16),
        jax.ShapeDtypeStruct((batch, seq // SEL_KC, 256, SEL_KC), BF16),
        jax.ShapeDtypeStruct((t, 128), BF16),
        jax.ShapeDtypeStruct((batch, seq // Q_BLK, 256, Q_BLK), BF16),
        jax.ShapeDtypeStruct((t, 384), BF16),
        jax.ShapeDtypeStruct((t, 128), BF16),
        jax.ShapeDtypeStruct((batch, seq // SEL_KC, 128, SEL_KC), BF16),
        jax.ShapeDtypeStruct((t, 128), BF16),
        jax.ShapeDtypeStruct((t, 128), F32),
        jax.ShapeDtypeStruct((t, 512), F32),
    )
    out_specs = (row(384), row(384), row(256), row(128), trk(256), row(128), tr128,
                 row(384), row(128), trk(128), row(128), row(128), row(512))
    return pl.pallas_call(
        _inproj_kernel,
        grid=(t // tm,),
        in_specs=[row(d),
                  pl.BlockSpec((None, 6, d), lambda i: (i // per_b, 0, 0)),
                  row(128), row(128), row(128), row(128),
                  full(w_ext), full(wkv), full(kvg)],
        out_specs=out_specs,
        out_shape=out_shape,
        compiler_params=_cparams(("parallel",)),
        name="inproj",
    )(xf, mod, *tabs, w_ext, wkv, kvg)


def _compress_kernel(h_ref, plo_ref, phi_ref, wlo_ref, whi_ref, w2_ref, kc_ref, vct_ref):
    h = h_ref[...]
    a = _dot(h + plo_ref[...], wlo_ref[...])
    b = _dot(h + phi_ref[...], whi_ref[...])
    nh = h.shape[0]
    pre = a + pltpu.roll(b, nh - 1, 0)
    cmp = _dot(_gelu(pre), w2_ref[...])
    kc_ref[...] = cmp[:, :128].astype(BF16)
    v_t = cmp[:, 128:].T
    ones_t = jnp.ones((HEAD_DIM, nh), F32)
    vct_ref[...] = jnp.concatenate([v_t[:HEAD_DIM], ones_t, v_t[HEAD_DIM:], ones_t], axis=0).astype(BF16)


def _compress(kvcmp_h, plo, phi, wlo, whi, w2):
    batch, nh, width = kvcmp_h.shape
    full = lambda a: pl.BlockSpec(a.shape, lambda b: (0,) * a.ndim)
    return pl.pallas_call(
        _compress_kernel,
        grid=(batch,),
        in_specs=[pl.BlockSpec((None, nh, width), lambda b: (b, 0, 0)),
                  full(plo), full(phi), full(wlo), full(whi), full(w2)],
        out_specs=(pl.BlockSpec((None, nh, 128), lambda b: (b, 0, 0)),
                   pl.BlockSpec((None, 256, nh), lambda b: (b, 0, 0))),
        out_shape=(jax.ShapeDtypeStruct((batch, nh, 128), BF16),
                   jax.ShapeDtypeStruct((batch, 256, nh), BF16)),
        compiler_params=_cparams(("parallel",)),
        name="nsa_compress",
    )(kvcmp_h, plo, phi, wlo, whi, w2)


def _softmax_pv(s, mask, vt_ones):
    sm = jnp.where(mask, s, NEG)
    m = jnp.max(sm, axis=0, keepdims=True)
    p = jnp.exp2(sm - m).astype(BF16)
    pv = jnp.dot(vt_ones, p, preferred_element_type=F32)
    dh = vt_ones.shape[0] // 2
    inv = jnp.where(m > 0.5 * NEG, 1.0 / jnp.maximum(pv[dh:dh + 1], 1e-30), 0.0)
    return p, inv, pv[:dh]


def _flash_step(s, on, vt_ones, m_ref, acc_ref):
    m_new, acc_new = _flash_update(s, on, vt_ones, m_ref[...], acc_ref[...])
    acc_ref[...] = acc_new
    m_ref[...] = m_new


def _flash_update(s, on, vt_ones, m_old, acc_old):
    n_grp = s.shape[1] // Q_BLK
    sm = [jnp.where(on, s[:, j * Q_BLK:(j + 1) * Q_BLK], NEG) for j in range(n_grp)]
    m_new = jnp.maximum(m_old, jnp.concatenate([jnp.max(x, axis=0, keepdims=True) for x in sm], axis=1))
    p = jnp.concatenate([jnp.exp2(sm[j] - m_new[:, j * Q_BLK:(j + 1) * Q_BLK]).astype(BF16)
                         for j in range(n_grp)], axis=1)
    acc_new = jnp.exp2(m_old - m_new) * acc_old + jnp.dot(vt_ones, p, preferred_element_type=F32)
    return m_new, acc_new


def _nsa_kernel(nq_ref, nqr_ref, small_ref, kc_ref, vct_ref, ksel_ref, vselT_ref, kwin_ref,
                vwinT_ref, ovt_ref, o_ref, sc_ref, lim_ref, oc_ref, m_ref, acc_ref, sa_ref, sb_ref,
                *, n_blk, n_sel, n_cmp):
    i = pl.program_id(1)
    t0 = i * Q_BLK
    tq = t0 + lax.broadcasted_iota(I32, (1, Q_BLK), 1)
    tq3 = jnp.concatenate([tq, tq, tq], axis=1)
    nq = nq_ref[...]
    nqr = nqr_ref[...]
    nh = kc_ref.shape[0]

    def stack_heads(q, g):
        hs = [g * NSA_GROUP + r for r in range(NSA_GROUP)]
        return jnp.concatenate([q[:, h * HEAD_DIM:(h + 1) * HEAD_DIM] for h in hs], axis=0)

    qrs = [stack_heads(nqr, g) for g in range(NSA_KV_HEADS)]
    for g in range(NSA_KV_HEADS):
        lo, hi = g * HEAD_DIM, (g + 1) * HEAD_DIM
        s_c = _dot_nt(kc_ref[:, lo:hi], stack_heads(nq, g))
        n_io = lax.broadcasted_iota(I32, (nh, 1), 0)
        m_c = (n_io * CMP_STRIDE + (CMP_LEN - 1) <= tq3) & (n_io < n_cmp)
        p_c, inv_c, oc = _softmax_pv(s_c, m_c, vct_ref[2 * lo:2 * hi, :])
        oc_ref[g] = oc * inv_c

        imp = jnp.zeros((n_blk, Q_BLK), F32)
        for r in range(NSA_GROUP):
            cs = slice(r * Q_BLK, (r + 1) * Q_BLK)
            imp = imp + jnp.dot(ovt_ref[...], p_c[:, cs], preferred_element_type=F32) * inv_c[:, cs]
        j_io = lax.broadcasted_iota(I32, (n_blk, 1), 0)
        cur = tq >> 6
        valid = j_io <= cur
        forced = (j_io == 0) | (j_io == cur) | (j_io == cur - 1)
        score = jnp.where(valid, imp + jnp.where(forced, FORCE_BONUS, 0.0), -jnp.inf)
        sc_ref[...] = score
        rank = jnp.zeros((n_blk, Q_BLK), F32)
        for b in range(n_blk):
            row = sc_ref[b:b + 1, :]
            beats = (row > score) | ((row == score) & (j_io > b))
            rank = rank + jnp.where(beats, 1.0, 0.0)
        lim_ref[g] = jnp.where((rank < n_sel) & valid, tq, -1)

    m_ref[...] = jnp.full(m_ref.shape, NEG, F32)
    acc_ref[...] = jnp.zeros(acc_ref.shape, F32)
    bpc = SEL_KC // SEL_BLOCK

    n_ch = (t0 + Q_BLK + SEL_KC - 1) // SEL_KC

    def scores(c, g):
        k0 = pl.multiple_of(c * SEL_KC, SEL_KC)
        return _dot_nt(ksel_ref[pl.ds(k0, SEL_KC), g * HEAD_DIM:(g + 1) * HEAD_DIM], qrs[g])

    def sel_step(c, carry):
        key = c * SEL_KC + lax.broadcasted_iota(I32, (SEL_KC, 1), 0)

        def flash(s, g):
            rows = [jnp.broadcast_to(lim_ref[g, pl.ds(c * bpc + u, 1), :], (SEL_BLOCK, Q_BLK))
                    for u in range(bpc)]
            on = key <= jnp.concatenate(rows, axis=0)
            _flash_step(s, on, vselT_ref[c, 2 * g * HEAD_DIM:2 * (g + 1) * HEAD_DIM, :],
                        m_ref.at[g], acc_ref.at[g])

        sb_ref[...] = scores(c, 1)
        flash(sa_ref[...], 0)
        sa_ref[...] = scores(jnp.minimum(c + 1, n_ch - 1), 0)
        flash(sb_ref[...], 1)
        return carry

    sa_ref[...] = scores(0, 0)
    lax.fori_loop(0, n_ch, sel_step, 0)

    small_t = small_ref[...].T
    heads = []
    for g in range(NSA_KV_HEADS):
        lo, hi = g * HEAD_DIM, (g + 1) * HEAD_DIM
        acc = acc_ref[g]
        o_s = acc[:HEAD_DIM] / jnp.maximum(acc[HEAD_DIM:HEAD_DIM + 1], 1e-30)
        o_c = oc_ref[g]

        nband = WINDOW // Q_BLK + 1
        cb = jnp.maximum(i - WINDOW // Q_BLK, 0)
        b0 = pl.multiple_of(cb * Q_BLK, Q_BLK)
        s_w = _dot_nt(kwin_ref[pl.ds(b0, nband * Q_BLK), lo:hi], qrs[g])
        key = b0 + lax.broadcasted_iota(I32, (nband * Q_BLK, 1), 0)
        diff = tq3 - key
        vt_w = jnp.concatenate([vwinT_ref[cb + u, 2 * lo:2 * hi, :] for u in range(nband)], axis=1)
        _, inv_w, o_w = _softmax_pv(s_w, (diff >= 0) & (diff < WINDOW), vt_w)
        o_w = o_w * inv_w

        for r in range(NSA_GROUP):
            gi = SMALL_GATE + (g * NSA_GROUP + r) * 3
            gt = _sigmoid(small_t[gi:gi + 3, :])
            cs = slice(r * Q_BLK, (r + 1) * Q_BLK)
            heads.append(gt[0:1, :] * o_c[:, cs] + gt[1:2, :] * o_s[:, cs] + gt[2:3, :] * o_w[:, cs])
    o_ref[...] = jnp.concatenate(heads, axis=0).T.astype(BF16)


def _nsa(nq, nqr, small, kc, vct, ksel, vselT, kwin, vwinT, ovt, batch, seq):
    n_q = seq // Q_BLK
    n_blk = seq // SEL_BLOCK
    n_cmp = (seq - CMP_LEN) // CMP_STRIDE + 1
    nh = kc.shape[1]
    blk = lambda n: pl.BlockSpec((Q_BLK, n), lambda b, i: (b * n_q + i, 0))
    perb2 = lambda r, c: pl.BlockSpec((None, r, c), lambda b, i: (b, 0, 0))
    perb_rows = pl.BlockSpec((seq, 128), lambda b, i: (b, 0))
    kern = functools.partial(_nsa_kernel, n_blk=n_blk, n_sel=min(SEL_TOP, n_blk), n_cmp=n_cmp)
    n_col = NSA_GROUP * Q_BLK
    return pl.pallas_call(
        kern,
        grid=(batch, n_q),
        in_specs=[blk(384), blk(384), blk(128),
                  perb2(nh, 128), perb2(256, nh),
                  perb_rows,
                  pl.BlockSpec((None, seq // SEL_KC, 256, SEL_KC), lambda b, i: (b, 0, 0, 0)),
                  perb_rows,
                  pl.BlockSpec((None, seq // Q_BLK, 256, Q_BLK), lambda b, i: (b, 0, 0, 0)),
                  pl.BlockSpec(ovt.shape, lambda b, i: (0, 0))],
        out_specs=blk(384),
        out_shape=jax.ShapeDtypeStruct((batch * seq, 384), BF16),
        scratch_shapes=[pltpu.VMEM((n_blk, Q_BLK), F32),
                        pltpu.VMEM((NSA_KV_HEADS, n_blk, Q_BLK), I32),
                        pltpu.VMEM((NSA_KV_HEADS, HEAD_DIM, n_col), F32),
                        pltpu.VMEM((NSA_KV_HEADS, 1, n_col), F32),
                        pltpu.VMEM((NSA_KV_HEADS, 2 * HEAD_DIM, n_col), F32),
                        pltpu.VMEM((SEL_KC, n_col), F32), pltpu.VMEM((SEL_KC, n_col), F32)],
        compiler_params=_cparams(("parallel", "arbitrary")),
        name="nsa_attn",
    )(nq, nqr, small, kc, vct, ksel, vselT, kwin, vwinT, ovt)


PLANE_KEYS = 32 * 8


def _bit_transpose32(rows):
    rows = list(rows)
    j, mask = 16, 0x0000FFFF
    while j:
        m32 = np.array(mask, np.uint32).view(np.int32)
        k = 0
        while k < 32:
            t = (rows[k] ^ lax.shift_right_logical(rows[k + j], np.int32(j))) & m32
            rows[k] = rows[k] ^ t
            rows[k + j] = rows[k + j] ^ (t << j)
            k = (k + j + 1) & ~j
        j >>= 1
        mask = (mask ^ (mask << j)) & 0xFFFFFFFF
    return rows

def _dsa_kernel(dq_ref, iq_ref, small_ref, dk_ref, dvT_ref, ltri_ref, o_ref,
                ord_ref, plane_ref, m_ref, acc_ref, sa_ref, sb_ref, *, k_top):
    i = pl.program_id(1)
    t0 = i * Q_BLK
    kc = SEL_KC
    n_ch = (t0 + Q_BLK + kc - 1) // kc
    tq = t0 + lax.broadcasted_iota(I32, (1, Q_BLK), 1)

    small_t = small_ref[pl.ds(pl.multiple_of(t0, Q_BLK), Q_BLK), :].T
    w_rows = [small_t[SMALL_IW + h:SMALL_IW + h + 1, :] * (IDX_HEADS ** -0.5) for h in range(IDX_HEADS)]
    iq = iq_ref[...]
    iqs = jnp.concatenate([iq[:, h * IDX_DIM:(h + 1) * IDX_DIM] for h in range(IDX_HEADS)], axis=0)

    def score_step(c, carry):
        k0 = pl.multiple_of(c * kc, kc)
        ik = small_ref[pl.ds(k0, kc), :][:, 0:IDX_DIM]
        lg = _dot_nt(ik, iqs)
        sc = jnp.zeros((kc, Q_BLK), F32)
        for h in range(IDX_HEADS):
            sc = sc + w_rows[h] * jnp.maximum(lg[:, h * Q_BLK:(h + 1) * Q_BLK], 0.0)
        sc = jnp.where(sc == 0.0, 0.0, sc)
        bits = lax.bitcast_convert_type(sc, I32)
        ordv = bits ^ ((bits >> 31) & 0x7FFFFFFF)
        key = k0 + lax.broadcasted_iota(I32, (kc, 1), 0)
        ordv = jnp.where(key <= tq, ordv, INT_MIN)
        ord_ref[pl.ds(k0, kc), :] = ordv
        u = ordv ^ INT_MIN
        for g in range(kc // PLANE_KEYS):
            rows = [u[g * PLANE_KEYS + r * 8:g * PLANE_KEYS + (r + 1) * 8, :] for r in range(32)]
            cols = _bit_transpose32(rows)
            w0 = pl.multiple_of((c * (kc // PLANE_KEYS) + g) * 8, 8)
            for b in range(32):
                plane_ref[b, pl.ds(w0, 8), :] = cols[31 - b]
        return carry

    lax.fori_loop(0, n_ch, score_step, 0)

    n_words = plane_ref.shape[1] // 8
    alive = [jnp.broadcast_to(jnp.where(w < n_ch * (kc // PLANE_KEYS), -1, 0).astype(I32), (8, Q_BLK))
             for w in range(n_words)]
    k_rem = jnp.full((1, Q_BLK), k_top, I32)
    prefix = jnp.zeros((1, Q_BLK), I32)

    def popcount_rows(words):
        pcs = [lax.population_count(x) for x in words]
        while len(pcs) > 1:
            pcs = [a + b for a, b in zip(pcs[0::2], pcs[1::2])]
        return jnp.sum(pcs[0], axis=0, keepdims=True)

    for bit in range(31, -1, -1):
        ones = [alive[w] & plane_ref[bit, w * 8:(w + 1) * 8, :] for w in range(n_words)]
        cnt = popcount_rows(ones)
        take = cnt >= k_rem
        keep0 = jnp.broadcast_to(jnp.where(take, 0, -1).astype(I32), (8, Q_BLK))
        alive = [ones[w] ^ (alive[w] & keep0) for w in range(n_words)]
        k_rem = jnp.where(take, k_rem, k_rem - cnt)
        prefix = prefix | jnp.where(take, np.int32(INT_MIN) if bit == 31 else np.int32(1 << bit), 0)
    thr = prefix ^ np.int32(INT_MIN)
    n_eq = popcount_rows(alive)
    short = thr == INT_MIN
    need = jnp.where(short, 0, k_rem)
    thr_all = jnp.where(short, INT_MIN + 1, thr)
    no_cut = jnp.min(jnp.where(short | (n_eq == need), 1.0, 0.0)) > 0.5

    q = dq_ref[...]
    qs = jnp.concatenate([q[:, h * HEAD_DIM:(h + 1) * HEAD_DIM] for h in range(DSA_HEADS)], axis=0)
    m_ref[...] = jnp.full(m_ref.shape, NEG, F32)
    acc_ref[...] = jnp.zeros(acc_ref.shape, F32)

    def sweep(mask_fn):
        def attn_step(c, seen):
            k0 = pl.multiple_of(c * kc, kc)
            on, seen = mask_fn(ord_ref[pl.ds(k0, kc), :], seen)
            vt = dvT_ref[c]
            sb_ref[...] = scores(c, 1)
            _flash_step(sa_ref[...], on[:half], vt[:, :half], m_ref, acc_ref)
            sa_ref[...] = scores(jnp.minimum(c + 1, n_ch - 1), 0)
            _flash_step(sb_ref[...], on[half:], vt[:, half:], m_ref, acc_ref)
            return seen
        sa_ref[...] = scores(0, 0)
        lax.fori_loop(0, n_ch, attn_step, jnp.zeros((1, Q_BLK), F32))

    half = kc // 2

    def scores(c, sub):
        k0 = pl.multiple_of(c * kc + sub * half, half)
        return _dot_nt(dk_ref[pl.ds(k0, half), 0:HEAD_DIM], qs)

    @pl.when(no_cut)
    def _():
        sweep(lambda o, seen: (o >= thr_all, seen))

    @pl.when(jnp.logical_not(no_cut))
    def _():
        need_f = need.astype(F32)

        def cut_mask(o, seen):
            eq = jnp.where(o == thr, 1.0, 0.0)
            before = jnp.dot(ltri_ref[...], eq.astype(BF16), preferred_element_type=F32) + seen
            on = (o > thr) | ((eq > 0.5) & (before < need_f))
            return on, seen + jnp.sum(eq, axis=0, keepdims=True)

        sweep(cut_mask)

    acc = acc_ref[...]
    o_t = acc[:HEAD_DIM] / jnp.maximum(acc[HEAD_DIM:HEAD_DIM + 1], 1e-30)
    heads = [o_t[:, h * Q_BLK:(h + 1) * Q_BLK] for h in range(DSA_HEADS)]
    o_ref[...] = jnp.concatenate(heads, axis=0).T.astype(BF16)


def _dsa(dq, iq, small, dk, dvT, ltri, batch, seq):
    n_q = seq // Q_BLK
    blk = lambda n: pl.BlockSpec((Q_BLK, n), lambda b, i: (b * n_q + i, 0))
    perb_rows = pl.BlockSpec((seq, 128), lambda b, i: (b, 0))
    kern = functools.partial(_dsa_kernel, k_top=min(IDX_TOPK_MAX, seq // 4))
    return pl.pallas_call(
        kern,
        grid=(batch, n_q),
        in_specs=[blk(384), blk(128), perb_rows, perb_rows,
                  pl.BlockSpec((None, seq // SEL_KC, 128, SEL_KC), lambda b, i: (b, 0, 0, 0)),
                  pl.BlockSpec(ltri.shape, lambda b, i: (0, 0))],
        out_specs=blk(384),
        out_shape=jax.ShapeDtypeStruct((batch * seq, 384), BF16),
        scratch_shapes=[pltpu.VMEM((seq, Q_BLK), I32),
                        pltpu.VMEM((32, seq // 32, Q_BLK), I32),
                        pltpu.VMEM((1, DSA_HEADS * Q_BLK), F32),
                        pltpu.VMEM((2 * HEAD_DIM, DSA_HEADS * Q_BLK), F32),
                        pltpu.VMEM((SEL_KC // 2, DSA_HEADS * Q_BLK), F32),
                        pltpu.VMEM((SEL_KC // 2, DSA_HEADS * Q_BLK), F32)],
        compiler_params=_cparams(("parallel", "arbitrary")),
        name="dsa_attn",
    )(dq, iq, small, dk, dvT, ltri)


def _sgu_kernel(z_ref, g_ref, b_ref, w_ref, bs_ref, o_ref):
    z = _gelu(z_ref[...])
    row = lax.broadcasted_iota(I32, (SGU_CHUNK, SGU_CHUNK), 0)
    col = lax.broadcasted_iota(I32, (SGU_CHUNK, SGU_CHUNK), 1)
    outs = []
    for g in range(SGU_GROUPS):
        lo, hi = g * HEAD_DIM, (g + 1) * HEAD_DIM
        u = z[:, lo:hi]
        v = _standardize(z[:, SGU_WIDTH + lo:SGU_WIDTH + hi]) * g_ref[:, lo:hi] + b_ref[:, lo:hi]
        w = jnp.where(row >= col, w_ref[g], 0.0)
        outs.append(u * (_dot(w, v) + bs_ref[:, lo:hi]))
    o_ref[...] = jnp.concatenate(outs, axis=1).astype(BF16)


def _sgu(z, g, b, w, bs):
    t = z.shape[0]
    n_c = bs.shape[0] // SGU_CHUNK
    full = lambda a: pl.BlockSpec(a.shape, lambda i: (0,) * a.ndim)
    return pl.pallas_call(
        _sgu_kernel,
        grid=(t // SGU_CHUNK,),
        in_specs=[pl.BlockSpec((SGU_CHUNK, 2 * SGU_WIDTH), lambda i: (i, 0)),
                  full(g), full(b), full(w), full(bs)],
        out_specs=pl.BlockSpec((SGU_CHUNK, SGU_WIDTH), lambda i: (i, 0)),
        out_shape=jax.ShapeDtypeStruct((t, SGU_WIDTH), BF16),
        compiler_params=_cparams(("parallel",)),
        name="sgu",
    )(z, g, b, w, bs)


def _outproj_kernel(oa_ref, ob_ref, oc_ref, x_ref, mod_ref, w_ref, g_ref, b_ref, o_ref, *, alpha):
    mix = (jnp.dot(oa_ref[...], w_ref[0:384, :], preferred_element_type=F32)
           + jnp.dot(ob_ref[...], w_ref[384:768, :], preferred_element_type=F32)
           + jnp.dot(oc_ref[...], w_ref[768:1024, :], preferred_element_type=F32))
    y = alpha * x_ref[...] + mod_ref[2:3, :] * mix
    o_ref[...] = _standardize(y) * g_ref[...] + b_ref[...]


def _outproj(oa, ob, oc, xf, mod, w, g, b, seq, alpha):
    t, d = xf.shape
    tm = 512
    per_b = seq // tm
    row = lambda n: pl.BlockSpec((tm, n), lambda i: (i, 0))
    full = lambda a: pl.BlockSpec(a.shape, lambda i: (0,) * a.ndim)
    return pl.pallas_call(
        functools.partial(_outproj_kernel, alpha=alpha),
        grid=(t // tm,),
        in_specs=[row(384), row(384), row(256), row(d),
                  pl.BlockSpec((None, 6, d), lambda i: (i // per_b, 0, 0)),
                  full(w), full(g), full(b)],
        out_specs=row(d),
        out_shape=jax.ShapeDtypeStruct((t, d), F32),
        compiler_params=_cparams(("parallel",)),
        name="outproj",
    )(oa, ob, oc, xf, mod, w, g, b)


def _ffn_kernel(x_ref, mod_ref, wg_ref, wu_ref, wd_ref, g_ref, b_ref, o_ref, acc_ref, *, alpha, fc):
    x = x_ref[...]
    hb = (_standardize(x) * (1.0 + mod_ref[4:5, :]) + mod_ref[3:4, :]).astype(BF16)
    d_ff = wg_ref.shape[1]
    for j in range(d_ff // fc):
        cs = slice(j * fc, (j + 1) * fc)
        a = (_silu(jnp.dot(hb, wg_ref[:, cs], preferred_element_type=F32))
             * jnp.dot(hb, wu_ref[:, cs], preferred_element_type=F32)).astype(BF16)
        part = jnp.dot(a, wd_ref[cs, :], preferred_element_type=F32)
        if j == 0:
            acc_ref[...] = part
        else:
            acc_ref[...] += part
    y = alpha * x + mod_ref[5:6, :] * acc_ref[...]
    o_ref[...] = _standardize(y) * g_ref[...] + b_ref[...]


def _ffn(xf, mod, wg, wu, wd, g, b, seq, alpha):
    t, d = xf.shape
    tm = 512
    per_b = seq // tm
    row = pl.BlockSpec((tm, d), lambda i: (i, 0))
    once = lambda a: pl.BlockSpec(a.shape, lambda i: (0,) * a.ndim, pipeline_mode=pl.Buffered(1))
    return pl.pallas_call(
        functools.partial(_ffn_kernel, alpha=alpha, fc=256),
        grid=(t // tm,),
        in_specs=[row, pl.BlockSpec((None, 6, d), lambda i: (i // per_b, 0, 0)),
                  once(wg), once(wu), once(wd), once(g), once(b)],
        out_specs=row,
        out_shape=jax.ShapeDtypeStruct((t, d), F32),
        scratch_shapes=[pltpu.VMEM((tm, d), F32)],
        compiler_params=_cparams(("parallel",)),
        name="ffn",
    )(xf, mod, wg, wu, wd, g, b)


MOE_TM = 1024
MOE_SUB = 128
MOE_BM = 1024
MOE_FC = 512


def _route_kernel(x_ref, mod_ref, wr_ref, br_ref, utri_ref, h_ref, gate_ref, pos_ref, post_ref, cnt_ref):
    h = _standardize(x_ref[...]) * (1.0 + mod_ref[4:5, :]) + mod_ref[3:4, :]
    h_ref[...] = h.astype(BF16)
    lane = lax.broadcasted_iota(I32, (1, LANES), 1)
    logits = jnp.dot(h, wr_ref[...], preferred_element_type=F32,
                     precision=lax.Precision.HIGHEST) + br_ref[...]
    lg = jnp.where(lane < N_EXPERTS, logits, -jnp.inf)
    v0 = jnp.max(lg, axis=-1, keepdims=True)
    lane_f = lane.astype(F32)
    i0 = jnp.min(jnp.where(lg == v0, lane_f, float(LANES)), axis=-1, keepdims=True)
    lg1 = jnp.where(lane_f == i0, -jnp.inf, lg)
    v1 = jnp.max(lg1, axis=-1, keepdims=True)
    i1 = jnp.min(jnp.where(lg1 == v1, lane_f, float(LANES)), axis=-1, keepdims=True)
    e1 = jnp.exp(v1 - v0)
    den = 1.0 + e1
    gate_ref[...] = jnp.where(lane_f == i0, 1.0 / den, 0.0) + jnp.where(lane_f == i1, e1 / den, 0.0)
    sel_t = jnp.where((lane_f == i0) | (lane_f == i1), 1.0, 0.0).T
    rank_t = jnp.dot(sel_t.astype(BF16), utri_ref[...], preferred_element_type=F32)
    pos_t = jnp.where(sel_t > 0.5, rank_t, -1.0)
    post_ref[...] = pos_t[0:N_EXPERTS]
    pos_ref[...] = pos_t.T
    cnt_ref[...] = jnp.broadcast_to(jnp.sum(sel_t[0:N_EXPERTS], axis=1, keepdims=True), (N_EXPERTS, LANES))


def _route(xf, mod, wr, br, utri, seq):
    t, d = xf.shape
    tm = MOE_TM
    nt = t // tm
    per_b = seq // tm
    full = lambda a: pl.BlockSpec(a.shape, lambda i: (0,) * a.ndim)
    return pl.pallas_call(
        _route_kernel,
        grid=(nt,),
        in_specs=[pl.BlockSpec((tm, d), lambda i: (i, 0)),
                  pl.BlockSpec((None, 6, d), lambda i: (i // per_b, 0, 0)),
                  full(wr), full(br), full(utri)],
        out_specs=(pl.BlockSpec((tm, d), lambda i: (i, 0)),
                   pl.BlockSpec((tm, LANES), lambda i: (i, 0)),
                   pl.BlockSpec((tm, LANES), lambda i: (i, 0)),
                   pl.BlockSpec((None, N_EXPERTS, tm), lambda i: (i, 0, 0)),
                   pl.BlockSpec((None, N_EXPERTS, LANES), lambda i: (i, 0, 0))),
        out_shape=(jax.ShapeDtypeStruct((t, d), BF16),
                   jax.ShapeDtypeStruct((t, LANES), F32),
                   jax.ShapeDtypeStruct((t, LANES), F32),
                   jax.ShapeDtypeStruct((nt, N_EXPERTS, tm), F32),
                   jax.ShapeDtypeStruct((nt, N_EXPERTS, LANES), F32)),
        compiler_params=_cparams(("parallel",)),
        name="moe_route",
    )(xf, mod, wr, br, utri)


def _moe_schedule(cnt, n_blocks, k_max):
    nt, ne = cnt.shape
    per = MOE_BM // MOE_SUB
    nb = (cnt + MOE_SUB - 1) // MOE_SUB
    nbt = nb.T
    tot = jnp.sum(nbt, axis=1)
    reg = (tot + per - 1) // per * per
    reg_end = jnp.cumsum(reg)
    reg_start = reg_end - reg
    seg_end = jnp.cumsum(nbt, axis=1)
    seg_start = reg_start[:, None] + seg_end - nbt
    j = jnp.arange(n_blocks, dtype=I32)
    e_j = jnp.minimum(jnp.sum(reg_end[None, :] <= j[:, None], axis=1), ne - 1).astype(I32)
    valid_j = (j - reg_start[e_j]) < tot[e_j]
    step_e = e_j[::per]
    step_valid = valid_j[::per].astype(I32)
    cum = jnp.cumsum(nb, axis=1)
    n_tile = cum[:, -1]
    k = jnp.minimum(jnp.arange(k_max, dtype=I32)[None, :], n_tile[:, None] - 1)
    e_k = jnp.sum(cum[:, None, :] <= k[:, :, None], axis=2).astype(I32)
    tile = jnp.arange(nt, dtype=I32)[:, None]
    s_k = k - (jnp.take_along_axis(cum, e_k, axis=1) - jnp.take_along_axis(nb, e_k, axis=1))
    blk_k = seg_start[e_k, tile] + s_k
    valid_k = (jnp.arange(k_max, dtype=I32)[None, :] < n_tile[:, None]).astype(I32)
    return (step_e, step_valid), (blk_k.astype(I32), e_k, s_k.astype(I32), valid_k)


def _dispatch_kernel(blk_ref, e_ref, s_ref, h_ref, post_ref, gate_ref, xs0_ref, gs0_ref, xs_ref, gs_ref):
    i = pl.program_id(0)
    k = pl.program_id(1)
    e = e_ref[i, k]
    row = post_ref[pl.ds(e, 1), :]
    want = s_ref[i, k] * MOE_SUB + lax.broadcasted_iota(I32, (MOE_SUB, 1), 0)
    onehot = jnp.where(row == want.astype(F32), 1.0, 0.0).astype(BF16)
    xs_ref[...] = jnp.dot(onehot, h_ref[...], preferred_element_type=F32).astype(BF16)
    gates = gate_ref[...]
    g_hi = gates.astype(BF16)
    g_lo = (gates - g_hi.astype(F32)).astype(BF16)
    gs = (jnp.dot(onehot, g_hi, preferred_element_type=F32)
          + jnp.dot(onehot, g_lo, preferred_element_type=F32))
    lane = lax.broadcasted_iota(I32, (1, LANES), 1)
    gcol = jnp.sum(jnp.where(lane == e, gs, 0.0), axis=-1, keepdims=True)
    gs_ref[...] = jnp.broadcast_to(gcol, (MOE_SUB, LANES))


def _dispatch(h, post, gates, sched, n_blocks):
    t, d = h.shape
    tm = MOE_TM
    k_max = sched[0].shape[1]
    rows = n_blocks * MOE_SUB
    grid_spec = pltpu.PrefetchScalarGridSpec(
        num_scalar_prefetch=3,
        grid=(t // tm, k_max),
        in_specs=[pl.BlockSpec((tm, d), lambda i, k, *_: (i, 0)),
                  pl.BlockSpec((None, N_EXPERTS, tm), lambda i, k, *_: (i, 0, 0)),
                  pl.BlockSpec((tm, LANES), lambda i, k, *_: (i, 0)),
                  pl.BlockSpec(memory_space=pl.ANY), pl.BlockSpec(memory_space=pl.ANY)],
        out_specs=(pl.BlockSpec((MOE_SUB, d), lambda i, k, blk, e, s: (blk[i, k], 0)),
                   pl.BlockSpec((MOE_SUB, LANES), lambda i, k, blk, e, s: (blk[i, k], 0))),
    )
    return pl.pallas_call(
        _dispatch_kernel,
        grid_spec=grid_spec,
        out_shape=(jax.ShapeDtypeStruct((rows, d), BF16), jax.ShapeDtypeStruct((rows, LANES), F32)),
        input_output_aliases={6: 0, 7: 1},
        compiler_params=_cparams(("arbitrary", "arbitrary")),
        name="moe_dispatch",
    )(*sched, h, post, gates, jnp.zeros((rows, d), BF16), jnp.zeros((rows, LANES), F32))


def _experts_kernel(e_ref, v_ref, xs_ref, gs_ref, wg_ref, wu_ref, wd_ref, ys_ref, acc_ref):
    jb = pl.program_id(0)
    f = pl.program_id(1)
    last = pl.num_programs(1) - 1

    @pl.when(v_ref[jb] > 0)
    def _():
        xb = xs_ref[...]
        a = (_silu(jnp.dot(xb, wg_ref[...], preferred_element_type=F32))
             * jnp.dot(xb, wu_ref[...], preferred_element_type=F32)).astype(BF16)
        part = jnp.dot(a, wd_ref[...], preferred_element_type=F32)

        @pl.when(f == 0)
        def _():
            acc_ref[...] = part

        @pl.when(f > 0)
        def _():
            acc_ref[...] += part

        @pl.when(f == last)
        def _():
            gate = jnp.concatenate([gs_ref[...]] * (ys_ref.shape[1] // LANES), axis=1)
            ys_ref[...] = (acc_ref[...] * gate).astype(BF16)

    @pl.when((v_ref[jb] == 0) & (f == last))
    def _():
        ys_ref[...] = jnp.zeros(ys_ref.shape, BF16)


def _experts(xs, gs, wg, wu, wd, sched):
    rows, d = xs.shape
    n_e, _, d_ff = wg.shape
    fc = MOE_FC
    n_f = d_ff // fc
    fsel = lambda jb, f, e, v: jnp.where(v[jb] > 0, f, n_f - 1)
    grid_spec = pltpu.PrefetchScalarGridSpec(
        num_scalar_prefetch=2,
        grid=(rows // MOE_BM, n_f),
        in_specs=[pl.BlockSpec((MOE_BM, d), lambda jb, f, e, v: (jb, 0)),
                  pl.BlockSpec((MOE_BM, LANES), lambda jb, f, e, v: (jb, 0)),
                  pl.BlockSpec((None, d, fc), lambda jb, f, e, v: (e[jb], 0, fsel(jb, f, e, v))),
                  pl.BlockSpec((None, d, fc), lambda jb, f, e, v: (e[jb], 0, fsel(jb, f, e, v))),
                  pl.BlockSpec((None, fc, d), lambda jb, f, e, v: (e[jb], fsel(jb, f, e, v), 0))],
        out_specs=pl.BlockSpec((MOE_BM, d), lambda jb, f, e, v: (jb, 0)),
        scratch_shapes=[pltpu.VMEM((MOE_BM, d), F32)],
    )
    return pl.pallas_call(
        _experts_kernel,
        grid_spec=grid_spec,
        out_shape=jax.ShapeDtypeStruct((rows, d), BF16),
        compiler_params=_cparams(("arbitrary", "arbitrary")),
        name="moe_experts",
    )(*sched, xs, gs, wg, wu, wd)


def _combine_kernel(blk_ref, e_ref, s_ref, v_ref, *refs, alpha, k_max):
    ys_refs = refs[:k_max]
    pos_ref, x_ref, mod_ref, g_ref, b_ref, o_ref, pc_ref = refs[k_max:]
    i = pl.program_id(0)
    th = pos_ref.shape[0]
    lane = lax.broadcasted_iota(I32, (1, LANES), 1)
    pos = pos_ref[...]
    for e in range(N_EXPERTS):
        col = jnp.sum(jnp.where(lane == e, pos, 0.0), axis=-1, keepdims=True)
        pc_ref[e] = jnp.broadcast_to(col, (th, LANES))
    hots = []
    for k in range(k_max):
        s = jnp.where(v_ref[i, k] > 0, s_ref[i, k], -2)
        want = (s * MOE_SUB + lane).astype(F32)
        hots.append(jnp.where(pc_ref[e_ref[i, k]] == want, 1.0, 0.0).astype(BF16))
    f = jnp.dot(jnp.concatenate(hots, axis=1), jnp.concatenate([r[...] for r in ys_refs], axis=0),
                preferred_element_type=F32)
    y = alpha * x_ref[...] + mod_ref[5:6, :] * f
    o_ref[...] = _standardize(y) * g_ref[...] + b_ref[...]


def _combine(ys, pos, xf, mod, g, b, sched, seq, alpha):
    t, d = xf.shape
    th = MOE_TM // 2
    per_b = seq // th
    k_max = sched[0].shape[1]
    row = lambda n: pl.BlockSpec((th, n), lambda i, hh, *_: (2 * i + hh, 0))
    full = lambda a: pl.BlockSpec(a.shape, lambda i, hh, *_: (0,) * a.ndim)

    def ys_index(k, i, hh, blk, e, s, v):
        return (blk[i, k], 0)

    grid_spec = pltpu.PrefetchScalarGridSpec(
        num_scalar_prefetch=4,
        grid=(t // MOE_TM, 2),
        in_specs=([pl.BlockSpec((MOE_SUB, d), functools.partial(ys_index, k)) for k in range(k_max)]
                  + [row(LANES), row(d),
                     pl.BlockSpec((None, 6, d), lambda i, hh, *_: ((2 * i + hh) // per_b, 0, 0)),
                     full(g), full(b)]),
        out_specs=row(d),
        scratch_shapes=[pltpu.VMEM((N_EXPERTS, th, LANES), F32)],
    )
    return pl.pallas_call(
        functools.partial(_combine_kernel, alpha=alpha, k_max=k_max),
        grid_spec=grid_spec,
        out_shape=jax.ShapeDtypeStruct((t, d), F32),
        compiler_params=_cparams(("arbitrary", "arbitrary")),
        name="moe_combine",
    )(*sched, *([ys] * k_max), pos, xf, mod, g, b)


def _moe(xf, mod, wr, br, wg, wu, wd, g, b, seq, alpha):
    t, d = xf.shape
    nt = t // MOE_TM
    per = MOE_BM // MOE_SUB
    n_blocks = TOP_K * t // MOE_SUB + nt * N_EXPERTS + N_EXPERTS * (per - 1)
    n_blocks = (n_blocks + per - 1) // per * per
    k_max = TOP_K * MOE_TM // MOE_SUB + N_EXPERTS
    utri = jnp.asarray(np.triu(np.ones((MOE_TM, MOE_TM), np.float32), 1), BF16)
    h, gates, pos, post, cnt = _route(xf, mod, wr, br, utri, seq)
    sched_x, sched_c = _moe_schedule(cnt[:, :, 0].astype(I32), n_blocks, k_max)
    xs, gs = _dispatch(h, post, gates, sched_c[:3], n_blocks)
    ys = _experts(xs, gs, wg, wu, wd, sched_x)
    return _combine(ys, pos, xf, mod, g, b, sched_c, seq, alpha)


def _inproj_columns():
    sizes = (NSA_WIDTH, 6 * NSA_KV_HEADS * HEAD_DIM, 3 * NSA_HEADS, DSA_WIDTH, DSA_LATENT,
             IDX_HEADS * IDX_DIM, IDX_DIM, IDX_HEADS, 2 * SGU_WIDTH)
    starts = np.concatenate([[0], np.cumsum(sizes)])
    o_nq, o_kv, o_g, o_dq, o_ckv, o_iq, o_ik, o_iw, o_sgu = starts[:-1]
    perm = np.full((C_TOTAL,), -1, np.int64)

    def swap(base, n, dim):
        idx = np.arange(n)
        return base + (idx // dim) * dim + (idx % dim + dim // 2) % dim

    perm[C_NQ:C_NQ + 384] = o_nq + np.arange(384)
    perm[C_KV:C_KV + 768] = o_kv + np.arange(768)
    perm[C_DQ:C_DQ + 384] = o_dq + np.arange(384)
    perm[C_CKV:C_CKV + 128] = o_ckv + np.arange(128)
    perm[C_IQ:C_IQ + 128] = o_iq + np.arange(128)
    perm[C_SGU:C_SGU + 512] = o_sgu + np.arange(512)
    perm[C_SMALL:C_SMALL + IDX_DIM] = o_ik + np.arange(IDX_DIM)
    perm[C_SMALL + SMALL_IW:C_SMALL + SMALL_IW + IDX_HEADS] = o_iw + np.arange(IDX_HEADS)
    perm[C_SMALL + SMALL_GATE:C_SMALL + SMALL_GATE + 3 * NSA_HEADS] = o_g + np.arange(3 * NSA_HEADS)
    perm[C_NQ_SW:C_NQ_SW + 384] = swap(o_nq, 384, HEAD_DIM)
    perm[C_KSEL_SW:C_KSEL_SW + 128] = swap(o_kv + 256, 128, HEAD_DIM)
    perm[C_KWIN_SW:C_KWIN_SW + 128] = swap(o_kv + 512, 128, HEAD_DIM)
    perm[C_DQ_SW:C_DQ_SW + 384] = swap(o_dq, 384, HEAD_DIM)
    perm[C_IQ_SW:C_IQ_SW + 128] = swap(o_iq, 128, IDX_DIM)
    perm[C_SMALL_SW:C_SMALL_SW + IDX_DIM] = swap(o_ik, IDX_DIM, IDX_DIM)
    return perm, int(starts[-1])


def _swap_halves(w):
    half = w.shape[-1] // 2
    return jnp.concatenate([w[..., half:], w[..., :half]], axis=-1)


def _compress_weights(pos, w1, w2):
    half = CMP_LEN // 2
    eye_g = jnp.eye(NSA_KV_HEADS, dtype=F32)
    eye_j = jnp.eye(2, dtype=F32)

    def big(w1_half):
        w = jnp.einsum('jlde,jk,gh->ljgdkhe', w1_half, eye_j, eye_g)
        return w.reshape(half * 2 * NSA_KV_HEADS * HEAD_DIM, 2 * NSA_KV_HEADS * HEAD_DIM)

    def posrow(p_half):
        p = jnp.broadcast_to(p_half.transpose(1, 0, 2)[:, :, None, :], (half, 2, NSA_KV_HEADS, HEAD_DIM))
        return p.reshape(1, -1)

    w2big = jnp.einsum('jef,jk,gh->jgekhf', w2, eye_j, eye_g).reshape(256, 256)
    return (posrow(pos[:, :half]), posrow(pos[:, half:]),
            big(w1[:, :half]).astype(BF16), big(w1[:, half:]).astype(BF16), w2big.astype(BF16))


def _rope_tables(positions):
    def tab(dim, reps):
        inv = ROPE_THETA ** (-jnp.arange(0, dim, 2, dtype=F32) / dim)
        ang = positions.astype(F32)[..., None] * inv
        cos, sin = jnp.cos(ang), jnp.sin(ang)
        cos = jnp.tile(jnp.concatenate([cos, cos], axis=-1), (1, 1, reps))
        sin = jnp.tile(jnp.concatenate([-sin, sin], axis=-1), (1, 1, reps))
        return cos.reshape(-1, LANES), sin.reshape(-1, LANES)
    cos_h, sin_h = tab(HEAD_DIM, LANES // HEAD_DIM)
    cos_i, sin_i = tab(IDX_DIM, LANES // IDX_DIM)
    return cos_h, sin_h, cos_i, sin_i


def kernel(x, c, positions, w_ada, b_ada, w_in, nsa_cmp_pos, nsa_cmp_w1, nsa_cmp_w2, dsa_kv_norm, dsa_w_uk, dsa_w_uv, sgu_norm_g, sgu_norm_b, sgu_w, sgu_b, w_out, ln1_g, ln1_b, ln2_g, ln2_b, ffn_w_gate, ffn_w_up, ffn_w_down, moe_w_router, moe_b_router, moe_w_gate, moe_w_up, moe_w_down):
    batch, seq, d = x.shape
    depth = w_ada.shape[0]
    t = batch * seq
    alpha = (2 * depth) ** 0.25
    assert seq % 512 == 0 and seq >= WINDOW + Q_BLK

    tabs = _rope_tables(positions)
    mod_all = _adaln(c, w_ada, b_ada).reshape(depth, batch, 6, d)
    perm, in_width = _inproj_columns()
    perm = jnp.asarray(np.where(perm < 0, in_width, perm), I32)

    n_blk = seq // SEL_BLOCK
    n_half = seq // CMP_STRIDE
    cmp_start = np.arange(n_half)[None, :] * CMP_STRIDE
    blk_start = np.arange(n_blk)[:, None] * SEL_BLOCK
    ovt = jnp.asarray((cmp_start < blk_start + SEL_BLOCK) & (cmp_start + CMP_LEN > blk_start), BF16)
    ltri = jnp.asarray(np.tril(np.ones((SEL_KC, SEL_KC), np.float32), -1), BF16)

    xf = x.reshape(t, d)
    for layer in range(depth):
        mod = mod_all[layer]
        w_pad = jnp.concatenate([w_in[layer], jnp.zeros((d, 1), F32)], axis=1)
        w_ext = jnp.take(w_pad, perm, axis=1).astype(BF16)
        wkv = jnp.concatenate([dsa_w_uk[layer], dsa_w_uv[layer], _swap_halves(dsa_w_uk[layer]),
                               jnp.zeros((DSA_LATENT, HEAD_DIM), F32)], axis=1).astype(BF16)
        (nq, nqr, kvcmp, ksel, vselT, kwin, vwinT, dq, dk, dvT, iq, small, sgu_z) = _inproj(
            xf, mod, tabs, w_ext, wkv, dsa_kv_norm[layer].reshape(1, -1), batch, seq)

        plo, phi, wlo, whi, w2big = _compress_weights(nsa_cmp_pos[layer], nsa_cmp_w1[layer], nsa_cmp_w2[layer])
        kc, vct = _compress(kvcmp.reshape(batch, n_half, CMP_STRIDE * 256), plo, phi, wlo, whi, w2big)
        o_a = _nsa(nq, nqr, small, kc, vct, ksel, vselT, kwin, vwinT, ovt, batch, seq)
        o_b = _dsa(dq, iq, small, dk, dvT, ltri, batch, seq)
        bs = jnp.repeat(sgu_b[layer].T, HEAD_DIM, axis=1)
        o_c = _sgu(sgu_z, sgu_norm_g[layer].reshape(1, -1), sgu_norm_b[layer].reshape(1, -1),
                   sgu_w[layer], bs)
        xf = _outproj(o_a, o_b, o_c, xf, mod, w_out[layer].astype(BF16),
                      ln1_g[layer].reshape(1, -1), ln1_b[layer].reshape(1, -1), seq, alpha)

        j = layer // 2
        g2, b2 = ln2_g[layer].reshape(1, -1), ln2_b[layer].reshape(1, -1)
        if layer % 2 == 0:
            xf = _ffn(xf, mod, ffn_w_gate[j].astype(BF16), ffn_w_up[j].astype(BF16),
                      ffn_w_down[j].astype(BF16), g2, b2, seq, alpha)
        else:
            wr = jnp.pad(moe_w_router[j], ((0, 0), (0, LANES - N_EXPERTS)))
            br = jnp.pad(moe_b_router[j], (0, LANES - N_EXPERTS)).reshape(1, -1)
            xf = _moe(xf, mod, wr, br, moe_w_gate[j].astype(BF16), moe_w_up[j].astype(BF16),
                      moe_w_down[j].astype(BF16), g2, b2, seq, alpha)
    return xf.reshape(batch, seq, d)
```

```python
import functools

import numpy as np
import jax
import jax.numpy as jnp
from jax import lax
from jax.experimental import pallas as pl
from jax.experimental.pallas import tpu as pltpu

F32 = jnp.float32
BF16 = jnp.bfloat16
I32 = jnp.int32

HEAD_DIM = 64
Q_BLK = 128
ROPE_THETA = 10000.0
LN_EPS = 1e-5
RMS_EPS = 1e-6

NSA_HEADS = 6
NSA_KV_HEADS = 2
NSA_GROUP = NSA_HEADS // NSA_KV_HEADS
NSA_WIDTH = NSA_HEADS * HEAD_DIM
CMP_LEN = 32
CMP_STRIDE = 16
SEL_BLOCK = 64
SEL_TOP = 16
WINDOW = 512
FORCE_BONUS = 1e4

DSA_HEADS = 6
DSA_WIDTH = DSA_HEADS * HEAD_DIM
DSA_LATENT = 128
IDX_HEADS = 4
IDX_DIM = 32
IDX_TOPK_MAX = 256

SGU_GROUPS = 4
SGU_CHUNK = 128
SGU_WIDTH = SGU_GROUPS * HEAD_DIM

N_EXPERTS = 8
TOP_K = 2

LANES = 128
VMEM_LIMIT = 56 * 1024 * 1024
NEG = -1e30
INT_MIN = -(2 ** 31)

C_NQ = 0
C_KV = 384
C_DQ = 1152
C_CKV = 1536
C_IQ = 1664
C_SGU = 1792
C_SMALL = 2304
C_MAIN = 2432
C_NQ_SW = 2432
C_KSEL_SW = 2816
C_KWIN_SW = 2944
C_DQ_SW = 3072
C_IQ_SW = 3456
C_SMALL_SW = 3584
C_TOTAL = 3712
SMALL_IW = 32
SMALL_GATE = 36

SEL_KC = 512
Q_SCALE = HEAD_DIM ** -0.5 * 1.4426950408889634


def _cparams(sem):
    return pltpu.CompilerParams(dimension_semantics=sem, vmem_limit_bytes=VMEM_LIMIT)


def _dot(a, b):
    return jnp.dot(a.astype(BF16), b.astype(BF16), preferred_element_type=F32)


def _dot_nt(a, b):
    return lax.dot_general(a.astype(BF16), b.astype(BF16), (((1,), (1,)), ((), ())),
                           preferred_element_type=F32)


def _gelu(x):
    return 0.5 * x * (1.0 + jnp.tanh(0.7978845608028654 * (x + 0.044715 * (x * x * x))))


def _silu(x):
    return x * (1.0 / (1.0 + jnp.exp(-x)))


def _sigmoid(x):
    return 1.0 / (1.0 + jnp.exp(-x))


def _standardize(x):
    mu = jnp.mean(x, axis=-1, keepdims=True)
    xc = x - mu
    var = jnp.mean(xc * xc, axis=-1, keepdims=True)
    return xc * lax.rsqrt(var + LN_EPS)


def _adaln_kernel(c_ref, w_ref, b_ref, o_ref):
    c = c_ref[...]
    o_ref[...] = jnp.dot(_silu(c), w_ref[...], preferred_element_type=F32,
                         precision=lax.Precision.HIGHEST) + b_ref[...]


def _adaln(c, w_ada, b_ada):
    depth, d, n = w_ada.shape
    b = c.shape[0]
    tn = 512
    return pl.pallas_call(
        _adaln_kernel,
        grid=(depth, n // tn),
        in_specs=[pl.BlockSpec((b, d), lambda l, j: (0, 0)),
                  pl.BlockSpec((None, d, tn), lambda l, j: (l, 0, j)),
                  pl.BlockSpec((None, 1, tn), lambda l, j: (l, 0, j))],
        out_specs=pl.BlockSpec((None, b, tn), lambda l, j: (l, 0, j)),
        out_shape=jax.ShapeDtypeStruct((depth, b, n), F32),
        compiler_params=_cparams(("arbitrary", "arbitrary")),
        name="adaln",
    )(c, w_ada, b_ada.reshape(depth, 1, n))


def _inproj_kernel(x_ref, mod_ref, cos_ref, sin_ref, cosi_ref, sini_ref, w_ref, wkv_ref, kvg_ref,
                   nq_ref, nqr_ref, kvcmp_ref, ksel_ref, vselT_ref, kwin_ref, vwinT_ref,
                   dq_ref, dk_ref, dvT_ref, iq_ref, small_ref, sgu_ref):
    tm = x_ref.shape[0]
    shift = mod_ref[0:1, :]
    scale = mod_ref[1:2, :]
    hb = (_standardize(x_ref[...]) * (1.0 + scale) + shift).astype(BF16)

    def proj(c0, n):
        return jnp.dot(hb, w_ref[:, c0:c0 + n], preferred_element_type=F32)

    cos = cos_ref[...]
    sin = sin_ref[...]
    cos3 = jnp.concatenate([cos, cos, cos], axis=1)
    sin3 = jnp.concatenate([sin, sin, sin], axis=1)
    lane = lax.broadcasted_iota(I32, (1, LANES), 1)

    zq = proj(C_NQ, 384)
    nq_ref[...] = (zq * Q_SCALE).astype(BF16)
    nqr_ref[...] = ((zq * cos3 + proj(C_NQ_SW, 384) * sin3) * Q_SCALE).astype(BF16)

    kvcmp_ref[...] = proj(C_KV, 256)
    ksel_ref[...] = (proj(C_KV + 256, 128) * cos + proj(C_KSEL_SW, 128) * sin).astype(BF16)
    ones_t = jnp.ones((HEAD_DIM, tm), F32)
    vsel_t = proj(C_KV + 384, 128).T
    vsel_x = jnp.concatenate([vsel_t[:HEAD_DIM], ones_t, vsel_t[HEAD_DIM:], ones_t], axis=0).astype(BF16)
    for j in range(tm // SEL_KC):
        vselT_ref[j] = vsel_x[:, j * SEL_KC:(j + 1) * SEL_KC]
    kwin_ref[...] = (proj(C_KV + 512, 128) * cos + proj(C_KWIN_SW, 128) * sin).astype(BF16)
    vwin_t = proj(C_KV + 640, 128).T
    vwin_x = jnp.concatenate([vwin_t[:HEAD_DIM], ones_t, vwin_t[HEAD_DIM:], ones_t], axis=0).astype(BF16)
    for j in range(tm // Q_BLK):
        vwinT_ref[j] = vwin_x[:, j * Q_BLK:(j + 1) * Q_BLK]

    dq_ref[...] = ((proj(C_DQ, 384) * cos3 + proj(C_DQ_SW, 384) * sin3) * Q_SCALE).astype(BF16)

    ckv = proj(C_CKV, 128)
    ckv = ckv * lax.rsqrt(jnp.mean(ckv * ckv, axis=-1, keepdims=True) + RMS_EPS) * kvg_ref[...]
    kd = jnp.dot(ckv.astype(BF16), wkv_ref[...], preferred_element_type=F32)
    first = lane < HEAD_DIM
    dkv = kd[:, :128] * jnp.where(first, cos, 1.0) + kd[:, 128:] * jnp.where(first, sin, 0.0)
    dk_ref[...] = dkv.astype(BF16)
    dv_x = jnp.concatenate([dkv.T[HEAD_DIM:], ones_t], axis=0).astype(BF16)
    for j in range(tm // SEL_KC):
        dvT_ref[j] = dv_x[:, j * SEL_KC:(j + 1) * SEL_KC]

    cosi = cosi_ref[...]
    sini = sini_ref[...]
    iq_ref[...] = (proj(C_IQ, 128) * cosi + proj(C_IQ_SW, 128) * sini).astype(BF16)
    isk = lane < IDX_DIM
    small_ref[...] = (proj(C_SMALL, 128) * jnp.where(isk, cosi, 1.0)
                      + proj(C_SMALL_SW, 128) * jnp.where(isk, sini, 0.0))
    sgu_ref[...] = proj(C_SGU, 512)


def _inproj(xf, mod, tabs, w_ext, wkv, kvg, batch, seq):
    t, d = xf.shape
    tm = 512
    per_b = seq // tm
    row = lambda n: pl.BlockSpec((tm, n), lambda i: (i, 0))
    trk = lambda r: pl.BlockSpec((None, tm // SEL_KC, r, SEL_KC), lambda i: (i // per_b, i % per_b, 0, 0))
    tr128 = pl.BlockSpec((None, tm // Q_BLK, 256, Q_BLK), lambda i: (i // per_b, i % per_b, 0, 0))
    full = lambda a: pl.BlockSpec(a.shape, lambda i: (0,) * a.ndim)
    out_shape = (
        jax.ShapeDtypeStruct((t, 384), BF16),
        jax.ShapeDtypeStruct((t, 384), BF16),
        jax.ShapeDtypeStruct((t, 256), F32),
        jax.ShapeDtypeStruct((t, 128), BF16),
        jax.ShapeDtypeStruct((batch, seq // SEL_KC, 256, SEL_KC), BF16),
        jax.ShapeDtypeStruct((t, 128), BF16),
        jax.ShapeDtypeStruct((batch, seq // Q_BLK, 256, Q_BLK), BF16),
        jax.ShapeDtypeStruct((t, 384), BF16),
        jax.ShapeDtypeStruct((t, 128), BF16),
        jax.ShapeDtypeStruct((batch, seq // SEL_KC, 128, SEL_KC), BF16),
        jax.ShapeDtypeStruct((t, 128), BF16),
        jax.ShapeDtypeStruct((t, 128), F32),
        jax.ShapeDtypeStruct((t, 512), F32),
    )
    out_specs = (row(384), row(384), row(256), row(128), trk(256), row(128), tr128,
                 row(384), row(128), trk(128), row(128), row(128), row(512))
    return pl.pallas_call(
        _inproj_kernel,
        grid=(t // tm,),
        in_specs=[row(d),
                  pl.BlockSpec((None, 6, d), lambda i: (i // per_b, 0, 0)),
                  row(128), row(128), row(128), row(128),
                  full(w_ext), full(wkv), full(kvg)],
        out_specs=out_specs,
        out_shape=out_shape,
        compiler_params=_cparams(("parallel",)),
        name="inproj",
    )(xf, mod, *tabs, w_ext, wkv, kvg)


def _compress_kernel(h_ref, plo_ref, phi_ref, wlo_ref, whi_ref, w2_ref, kc_ref, vct_ref):
    h = h_ref[...]
    a = _dot(h + plo_ref[...], wlo_ref[...])
    b = _dot(h + phi_ref[...], whi_ref[...])
    nh = h.shape[0]
    pre = a + pltpu.roll(b, nh - 1, 0)
    cmp = _dot(_gelu(pre), w2_ref[...])
    kc_ref[...] = cmp[:, :128].astype(BF16)
    v_t = cmp[:, 128:].T
    ones_t = jnp.ones((HEAD_DIM, nh), F32)
    vct_ref[...] = jnp.concatenate([v_t[:HEAD_DIM], ones_t, v_t[HEAD_DIM:], ones_t], axis=0).astype(BF16)


def _compress(kvcmp_h, plo, phi, wlo, whi, w2):
    batch, nh, width = kvcmp_h.shape
    full = lambda a: pl.BlockSpec(a.shape, lambda b: (0,) * a.ndim)
    return pl.pallas_call(
        _compress_kernel,
        grid=(batch,),
        in_specs=[pl.BlockSpec((None, nh, width), lambda b: (b, 0, 0)),
                  full(plo), full(phi), full(wlo), full(whi), full(w2)],
        out_specs=(pl.BlockSpec((None, nh, 128), lambda b: (b, 0, 0)),
                   pl.BlockSpec((None, 256, nh), lambda b: (b, 0, 0))),
        out_shape=(jax.ShapeDtypeStruct((batch, nh, 128), BF16),
                   jax.ShapeDtypeStruct((batch, 256, nh), BF16)),
        compiler_params=_cparams(("parallel",)),
        name="nsa_compress",
    )(kvcmp_h, plo, phi, wlo, whi, w2)


def _softmax_pv(s, mask, vt_ones):
    sm = jnp.where(mask, s, NEG)
    m = jnp.max(sm, axis=0, keepdims=True)
    p = jnp.exp2(sm - m).astype(BF16)
    pv = jnp.dot(vt_ones, p, preferred_element_type=F32)
    dh = vt_ones.shape[0] // 2
    inv = jnp.where(m > 0.5 * NEG, 1.0 / jnp.maximum(pv[dh:dh + 1], 1e-30), 0.0)
    return p, inv, pv[:dh]


def _flash_step(s, on, vt_ones, m_ref, acc_ref):
    m_new, acc_new = _flash_update(s, on, vt_ones, m_ref[...], acc_ref[...])
    acc_ref[...] = acc_new
    m_ref[...] = m_new


def _flash_update(s, on, vt_ones, m_old, acc_old):
    n_grp = s.shape[1] // Q_BLK
    sm = [jnp.where(on, s[:, j * Q_BLK:(j + 1) * Q_BLK], NEG) for j in range(n_grp)]
    m_new = jnp.maximum(m_old, jnp.concatenate([jnp.max(x, axis=0, keepdims=True) for x in sm], axis=1))
    p = jnp.concatenate([jnp.exp2(sm[j] - m_new[:, j * Q_BLK:(j + 1) * Q_BLK]).astype(BF16)
                         for j in range(n_grp)], axis=1)
    acc_new = jnp.exp2(m_old - m_new) * acc_old + jnp.dot(vt_ones, p, preferred_element_type=F32)
    return m_new, acc_new


def _nsa_kernel(nq_ref, nqr_ref, small_ref, kc_ref, vct_ref, ksel_ref, vselT_ref, kwin_ref,
                vwinT_ref, ovt_ref, o_ref, sc_ref, lim_ref, oc_ref, m_ref, acc_ref, sa_ref, sb_ref,
                *, n_blk, n_sel, n_cmp):
    i = pl.program_id(1)
    t0 = i * Q_BLK
    tq = t0 + lax.broadcasted_iota(I32, (1, Q_BLK), 1)
    tq3 = jnp.concatenate([tq, tq, tq], axis=1)
    nq = nq_ref[...]
    nqr = nqr_ref[...]
    nh = kc_ref.shape[0]

    def stack_heads(q, g):
        hs = [g * NSA_GROUP + r for r in range(NSA_GROUP)]
        return jnp.concatenate([q[:, h * HEAD_DIM:(h + 1) * HEAD_DIM] for h in hs], axis=0)

    qrs = [stack_heads(nqr, g) for g in range(NSA_KV_HEADS)]
    for g in range(NSA_KV_HEADS):
        lo, hi = g * HEAD_DIM, (g + 1) * HEAD_DIM
        s_c = _dot_nt(kc_ref[:, lo:hi], stack_heads(nq, g))
        n_io = lax.broadcasted_iota(I32, (nh, 1), 0)
        m_c = (n_io * CMP_STRIDE + (CMP_LEN - 1) <= tq3) & (n_io < n_cmp)
        p_c, inv_c, oc = _softmax_pv(s_c, m_c, vct_ref[2 * lo:2 * hi, :])
        oc_ref[g] = oc * inv_c

        imp = jnp.zeros((n_blk, Q_BLK), F32)
        for r in range(NSA_GROUP):
            cs = slice(r * Q_BLK, (r + 1) * Q_BLK)
            imp = imp + jnp.dot(ovt_ref[...], p_c[:, cs], preferred_element_type=F32) * inv_c[:, cs]
        j_io = lax.broadcasted_iota(I32, (n_blk, 1), 0)
        cur = tq >> 6
        valid = j_io <= cur
        forced = (j_io == 0) | (j_io == cur) | (j_io == cur - 1)
        score = jnp.where(valid, imp + jnp.where(forced, FORCE_BONUS, 0.0), -jnp.inf)
        sc_ref[...] = score
        rank = jnp.zeros((n_blk, Q_BLK), F32)
        for b in range(n_blk):
            row = sc_ref[b:b + 1, :]
            beats = (row > score) | ((row == score) & (j_io > b))
            rank = rank + jnp.where(beats, 1.0, 0.0)
        lim_ref[g] = jnp.where((rank < n_sel) & valid, tq, -1)

    m_ref[...] = jnp.full(m_ref.shape, NEG, F32)
    acc_ref[...] = jnp.zeros(acc_ref.shape, F32)
    bpc = SEL_KC // SEL_BLOCK

    n_ch = (t0 + Q_BLK + SEL_KC - 1) // SEL_KC

    def scores(c, g):
        k0 = pl.multiple_of(c * SEL_KC, SEL_KC)
        return _dot_nt(ksel_ref[pl.ds(k0, SEL_KC), g * HEAD_DIM:(g + 1) * HEAD_DIM], qrs[g])

    def sel_step(c, carry):
        key = c * SEL_KC + lax.broadcasted_iota(I32, (SEL_KC, 1), 0)

        def flash(s, g):
            rows = [jnp.broadcast_to(lim_ref[g, pl.ds(c * bpc + u, 1), :], (SEL_BLOCK, Q_BLK))
                    for u in range(bpc)]
            on = key <= jnp.concatenate(rows, axis=0)
            _flash_step(s, on, vselT_ref[c, 2 * g * HEAD_DIM:2 * (g + 1) * HEAD_DIM, :],
                        m_ref.at[g], acc_ref.at[g])

        sb_ref[...] = scores(c, 1)
        flash(sa_ref[...], 0)
        sa_ref[...] = scores(jnp.minimum(c + 1, n_ch - 1), 0)
        flash(sb_ref[...], 1)
        return carry

    sa_ref[...] = scores(0, 0)
    lax.fori_loop(0, n_ch, sel_step, 0)

    small_t = small_ref[...].T
    heads = []
    for g in range(NSA_KV_HEADS):
        lo, hi = g * HEAD_DIM, (g + 1) * HEAD_DIM
        acc = acc_ref[g]
        o_s = acc[:HEAD_DIM] / jnp.maximum(acc[HEAD_DIM:HEAD_DIM + 1], 1e-30)
        o_c = oc_ref[g]

        nband = WINDOW // Q_BLK + 1
        cb = jnp.maximum(i - WINDOW // Q_BLK, 0)
        b0 = pl.multiple_of(cb * Q_BLK, Q_BLK)
        s_w = _dot_nt(kwin_ref[pl.ds(b0, nband * Q_BLK), lo:hi], qrs[g])
        key = b0 + lax.broadcasted_iota(I32, (nband * Q_BLK, 1), 0)
        diff = tq3 - key
        vt_w = jnp.concatenate([vwinT_ref[cb + u, 2 * lo:2 * hi, :] for u in range(nband)], axis=1)
        _, inv_w, o_w = _softmax_pv(s_w, (diff >= 0) & (diff < WINDOW), vt_w)
        o_w = o_w * inv_w

        for r in range(NSA_GROUP):
            gi = SMALL_GATE + (g * NSA_GROUP + r) * 3
            gt = _sigmoid(small_t[gi:gi + 3, :])
            cs = slice(r * Q_BLK, (r + 1) * Q_BLK)
            heads.append(gt[0:1, :] * o_c[:, cs] + gt[1:2, :] * o_s[:, cs] + gt[2:3, :] * o_w[:, cs])
    o_ref[...] = jnp.concatenate(heads, axis=0).T.astype(BF16)


def _nsa(nq, nqr, small, kc, vct, ksel, vselT, kwin, vwinT, ovt, batch, seq):
    n_q = seq // Q_BLK
    n_blk = seq // SEL_BLOCK
    n_cmp = (seq - CMP_LEN) // CMP_STRIDE + 1
    nh = kc.shape[1]
    blk = lambda n: pl.BlockSpec((Q_BLK, n), lambda b, i: (b * n_q + i, 0))
    perb2 = lambda r, c: pl.BlockSpec((None, r, c), lambda b, i: (b, 0, 0))
    perb_rows = pl.BlockSpec((seq, 128), lambda b, i: (b, 0))
    kern = functools.partial(_nsa_kernel, n_blk=n_blk, n_sel=min(SEL_TOP, n_blk), n_cmp=n_cmp)
    n_col = NSA_GROUP * Q_BLK
    return pl.pallas_call(
        kern,
        grid=(batch, n_q),
        in_specs=[blk(384), blk(384), blk(128),
                  perb2(nh, 128), perb2(256, nh),
                  perb_rows,
                  pl.BlockSpec((None, seq // SEL_KC, 256, SEL_KC), lambda b, i: (b, 0, 0, 0)),
                  perb_rows,
                  pl.BlockSpec((None, seq // Q_BLK, 256, Q_BLK), lambda b, i: (b, 0, 0, 0)),
                  pl.BlockSpec(ovt.shape, lambda b, i: (0, 0))],
        out_specs=blk(384),
        out_shape=jax.ShapeDtypeStruct((batch * seq, 384), BF16),
        scratch_shapes=[pltpu.VMEM((n_blk, Q_BLK), F32),
                        pltpu.VMEM((NSA_KV_HEADS, n_blk, Q_BLK), I32),
                        pltpu.VMEM((NSA_KV_HEADS, HEAD_DIM, n_col), F32),
                        pltpu.VMEM((NSA_KV_HEADS, 1, n_col), F32),
                        pltpu.VMEM((NSA_KV_HEADS, 2 * HEAD_DIM, n_col), F32),
                        pltpu.VMEM((SEL_KC, n_col), F32), pltpu.VMEM((SEL_KC, n_col), F32)],
        compiler_params=_cparams(("parallel", "arbitrary")),
        name="nsa_attn",
    )(nq, nqr, small, kc, vct, ksel, vselT, kwin, vwinT, ovt)


PLANE_KEYS = 32 * 8


def _bit_transpose32(rows):
    rows = list(rows)
    j, mask = 16, 0x0000FFFF
    while j:
        m32 = np.array(mask, np.uint32).view(np.int32)
        k = 0
        while k < 32:
            t = (rows[k] ^ lax.shift_right_logical(rows[k + j], np.int32(j))) & m32
            rows[k] = rows[k] ^ t
            rows[k + j] = rows[k + j] ^ (t << j)
            k = (k + j + 1) & ~j
        j >>= 1
        mask = (mask ^ (mask << j)) & 0xFFFFFFFF
    return rows

def _dsa_kernel(dq_ref, iq_ref, small_ref, dk_ref, dvT_ref, ltri_ref, o_ref,
                ord_ref, plane_ref, m_ref, acc_ref, sa_ref, sb_ref, *, k_top):
    i = pl.program_id(1)
    t0 = i * Q_BLK
    kc = SEL_KC
    n_ch = (t0 + Q_BLK + kc - 1) // kc
    tq = t0 + lax.broadcasted_iota(I32, (1, Q_BLK), 1)

    small_t = small_ref[pl.ds(pl.multiple_of(t0, Q_BLK), Q_BLK), :].T
    w_rows = [small_t[SMALL_IW + h:SMALL_IW + h + 1, :] * (IDX_HEADS ** -0.5) for h in range(IDX_HEADS)]
    iq = iq_ref[...]
    iqs = jnp.concatenate([iq[:, h * IDX_DIM:(h + 1) * IDX_DIM] for h in range(IDX_HEADS)], axis=0)

    def score_step(c, carry):
        k0 = pl.multiple_of(c * kc, kc)
        ik = small_ref[pl.ds(k0, kc), :][:, 0:IDX_DIM]
        lg = _dot_nt(ik, iqs)
        sc = jnp.zeros((kc, Q_BLK), F32)
        for h in range(IDX_HEADS):
            sc = sc + w_rows[h] * jnp.maximum(lg[:, h * Q_BLK:(h + 1) * Q_BLK], 0.0)
        sc = jnp.where(sc == 0.0, 0.0, sc)
        bits = lax.bitcast_convert_type(sc, I32)
        ordv = bits ^ ((bits >> 31) & 0x7FFFFFFF)
        key = k0 + lax.broadcasted_iota(I32, (kc, 1), 0)
        ordv = jnp.where(key <= tq, ordv, INT_MIN)
        ord_ref[pl.ds(k0, kc), :] = ordv
        u = ordv ^ INT_MIN
        for g in range(kc // PLANE_KEYS):
            rows = [u[g * PLANE_KEYS + r * 8:g * PLANE_KEYS + (r + 1) * 8, :] for r in range(32)]
            cols = _bit_transpose32(rows)
            w0 = pl.multiple_of((c * (kc // PLANE_KEYS) + g) * 8, 8)
            for b in range(32):
                plane_ref[b, pl.ds(w0, 8), :] = cols[31 - b]
        return carry

    lax.fori_loop(0, n_ch, score_step, 0)

    n_words = plane_ref.shape[1] // 8
    alive = [jnp.broadcast_to(jnp.where(w < n_ch * (kc // PLANE_KEYS), -1, 0).astype(I32), (8, Q_BLK))
             for w in range(n_words)]
    k_rem = jnp.full((1, Q_BLK), k_top, I32)
    prefix = jnp.zeros((1, Q_BLK), I32)

    def popcount_rows(words):
        pcs = [lax.population_count(x) for x in words]
        while len(pcs) > 1:
            pcs = [a + b for a, b in zip(pcs[0::2], pcs[1::2])]
        return jnp.sum(pcs[0], axis=0, keepdims=True)

    for bit in range(31, -1, -1):
        ones = [alive[w] & plane_ref[bit, w * 8:(w + 1) * 8, :] for w in range(n_words)]
        cnt = popcount_rows(ones)
        take = cnt >= k_rem
        keep0 = jnp.broadcast_to(jnp.where(take, 0, -1).astype(I32), (8, Q_BLK))
        alive = [ones[w] ^ (alive[w] & keep0) for w in range(n_words)]
        k_rem = jnp.where(take, k_rem, k_rem - cnt)
        prefix = prefix | jnp.where(take, np.int32(INT_MIN) if bit == 31 else np.int32(1 << bit), 0)
    thr = prefix ^ np.int32(INT_MIN)
    n_eq = popcount_rows(alive)
    short = thr == INT_MIN
    need = jnp.where(short, 0, k_rem)
    thr_all = jnp.where(short, INT_MIN + 1, thr)
    no_cut = jnp.min(jnp.where(short | (n_eq == need), 1.0, 0.0)) > 0.5

    q = dq_ref[...]
    qs = jnp.concatenate([q[:, h * HEAD_DIM:(h + 1) * HEAD_DIM] for h in range(DSA_HEADS)], axis=0)
    m_ref[...] = jnp.full(m_ref.shape, NEG, F32)
    acc_ref[...] = jnp.zeros(acc_ref.shape, F32)

    def sweep(mask_fn):
        def attn_step(c, seen):
            k0 = pl.multiple_of(c * kc, kc)
            on, seen = mask_fn(ord_ref[pl.ds(k0, kc), :], seen)
            vt = dvT_ref[c]
            sb_ref[...] = scores(c, 1)
            _flash_step(sa_ref[...], on[:half], vt[:, :half], m_ref, acc_ref)
            sa_ref[...] = scores(jnp.minimum(c + 1, n_ch - 1), 0)
            _flash_step(sb_ref[...], on[half:], vt[:, half:], m_ref, acc_ref)
            return seen
        sa_ref[...] = scores(0, 0)
        lax.fori_loop(0, n_ch, attn_step, jnp.zeros((1, Q_BLK), F32))

    half = kc // 2

    def scores(c, sub):
        k0 = pl.multiple_of(c * kc + sub * half, half)
        return _dot_nt(dk_ref[pl.ds(k0, half), 0:HEAD_DIM], qs)

    @pl.when(no_cut)
    def _():
        sweep(lambda o, seen: (o >= thr_all, seen))

    @pl.when(jnp.logical_not(no_cut))
    def _():
        need_f = need.astype(F32)

        def cut_mask(o, seen):
            eq = jnp.where(o == thr, 1.0, 0.0)
            before = jnp.dot(ltri_ref[...], eq.astype(BF16), preferred_element_type=F32) + seen
            on = (o > thr) | ((eq > 0.5) & (before < need_f))
            return on, seen + jnp.sum(eq, axis=0, keepdims=True)

        sweep(cut_mask)

    acc = acc_ref[...]
    o_t = acc[:HEAD_DIM] / jnp.maximum(acc[HEAD_DIM:HEAD_DIM + 1], 1e-30)
    heads = [o_t[:, h * Q_BLK:(h + 1) * Q_BLK] for h in range(DSA_HEADS)]
    o_ref[...] = jnp.concatenate(heads, axis=0).T.astype(BF16)


def _dsa(dq, iq, small, dk, dvT, ltri, batch, seq):
    n_q = seq // Q_BLK
    blk = lambda n: pl.BlockSpec((Q_BLK, n), lambda b, i: (b * n_q + i, 0))
    perb_rows = pl.BlockSpec((seq, 128), lambda b, i: (b, 0))
    kern = functools.partial(_dsa_kernel, k_top=min(IDX_TOPK_MAX, seq // 4))
    return pl.pallas_call(
        kern,
        grid=(batch, n_q),
        in_specs=[blk(384), blk(128), perb_rows, perb_rows,
                  pl.BlockSpec((None, seq // SEL_KC, 128, SEL_KC), lambda b, i: (b, 0, 0, 0)),
                  pl.BlockSpec(ltri.shape, lambda b, i: (0, 0))],
        out_specs=blk(384),
        out_shape=jax.ShapeDtypeStruct((batch * seq, 384), BF16),
        scratch_shapes=[pltpu.VMEM((seq, Q_BLK), I32),
                        pltpu.VMEM((32, seq // 32, Q_BLK), I32),
                        pltpu.VMEM((1, DSA_HEADS * Q_BLK), F32),
                        pltpu.VMEM((2 * HEAD_DIM, DSA_HEADS * Q_BLK), F32),
                        pltpu.VMEM((SEL_KC // 2, DSA_HEADS * Q_BLK), F32),
                        pltpu.VMEM((SEL_KC // 2, DSA_HEADS * Q_BLK), F32)],
        compiler_params=_cparams(("parallel", "arbitrary")),
        name="dsa_attn",
    )(dq, iq, small, dk, dvT, ltri)


def _sgu_kernel(z_ref, g_ref, b_ref, w_ref, bs_ref, o_ref):
    z = _gelu(z_ref[...])
    row = lax.broadcasted_iota(I32, (SGU_CHUNK, SGU_CHUNK), 0)
    col = lax.broadcasted_iota(I32, (SGU_CHUNK, SGU_CHUNK), 1)
    outs = []
    for g in range(SGU_GROUPS):
        lo, hi = g * HEAD_DIM, (g + 1) * HEAD_DIM
        u = z[:, lo:hi]
        v = _standardize(z[:, SGU_WIDTH + lo:SGU_WIDTH + hi]) * g_ref[:, lo:hi] + b_ref[:, lo:hi]
        w = jnp.where(row >= col, w_ref[g], 0.0)
        outs.append(u * (_dot(w, v) + bs_ref[:, lo:hi]))
    o_ref[...] = jnp.concatenate(outs, axis=1).astype(BF16)


def _sgu(z, g, b, w, bs):
    t = z.shape[0]
    n_c = bs.shape[0] // SGU_CHUNK
    full = lambda a: pl.BlockSpec(a.shape, lambda i: (0,) * a.ndim)
    return pl.pallas_call(
        _sgu_kernel,
        grid=(t // SGU_CHUNK,),
        in_specs=[pl.BlockSpec((SGU_CHUNK, 2 * SGU_WIDTH), lambda i: (i, 0)),
                  full(g), full(b), full(w), full(bs)],
        out_specs=pl.BlockSpec((SGU_CHUNK, SGU_WIDTH), lambda i: (i, 0)),
        out_shape=jax.ShapeDtypeStruct((t, SGU_WIDTH), BF16),
        compiler_params=_cparams(("parallel",)),
        name="sgu",
    )(z, g, b, w, bs)


def _outproj_kernel(oa_ref, ob_ref, oc_ref, x_ref, mod_ref, w_ref, g_ref, b_ref, o_ref, *, alpha):
    mix = (jnp.dot(oa_ref[...], w_ref[0:384, :], preferred_element_type=F32)
           + jnp.dot(ob_ref[...], w_ref[384:768, :], preferred_element_type=F32)
           + jnp.dot(oc_ref[...], w_ref[768:1024, :], preferred_element_type=F32))
    y = alpha * x_ref[...] + mod_ref[2:3, :] * mix
    o_ref[...] = _standardize(y) * g_ref[...] + b_ref[...]


def _outproj(oa, ob, oc, xf, mod, w, g, b, seq, alpha):
    t, d = xf.shape
    tm = 512
    per_b = seq // tm
    row = lambda n: pl.BlockSpec((tm, n), lambda i: (i, 0))
    full = lambda a: pl.BlockSpec(a.shape, lambda i: (0,) * a.ndim)
    return pl.pallas_call(
        functools.partial(_outproj_kernel, alpha=alpha),
        grid=(t // tm,),
        in_specs=[row(384), row(384), row(256), row(d),
                  pl.BlockSpec((None, 6, d), lambda i: (i // per_b, 0, 0)),
                  full(w), full(g), full(b)],
        out_specs=row(d),
        out_shape=jax.ShapeDtypeStruct((t, d), F32),
        compiler_params=_cparams(("parallel",)),
        name="outproj",
    )(oa, ob, oc, xf, mod, w, g, b)


def _ffn_kernel(x_ref, mod_ref, wg_ref, wu_ref, wd_ref, g_ref, b_ref, o_ref, acc_ref, *, alpha, fc):
    x = x_ref[...]
    hb = (_standardize(x) * (1.0 + mod_ref[4:5, :]) + mod_ref[3:4, :]).astype(BF16)
    d_ff = wg_ref.shape[1]
    for j in range(d_ff // fc):
        cs = slice(j * fc, (j + 1) * fc)
        a = (_silu(jnp.dot(hb, wg_ref[:, cs], preferred_element_type=F32))
             * jnp.dot(hb, wu_ref[:, cs], preferred_element_type=F32)).astype(BF16)
        part = jnp.dot(a, wd_ref[cs, :], preferred_element_type=F32)
        if j == 0:
            acc_ref[...] = part
        else:
            acc_ref[...] += part
    y = alpha * x + mod_ref[5:6, :] * acc_ref[...]
    o_ref[...] = _standardize(y) * g_ref[...] + b_ref[...]


def _ffn(xf, mod, wg, wu, wd, g, b, seq, alpha):
    t, d = xf.shape
    tm = 512
    per_b = seq // tm
    row = pl.BlockSpec((tm, d), lambda i: (i, 0))
    once = lambda a: pl.BlockSpec(a.shape, lambda i: (0,) * a.ndim, pipeline_mode=pl.Buffered(1))
    return pl.pallas_call(
        functools.partial(_ffn_kernel, alpha=alpha, fc=256),
        grid=(t // tm,),
        in_specs=[row, pl.BlockSpec((None, 6, d), lambda i: (i // per_b, 0, 0)),
                  once(wg), once(wu), once(wd), once(g), once(b)],
        out_specs=row,
        out_shape=jax.ShapeDtypeStruct((t, d), F32),
        scratch_shapes=[pltpu.VMEM((tm, d), F32)],
        compiler_params=_cparams(("parallel",)),
        name="ffn",
    )(xf, mod, wg, wu, wd, g, b)


MOE_TM = 1024
MOE_SUB = 128
MOE_BM = 1024
MOE_FC = 512


def _route_kernel(x_ref, mod_ref, wr_ref, br_ref, utri_ref, h_ref, gate_ref, pos_ref, post_ref, cnt_ref):
    h = _standardize(x_ref[...]) * (1.0 + mod_ref[4:5, :]) + mod_ref[3:4, :]
    h_ref[...] = h.astype(BF16)
    lane = lax.broadcasted_iota(I32, (1, LANES), 1)
    logits = jnp.dot(h, wr_ref[...], preferred_element_type=F32,
                     precision=lax.Precision.HIGHEST) + br_ref[...]
    lg = jnp.where(lane < N_EXPERTS, logits, -jnp.inf)
    v0 = jnp.max(lg, axis=-1, keepdims=True)
    lane_f = lane.astype(F32)
    i0 = jnp.min(jnp.where(lg == v0, lane_f, float(LANES)), axis=-1, keepdims=True)
    lg1 = jnp.where(lane_f == i0, -jnp.inf, lg)
    v1 = jnp.max(lg1, axis=-1, keepdims=True)
    i1 = jnp.min(jnp.where(lg1 == v1, lane_f, float(LANES)), axis=-1, keepdims=True)
    e1 = jnp.exp(v1 - v0)
    den = 1.0 + e1
    gate_ref[...] = jnp.where(lane_f == i0, 1.0 / den, 0.0) + jnp.where(lane_f == i1, e1 / den, 0.0)
    sel_t = jnp.where((lane_f == i0) | (lane_f == i1), 1.0, 0.0).T
    rank_t = jnp.dot(sel_t.astype(BF16), utri_ref[...], preferred_element_type=F32)
    pos_t = jnp.where(sel_t > 0.5, rank_t, -1.0)
    post_ref[...] = pos_t[0:N_EXPERTS]
    pos_ref[...] = pos_t.T
    cnt_ref[...] = jnp.broadcast_to(jnp.sum(sel_t[0:N_EXPERTS], axis=1, keepdims=True), (N_EXPERTS, LANES))


def _route(xf, mod, wr, br, utri, seq):
    t, d = xf.shape
    tm = MOE_TM
    nt = t // tm
    per_b = seq // tm
    full = lambda a: pl.BlockSpec(a.shape, lambda i: (0,) * a.ndim)
    return pl.pallas_call(
        _route_kernel,
        grid=(nt,),
        in_specs=[pl.BlockSpec((tm, d), lambda i: (i, 0)),
                  pl.BlockSpec((None, 6, d), lambda i: (i // per_b, 0, 0)),
                  full(wr), full(br), full(utri)],
        out_specs=(pl.BlockSpec((tm, d), lambda i: (i, 0)),
                   pl.BlockSpec((tm, LANES), lambda i: (i, 0)),
                   pl.BlockSpec((tm, LANES), lambda i: (i, 0)),
                   pl.BlockSpec((None, N_EXPERTS, tm), lambda i: (i, 0, 0)),
                   pl.BlockSpec((None, N_EXPERTS, LANES), lambda i: (i, 0, 0))),
        out_shape=(jax.ShapeDtypeStruct((t, d), BF16),
                   jax.ShapeDtypeStruct((t, LANES), F32),
                   jax.ShapeDtypeStruct((t, LANES), F32),
                   jax.ShapeDtypeStruct((nt, N_EXPERTS, tm), F32),
                   jax.ShapeDtypeStruct((nt, N_EXPERTS, LANES), F32)),
        compiler_params=_cparams(("parallel",)),
        name="moe_route",
    )(xf, mod, wr, br, utri)


def _moe_schedule(cnt, n_blocks, k_max):
    nt, ne = cnt.shape
    per = MOE_BM // MOE_SUB
    nb = (cnt + MOE_SUB - 1) // MOE_SUB
    nbt = nb.T
    tot = jnp.sum(nbt, axis=1)
    reg = (tot + per - 1) // per * per
    reg_end = jnp.cumsum(reg)
    reg_start = reg_end - reg
    seg_end = jnp.cumsum(nbt, axis=1)
    seg_start = reg_start[:, None] + seg_end - nbt
    j = jnp.arange(n_blocks, dtype=I32)
    e_j = jnp.minimum(jnp.sum(reg_end[None, :] <= j[:, None], axis=1), ne - 1).astype(I32)
    valid_j = (j - reg_start[e_j]) < tot[e_j]
    step_e = e_j[::per]
    step_valid = valid_j[::per].astype(I32)
    cum = jnp.cumsum(nb, axis=1)
    n_tile = cum[:, -1]
    k = jnp.minimum(jnp.arange(k_max, dtype=I32)[None, :], n_tile[:, None] - 1)
    e_k = jnp.sum(cum[:, None, :] <= k[:, :, None], axis=2).astype(I32)
    tile = jnp.arange(nt, dtype=I32)[:, None]
    s_k = k - (jnp.take_along_axis(cum, e_k, axis=1) - jnp.take_along_axis(nb, e_k, axis=1))
    blk_k = seg_start[e_k, tile] + s_k
    valid_k = (jnp.arange(k_max, dtype=I32)[None, :] < n_tile[:, None]).astype(I32)
    return (step_e, step_valid), (blk_k.astype(I32), e_k, s_k.astype(I32), valid_k)


def _dispatch_kernel(blk_ref, e_ref, s_ref, h_ref, post_ref, gate_ref, xs0_ref, gs0_ref, xs_ref, gs_ref):
    i = pl.program_id(0)
    k = pl.program_id(1)
    e = e_ref[i, k]
    row = post_ref[pl.ds(e, 1), :]
    want = s_ref[i, k] * MOE_SUB + lax.broadcasted_iota(I32, (MOE_SUB, 1), 0)
    onehot = jnp.where(row == want.astype(F32), 1.0, 0.0).astype(BF16)
    xs_ref[...] = jnp.dot(onehot, h_ref[...], preferred_element_type=F32).astype(BF16)
    gates = gate_ref[...]
    g_hi = gates.astype(BF16)
    g_lo = (gates - g_hi.astype(F32)).astype(BF16)
    gs = (jnp.dot(onehot, g_hi, preferred_element_type=F32)
          + jnp.dot(onehot, g_lo, preferred_element_type=F32))
    lane = lax.broadcasted_iota(I32, (1, LANES), 1)
    gcol = jnp.sum(jnp.where(lane == e, gs, 0.0), axis=-1, keepdims=True)
    gs_ref[...] = jnp.broadcast_to(gcol, (MOE_SUB, LANES))


def _dispatch(h, post, gates, sched, n_blocks):
    t, d = h.shape
    tm = MOE_TM
    k_max = sched[0].shape[1]
    rows = n_blocks * MOE_SUB
    grid_spec = pltpu.PrefetchScalarGridSpec(
        num_scalar_prefetch=3,
        grid=(t // tm, k_max),
        in_specs=[pl.BlockSpec((tm, d), lambda i, k, *_: (i, 0)),
                  pl.BlockSpec((None, N_EXPERTS, tm), lambda i, k, *_: (i, 0, 0)),
                  pl.BlockSpec((tm, LANES), lambda i, k, *_: (i, 0)),
                  pl.BlockSpec(memory_space=pl.ANY), pl.BlockSpec(memory_space=pl.ANY)],
        out_specs=(pl.BlockSpec((MOE_SUB, d), lambda i, k, blk, e, s: (blk[i, k], 0)),
                   pl.BlockSpec((MOE_SUB, LANES), lambda i, k, blk, e, s: (blk[i, k], 0))),
    )
    return pl.pallas_call(
        _dispatch_kernel,
        grid_spec=grid_spec,
        out_shape=(jax.ShapeDtypeStruct((rows, d), BF16), jax.ShapeDtypeStruct((rows, LANES), F32)),
        input_output_aliases={6: 0, 7: 1},
        compiler_params=_cparams(("arbitrary", "arbitrary")),
        name="moe_dispatch",
    )(*sched, h, post, gates, jnp.zeros((rows, d), BF16), jnp.zeros((rows, LANES), F32))


def _experts_kernel(e_ref, v_ref, xs_ref, gs_ref, wg_ref, wu_ref, wd_ref, ys_ref, acc_ref):
    jb = pl.program_id(0)
    f = pl.program_id(1)
    last = pl.num_programs(1) - 1

    @pl.when(v_ref[jb] > 0)
    def _():
        xb = xs_ref[...]
        a = (_silu(jnp.dot(xb, wg_ref[...].astype(BF16), preferred_element_type=F32))
             * jnp.dot(xb, wu_ref[...].astype(BF16), preferred_element_type=F32)).astype(BF16)
        part = jnp.dot(a, wd_ref[...].astype(BF16), preferred_element_type=F32)

        @pl.when(f == 0)
        def _():
            acc_ref[...] = part

        @pl.when(f > 0)
        def _():
            acc_ref[...] += part

        @pl.when(f == last)
        def _():
            gate = jnp.concatenate([gs_ref[...]] * (ys_ref.shape[1] // LANES), axis=1)
            ys_ref[...] = (acc_ref[...] * gate).astype(BF16)

    @pl.when((v_ref[jb] == 0) & (f == last))
    def _():
        ys_ref[...] = jnp.zeros(ys_ref.shape, BF16)


def _experts(xs, gs, wg, wu, wd, sched):
    rows, d = xs.shape
    n_e, _, d_ff = wg.shape
    fc = MOE_FC
    n_f = d_ff // fc
    fsel = lambda jb, f, e, v: jnp.where(v[jb] > 0, f, n_f - 1)
    grid_spec = pltpu.PrefetchScalarGridSpec(
        num_scalar_prefetch=2,
        grid=(rows // MOE_BM, n_f),
        in_specs=[pl.BlockSpec((MOE_BM, d), lambda jb, f, e, v: (jb, 0)),
                  pl.BlockSpec((MOE_BM, LANES), lambda jb, f, e, v: (jb, 0)),
                  pl.BlockSpec((None, d, fc), lambda jb, f, e, v: (e[jb], 0, fsel(jb, f, e, v))),
                  pl.BlockSpec((None, d, fc), lambda jb, f, e, v: (e[jb], 0, fsel(jb, f, e, v))),
                  pl.BlockSpec((None, fc, d), lambda jb, f, e, v: (e[jb], fsel(jb, f, e, v), 0))],
        out_specs=pl.BlockSpec((MOE_BM, d), lambda jb, f, e, v: (jb, 0)),
        scratch_shapes=[pltpu.VMEM((MOE_BM, d), F32)],
    )
    return pl.pallas_call(
        _experts_kernel,
        grid_spec=grid_spec,
        out_shape=jax.ShapeDtypeStruct((rows, d), BF16),
        compiler_params=_cparams(("arbitrary", "arbitrary")),
        name="moe_experts",
    )(*sched, xs, gs, wg, wu, wd)


def _combine_kernel(blk_ref, e_ref, s_ref, v_ref, *refs, alpha, k_max):
    ys_refs = refs[:k_max]
    pos_ref, x_ref, mod_ref, g_ref, b_ref, o_ref, pc_ref = refs[k_max:]
    i = pl.program_id(0)
    th = pos_ref.shape[0]
    lane = lax.broadcasted_iota(I32, (1, LANES), 1)
    pos = pos_ref[...]
    for e in range(N_EXPERTS):
        col = jnp.sum(jnp.where(lane == e, pos, 0.0), axis=-1, keepdims=True)
        pc_ref[e] = jnp.broadcast_to(col, (th, LANES))
    hots = []
    for k in range(k_max):
        s = jnp.where(v_ref[i, k] > 0, s_ref[i, k], -2)
        want = (s * MOE_SUB + lane).astype(F32)
        hots.append(jnp.where(pc_ref[e_ref[i, k]] == want, 1.0, 0.0).astype(BF16))
    f = jnp.dot(jnp.concatenate(hots, axis=1), jnp.concatenate([r[...] for r in ys_refs], axis=0),
                preferred_element_type=F32)
    y = alpha * x_ref[...] + mod_ref[5:6, :] * f
    o_ref[...] = _standardize(y) * g_ref[...] + b_ref[...]


def _combine(ys, pos, xf, mod, g, b, sched, seq, alpha):
    t, d = xf.shape
    th = MOE_TM // 2
    per_b = seq // th
    k_max = sched[0].shape[1]
    row = lambda n: pl.BlockSpec((th, n), lambda i, hh, *_: (2 * i + hh, 0))
    full = lambda a: pl.BlockSpec(a.shape, lambda i, hh, *_: (0,) * a.ndim)

    def ys_index(k, i, hh, blk, e, s, v):
        return (blk[i, k], 0)

    grid_spec = pltpu.PrefetchScalarGridSpec(
        num_scalar_prefetch=4,
        grid=(t // MOE_TM, 2),
        in_specs=([pl.BlockSpec((MOE_SUB, d), functools.partial(ys_index, k)) for k in range(k_max)]
                  + [row(LANES), row(d),
                     pl.BlockSpec((None, 6, d), lambda i, hh, *_: ((2 * i + hh) // per_b, 0, 0)),
                     full(g), full(b)]),
        out_specs=row(d),
        scratch_shapes=[pltpu.VMEM((N_EXPERTS, th, LANES), F32)],
    )
    return pl.pallas_call(
        functools.partial(_combine_kernel, alpha=alpha, k_max=k_max),
        grid_spec=grid_spec,
        out_shape=jax.ShapeDtypeStruct((t, d), F32),
        compiler_params=_cparams(("arbitrary", "arbitrary")),
        name="moe_combine",
    )(*sched, *([ys] * k_max), pos, xf, mod, g, b)


def _moe(xf, mod, wr, br, wg, wu, wd, g, b, seq, alpha):
    t, d = xf.shape
    nt = t // MOE_TM
    per = MOE_BM // MOE_SUB
    n_blocks = TOP_K * t // MOE_SUB + nt * N_EXPERTS + N_EXPERTS * (per - 1)
    n_blocks = (n_blocks + per - 1) // per * per
    k_max = TOP_K * MOE_TM // MOE_SUB + N_EXPERTS
    utri = jnp.asarray(np.triu(np.ones((MOE_TM, MOE_TM), np.float32), 1), BF16)
    h, gates, pos, post, cnt = _route(xf, mod, wr, br, utri, seq)
    sched_x, sched_c = _moe_schedule(cnt[:, :, 0].astype(I32), n_blocks, k_max)
    xs, gs = _dispatch(h, post, gates, sched_c[:3], n_blocks)
    ys = _experts(xs, gs, wg, wu, wd, sched_x)
    return _combine(ys, pos, xf, mod, g, b, sched_c, seq, alpha)


def _inproj_columns():
    sizes = (NSA_WIDTH, 6 * NSA_KV_HEADS * HEAD_DIM, 3 * NSA_HEADS, DSA_WIDTH, DSA_LATENT,
             IDX_HEADS * IDX_DIM, IDX_DIM, IDX_HEADS, 2 * SGU_WIDTH)
    starts = np.concatenate([[0], np.cumsum(sizes)])
    o_nq, o_kv, o_g, o_dq, o_ckv, o_iq, o_ik, o_iw, o_sgu = starts[:-1]
    perm = np.full((C_TOTAL,), -1, np.int64)

    def swap(base, n, dim):
        idx = np.arange(n)
        return base + (idx // dim) * dim + (idx % dim + dim // 2) % dim

    perm[C_NQ:C_NQ + 384] = o_nq + np.arange(384)
    perm[C_KV:C_KV + 768] = o_kv + np.arange(768)
    perm[C_DQ:C_DQ + 384] = o_dq + np.arange(384)
    perm[C_CKV:C_CKV + 128] = o_ckv + np.arange(128)
    perm[C_IQ:C_IQ + 128] = o_iq + np.arange(128)
    perm[C_SGU:C_SGU + 512] = o_sgu + np.arange(512)
    perm[C_SMALL:C_SMALL + IDX_DIM] = o_ik + np.arange(IDX_DIM)
    perm[C_SMALL + SMALL_IW:C_SMALL + SMALL_IW + IDX_HEADS] = o_iw + np.arange(IDX_HEADS)
    perm[C_SMALL + SMALL_GATE:C_SMALL + SMALL_GATE + 3 * NSA_HEADS] = o_g + np.arange(3 * NSA_HEADS)
    perm[C_NQ_SW:C_NQ_SW + 384] = swap(o_nq, 384, HEAD_DIM)
    perm[C_KSEL_SW:C_KSEL_SW + 128] = swap(o_kv + 256, 128, HEAD_DIM)
    perm[C_KWIN_SW:C_KWIN_SW + 128] = swap(o_kv + 512, 128, HEAD_DIM)
    perm[C_DQ_SW:C_DQ_SW + 384] = swap(o_dq, 384, HEAD_DIM)
    perm[C_IQ_SW:C_IQ_SW + 128] = swap(o_iq, 128, IDX_DIM)
    perm[C_SMALL_SW:C_SMALL_SW + IDX_DIM] = swap(o_ik, IDX_DIM, IDX_DIM)
    return perm, int(starts[-1])


def _swap_halves(w):
    half = w.shape[-1] // 2
    return jnp.concatenate([w[..., half:], w[..., :half]], axis=-1)


def _compress_weights(pos, w1, w2):
    half = CMP_LEN // 2
    eye_g = jnp.eye(NSA_KV_HEADS, dtype=F32)
    eye_j = jnp.eye(2, dtype=F32)

    def big(w1_half):
        w = jnp.einsum('jlde,jk,gh->ljgdkhe', w1_half, eye_j, eye_g)
        return w.reshape(half * 2 * NSA_KV_HEADS * HEAD_DIM, 2 * NSA_KV_HEADS * HEAD_DIM)

    def posrow(p_half):
        p = jnp.broadcast_to(p_half.transpose(1, 0, 2)[:, :, None, :], (half, 2, NSA_KV_HEADS, HEAD_DIM))
        return p.reshape(1, -1)

    w2big = jnp.einsum('jef,jk,gh->jgekhf', w2, eye_j, eye_g).reshape(256, 256)
    return (posrow(pos[:, :half]), posrow(pos[:, half:]),
            big(w1[:, :half]).astype(BF16), big(w1[:, half:]).astype(BF16), w2big.astype(BF16))


def _rope_tables(positions):
    pos = positions.astype(F32).reshape(-1, 1)

    def tab(dim):
        inv = ROPE_THETA ** (-jnp.arange(0, dim, 2, dtype=F32) / dim)
        lane = np.arange(LANES)
        ang = pos * inv[lane % (dim // 2)][None, :]
        sign = np.where(lane % dim < dim // 2, -1.0, 1.0).astype(np.float32)
        return jnp.cos(ang), jnp.sin(ang) * sign

    cos_h, sin_h = tab(HEAD_DIM)
    cos_i, sin_i = tab(IDX_DIM)
    return cos_h, sin_h, cos_i, sin_i


def kernel(x, c, positions, w_ada, b_ada, w_in, nsa_cmp_pos, nsa_cmp_w1, nsa_cmp_w2, dsa_kv_norm, dsa_w_uk, dsa_w_uv, sgu_norm_g, sgu_norm_b, sgu_w, sgu_b, w_out, ln1_g, ln1_b, ln2_g, ln2_b, ffn_w_gate, ffn_w_up, ffn_w_down, moe_w_router, moe_b_router, moe_w_gate, moe_w_up, moe_w_down):
    batch, seq, d = x.shape
    depth = w_ada.shape[0]
    t = batch * seq
    alpha = (2 * depth) ** 0.25
    assert seq % 512 == 0 and seq >= WINDOW + Q_BLK

    tabs = _rope_tables(positions)
    mod_all = _adaln(c, w_ada, b_ada).reshape(depth, batch, 6, d)
    perm, in_width = _inproj_columns()
    perm = jnp.asarray(np.where(perm < 0, in_width, perm), I32)

    n_blk = seq // SEL_BLOCK
    n_half = seq // CMP_STRIDE
    cmp_start = np.arange(n_half)[None, :] * CMP_STRIDE
    blk_start = np.arange(n_blk)[:, None] * SEL_BLOCK
    ovt = jnp.asarray((cmp_start < blk_start + SEL_BLOCK) & (cmp_start + CMP_LEN > blk_start), BF16)
    ltri = jnp.asarray(np.tril(np.ones((SEL_KC, SEL_KC), np.float32), -1), BF16)

    xf = x.reshape(t, d)
    for layer in range(depth):
        mod = mod_all[layer]
        w_pad = jnp.concatenate([w_in[layer], jnp.zeros((d, 1), F32)], axis=1)
        w_ext = jnp.take(w_pad, perm, axis=1).astype(BF16)
        wkv = jnp.concatenate([dsa_w_uk[layer], dsa_w_uv[layer], _swap_halves(dsa_w_uk[layer]),
                               jnp.zeros((DSA_LATENT, HEAD_DIM), F32)], axis=1).astype(BF16)
        (nq, nqr, kvcmp, ksel, vselT, kwin, vwinT, dq, dk, dvT, iq, small, sgu_z) = _inproj(
            xf, mod, tabs, w_ext, wkv, dsa_kv_norm[layer].reshape(1, -1), batch, seq)

        plo, phi, wlo, whi, w2big = _compress_weights(nsa_cmp_pos[layer], nsa_cmp_w1[layer], nsa_cmp_w2[layer])
        kc, vct = _compress(kvcmp.reshape(batch, n_half, CMP_STRIDE * 256), plo, phi, wlo, whi, w2big)
        o_a = _nsa(nq, nqr, small, kc, vct, ksel, vselT, kwin, vwinT, ovt, batch, seq)
        o_b = _dsa(dq, iq, small, dk, dvT, ltri, batch, seq)
        bs = jnp.repeat(sgu_b[layer].T, HEAD_DIM, axis=1)
        o_c = _sgu(sgu_z, sgu_norm_g[layer].reshape(1, -1), sgu_norm_b[layer].reshape(1, -1),
                   sgu_w[layer], bs)
        xf = _outproj(o_a, o_b, o_c, xf, mod, w_out[layer].astype(BF16),
                      ln1_g[layer].reshape(1, -1), ln1_b[layer].reshape(1, -1), seq, alpha)

        j = layer // 2
        g2, b2 = ln2_g[layer].reshape(1, -1), ln2_b[layer].reshape(1, -1)
        if layer % 2 == 0:
            xf = _ffn(xf, mod, ffn_w_gate[j].astype(BF16), ffn_w_up[j].astype(BF16),
                      ffn_w_down[j].astype(BF16), g2, b2, seq, alpha)
        else:
            wr = jnp.pad(moe_w_router[j], ((0, 0), (0, LANES - N_EXPERTS)))
            br = jnp.pad(moe_b_router[j], (0, LANES - N_EXPERTS)).reshape(1, -1)
            xf = _moe(xf, mod, wr, br, moe_w_gate[j], moe_w_up[j], moe_w_down[j], g2, b2, seq, alpha)
    return xf.reshape(batch, seq, d)
```

```python
import functools

import numpy as np
import jax
import jax.numpy as jnp
from jax import lax
from jax.experimental import pallas as pl
from jax.experimental.pallas import tpu as pltpu

F32 = jnp.float32
BF16 = jnp.bfloat16
I32 = jnp.int32

HEAD_DIM = 64
Q_BLK = 128
ROPE_THETA = 10000.0
LN_EPS = 1e-5
RMS_EPS = 1e-6

NSA_HEADS = 6
NSA_KV_HEADS = 2
NSA_GROUP = NSA_HEADS // NSA_KV_HEADS
NSA_WIDTH = NSA_HEADS * HEAD_DIM
CMP_LEN = 32
CMP_STRIDE = 16
SEL_BLOCK = 64
SEL_TOP = 16
WINDOW = 512
FORCE_BONUS = 1e4

DSA_HEADS = 6
DSA_WIDTH = DSA_HEADS * HEAD_DIM
DSA_LATENT = 128
IDX_HEADS = 4
IDX_DIM = 32
IDX_TOPK_MAX = 256

SGU_GROUPS = 4
SGU_CHUNK = 128
SGU_WIDTH = SGU_GROUPS * HEAD_DIM

N_EXPERTS = 8
TOP_K = 2

LANES = 128
VMEM_LIMIT = 56 * 1024 * 1024
NEG = -1e30
INT_MIN = -(2 ** 31)

C_NQ = 0
C_KV = 384
C_DQ = 1152
C_CKV = 1536
C_IQ = 1664
C_SGU = 1792
C_SMALL = 2304
C_TOTAL = 2432
SMALL_IW = 32
SMALL_GATE = 36

SEL_KC = 512
Q_SCALE = HEAD_DIM ** -0.5 * 1.4426950408889634


def _cparams(sem):
    return pltpu.CompilerParams(dimension_semantics=sem, vmem_limit_bytes=VMEM_LIMIT)


def _dot(a, b):
    return jnp.dot(a.astype(BF16), b.astype(BF16), preferred_element_type=F32)


def _dot_nt(a, b):
    return lax.dot_general(a.astype(BF16), b.astype(BF16), (((1,), (1,)), ((), ())),
                           preferred_element_type=F32)


def _gelu(x):
    return 0.5 * x * (1.0 + jnp.tanh(0.7978845608028654 * (x + 0.044715 * (x * x * x))))


def _silu(x):
    return x * (1.0 / (1.0 + jnp.exp(-x)))


def _sigmoid(x):
    return 1.0 / (1.0 + jnp.exp(-x))


def _standardize(x):
    mu = jnp.mean(x, axis=-1, keepdims=True)
    xc = x - mu
    var = jnp.mean(xc * xc, axis=-1, keepdims=True)
    return xc * lax.rsqrt(var + LN_EPS)


def _adaln_kernel(c_ref, w_ref, b_ref, o_ref):
    c = c_ref[...]
    o_ref[...] = jnp.dot(_silu(c), w_ref[...], preferred_element_type=F32,
                         precision=lax.Precision.HIGHEST) + b_ref[...]


def _adaln(c, w_ada, b_ada):
    depth, d, n = w_ada.shape
    b = c.shape[0]
    tn = 512
    return pl.pallas_call(
        _adaln_kernel,
        grid=(depth, n // tn),
        in_specs=[pl.BlockSpec((b, d), lambda l, j: (0, 0)),
                  pl.BlockSpec((None, d, tn), lambda l, j: (l, 0, j)),
                  pl.BlockSpec((None, 1, tn), lambda l, j: (l, 0, j))],
        out_specs=pl.BlockSpec((None, b, tn), lambda l, j: (l, 0, j)),
        out_shape=jax.ShapeDtypeStruct((depth, b, n), F32),
        compiler_params=_cparams(("arbitrary", "arbitrary")),
        name="adaln",
    )(c, w_ada, b_ada.reshape(depth, 1, n))


def _inproj_kernel(x_ref, mod_ref, cos_ref, sin_ref, cosi_ref, sini_ref, w_ref, wkv_ref, kvg_ref,
                   nq_ref, nqr_ref, kvcmp_ref, ksel_ref, vselT_ref, kwin_ref, vwinT_ref,
                   dq_ref, dk_ref, dvT_ref, iq_ref, small_ref, sgu_ref):
    tm = x_ref.shape[0]
    shift = mod_ref[0:1, :]
    scale = mod_ref[1:2, :]
    hb = (_standardize(x_ref[...]) * (1.0 + scale) + shift).astype(BF16)

    def proj(c0, n):
        return jnp.dot(hb, w_ref[:, c0:c0 + n], preferred_element_type=F32)

    cos = cos_ref[...]
    sin = sin_ref[...]
    lane = lax.broadcasted_iota(I32, (1, LANES), 1)

    def rope(z, cos_t, sin_t, dim):
        half = dim // 2
        low = (lane & (dim - 1)) < half
        outs = []
        for c in range(z.shape[1] // LANES):
            zc = z[:, c * LANES:(c + 1) * LANES]
            swapped = jnp.where(low, pltpu.roll(zc, LANES - half, 1), pltpu.roll(zc, half, 1))
            outs.append(zc * cos_t + swapped * sin_t)
        return outs[0] if len(outs) == 1 else jnp.concatenate(outs, axis=1)

    zq = proj(C_NQ, 384)
    nq_ref[...] = (zq * Q_SCALE).astype(BF16)
    nqr_ref[...] = (rope(zq, cos, sin, HEAD_DIM) * Q_SCALE).astype(BF16)

    kvcmp_ref[...] = proj(C_KV, 256)
    ksel_ref[...] = rope(proj(C_KV + 256, 128), cos, sin, HEAD_DIM).astype(BF16)
    ones_t = jnp.ones((HEAD_DIM, tm), F32)
    vsel_t = proj(C_KV + 384, 128).T
    vsel_x = jnp.concatenate([vsel_t[:HEAD_DIM], ones_t, vsel_t[HEAD_DIM:], ones_t], axis=0).astype(BF16)
    for j in range(tm // SEL_KC):
        vselT_ref[j] = vsel_x[:, j * SEL_KC:(j + 1) * SEL_KC]
    kwin_ref[...] = rope(proj(C_KV + 512, 128), cos, sin, HEAD_DIM).astype(BF16)
    vwin_t = proj(C_KV + 640, 128).T
    vwin_x = jnp.concatenate([vwin_t[:HEAD_DIM], ones_t, vwin_t[HEAD_DIM:], ones_t], axis=0).astype(BF16)
    for j in range(tm // Q_BLK):
        vwinT_ref[j] = vwin_x[:, j * Q_BLK:(j + 1) * Q_BLK]

    dq_ref[...] = (rope(proj(C_DQ, 384), cos, sin, HEAD_DIM) * Q_SCALE).astype(BF16)

    ckv = proj(C_CKV, 128)
    ckv = ckv * lax.rsqrt(jnp.mean(ckv * ckv, axis=-1, keepdims=True) + RMS_EPS) * kvg_ref[...]
    kd = jnp.dot(ckv.astype(BF16), wkv_ref[...], preferred_element_type=F32)
    first = lane < HEAD_DIM
    dkv = rope(kd, jnp.where(first, cos, 1.0), jnp.where(first, sin, 0.0), HEAD_DIM)
    dk_ref[...] = dkv.astype(BF16)
    dv_x = jnp.concatenate([dkv.T[HEAD_DIM:], ones_t], axis=0).astype(BF16)
    for j in range(tm // SEL_KC):
        dvT_ref[j] = dv_x[:, j * SEL_KC:(j + 1) * SEL_KC]

    cosi = cosi_ref[...]
    sini = sini_ref[...]
    iq_ref[...] = rope(proj(C_IQ, 128), cosi, sini, IDX_DIM).astype(BF16)
    isk = lane < IDX_DIM
    small_ref[...] = rope(proj(C_SMALL, 128), jnp.where(isk, cosi, 1.0), jnp.where(isk, sini, 0.0), IDX_DIM)
    sgu_ref[...] = proj(C_SGU, 512)


def _inproj(xf, mod, tabs, w_ext, wkv, kvg, batch, seq):
    t, d = xf.shape
    tm = 512
    per_b = seq // tm
    row = lambda n: pl.BlockSpec((tm, n), lambda i: (i, 0))
    trk = lambda r: pl.BlockSpec((None, tm // SEL_KC, r, SEL_KC), lambda i: (i // per_b, i % per_b, 0, 0))
    tr128 = pl.BlockSpec((None, tm // Q_BLK, 256, Q_BLK), lambda i: (i // per_b, i % per_b, 0, 0))
    full = lambda a: pl.BlockSpec(a.shape, lambda i: (0,) * a.ndim)
    out_shape = (
        jax.ShapeDtypeStruct((t, 384), BF16),
        jax.ShapeDtypeStruct((t, 384), BF16),
        jax.ShapeDtypeStruct((t, 256), F32),
        jax.ShapeDtypeStruct((t, 128), BF16),
        jax.ShapeDtypeStruct((batch, seq // SEL_KC, 256, SEL_KC), BF16),
        jax.ShapeDtypeStruct((t, 128), BF16),
        jax.ShapeDtypeStruct((batch, seq // Q_BLK, 256, Q_BLK), BF16),
        jax.ShapeDtypeStruct((t, 384), BF16),
        jax.ShapeDtypeStruct((t, 128), BF16),
        jax.ShapeDtypeStruct((batch, seq // SEL_KC, 128, SEL_KC), BF16),
        jax.ShapeDtypeStruct((t, 128), BF16),
        jax.ShapeDtypeStruct((t, 128), F32),
        jax.ShapeDtypeStruct((t, 512), F32),
    )
    out_specs = (row(384), row(384), row(256), row(128), trk(256), row(128), tr128,
                 row(384), row(128), trk(128), row(128), row(128), row(512))
    return pl.pallas_call(
        _inproj_kernel,
        grid=(t // tm,),
        in_specs=[row(d),
                  pl.BlockSpec((None, 6, d), lambda i: (i // per_b, 0, 0)),
                  row(128), row(128), row(128), row(128),
                  full(w_ext), full(wkv), full(kvg)],
        out_specs=out_specs,
        out_shape=out_shape,
        compiler_params=_cparams(("parallel",)),
        name="inproj",
    )(xf, mod, *tabs, w_ext, wkv, kvg)


def _compress_kernel(h_ref, plo_ref, phi_ref, wlo_ref, whi_ref, w2_ref, kc_ref, vct_ref):
    h = h_ref[...]
    a = _dot(h + plo_ref[...], wlo_ref[...])
    b = _dot(h + phi_ref[...], whi_ref[...])
    nh = h.shape[0]
    pre = a + pltpu.roll(b, nh - 1, 0)
    cmp = _dot(_gelu(pre), w2_ref[...])
    kc_ref[...] = cmp[:, :128].astype(BF16)
    v_t = cmp[:, 128:].T
    ones_t = jnp.ones((HEAD_DIM, nh), F32)
    vct_ref[...] = jnp.concatenate([v_t[:HEAD_DIM], ones_t, v_t[HEAD_DIM:], ones_t], axis=0).astype(BF16)


def _compress(kvcmp_h, plo, phi, wlo, whi, w2):
    batch, nh, width = kvcmp_h.shape
    full = lambda a: pl.BlockSpec(a.shape, lambda b: (0,) * a.ndim)
    return pl.pallas_call(
        _compress_kernel,
        grid=(batch,),
        in_specs=[pl.BlockSpec((None, nh, width), lambda b: (b, 0, 0)),
                  full(plo), full(phi), full(wlo), full(whi), full(w2)],
        out_specs=(pl.BlockSpec((None, nh, 128), lambda b: (b, 0, 0)),
                   pl.BlockSpec((None, 256, nh), lambda b: (b, 0, 0))),
        out_shape=(jax.ShapeDtypeStruct((batch, nh, 128), BF16),
                   jax.ShapeDtypeStruct((batch, 256, nh), BF16)),
        compiler_params=_cparams(("parallel",)),
        name="nsa_compress",
    )(kvcmp_h, plo, phi, wlo, whi, w2)


def _softmax_pv(s, mask, vt_ones):
    sm = jnp.where(mask, s, NEG)
    m = jnp.max(sm, axis=0, keepdims=True)
    p = jnp.exp2(sm - m).astype(BF16)
    pv = jnp.dot(vt_ones, p, preferred_element_type=F32)
    dh = vt_ones.shape[0] // 2
    inv = jnp.where(m > 0.5 * NEG, 1.0 / jnp.maximum(pv[dh:dh + 1], 1e-30), 0.0)
    return p, inv, pv[:dh]


def _flash_step(s, on, vt_ones, m_ref, acc_ref):
    m_new, acc_new = _flash_update(s, on, vt_ones, m_ref[...], acc_ref[...])
    acc_ref[...] = acc_new
    m_ref[...] = m_new


def _masked_max(s, on, m_old):
    n_grp = s.shape[1] // Q_BLK
    sm = [jnp.where(on, s[:, j * Q_BLK:(j + 1) * Q_BLK], NEG) for j in range(n_grp)]
    m_new = jnp.maximum(m_old, jnp.concatenate([jnp.max(x, axis=0, keepdims=True) for x in sm], axis=1))
    return sm, m_new


def _exp2_weights(sm, m_new):
    return jnp.concatenate([jnp.exp2(x - m_new[:, j * Q_BLK:(j + 1) * Q_BLK]).astype(BF16)
                            for j, x in enumerate(sm)], axis=1)


def _flash_update(s, on, vt_ones, m_old, acc_old):
    sm, m_new = _masked_max(s, on, m_old)
    p = _exp2_weights(sm, m_new)
    acc_new = jnp.exp2(m_old - m_new) * acc_old + jnp.dot(vt_ones, p, preferred_element_type=F32)
    return m_new, acc_new


def _nsa_kernel(nq_ref, nqr_ref, small_ref, kc_ref, vct_ref, ksel_ref, vselT_ref, kwin_ref,
                vwinT_ref, ovt_ref, o_ref, sc_ref, lim_ref, oc_ref, m_ref, acc_ref, sa_ref, sb_ref,
                *, n_blk, n_sel, n_cmp):
    i = pl.program_id(1)
    t0 = i * Q_BLK
    tq = t0 + lax.broadcasted_iota(I32, (1, Q_BLK), 1)
    tq3 = jnp.concatenate([tq, tq, tq], axis=1)
    nq = nq_ref[...]
    nqr = nqr_ref[...]
    nh = kc_ref.shape[0]

    def stack_heads(q, g):
        hs = [g * NSA_GROUP + r for r in range(NSA_GROUP)]
        return jnp.concatenate([q[:, h * HEAD_DIM:(h + 1) * HEAD_DIM] for h in hs], axis=0)

    qrs = [stack_heads(nqr, g) for g in range(NSA_KV_HEADS)]
    for g in range(NSA_KV_HEADS):
        lo, hi = g * HEAD_DIM, (g + 1) * HEAD_DIM
        s_c = _dot_nt(kc_ref[:, lo:hi], stack_heads(nq, g))
        n_io = lax.broadcasted_iota(I32, (nh, 1), 0)
        m_c = (n_io * CMP_STRIDE + (CMP_LEN - 1) <= tq3) & (n_io < n_cmp)
        p_c, inv_c, oc = _softmax_pv(s_c, m_c, vct_ref[2 * lo:2 * hi, :])
        oc_ref[g] = oc * inv_c

        imp = jnp.zeros((n_blk, Q_BLK), F32)
        for r in range(NSA_GROUP):
            cs = slice(r * Q_BLK, (r + 1) * Q_BLK)
            imp = imp + jnp.dot(ovt_ref[...], p_c[:, cs], preferred_element_type=F32) * inv_c[:, cs]
        j_io = lax.broadcasted_iota(I32, (n_blk, 1), 0)
        cur = tq >> 6
        valid = j_io <= cur
        forced = (j_io == 0) | (j_io == cur) | (j_io == cur - 1)
        score = jnp.where(valid, imp + jnp.where(forced, FORCE_BONUS, 0.0), -jnp.inf)
        sc_ref[...] = score
        rank = jnp.zeros((n_blk, Q_BLK), F32)
        for b in range(n_blk):
            row = sc_ref[b:b + 1, :]
            beats = (row > score) | ((row == score) & (j_io > b))
            rank = rank + jnp.where(beats, 1.0, 0.0)
        lim_ref[g] = jnp.where((rank < n_sel) & valid, tq, -1)

    m_ref[...] = jnp.full(m_ref.shape, NEG, F32)
    acc_ref[...] = jnp.zeros(acc_ref.shape, F32)
    bpc = SEL_KC // SEL_BLOCK

    n_ch = (t0 + Q_BLK + SEL_KC - 1) // SEL_KC

    def scores(c, g):
        k0 = pl.multiple_of(c * SEL_KC, SEL_KC)
        return _dot_nt(ksel_ref[pl.ds(k0, SEL_KC), g * HEAD_DIM:(g + 1) * HEAD_DIM], qrs[g])

    def sel_step(c, carry):
        key = c * SEL_KC + lax.broadcasted_iota(I32, (SEL_KC, 1), 0)

        def flash(s, g):
            rows = [jnp.broadcast_to(lim_ref[g, pl.ds(c * bpc + u, 1), :], (SEL_BLOCK, Q_BLK))
                    for u in range(bpc)]
            on = key <= jnp.concatenate(rows, axis=0)
            _flash_step(s, on, vselT_ref[c, 2 * g * HEAD_DIM:2 * (g + 1) * HEAD_DIM, :],
                        m_ref.at[g], acc_ref.at[g])

        sb_ref[...] = scores(c, 1)
        flash(sa_ref[...], 0)
        sa_ref[...] = scores(jnp.minimum(c + 1, n_ch - 1), 0)
        flash(sb_ref[...], 1)
        return carry

    sa_ref[...] = scores(0, 0)
    lax.fori_loop(0, n_ch, sel_step, 0)

    small_t = small_ref[...].T
    heads = []
    for g in range(NSA_KV_HEADS):
        lo, hi = g * HEAD_DIM, (g + 1) * HEAD_DIM
        acc = acc_ref[g]
        o_s = acc[:HEAD_DIM] / jnp.maximum(acc[HEAD_DIM:HEAD_DIM + 1], 1e-30)
        o_c = oc_ref[g]

        nband = WINDOW // Q_BLK + 1
        cb = jnp.maximum(i - WINDOW // Q_BLK, 0)
        b0 = pl.multiple_of(cb * Q_BLK, Q_BLK)
        s_w = _dot_nt(kwin_ref[pl.ds(b0, nband * Q_BLK), lo:hi], qrs[g])
        key = b0 + lax.broadcasted_iota(I32, (nband * Q_BLK, 1), 0)
        diff = tq3 - key
        vt_w = jnp.concatenate([vwinT_ref[cb + u, 2 * lo:2 * hi, :] for u in range(nband)], axis=1)
        _, inv_w, o_w = _softmax_pv(s_w, (diff >= 0) & (diff < WINDOW), vt_w)
        o_w = o_w * inv_w

        for r in range(NSA_GROUP):
            gi = SMALL_GATE + (g * NSA_GROUP + r) * 3
            gt = _sigmoid(small_t[gi:gi + 3, :])
            cs = slice(r * Q_BLK, (r + 1) * Q_BLK)
            heads.append(gt[0:1, :] * o_c[:, cs] + gt[1:2, :] * o_s[:, cs] + gt[2:3, :] * o_w[:, cs])
    o_ref[...] = jnp.concatenate(heads, axis=0).T.astype(BF16)


def _nsa(nq, nqr, small, kc, vct, ksel, vselT, kwin, vwinT, ovt, batch, seq):
    n_q = seq // Q_BLK
    n_blk = seq // SEL_BLOCK
    n_cmp = (seq - CMP_LEN) // CMP_STRIDE + 1
    nh = kc.shape[1]
    blk = lambda n: pl.BlockSpec((Q_BLK, n), lambda b, i: (b * n_q + i, 0))
    perb2 = lambda r, c: pl.BlockSpec((None, r, c), lambda b, i: (b, 0, 0))
    perb_rows = pl.BlockSpec((seq, 128), lambda b, i: (b, 0))
    kern = functools.partial(_nsa_kernel, n_blk=n_blk, n_sel=min(SEL_TOP, n_blk), n_cmp=n_cmp)
    n_col = NSA_GROUP * Q_BLK
    return pl.pallas_call(
        kern,
        grid=(batch, n_q),
        in_specs=[blk(384), blk(384), blk(128),
                  perb2(nh, 128), perb2(256, nh),
                  perb_rows,
                  pl.BlockSpec((None, seq // SEL_KC, 256, SEL_KC), lambda b, i: (b, 0, 0, 0)),
                  perb_rows,
                  pl.BlockSpec((None, seq // Q_BLK, 256, Q_BLK), lambda b, i: (b, 0, 0, 0)),
                  pl.BlockSpec(ovt.shape, lambda b, i: (0, 0))],
        out_specs=blk(384),
        out_shape=jax.ShapeDtypeStruct((batch * seq, 384), BF16),
        scratch_shapes=[pltpu.VMEM((n_blk, Q_BLK), F32),
                        pltpu.VMEM((NSA_KV_HEADS, n_blk, Q_BLK), I32),
                        pltpu.VMEM((NSA_KV_HEADS, HEAD_DIM, n_col), F32),
                        pltpu.VMEM((NSA_KV_HEADS, 1, n_col), F32),
                        pltpu.VMEM((NSA_KV_HEADS, 2 * HEAD_DIM, n_col), F32),
                        pltpu.VMEM((SEL_KC, n_col), F32), pltpu.VMEM((SEL_KC, n_col), F32)],
        compiler_params=_cparams(("parallel", "arbitrary")),
        name="nsa_attn",
    )(nq, nqr, small, kc, vct, ksel, vselT, kwin, vwinT, ovt)


PLANE_KEYS = 32 * 8


def _bit_transpose32(rows):
    rows = list(rows)
    j, mask = 16, 0x0000FFFF
    while j:
        m32 = np.array(mask, np.uint32).view(np.int32)
        k = 0
        while k < 32:
            t = (rows[k] ^ lax.shift_right_logical(rows[k + j], np.int32(j))) & m32
            rows[k] = rows[k] ^ t
            rows[k + j] = rows[k + j] ^ (t << j)
            k = (k + j + 1) & ~j
        j >>= 1
        mask = (mask ^ (mask << j)) & 0xFFFFFFFF
    return rows

def _dsa_kernel(dq_ref, iq_ref, small_ref, dk_ref, dvT_ref, ltri_ref, o_ref,
                ord_ref, plane_ref, m_ref, acc_ref, sa_ref, sb_ref, *, k_top):
    i = pl.program_id(1)
    t0 = i * Q_BLK
    kc = SEL_KC
    n_ch = (t0 + Q_BLK + kc - 1) // kc
    tq = t0 + lax.broadcasted_iota(I32, (1, Q_BLK), 1)

    small_t = small_ref[pl.ds(pl.multiple_of(t0, Q_BLK), Q_BLK), :].T
    w_rows = [small_t[SMALL_IW + h:SMALL_IW + h + 1, :] * (IDX_HEADS ** -0.5) for h in range(IDX_HEADS)]
    iq = iq_ref[...]
    iqs = jnp.concatenate([iq[:, h * IDX_DIM:(h + 1) * IDX_DIM] for h in range(IDX_HEADS)], axis=0)

    def score_step(c, carry):
        k0 = pl.multiple_of(c * kc, kc)
        ik = small_ref[pl.ds(k0, kc), :][:, 0:IDX_DIM]
        lg = _dot_nt(ik, iqs)
        sc = jnp.zeros((kc, Q_BLK), F32)
        for h in range(IDX_HEADS):
            sc = sc + w_rows[h] * jnp.maximum(lg[:, h * Q_BLK:(h + 1) * Q_BLK], 0.0)
        sc = jnp.where(sc == 0.0, 0.0, sc)
        bits = lax.bitcast_convert_type(sc, I32)
        ordv = bits ^ ((bits >> 31) & 0x7FFFFFFF)
        key = k0 + lax.broadcasted_iota(I32, (kc, 1), 0)
        ordv = jnp.where(key <= tq, ordv, INT_MIN)
        ord_ref[pl.ds(k0, kc), :] = ordv
        u = ordv ^ INT_MIN
        for g in range(kc // PLANE_KEYS):
            rows = [u[g * PLANE_KEYS + r * 8:g * PLANE_KEYS + (r + 1) * 8, :] for r in range(32)]
            cols = _bit_transpose32(rows)
            w0 = pl.multiple_of((c * (kc // PLANE_KEYS) + g) * 8, 8)
            for b in range(32):
                plane_ref[b, pl.ds(w0, 8), :] = cols[31 - b]
        return carry

    lax.fori_loop(0, n_ch, score_step, 0)

    n_words = plane_ref.shape[1] // 8
    alive = [jnp.broadcast_to(jnp.where(w < n_ch * (kc // PLANE_KEYS), -1, 0).astype(I32), (8, Q_BLK))
             for w in range(n_words)]
    k_rem = jnp.full((1, Q_BLK), k_top, I32)
    prefix = jnp.zeros((1, Q_BLK), I32)

    def popcount_rows(words):
        pcs = [lax.population_count(x) for x in words]
        while len(pcs) > 1:
            pcs = [a + b for a, b in zip(pcs[0::2], pcs[1::2])]
        return jnp.sum(pcs[0], axis=0, keepdims=True)

    for bit in range(31, -1, -1):
        ones = [alive[w] & plane_ref[bit, w * 8:(w + 1) * 8, :] for w in range(n_words)]
        cnt = popcount_rows(ones)
        take = cnt >= k_rem
        keep0 = jnp.broadcast_to(jnp.where(take, 0, -1).astype(I32), (8, Q_BLK))
        alive = [ones[w] ^ (alive[w] & keep0) for w in range(n_words)]
        k_rem = jnp.where(take, k_rem, k_rem - cnt)
        prefix = prefix | jnp.where(take, np.int32(INT_MIN) if bit == 31 else np.int32(1 << bit), 0)
    thr = prefix ^ np.int32(INT_MIN)
    n_eq = popcount_rows(alive)
    short = thr == INT_MIN
    need = jnp.where(short, 0, k_rem)
    thr_all = jnp.where(short, INT_MIN + 1, thr)
    no_cut = jnp.min(jnp.where(short | (n_eq == need), 1.0, 0.0)) > 0.5

    q = dq_ref[...]
    qs = jnp.concatenate([q[:, h * HEAD_DIM:(h + 1) * HEAD_DIM] for h in range(DSA_HEADS)], axis=0)
    m_ref[...] = jnp.full(m_ref.shape, NEG, F32)
    acc_ref[...] = jnp.zeros(acc_ref.shape, F32)

    def sweep(mask_fn):
        def attn_step(c, seen):
            k0 = pl.multiple_of(c * kc, kc)
            on, seen = mask_fn(ord_ref[pl.ds(k0, kc), :], seen)
            vt = dvT_ref[c]
            sb_ref[...] = scores(c, 1)
            _flash_step(sa_ref[...], on[:half], vt[:, :half], m_ref, acc_ref)
            sa_ref[...] = scores(jnp.minimum(c + 1, n_ch - 1), 0)
            _flash_step(sb_ref[...], on[half:], vt[:, half:], m_ref, acc_ref)
            return seen
        sa_ref[...] = scores(0, 0)
        lax.fori_loop(0, n_ch, attn_step, jnp.zeros((1, Q_BLK), F32))

    half = kc // 2

    def scores(c, sub):
        k0 = pl.multiple_of(c * kc + sub * half, half)
        return _dot_nt(dk_ref[pl.ds(k0, half), 0:HEAD_DIM], qs)

    @pl.when(no_cut)
    def _():
        sweep(lambda o, seen: (o >= thr_all, seen))

    @pl.when(jnp.logical_not(no_cut))
    def _():
        need_f = need.astype(F32)

        def cut_mask(o, seen):
            eq = jnp.where(o == thr, 1.0, 0.0)
            before = jnp.dot(ltri_ref[...], eq.astype(BF16), preferred_element_type=F32) + seen
            on = (o > thr) | ((eq > 0.5) & (before < need_f))
            return on, seen + jnp.sum(eq, axis=0, keepdims=True)

        sweep(cut_mask)

    acc = acc_ref[...]
    o_t = acc[:HEAD_DIM] / jnp.maximum(acc[HEAD_DIM:HEAD_DIM + 1], 1e-30)
    heads = [o_t[:, h * Q_BLK:(h + 1) * Q_BLK] for h in range(DSA_HEADS)]
    o_ref[...] = jnp.concatenate(heads, axis=0).T.astype(BF16)


def _dsa(dq, iq, small, dk, dvT, ltri, batch, seq):
    n_q = seq // Q_BLK
    blk = lambda n: pl.BlockSpec((Q_BLK, n), lambda b, i: (b * n_q + i, 0))
    perb_rows = pl.BlockSpec((seq, 128), lambda b, i: (b, 0))
    kern = functools.partial(_dsa_kernel, k_top=min(IDX_TOPK_MAX, seq // 4))
    return pl.pallas_call(
        kern,
        grid=(batch, n_q),
        in_specs=[blk(384), blk(128), perb_rows, perb_rows,
                  pl.BlockSpec((None, seq // SEL_KC, 128, SEL_KC), lambda b, i: (b, 0, 0, 0)),
                  pl.BlockSpec(ltri.shape, lambda b, i: (0, 0))],
        out_specs=blk(384),
        out_shape=jax.ShapeDtypeStruct((batch * seq, 384), BF16),
        scratch_shapes=[pltpu.VMEM((seq, Q_BLK), I32),
                        pltpu.VMEM((32, seq // 32, Q_BLK), I32),
                        pltpu.VMEM((1, DSA_HEADS * Q_BLK), F32),
                        pltpu.VMEM((2 * HEAD_DIM, DSA_HEADS * Q_BLK), F32),
                        pltpu.VMEM((SEL_KC // 2, DSA_HEADS * Q_BLK), F32),
                        pltpu.VMEM((SEL_KC // 2, DSA_HEADS * Q_BLK), F32)],
        compiler_params=_cparams(("parallel", "arbitrary")),
        name="dsa_attn",
    )(dq, iq, small, dk, dvT, ltri)


def _sgu_kernel(z_ref, g_ref, b_ref, w_ref, bs_ref, o_ref):
    row = lax.broadcasted_iota(I32, (SGU_CHUNK, SGU_CHUNK), 0)
    col = lax.broadcasted_iota(I32, (SGU_CHUNK, SGU_CHUNK), 1)
    ws = [jnp.where(row >= col, w_ref[g], 0.0).astype(BF16) for g in range(SGU_GROUPS)]
    for c in range(z_ref.shape[0] // SGU_CHUNK):
        rows = slice(c * SGU_CHUNK, (c + 1) * SGU_CHUNK)
        z = _gelu(z_ref[rows, :])
        outs = []
        for g in range(SGU_GROUPS):
            lo, hi = g * HEAD_DIM, (g + 1) * HEAD_DIM
            u = z[:, lo:hi]
            v = _standardize(z[:, SGU_WIDTH + lo:SGU_WIDTH + hi]) * g_ref[:, lo:hi] + b_ref[:, lo:hi]
            outs.append(u * (_dot(ws[g], v) + bs_ref[:, lo:hi]))
        o_ref[rows, :] = jnp.concatenate(outs, axis=1).astype(BF16)


def _sgu(z, g, b, w, bs):
    t = z.shape[0]
    tm = 4 * SGU_CHUNK
    full = lambda a: pl.BlockSpec(a.shape, lambda i: (0,) * a.ndim)
    return pl.pallas_call(
        _sgu_kernel,
        grid=(t // tm,),
        in_specs=[pl.BlockSpec((tm, 2 * SGU_WIDTH), lambda i: (i, 0)),
                  full(g), full(b), full(w), full(bs)],
        out_specs=pl.BlockSpec((tm, SGU_WIDTH), lambda i: (i, 0)),
        out_shape=jax.ShapeDtypeStruct((t, SGU_WIDTH), BF16),
        compiler_params=_cparams(("parallel",)),
        name="sgu",
    )(z, g, b, w, bs)


def _outproj_kernel(oa_ref, ob_ref, oc_ref, x_ref, mod_ref, w_ref, g_ref, b_ref, o_ref, *, alpha):
    mix = (jnp.dot(oa_ref[...], w_ref[0:384, :], preferred_element_type=F32)
           + jnp.dot(ob_ref[...], w_ref[384:768, :], preferred_element_type=F32)
           + jnp.dot(oc_ref[...], w_ref[768:1024, :], preferred_element_type=F32))
    y = alpha * x_ref[...] + mod_ref[2:3, :] * mix
    o_ref[...] = _standardize(y) * g_ref[...] + b_ref[...]


def _outproj(oa, ob, oc, xf, mod, w, g, b, seq, alpha):
    t, d = xf.shape
    tm = 512
    per_b = seq // tm
    row = lambda n: pl.BlockSpec((tm, n), lambda i: (i, 0))
    full = lambda a: pl.BlockSpec(a.shape, lambda i: (0,) * a.ndim)
    return pl.pallas_call(
        functools.partial(_outproj_kernel, alpha=alpha),
        grid=(t // tm,),
        in_specs=[row(384), row(384), row(256), row(d),
                  pl.BlockSpec((None, 6, d), lambda i: (i // per_b, 0, 0)),
                  full(w), full(g), full(b)],
        out_specs=row(d),
        out_shape=jax.ShapeDtypeStruct((t, d), F32),
        compiler_params=_cparams(("parallel",)),
        name="outproj",
    )(oa, ob, oc, xf, mod, w, g, b)


def _ffn_kernel(x_ref, mod_ref, wg_ref, wu_ref, wd_ref, g_ref, b_ref, o_ref, acc_ref, *, alpha, fc):
    x = x_ref[...]
    hb = (_standardize(x) * (1.0 + mod_ref[4:5, :]) + mod_ref[3:4, :]).astype(BF16)
    d_ff = wg_ref.shape[1]
    for j in range(d_ff // fc):
        cs = slice(j * fc, (j + 1) * fc)
        a = (_silu(jnp.dot(hb, wg_ref[:, cs], preferred_element_type=F32))
             * jnp.dot(hb, wu_ref[:, cs], preferred_element_type=F32)).astype(BF16)
        part = jnp.dot(a, wd_ref[cs, :], preferred_element_type=F32)
        if j == 0:
            acc_ref[...] = part
        else:
            acc_ref[...] += part
    y = alpha * x + mod_ref[5:6, :] * acc_ref[...]
    o_ref[...] = _standardize(y) * g_ref[...] + b_ref[...]


def _ffn(xf, mod, wg, wu, wd, g, b, seq, alpha):
    t, d = xf.shape
    tm = 512
    per_b = seq // tm
    row = pl.BlockSpec((tm, d), lambda i: (i, 0))
    once = lambda a: pl.BlockSpec(a.shape, lambda i: (0,) * a.ndim, pipeline_mode=pl.Buffered(1))
    return pl.pallas_call(
        functools.partial(_ffn_kernel, alpha=alpha, fc=256),
        grid=(t // tm,),
        in_specs=[row, pl.BlockSpec((None, 6, d), lambda i: (i // per_b, 0, 0)),
                  once(wg), once(wu), once(wd), once(g), once(b)],
        out_specs=row,
        out_shape=jax.ShapeDtypeStruct((t, d), F32),
        scratch_shapes=[pltpu.VMEM((tm, d), F32)],
        compiler_params=_cparams(("parallel",)),
        name="ffn",
    )(xf, mod, wg, wu, wd, g, b)


MOE_TM = 1024
MOE_SUB = 128
MOE_BM = 1024
MOE_FC = 512


def _route_kernel(x_ref, mod_ref, wr_ref, br_ref, utri_ref, h_ref, gate_ref, pos_ref, post_ref, cnt_ref):
    h = _standardize(x_ref[...]) * (1.0 + mod_ref[4:5, :]) + mod_ref[3:4, :]
    h_ref[...] = h.astype(BF16)
    lane = lax.broadcasted_iota(I32, (1, LANES), 1)
    logits = jnp.dot(h, wr_ref[...], preferred_element_type=F32,
                     precision=lax.Precision.HIGHEST) + br_ref[...]
    lg = jnp.where(lane < N_EXPERTS, logits, -jnp.inf)
    v0 = jnp.max(lg, axis=-1, keepdims=True)
    lane_f = lane.astype(F32)
    i0 = jnp.min(jnp.where(lg == v0, lane_f, float(LANES)), axis=-1, keepdims=True)
    lg1 = jnp.where(lane_f == i0, -jnp.inf, lg)
    v1 = jnp.max(lg1, axis=-1, keepdims=True)
    i1 = jnp.min(jnp.where(lg1 == v1, lane_f, float(LANES)), axis=-1, keepdims=True)
    e1 = jnp.exp(v1 - v0)
    den = 1.0 + e1
    gate_ref[...] = jnp.where(lane_f == i0, 1.0 / den, 0.0) + jnp.where(lane_f == i1, e1 / den, 0.0)
    sel_t = jnp.where((lane_f == i0) | (lane_f == i1), 1.0, 0.0).T
    rank_t = jnp.dot(sel_t.astype(BF16), utri_ref[...], preferred_element_type=F32)
    pos_t = jnp.where(sel_t > 0.5, rank_t, -1.0)
    post_ref[...] = pos_t[0:N_EXPERTS]
    pos_ref[...] = pos_t.T
    cnt_ref[...] = jnp.broadcast_to(jnp.sum(sel_t[0:N_EXPERTS], axis=1, keepdims=True), (N_EXPERTS, LANES))


def _route(xf, mod, wr, br, utri, seq):
    t, d = xf.shape
    tm = MOE_TM
    nt = t // tm
    per_b = seq // tm
    full = lambda a: pl.BlockSpec(a.shape, lambda i: (0,) * a.ndim)
    return pl.pallas_call(
        _route_kernel,
        grid=(nt,),
        in_specs=[pl.BlockSpec((tm, d), lambda i: (i, 0)),
                  pl.BlockSpec((None, 6, d), lambda i: (i // per_b, 0, 0)),
                  full(wr), full(br), full(utri)],
        out_specs=(pl.BlockSpec((tm, d), lambda i: (i, 0)),
                   pl.BlockSpec((tm, LANES), lambda i: (i, 0)),
                   pl.BlockSpec((tm, LANES), lambda i: (i, 0)),
                   pl.BlockSpec((None, N_EXPERTS, tm), lambda i: (i, 0, 0)),
                   pl.BlockSpec((None, N_EXPERTS, LANES), lambda i: (i, 0, 0))),
        out_shape=(jax.ShapeDtypeStruct((t, d), BF16),
                   jax.ShapeDtypeStruct((t, LANES), F32),
                   jax.ShapeDtypeStruct((t, LANES), F32),
                   jax.ShapeDtypeStruct((nt, N_EXPERTS, tm), F32),
                   jax.ShapeDtypeStruct((nt, N_EXPERTS, LANES), F32)),
        compiler_params=_cparams(("parallel",)),
        name="moe_route",
    )(xf, mod, wr, br, utri)


def _moe_schedule(cnt, n_blocks, k_max):
    nt, ne = cnt.shape
    per = MOE_BM // MOE_SUB
    nb = (cnt + MOE_SUB - 1) // MOE_SUB
    nbt = nb.T
    tot = jnp.sum(nbt, axis=1)
    reg = (tot + per - 1) // per * per
    reg_end = jnp.cumsum(reg)
    reg_start = reg_end - reg
    seg_end = jnp.cumsum(nbt, axis=1)
    seg_start = reg_start[:, None] + seg_end - nbt
    j = jnp.arange(n_blocks, dtype=I32)
    e_j = jnp.minimum(jnp.sum(reg_end[None, :] <= j[:, None], axis=1), ne - 1).astype(I32)
    valid_j = (j - reg_start[e_j]) < tot[e_j]
    step_e = e_j[::per]
    step_valid = valid_j[::per].astype(I32)
    cum = jnp.cumsum(nb, axis=1)
    n_tile = cum[:, -1]
    k = jnp.minimum(jnp.arange(k_max, dtype=I32)[None, :], n_tile[:, None] - 1)
    e_k = jnp.sum(cum[:, None, :] <= k[:, :, None], axis=2).astype(I32)
    tile = jnp.arange(nt, dtype=I32)[:, None]
    s_k = k - (jnp.take_along_axis(cum, e_k, axis=1) - jnp.take_along_axis(nb, e_k, axis=1))
    blk_k = seg_start[e_k, tile] + s_k
    valid_k = (jnp.arange(k_max, dtype=I32)[None, :] < n_tile[:, None]).astype(I32)
    return (step_e, step_valid), (blk_k.astype(I32), e_k, s_k.astype(I32), valid_k)


def _dispatch_kernel(blk_ref, e_ref, s_ref, h_ref, post_ref, gate_ref, xs0_ref, gs0_ref, xs_ref, gs_ref):
    i = pl.program_id(0)
    k = pl.program_id(1)
    e = e_ref[i, k]
    row = post_ref[pl.ds(e, 1), :]
    want = s_ref[i, k] * MOE_SUB + lax.broadcasted_iota(I32, (MOE_SUB, 1), 0)
    onehot = jnp.where(row == want.astype(F32), 1.0, 0.0).astype(BF16)
    xs_ref[...] = jnp.dot(onehot, h_ref[...], preferred_element_type=F32).astype(BF16)
    gates = gate_ref[...]
    g_hi = gates.astype(BF16)
    g_lo = (gates - g_hi.astype(F32)).astype(BF16)
    gs = (jnp.dot(onehot, g_hi, preferred_element_type=F32)
          + jnp.dot(onehot, g_lo, preferred_element_type=F32))
    lane = lax.broadcasted_iota(I32, (1, LANES), 1)
    gcol = jnp.sum(jnp.where(lane == e, gs, 0.0), axis=-1, keepdims=True)
    gs_ref[...] = jnp.broadcast_to(gcol, (MOE_SUB, LANES))


def _dispatch(h, post, gates, sched, n_blocks):
    t, d = h.shape
    tm = MOE_TM
    k_max = sched[0].shape[1]
    rows = n_blocks * MOE_SUB
    grid_spec = pltpu.PrefetchScalarGridSpec(
        num_scalar_prefetch=3,
        grid=(t // tm, k_max),
        in_specs=[pl.BlockSpec((tm, d), lambda i, k, *_: (i, 0)),
                  pl.BlockSpec((None, N_EXPERTS, tm), lambda i, k, *_: (i, 0, 0)),
                  pl.BlockSpec((tm, LANES), lambda i, k, *_: (i, 0)),
                  pl.BlockSpec(memory_space=pl.ANY), pl.BlockSpec(memory_space=pl.ANY)],
        out_specs=(pl.BlockSpec((MOE_SUB, d), lambda i, k, blk, e, s: (blk[i, k], 0)),
                   pl.BlockSpec((MOE_SUB, LANES), lambda i, k, blk, e, s: (blk[i, k], 0))),
    )
    return pl.pallas_call(
        _dispatch_kernel,
        grid_spec=grid_spec,
        out_shape=(jax.ShapeDtypeStruct((rows, d), BF16), jax.ShapeDtypeStruct((rows, LANES), F32)),
        input_output_aliases={6: 0, 7: 1},
        compiler_params=_cparams(("arbitrary", "arbitrary")),
        name="moe_dispatch",
    )(*sched, h, post, gates, jnp.zeros((rows, d), BF16), jnp.zeros((rows, LANES), F32))


def _experts_kernel(e_ref, v_ref, xs_ref, gs_ref, wg_ref, wu_ref, wd_ref, ys_ref, acc_ref):
    jb = pl.program_id(0)
    f = pl.program_id(1)
    last = pl.num_programs(1) - 1

    @pl.when(v_ref[jb] > 0)
    def _():
        xb = xs_ref[...]
        a = (_silu(jnp.dot(xb, wg_ref[...].astype(BF16), preferred_element_type=F32))
             * jnp.dot(xb, wu_ref[...].astype(BF16), preferred_element_type=F32)).astype(BF16)
        part = jnp.dot(a, wd_ref[...].astype(BF16), preferred_element_type=F32)

        @pl.when(f == 0)
        def _():
            acc_ref[...] = part

        @pl.when(f > 0)
        def _():
            acc_ref[...] += part

        @pl.when(f == last)
        def _():
            gate = jnp.concatenate([gs_ref[...]] * (ys_ref.shape[1] // LANES), axis=1)
            ys_ref[...] = (acc_ref[...] * gate).astype(BF16)

    @pl.when((v_ref[jb] == 0) & (f == last))
    def _():
        ys_ref[...] = jnp.zeros(ys_ref.shape, BF16)


def _experts(xs, gs, wg, wu, wd, sched):
    rows, d = xs.shape
    n_e, _, d_ff = wg.shape
    fc = MOE_FC
    n_f = d_ff // fc
    fsel = lambda jb, f, e, v: jnp.where(v[jb] > 0, f, n_f - 1)
    grid_spec = pltpu.PrefetchScalarGridSpec(
        num_scalar_prefetch=2,
        grid=(rows // MOE_BM, n_f),
        in_specs=[pl.BlockSpec((MOE_BM, d), lambda jb, f, e, v: (jb, 0)),
                  pl.BlockSpec((MOE_BM, LANES), lambda jb, f, e, v: (jb, 0)),
                  pl.BlockSpec((None, d, fc), lambda jb, f, e, v: (e[jb], 0, fsel(jb, f, e, v))),
                  pl.BlockSpec((None, d, fc), lambda jb, f, e, v: (e[jb], 0, fsel(jb, f, e, v))),
                  pl.BlockSpec((None, fc, d), lambda jb, f, e, v: (e[jb], fsel(jb, f, e, v), 0))],
        out_specs=pl.BlockSpec((MOE_BM, d), lambda jb, f, e, v: (jb, 0)),
        scratch_shapes=[pltpu.VMEM((MOE_BM, d), F32)],
    )
    return pl.pallas_call(
        _experts_kernel,
        grid_spec=grid_spec,
        out_shape=jax.ShapeDtypeStruct((rows, d), BF16),
        compiler_params=_cparams(("arbitrary", "arbitrary")),
        name="moe_experts",
    )(*sched, xs, gs, wg, wu, wd)


def _combine_kernel(blk_ref, e_ref, s_ref, v_ref, *refs, alpha, k_max):
    ys_refs = refs[:k_max]
    pos_ref, x_ref, mod_ref, g_ref, b_ref, o_ref, pc_ref = refs[k_max:]
    i = pl.program_id(0)
    th = pos_ref.shape[0]
    lane = lax.broadcasted_iota(I32, (1, LANES), 1)
    pos = pos_ref[...]
    for e in range(N_EXPERTS):
        col = jnp.sum(jnp.where(lane == e, pos, 0.0), axis=-1, keepdims=True)
        pc_ref[e] = jnp.broadcast_to(col, (th, LANES))
    hots = []
    for k in range(k_max):
        s = jnp.where(v_ref[i, k] > 0, s_ref[i, k], -2)
        want = (s * MOE_SUB + lane).astype(F32)
        hots.append(jnp.where(pc_ref[e_ref[i, k]] == want, 1.0, 0.0).astype(BF16))
    f = jnp.dot(jnp.concatenate(hots, axis=1), jnp.concatenate([r[...] for r in ys_refs], axis=0),
                preferred_element_type=F32)
    y = alpha * x_ref[...] + mod_ref[5:6, :] * f
    o_ref[...] = _standardize(y) * g_ref[...] + b_ref[...]


def _combine(ys, pos, xf, mod, g, b, sched, seq, alpha):
    t, d = xf.shape
    th = MOE_TM // 2
    per_b = seq // th
    k_max = sched[0].shape[1]
    row = lambda n: pl.BlockSpec((th, n), lambda i, hh, *_: (2 * i + hh, 0))
    full = lambda a: pl.BlockSpec(a.shape, lambda i, hh, *_: (0,) * a.ndim)

    def ys_index(k, i, hh, blk, e, s, v):
        return (blk[i, k], 0)

    grid_spec = pltpu.PrefetchScalarGridSpec(
        num_scalar_prefetch=4,
        grid=(t // MOE_TM, 2),
        in_specs=([pl.BlockSpec((MOE_SUB, d), functools.partial(ys_index, k)) for k in range(k_max)]
                  + [row(LANES), row(d),
                     pl.BlockSpec((None, 6, d), lambda i, hh, *_: ((2 * i + hh) // per_b, 0, 0)),
                     full(g), full(b)]),
        out_specs=row(d),
        scratch_shapes=[pltpu.VMEM((N_EXPERTS, th, LANES), F32)],
    )
    return pl.pallas_call(
        functools.partial(_combine_kernel, alpha=alpha, k_max=k_max),
        grid_spec=grid_spec,
        out_shape=jax.ShapeDtypeStruct((t, d), F32),
        compiler_params=_cparams(("arbitrary", "arbitrary")),
        name="moe_combine",
    )(*sched, *([ys] * k_max), pos, xf, mod, g, b)


def _moe(xf, mod, wr, br, wg, wu, wd, g, b, seq, alpha):
    t, d = xf.shape
    nt = t // MOE_TM
    per = MOE_BM // MOE_SUB
    n_blocks = TOP_K * t // MOE_SUB + nt * N_EXPERTS + N_EXPERTS * (per - 1)
    n_blocks = (n_blocks + per - 1) // per * per
    k_max = TOP_K * MOE_TM // MOE_SUB + N_EXPERTS
    utri = jnp.asarray(np.triu(np.ones((MOE_TM, MOE_TM), np.float32), 1), BF16)
    h, gates, pos, post, cnt = _route(xf, mod, wr, br, utri, seq)
    sched_x, sched_c = _moe_schedule(cnt[:, :, 0].astype(I32), n_blocks, k_max)
    xs, gs = _dispatch(h, post, gates, sched_c[:3], n_blocks)
    ys = _experts(xs, gs, wg, wu, wd, sched_x)
    return _combine(ys, pos, xf, mod, g, b, sched_c, seq, alpha)


def _inproj_columns():
    sizes = (NSA_WIDTH, 6 * NSA_KV_HEADS * HEAD_DIM, 3 * NSA_HEADS, DSA_WIDTH, DSA_LATENT,
             IDX_HEADS * IDX_DIM, IDX_DIM, IDX_HEADS, 2 * SGU_WIDTH)
    starts = np.concatenate([[0], np.cumsum(sizes)])
    o_nq, o_kv, o_g, o_dq, o_ckv, o_iq, o_ik, o_iw, o_sgu = starts[:-1]
    perm = np.full((C_TOTAL,), -1, np.int64)
    perm[C_NQ:C_NQ + 384] = o_nq + np.arange(384)
    perm[C_KV:C_KV + 768] = o_kv + np.arange(768)
    perm[C_DQ:C_DQ + 384] = o_dq + np.arange(384)
    perm[C_CKV:C_CKV + 128] = o_ckv + np.arange(128)
    perm[C_IQ:C_IQ + 128] = o_iq + np.arange(128)
    perm[C_SGU:C_SGU + 512] = o_sgu + np.arange(512)
    perm[C_SMALL:C_SMALL + IDX_DIM] = o_ik + np.arange(IDX_DIM)
    perm[C_SMALL + SMALL_IW:C_SMALL + SMALL_IW + IDX_HEADS] = o_iw + np.arange(IDX_HEADS)
    perm[C_SMALL + SMALL_GATE:C_SMALL + SMALL_GATE + 3 * NSA_HEADS] = o_g + np.arange(3 * NSA_HEADS)
    return perm, int(starts[-1])


def _compress_weights(pos, w1, w2):
    half = CMP_LEN // 2
    eye_g = jnp.eye(NSA_KV_HEADS, dtype=F32)
    eye_j = jnp.eye(2, dtype=F32)

    def big(w1_half):
        w = jnp.einsum('jlde,jk,gh->ljgdkhe', w1_half, eye_j, eye_g)
        return w.reshape(half * 2 * NSA_KV_HEADS * HEAD_DIM, 2 * NSA_KV_HEADS * HEAD_DIM)

    def posrow(p_half):
        p = jnp.broadcast_to(p_half.transpose(1, 0, 2)[:, :, None, :], (half, 2, NSA_KV_HEADS, HEAD_DIM))
        return p.reshape(1, -1)

    w2big = jnp.einsum('jef,jk,gh->jgekhf', w2, eye_j, eye_g).reshape(256, 256)
    return (posrow(pos[:, :half]), posrow(pos[:, half:]),
            big(w1[:, :half]).astype(BF16), big(w1[:, half:]).astype(BF16), w2big.astype(BF16))


def _rope_tables(positions):
    pos = positions.astype(F32).reshape(-1, 1)

    def tab(dim):
        inv = ROPE_THETA ** (-jnp.arange(0, dim, 2, dtype=F32) / dim)
        lane = np.arange(LANES)
        ang = pos * inv[lane % (dim // 2)][None, :]
        sign = np.where(lane % dim < dim // 2, -1.0, 1.0).astype(np.float32)
        return jnp.cos(ang), jnp.sin(ang) * sign

    cos_h, sin_h = tab(HEAD_DIM)
    cos_i, sin_i = tab(IDX_DIM)
    return cos_h, sin_h, cos_i, sin_i


def kernel(x, c, positions, w_ada, b_ada, w_in, nsa_cmp_pos, nsa_cmp_w1, nsa_cmp_w2, dsa_kv_norm, dsa_w_uk, dsa_w_uv, sgu_norm_g, sgu_norm_b, sgu_w, sgu_b, w_out, ln1_g, ln1_b, ln2_g, ln2_b, ffn_w_gate, ffn_w_up, ffn_w_down, moe_w_router, moe_b_router, moe_w_gate, moe_w_up, moe_w_down):
    batch, seq, d = x.shape
    depth = w_ada.shape[0]
    t = batch * seq
    alpha = (2 * depth) ** 0.25
    assert seq % 512 == 0 and seq >= WINDOW + Q_BLK

    tabs = _rope_tables(positions)
    mod_all = _adaln(c, w_ada, b_ada).reshape(depth, batch, 6, d)
    perm, in_width = _inproj_columns()
    perm = jnp.asarray(np.where(perm < 0, in_width, perm), I32)

    n_blk = seq // SEL_BLOCK
    n_half = seq // CMP_STRIDE
    cmp_start = np.arange(n_half)[None, :] * CMP_STRIDE
    blk_start = np.arange(n_blk)[:, None] * SEL_BLOCK
    ovt = jnp.asarray((cmp_start < blk_start + SEL_BLOCK) & (cmp_start + CMP_LEN > blk_start), BF16)
    ltri = jnp.asarray(np.tril(np.ones((SEL_KC, SEL_KC), np.float32), -1), BF16)

    xf = x.reshape(t, d)
    for layer in range(depth):
        mod = mod_all[layer]
        w_pad = jnp.concatenate([w_in[layer], jnp.zeros((d, 1), F32)], axis=1)
        w_ext = jnp.take(w_pad, perm, axis=1).astype(BF16)
        wkv = jnp.concatenate([dsa_w_uk[layer], dsa_w_uv[layer]], axis=1).astype(BF16)
        (nq, nqr, kvcmp, ksel, vselT, kwin, vwinT, dq, dk, dvT, iq, small, sgu_z) = _inproj(
            xf, mod, tabs, w_ext, wkv, dsa_kv_norm[layer].reshape(1, -1), batch, seq)

        plo, phi, wlo, whi, w2big = _compress_weights(nsa_cmp_pos[layer], nsa_cmp_w1[layer], nsa_cmp_w2[layer])
        kc, vct = _compress(kvcmp.reshape(batch, n_half, CMP_STRIDE * 256), plo, phi, wlo, whi, w2big)
        o_a = _nsa(nq, nqr, small, kc, vct, ksel, vselT, kwin, vwinT, ovt, batch, seq)
        o_b = _dsa(dq, iq, small, dk, dvT, ltri, batch, seq)
        bs = jnp.repeat(sgu_b[layer].T, HEAD_DIM, axis=1)
        o_c = _sgu(sgu_z, sgu_norm_g[layer].reshape(1, -1), sgu_norm_b[layer].reshape(1, -1),
                   sgu_w[layer], bs)
        xf = _outproj(o_a, o_b, o_c, xf, mod, w_out[layer].astype(BF16),
                      ln1_g[layer].reshape(1, -1), ln1_b[layer].reshape(1, -1), seq, alpha)

        j = layer // 2
        g2, b2 = ln2_g[layer].reshape(1, -1), ln2_b[layer].reshape(1, -1)
        if layer % 2 == 0:
            xf = _ffn(xf, mod, ffn_w_gate[j].astype(BF16), ffn_w_up[j].astype(BF16),
                      ffn_w_down[j].astype(BF16), g2, b2, seq, alpha)
        else:
            wr = jnp.pad(moe_w_router[j], ((0, 0), (0, LANES - N_EXPERTS)))
            br = jnp.pad(moe_b_router[j], (0, LANES - N_EXPERTS)).reshape(1, -1)
            xf = _moe(xf, mod, wr, br, moe_w_gate[j], moe_w_up[j], moe_w_down[j], g2, b2, seq, alpha)
    return xf.reshape(batch, seq, d)
```

```python
import functools

import numpy as np
import jax
import jax.numpy as jnp
from jax import lax
from jax.experimental import pallas as pl
from jax.experimental.pallas import tpu as pltpu

F32 = jnp.float32
BF16 = jnp.bfloat16
I32 = jnp.int32

HEAD_DIM = 64
Q_BLK = 128
ROPE_THETA = 10000.0
LN_EPS = 1e-5
RMS_EPS = 1e-6

NSA_HEADS = 6
NSA_KV_HEADS = 2
NSA_GROUP = NSA_HEADS // NSA_KV_HEADS
NSA_WIDTH = NSA_HEADS * HEAD_DIM
CMP_LEN = 32
CMP_STRIDE = 16
SEL_BLOCK = 64
SEL_TOP = 16
WINDOW = 512
FORCE_BONUS = 1e4

DSA_HEADS = 6
DSA_WIDTH = DSA_HEADS * HEAD_DIM
DSA_LATENT = 128
IDX_HEADS = 4
IDX_DIM = 32
IDX_TOPK_MAX = 256

SGU_GROUPS = 4
SGU_CHUNK = 128
SGU_WIDTH = SGU_GROUPS * HEAD_DIM

N_EXPERTS = 8
TOP_K = 2

LANES = 128
VMEM_LIMIT = 56 * 1024 * 1024
NEG = -1e30
INT_MIN = -(2 ** 31)

C_NQ = 0
C_KV = 384
C_DQ = 1152
C_CKV = 1536
C_IQ = 1664
C_SGU = 1792
C_SMALL = 2304
C_TOTAL = 2432
SMALL_IW = 32
SMALL_GATE = 36

SEL_KC = 512
Q_SCALE = HEAD_DIM ** -0.5 * 1.4426950408889634


def _cparams(sem):
    return pltpu.CompilerParams(dimension_semantics=sem, vmem_limit_bytes=VMEM_LIMIT)


def _dot(a, b):
    return jnp.dot(a.astype(BF16), b.astype(BF16), preferred_element_type=F32)


def _dot_nt(a, b):
    return lax.dot_general(a.astype(BF16), b.astype(BF16), (((1,), (1,)), ((), ())),
                           preferred_element_type=F32)


def _gelu(x):
    return 0.5 * x * (1.0 + jnp.tanh(0.7978845608028654 * (x + 0.044715 * (x * x * x))))


def _silu(x):
    return x * (1.0 / (1.0 + jnp.exp(-x)))


def _sigmoid(x):
    return 1.0 / (1.0 + jnp.exp(-x))


def _standardize(x):
    mu = jnp.mean(x, axis=-1, keepdims=True)
    xc = x - mu
    var = jnp.mean(xc * xc, axis=-1, keepdims=True)
    return xc * lax.rsqrt(var + LN_EPS)


def _adaln_kernel(c_ref, w_ref, b_ref, o_ref):
    c = c_ref[...]
    o_ref[...] = jnp.dot(_silu(c), w_ref[...], preferred_element_type=F32,
                         precision=lax.Precision.HIGHEST) + b_ref[...]


def _adaln(c, w_ada, b_ada):
    depth, d, n = w_ada.shape
    b = c.shape[0]
    tn = 512
    return pl.pallas_call(
        _adaln_kernel,
        grid=(depth, n // tn),
        in_specs=[pl.BlockSpec((b, d), lambda l, j: (0, 0)),
                  pl.BlockSpec((None, d, tn), lambda l, j: (l, 0, j)),
                  pl.BlockSpec((None, 1, tn), lambda l, j: (l, 0, j))],
        out_specs=pl.BlockSpec((None, b, tn), lambda l, j: (l, 0, j)),
        out_shape=jax.ShapeDtypeStruct((depth, b, n), F32),
        compiler_params=_cparams(("arbitrary", "arbitrary")),
        name="adaln",
    )(c, w_ada, b_ada.reshape(depth, 1, n))


def _inproj_kernel(x_ref, mod_ref, cos_ref, sin_ref, cosi_ref, sini_ref, w_ref, wkv_ref, kvg_ref,
                   nq_ref, nqr_ref, kvcmp_ref, ksel_ref, vselT_ref, kwin_ref, vwinT_ref,
                   dq_ref, dk_ref, dvT_ref, iq_ref, small_ref, sgu_ref):
    tm = x_ref.shape[0]
    shift = mod_ref[0:1, :]
    scale = mod_ref[1:2, :]
    hb = (_standardize(x_ref[...]) * (1.0 + scale) + shift).astype(BF16)

    def proj(c0, n):
        return jnp.dot(hb, w_ref[:, c0:c0 + n], preferred_element_type=F32)

    cos = cos_ref[...]
    sin = sin_ref[...]
    lane = lax.broadcasted_iota(I32, (1, LANES), 1)

    def rope(z, cos_t, sin_t, dim):
        half = dim // 2
        low = (lane & (dim - 1)) < half
        outs = []
        for c in range(z.shape[1] // LANES):
            zc = z[:, c * LANES:(c + 1) * LANES]
            swapped = jnp.where(low, pltpu.roll(zc, LANES - half, 1), pltpu.roll(zc, half, 1))
            outs.append(zc * cos_t + swapped * sin_t)
        return outs[0] if len(outs) == 1 else jnp.concatenate(outs, axis=1)

    zq = proj(C_NQ, 384)
    nq_ref[...] = (zq * Q_SCALE).astype(BF16)
    nqr_ref[...] = (rope(zq, cos, sin, HEAD_DIM) * Q_SCALE).astype(BF16)

    kvcmp_ref[...] = proj(C_KV, 256)
    ksel_ref[...] = rope(proj(C_KV + 256, 128), cos, sin, HEAD_DIM).astype(BF16)
    ones_t = jnp.ones((HEAD_DIM, tm), F32)
    vsel_t = proj(C_KV + 384, 128).T
    vsel_x = jnp.concatenate([vsel_t[:HEAD_DIM], ones_t, vsel_t[HEAD_DIM:], ones_t], axis=0).astype(BF16)
    for j in range(tm // SEL_KC):
        vselT_ref[j] = vsel_x[:, j * SEL_KC:(j + 1) * SEL_KC]
    kwin_ref[...] = rope(proj(C_KV + 512, 128), cos, sin, HEAD_DIM).astype(BF16)
    vwin_t = proj(C_KV + 640, 128).T
    vwin_x = jnp.concatenate([vwin_t[:HEAD_DIM], ones_t, vwin_t[HEAD_DIM:], ones_t], axis=0).astype(BF16)
    for j in range(tm // Q_BLK):
        vwinT_ref[j] = vwin_x[:, j * Q_BLK:(j + 1) * Q_BLK]

    dq_ref[...] = (rope(proj(C_DQ, 384), cos, sin, HEAD_DIM) * Q_SCALE).astype(BF16)

    ckv = proj(C_CKV, 128)
    ckv = ckv * lax.rsqrt(jnp.mean(ckv * ckv, axis=-1, keepdims=True) + RMS_EPS) * kvg_ref[...]
    kd = jnp.dot(ckv.astype(BF16), wkv_ref[...], preferred_element_type=F32)
    first = lane < HEAD_DIM
    dkv = rope(kd, jnp.where(first, cos, 1.0), jnp.where(first, sin, 0.0), HEAD_DIM)
    dk_ref[...] = dkv.astype(BF16)
    dv_x = jnp.concatenate([dkv.T[HEAD_DIM:], ones_t], axis=0).astype(BF16)
    for j in range(tm // SEL_KC):
        dvT_ref[j] = dv_x[:, j * SEL_KC:(j + 1) * SEL_KC]

    cosi = cosi_ref[...]
    sini = sini_ref[...]
    iq_ref[...] = rope(proj(C_IQ, 128), cosi, sini, IDX_DIM).astype(BF16)
    isk = lane < IDX_DIM
    small_ref[...] = rope(proj(C_SMALL, 128), jnp.where(isk, cosi, 1.0), jnp.where(isk, sini, 0.0), IDX_DIM)
    sgu_ref[...] = proj(C_SGU, 512)


def _inproj(xf, mod, tabs, w_ext, wkv, kvg, batch, seq):
    t, d = xf.shape
    tm = 512
    per_b = seq // tm
    row = lambda n: pl.BlockSpec((tm, n), lambda i: (i, 0))
    trk = lambda r: pl.BlockSpec((None, tm // SEL_KC, r, SEL_KC), lambda i: (i // per_b, i % per_b, 0, 0))
    tr128 = pl.BlockSpec((None, tm // Q_BLK, 256, Q_BLK), lambda i: (i // per_b, i % per_b, 0, 0))
    full = lambda a: pl.BlockSpec(a.shape, lambda i: (0,) * a.ndim)
    out_shape = (
        jax.ShapeDtypeStruct((t, 384), BF16),
        jax.ShapeDtypeStruct((t, 384), BF16),
        jax.ShapeDtypeStruct((t, 256), F32),
        jax.ShapeDtypeStruct((t, 128), BF16),
        jax.ShapeDtypeStruct((batch, seq // SEL_KC, 256, SEL_KC), BF16),
        jax.ShapeDtypeStruct((t, 128), BF16),
        jax.ShapeDtypeStruct((batch, seq // Q_BLK, 256, Q_BLK), BF16),
        jax.ShapeDtypeStruct((t, 384), BF16),
        jax.ShapeDtypeStruct((t, 128), BF16),
        jax.ShapeDtypeStruct((batch, seq // SEL_KC, 128, SEL_KC), BF16),
        jax.ShapeDtypeStruct((t, 128), BF16),
        jax.ShapeDtypeStruct((t, 128), F32),
        jax.ShapeDtypeStruct((t, 512), F32),
    )
    out_specs = (row(384), row(384), row(256), row(128), trk(256), row(128), tr128,
                 row(384), row(128), trk(128), row(128), row(128), row(512))
    return pl.pallas_call(
        _inproj_kernel,
        grid=(t // tm,),
        in_specs=[row(d),
                  pl.BlockSpec((None, 6, d), lambda i: (i // per_b, 0, 0)),
                  row(128), row(128), row(128), row(128),
                  full(w_ext), full(wkv), full(kvg)],
        out_specs=out_specs,
        out_shape=out_shape,
        compiler_params=_cparams(("parallel",)),
        name="inproj",
    )(xf, mod, *tabs, w_ext, wkv, kvg)


def _compress_kernel(h_ref, plo_ref, phi_ref, wlo_ref, whi_ref, w2_ref, kc_ref, vct_ref):
    h = h_ref[...]
    a = _dot(h + plo_ref[...], wlo_ref[...])
    b = _dot(h + phi_ref[...], whi_ref[...])
    nh = h.shape[0]
    pre = a + pltpu.roll(b, nh - 1, 0)
    cmp = _dot(_gelu(pre), w2_ref[...])
    kc_ref[...] = cmp[:, :128].astype(BF16)
    v_t = cmp[:, 128:].T
    ones_t = jnp.ones((HEAD_DIM, nh), F32)
    vct_ref[...] = jnp.concatenate([v_t[:HEAD_DIM], ones_t, v_t[HEAD_DIM:], ones_t], axis=0).astype(BF16)


def _compress(kvcmp_h, plo, phi, wlo, whi, w2):
    batch, nh, width = kvcmp_h.shape
    full = lambda a: pl.BlockSpec(a.shape, lambda b: (0,) * a.ndim)
    return pl.pallas_call(
        _compress_kernel,
        grid=(batch,),
        in_specs=[pl.BlockSpec((None, nh, width), lambda b: (b, 0, 0)),
                  full(plo), full(phi), full(wlo), full(whi), full(w2)],
        out_specs=(pl.BlockSpec((None, nh, 128), lambda b: (b, 0, 0)),
                   pl.BlockSpec((None, 256, nh), lambda b: (b, 0, 0))),
        out_shape=(jax.ShapeDtypeStruct((batch, nh, 128), BF16),
                   jax.ShapeDtypeStruct((batch, 256, nh), BF16)),
        compiler_params=_cparams(("parallel",)),
        name="nsa_compress",
    )(kvcmp_h, plo, phi, wlo, whi, w2)


def _softmax_pv(s, mask, vt_ones):
    sm = jnp.where(mask, s, NEG)
    m = jnp.max(sm, axis=0, keepdims=True)
    p = jnp.exp2(sm - m).astype(BF16)
    pv = jnp.dot(vt_ones, p, preferred_element_type=F32)
    dh = vt_ones.shape[0] // 2
    inv = jnp.where(m > 0.5 * NEG, 1.0 / jnp.maximum(pv[dh:dh + 1], 1e-30), 0.0)
    return p, inv, pv[:dh]


def _flash_step(s, on, vt_ones, m_ref, acc_ref):
    m_new, acc_new = _flash_update(s, on, vt_ones, m_ref[...], acc_ref[...])
    acc_ref[...] = acc_new
    m_ref[...] = m_new


def _masked_max(s, on, m_old):
    n_grp = s.shape[1] // Q_BLK
    sm = [jnp.where(on, s[:, j * Q_BLK:(j + 1) * Q_BLK], NEG) for j in range(n_grp)]
    m_new = jnp.maximum(m_old, jnp.concatenate([jnp.max(x, axis=0, keepdims=True) for x in sm], axis=1))
    return sm, m_new


def _exp2_weights(sm, m_new):
    return jnp.concatenate([jnp.exp2(x - m_new[:, j * Q_BLK:(j + 1) * Q_BLK]).astype(BF16)
                            for j, x in enumerate(sm)], axis=1)


def _flash_update(s, on, vt_ones, m_old, acc_old):
    sm, m_new = _masked_max(s, on, m_old)
    p = _exp2_weights(sm, m_new)
    acc_new = jnp.exp2(m_old - m_new) * acc_old + jnp.dot(vt_ones, p, preferred_element_type=F32)
    return m_new, acc_new


def _nsa_kernel(nq_ref, nqr_ref, small_ref, kc_ref, vct_ref, ksel_ref, vselT_ref, kwin_ref,
                vwinT_ref, ovt_ref, o_ref, sc_ref, lim_ref, oc_ref, m_ref, acc_ref, sa_ref, sb_ref,
                *, n_blk, n_sel, n_cmp):
    i = pl.program_id(1)
    t0 = i * Q_BLK
    tq = t0 + lax.broadcasted_iota(I32, (1, Q_BLK), 1)
    tq3 = jnp.concatenate([tq, tq, tq], axis=1)
    nq = nq_ref[...]
    nqr = nqr_ref[...]
    nh = kc_ref.shape[0]

    def stack_heads(q, g):
        hs = [g * NSA_GROUP + r for r in range(NSA_GROUP)]
        return jnp.concatenate([q[:, h * HEAD_DIM:(h + 1) * HEAD_DIM] for h in hs], axis=0)

    qrs = [stack_heads(nqr, g) for g in range(NSA_KV_HEADS)]
    for g in range(NSA_KV_HEADS):
        lo, hi = g * HEAD_DIM, (g + 1) * HEAD_DIM
        s_c = _dot_nt(kc_ref[:, lo:hi], stack_heads(nq, g))
        n_io = lax.broadcasted_iota(I32, (nh, 1), 0)
        m_c = (n_io * CMP_STRIDE + (CMP_LEN - 1) <= tq3) & (n_io < n_cmp)
        p_c, inv_c, oc = _softmax_pv(s_c, m_c, vct_ref[2 * lo:2 * hi, :])
        oc_ref[g] = oc * inv_c

        imp = jnp.zeros((n_blk, Q_BLK), F32)
        for r in range(NSA_GROUP):
            cs = slice(r * Q_BLK, (r + 1) * Q_BLK)
            imp = imp + jnp.dot(ovt_ref[...], p_c[:, cs], preferred_element_type=F32) * inv_c[:, cs]
        j_io = lax.broadcasted_iota(I32, (n_blk, 1), 0)
        cur = tq >> 6
        valid = j_io <= cur
        forced = (j_io == 0) | (j_io == cur) | (j_io == cur - 1)
        score = jnp.where(valid, imp + jnp.where(forced, FORCE_BONUS, 0.0), -jnp.inf)
        sc_ref[...] = score
        rank = jnp.zeros((n_blk, Q_BLK), F32)
        for b in range(n_blk):
            row = sc_ref[b:b + 1, :]
            beats = (row > score) | ((row == score) & (j_io > b))
            rank = rank + jnp.where(beats, 1.0, 0.0)
        lim_ref[g] = jnp.where((rank < n_sel) & valid, tq, -1)

    m_ref[...] = jnp.full(m_ref.shape, NEG, F32)
    acc_ref[...] = jnp.zeros(acc_ref.shape, F32)
    bpc = SEL_KC // SEL_BLOCK

    n_ch = (t0 + Q_BLK + SEL_KC - 1) // SEL_KC

    def scores(c, g):
        k0 = pl.multiple_of(c * SEL_KC, SEL_KC)
        return _dot_nt(ksel_ref[pl.ds(k0, SEL_KC), g * HEAD_DIM:(g + 1) * HEAD_DIM], qrs[g])

    def sel_step(c, carry):
        key = c * SEL_KC + lax.broadcasted_iota(I32, (SEL_KC, 1), 0)

        def flash(s, g):
            rows = [jnp.broadcast_to(lim_ref[g, pl.ds(c * bpc + u, 1), :], (SEL_BLOCK, Q_BLK))
                    for u in range(bpc)]
            on = key <= jnp.concatenate(rows, axis=0)
            _flash_step(s, on, vselT_ref[c, 2 * g * HEAD_DIM:2 * (g + 1) * HEAD_DIM, :],
                        m_ref.at[g], acc_ref.at[g])

        sb_ref[...] = scores(c, 1)
        flash(sa_ref[...], 0)
        sa_ref[...] = scores(jnp.minimum(c + 1, n_ch - 1), 0)
        flash(sb_ref[...], 1)
        return carry

    sa_ref[...] = scores(0, 0)
    lax.fori_loop(0, n_ch, sel_step, 0)

    small_t = small_ref[...].T
    heads = []
    for g in range(NSA_KV_HEADS):
        lo, hi = g * HEAD_DIM, (g + 1) * HEAD_DIM
        acc = acc_ref[g]
        o_s = acc[:HEAD_DIM] / jnp.maximum(acc[HEAD_DIM:HEAD_DIM + 1], 1e-30)
        o_c = oc_ref[g]

        nband = WINDOW // Q_BLK + 1
        cb = jnp.maximum(i - WINDOW // Q_BLK, 0)
        b0 = pl.multiple_of(cb * Q_BLK, Q_BLK)
        s_w = _dot_nt(kwin_ref[pl.ds(b0, nband * Q_BLK), lo:hi], qrs[g])
        key = b0 + lax.broadcasted_iota(I32, (nband * Q_BLK, 1), 0)
        diff = tq3 - key
        vt_w = jnp.concatenate([vwinT_ref[cb + u, 2 * lo:2 * hi, :] for u in range(nband)], axis=1)
        _, inv_w, o_w = _softmax_pv(s_w, (diff >= 0) & (diff < WINDOW), vt_w)
        o_w = o_w * inv_w

        for r in range(NSA_GROUP):
            gi = SMALL_GATE + (g * NSA_GROUP + r) * 3
            gt = _sigmoid(small_t[gi:gi + 3, :])
            cs = slice(r * Q_BLK, (r + 1) * Q_BLK)
            heads.append(gt[0:1, :] * o_c[:, cs] + gt[1:2, :] * o_s[:, cs] + gt[2:3, :] * o_w[:, cs])
    o_ref[...] = jnp.concatenate(heads, axis=0).T.astype(BF16)


def _nsa(nq, nqr, small, kc, vct, ksel, vselT, kwin, vwinT, ovt, batch, seq):
    n_q = seq // Q_BLK
    n_blk = seq // SEL_BLOCK
    n_cmp = (seq - CMP_LEN) // CMP_STRIDE + 1
    nh = kc.shape[1]
    blk = lambda n: pl.BlockSpec((Q_BLK, n), lambda b, i: (b * n_q + i, 0))
    perb2 = lambda r, c: pl.BlockSpec((None, r, c), lambda b, i: (b, 0, 0))
    perb_rows = pl.BlockSpec((seq, 128), lambda b, i: (b, 0))
    kern = functools.partial(_nsa_kernel, n_blk=n_blk, n_sel=min(SEL_TOP, n_blk), n_cmp=n_cmp)
    n_col = NSA_GROUP * Q_BLK
    return pl.pallas_call(
        kern,
        grid=(batch, n_q),
        in_specs=[blk(384), blk(384), blk(128),
                  perb2(nh, 128), perb2(256, nh),
                  perb_rows,
                  pl.BlockSpec((None, seq // SEL_KC, 256, SEL_KC), lambda b, i: (b, 0, 0, 0)),
                  perb_rows,
                  pl.BlockSpec((None, seq // Q_BLK, 256, Q_BLK), lambda b, i: (b, 0, 0, 0)),
                  pl.BlockSpec(ovt.shape, lambda b, i: (0, 0))],
        out_specs=blk(384),
        out_shape=jax.ShapeDtypeStruct((batch * seq, 384), BF16),
        scratch_shapes=[pltpu.VMEM((n_blk, Q_BLK), F32),
                        pltpu.VMEM((NSA_KV_HEADS, n_blk, Q_BLK), I32),
                        pltpu.VMEM((NSA_KV_HEADS, HEAD_DIM, n_col), F32),
                        pltpu.VMEM((NSA_KV_HEADS, 1, n_col), F32),
                        pltpu.VMEM((NSA_KV_HEADS, 2 * HEAD_DIM, n_col), F32),
                        pltpu.VMEM((SEL_KC, n_col), F32), pltpu.VMEM((SEL_KC, n_col), F32)],
        compiler_params=_cparams(("parallel", "arbitrary")),
        name="nsa_attn",
    )(nq, nqr, small, kc, vct, ksel, vselT, kwin, vwinT, ovt)


PLANE_KEYS = 32 * 8


def _bit_transpose32(rows):
    rows = list(rows)
    j, mask = 16, 0x0000FFFF
    while j:
        m32 = np.array(mask, np.uint32).view(np.int32)
        k = 0
        while k < 32:
            t = (rows[k] ^ lax.shift_right_logical(rows[k + j], np.int32(j))) & m32
            rows[k] = rows[k] ^ t
            rows[k + j] = rows[k + j] ^ (t << j)
            k = (k + j + 1) & ~j
        j >>= 1
        mask = (mask ^ (mask << j)) & 0xFFFFFFFF
    return rows

def _dsa_kernel(dq_ref, iq_ref, small_ref, dk_ref, dvT_ref, ltri_ref, o_ref,
                ord_ref, plane_ref, m_ref, acc_ref, sa_ref, sb_ref, *, k_top):
    i = pl.program_id(1)
    t0 = i * Q_BLK
    kc = SEL_KC
    n_ch = (t0 + Q_BLK + kc - 1) // kc
    tq = t0 + lax.broadcasted_iota(I32, (1, Q_BLK), 1)

    small_t = small_ref[pl.ds(pl.multiple_of(t0, Q_BLK), Q_BLK), :].T
    w_rows = [small_t[SMALL_IW + h:SMALL_IW + h + 1, :] * (IDX_HEADS ** -0.5) for h in range(IDX_HEADS)]
    iq = iq_ref[...]
    iqs = jnp.concatenate([iq[:, h * IDX_DIM:(h + 1) * IDX_DIM] for h in range(IDX_HEADS)], axis=0)

    def score_step(c, carry):
        k0 = pl.multiple_of(c * kc, kc)
        ik = small_ref[pl.ds(k0, kc), :][:, 0:IDX_DIM]
        lg = _dot_nt(ik, iqs)
        sc = jnp.zeros((kc, Q_BLK), F32)
        for h in range(IDX_HEADS):
            sc = sc + w_rows[h] * jnp.maximum(lg[:, h * Q_BLK:(h + 1) * Q_BLK], 0.0)
        sc = jnp.where(sc == 0.0, 0.0, sc)
        bits = lax.bitcast_convert_type(sc, I32)
        ordv = bits ^ ((bits >> 31) & 0x7FFFFFFF)
        key = k0 + lax.broadcasted_iota(I32, (kc, 1), 0)
        ordv = jnp.where(key <= tq, ordv, INT_MIN)
        ord_ref[pl.ds(k0, kc), :] = ordv
        u = ordv ^ INT_MIN
        for g in range(kc // PLANE_KEYS):
            rows = [u[g * PLANE_KEYS + r * 8:g * PLANE_KEYS + (r + 1) * 8, :] for r in range(32)]
            cols = _bit_transpose32(rows)
            w0 = pl.multiple_of((c * (kc // PLANE_KEYS) + g) * 8, 8)
            for b in range(32):
                plane_ref[b, pl.ds(w0, 8), :] = cols[31 - b]
        return carry

    lax.fori_loop(0, n_ch, score_step, 0)

    n_words = plane_ref.shape[1] // 8
    alive = [jnp.broadcast_to(jnp.where(w < n_ch * (kc // PLANE_KEYS), -1, 0).astype(I32), (8, Q_BLK))
             for w in range(n_words)]
    k_rem = jnp.full((1, Q_BLK), k_top, I32)
    prefix = jnp.zeros((1, Q_BLK), I32)

    def popcount_rows(words):
        pcs = [lax.population_count(x) for x in words]
        while len(pcs) > 1:
            pcs = [a + b for a, b in zip(pcs[0::2], pcs[1::2])]
        return jnp.sum(pcs[0], axis=0, keepdims=True)

    for bit in range(31, -1, -1):
        ones = [alive[w] & plane_ref[bit, w * 8:(w + 1) * 8, :] for w in range(n_words)]
        cnt = popcount_rows(ones)
        take = cnt >= k_rem
        keep0 = jnp.broadcast_to(jnp.where(take, 0, -1).astype(I32), (8, Q_BLK))
        alive = [ones[w] ^ (alive[w] & keep0) for w in range(n_words)]
        k_rem = jnp.where(take, k_rem, k_rem - cnt)
        prefix = prefix | jnp.where(take, np.int32(INT_MIN) if bit == 31 else np.int32(1 << bit), 0)
    thr = prefix ^ np.int32(INT_MIN)
    n_eq = popcount_rows(alive)
    short = thr == INT_MIN
    need = jnp.where(short, 0, k_rem)
    thr_all = jnp.where(short, INT_MIN + 1, thr)
    no_cut = jnp.min(jnp.where(short | (n_eq == need), 1.0, 0.0)) > 0.5

    q = dq_ref[...]
    qs = jnp.concatenate([q[:, h * HEAD_DIM:(h + 1) * HEAD_DIM] for h in range(DSA_HEADS)], axis=0)
    m_ref[...] = jnp.full(m_ref.shape, NEG, F32)
    acc_ref[...] = jnp.zeros(acc_ref.shape, F32)

    def sweep(mask_fn):
        def attn_step(c, seen):
            k0 = pl.multiple_of(c * kc, kc)
            on, seen = mask_fn(ord_ref[pl.ds(k0, kc), :], seen)
            vt = dvT_ref[c]
            sb_ref[...] = scores(c, 1)
            _flash_step(sa_ref[...], on[:half], vt[:, :half], m_ref, acc_ref)
            sa_ref[...] = scores(jnp.minimum(c + 1, n_ch - 1), 0)
            _flash_step(sb_ref[...], on[half:], vt[:, half:], m_ref, acc_ref)
            return seen
        sa_ref[...] = scores(0, 0)
        lax.fori_loop(0, n_ch, attn_step, jnp.zeros((1, Q_BLK), F32))

    half = kc // 2

    def scores(c, sub):
        k0 = pl.multiple_of(c * kc + sub * half, half)
        return _dot_nt(dk_ref[pl.ds(k0, half), 0:HEAD_DIM], qs)

    @pl.when(no_cut)
    def _():
        sweep(lambda o, seen: (o >= thr_all, seen))

    @pl.when(jnp.logical_not(no_cut))
    def _():
        need_f = need.astype(F32)

        def cut_mask(o, seen):
            eq = jnp.where(o == thr, 1.0, 0.0)
            before = jnp.dot(ltri_ref[...], eq.astype(BF16), preferred_element_type=F32) + seen
            on = (o > thr) | ((eq > 0.5) & (before < need_f))
            return on, seen + jnp.sum(eq, axis=0, keepdims=True)

        sweep(cut_mask)

    acc = acc_ref[...]
    o_t = acc[:HEAD_DIM] / jnp.maximum(acc[HEAD_DIM:HEAD_DIM + 1], 1e-30)
    heads = [o_t[:, h * Q_BLK:(h + 1) * Q_BLK] for h in range(DSA_HEADS)]
    o_ref[...] = jnp.concatenate(heads, axis=0).T.astype(BF16)


def _dsa(dq, iq, small, dk, dvT, ltri, batch, seq):
    n_q = seq // Q_BLK
    blk = lambda n: pl.BlockSpec((Q_BLK, n), lambda b, i: (b * n_q + i, 0))
    perb_rows = pl.BlockSpec((seq, 128), lambda b, i: (b, 0))
    kern = functools.partial(_dsa_kernel, k_top=min(IDX_TOPK_MAX, seq // 4))
    return pl.pallas_call(
        kern,
        grid=(batch, n_q),
        in_specs=[blk(384), blk(128), perb_rows, perb_rows,
                  pl.BlockSpec((None, seq // SEL_KC, 128, SEL_KC), lambda b, i: (b, 0, 0, 0)),
                  pl.BlockSpec(ltri.shape, lambda b, i: (0, 0))],
        out_specs=blk(384),
        out_shape=jax.ShapeDtypeStruct((batch * seq, 384), BF16),
        scratch_shapes=[pltpu.VMEM((seq, Q_BLK), I32),
                        pltpu.VMEM((32, seq // 32, Q_BLK), I32),
                        pltpu.VMEM((1, DSA_HEADS * Q_BLK), F32),
                        pltpu.VMEM((2 * HEAD_DIM, DSA_HEADS * Q_BLK), F32),
                        pltpu.VMEM((SEL_KC // 2, DSA_HEADS * Q_BLK), F32),
                        pltpu.VMEM((SEL_KC // 2, DSA_HEADS * Q_BLK), F32)],
        compiler_params=_cparams(("parallel", "arbitrary")),
        name="dsa_attn",
    )(dq, iq, small, dk, dvT, ltri)


def _sgu_kernel(z_ref, g_ref, b_ref, w_ref, bs_ref, o_ref):
    row = lax.broadcasted_iota(I32, (SGU_CHUNK, SGU_CHUNK), 0)
    col = lax.broadcasted_iota(I32, (SGU_CHUNK, SGU_CHUNK), 1)
    ws = [jnp.where(row >= col, w_ref[g], 0.0).astype(BF16) for g in range(SGU_GROUPS)]
    for c in range(z_ref.shape[0] // SGU_CHUNK):
        rows = slice(c * SGU_CHUNK, (c + 1) * SGU_CHUNK)
        z = _gelu(z_ref[rows, :])
        outs = []
        for g in range(SGU_GROUPS):
            lo, hi = g * HEAD_DIM, (g + 1) * HEAD_DIM
            u = z[:, lo:hi]
            v = _standardize(z[:, SGU_WIDTH + lo:SGU_WIDTH + hi]) * g_ref[:, lo:hi] + b_ref[:, lo:hi]
            outs.append(u * (_dot(ws[g], v) + bs_ref[:, lo:hi]))
        o_ref[rows, :] = jnp.concatenate(outs, axis=1).astype(BF16)


def _sgu(z, g, b, w, bs):
    t = z.shape[0]
    tm = SGU_CHUNK
    full = lambda a: pl.BlockSpec(a.shape, lambda i: (0,) * a.ndim)
    return pl.pallas_call(
        _sgu_kernel,
        grid=(t // tm,),
        in_specs=[pl.BlockSpec((tm, 2 * SGU_WIDTH), lambda i: (i, 0)),
                  full(g), full(b), full(w), full(bs)],
        out_specs=pl.BlockSpec((tm, SGU_WIDTH), lambda i: (i, 0)),
        out_shape=jax.ShapeDtypeStruct((t, SGU_WIDTH), BF16),
        compiler_params=_cparams(("parallel",)),
        name="sgu",
    )(z, g, b, w, bs)


def _outproj_kernel(oa_ref, ob_ref, oc_ref, x_ref, mod_ref, w_ref, g_ref, b_ref, o_ref, *, alpha):
    mix = (jnp.dot(oa_ref[...], w_ref[0:384, :], preferred_element_type=F32)
           + jnp.dot(ob_ref[...], w_ref[384:768, :], preferred_element_type=F32)
           + jnp.dot(oc_ref[...], w_ref[768:1024, :], preferred_element_type=F32))
    y = alpha * x_ref[...] + mod_ref[2:3, :] * mix
    o_ref[...] = _standardize(y) * g_ref[...] + b_ref[...]


def _outproj(oa, ob, oc, xf, mod, w, g, b, seq, alpha):
    t, d = xf.shape
    tm = 512
    per_b = seq // tm
    row = lambda n: pl.BlockSpec((tm, n), lambda i: (i, 0))
    full = lambda a: pl.BlockSpec(a.shape, lambda i: (0,) * a.ndim)
    return pl.pallas_call(
        functools.partial(_outproj_kernel, alpha=alpha),
        grid=(t // tm,),
        in_specs=[row(384), row(384), row(256), row(d),
                  pl.BlockSpec((None, 6, d), lambda i: (i // per_b, 0, 0)),
                  full(w), full(g), full(b)],
        out_specs=row(d),
        out_shape=jax.ShapeDtypeStruct((t, d), F32),
        compiler_params=_cparams(("parallel",)),
        name="outproj",
    )(oa, ob, oc, xf, mod, w, g, b)


def _ffn_kernel(x_ref, mod_ref, wg_ref, wu_ref, wd_ref, g_ref, b_ref, o_ref, acc_ref, *, alpha, fc):
    x = x_ref[...]
    hb = (_standardize(x) * (1.0 + mod_ref[4:5, :]) + mod_ref[3:4, :]).astype(BF16)
    d_ff = wg_ref.shape[1]
    for j in range(d_ff // fc):
        cs = slice(j * fc, (j + 1) * fc)
        a = (_silu(jnp.dot(hb, wg_ref[:, cs], preferred_element_type=F32))
             * jnp.dot(hb, wu_ref[:, cs], preferred_element_type=F32)).astype(BF16)
        part = jnp.dot(a, wd_ref[cs, :], preferred_element_type=F32)
        if j == 0:
            acc_ref[...] = part
        else:
            acc_ref[...] += part
    y = alpha * x + mod_ref[5:6, :] * acc_ref[...]
    o_ref[...] = _standardize(y) * g_ref[...] + b_ref[...]


def _ffn(xf, mod, wg, wu, wd, g, b, seq, alpha):
    t, d = xf.shape
    tm = 512
    per_b = seq // tm
    row = pl.BlockSpec((tm, d), lambda i: (i, 0))
    once = lambda a: pl.BlockSpec(a.shape, lambda i: (0,) * a.ndim, pipeline_mode=pl.Buffered(1))
    return pl.pallas_call(
        functools.partial(_ffn_kernel, alpha=alpha, fc=256),
        grid=(t // tm,),
        in_specs=[row, pl.BlockSpec((None, 6, d), lambda i: (i // per_b, 0, 0)),
                  once(wg), once(wu), once(wd), once(g), once(b)],
        out_specs=row,
        out_shape=jax.ShapeDtypeStruct((t, d), F32),
        scratch_shapes=[pltpu.VMEM((tm, d), F32)],
        compiler_params=_cparams(("parallel",)),
        name="ffn",
    )(xf, mod, wg, wu, wd, g, b)


MOE_TM = 1024
MOE_SUB = 128
MOE_BM = 1024
MOE_FC = 512


def _route_kernel(x_ref, mod_ref, wr_ref, br_ref, utri_ref, h_ref, gate_ref, pos_ref, post_ref, cnt_ref):
    h = _standardize(x_ref[...]) * (1.0 + mod_ref[4:5, :]) + mod_ref[3:4, :]
    h_ref[...] = h.astype(BF16)
    lane = lax.broadcasted_iota(I32, (1, LANES), 1)
    logits = jnp.dot(h, wr_ref[...], preferred_element_type=F32,
                     precision=lax.Precision.HIGHEST) + br_ref[...]
    lg = jnp.where(lane < N_EXPERTS, logits, -jnp.inf)
    v0 = jnp.max(lg, axis=-1, keepdims=True)
    lane_f = lane.astype(F32)
    i0 = jnp.min(jnp.where(lg == v0, lane_f, float(LANES)), axis=-1, keepdims=True)
    lg1 = jnp.where(lane_f == i0, -jnp.inf, lg)
    v1 = jnp.max(lg1, axis=-1, keepdims=True)
    i1 = jnp.min(jnp.where(lg1 == v1, lane_f, float(LANES)), axis=-1, keepdims=True)
    e1 = jnp.exp(v1 - v0)
    den = 1.0 + e1
    gate_ref[...] = jnp.where(lane_f == i0, 1.0 / den, 0.0) + jnp.where(lane_f == i1, e1 / den, 0.0)
    sel_t = jnp.where((lane_f == i0) | (lane_f == i1), 1.0, 0.0).T
    rank_t = jnp.dot(sel_t.astype(BF16), utri_ref[...], preferred_element_type=F32)
    pos_t = jnp.where(sel_t > 0.5, rank_t, -1.0)
    post_ref[...] = pos_t[0:N_EXPERTS]
    pos_ref[...] = pos_t.T
    cnt_ref[...] = jnp.broadcast_to(jnp.sum(sel_t[0:N_EXPERTS], axis=1, keepdims=True), (N_EXPERTS, LANES))


def _route(xf, mod, wr, br, utri, seq):
    t, d = xf.shape
    tm = MOE_TM
    nt = t // tm
    per_b = seq // tm
    full = lambda a: pl.BlockSpec(a.shape, lambda i: (0,) * a.ndim)
    return pl.pallas_call(
        _route_kernel,
        grid=(nt,),
        in_specs=[pl.BlockSpec((tm, d), lambda i: (i, 0)),
                  pl.BlockSpec((None, 6, d), lambda i: (i // per_b, 0, 0)),
                  full(wr), full(br), full(utri)],
        out_specs=(pl.BlockSpec((tm, d), lambda i: (i, 0)),
                   pl.BlockSpec((tm, LANES), lambda i: (i, 0)),
                   pl.BlockSpec((tm, LANES), lambda i: (i, 0)),
                   pl.BlockSpec((None, N_EXPERTS, tm), lambda i: (i, 0, 0)),
                   pl.BlockSpec((None, N_EXPERTS, LANES), lambda i: (i, 0, 0))),
        out_shape=(jax.ShapeDtypeStruct((t, d), BF16),
                   jax.ShapeDtypeStruct((t, LANES), F32),
                   jax.ShapeDtypeStruct((t, LANES), F32),
                   jax.ShapeDtypeStruct((nt, N_EXPERTS, tm), F32),
                   jax.ShapeDtypeStruct((nt, N_EXPERTS, LANES), F32)),
        compiler_params=_cparams(("parallel",)),
        name="moe_route",
    )(xf, mod, wr, br, utri)


def _moe_schedule(cnt, n_blocks, k_max):
    nt, ne = cnt.shape
    per = MOE_BM // MOE_SUB
    nb = (cnt + MOE_SUB - 1) // MOE_SUB
    nbt = nb.T
    tot = jnp.sum(nbt, axis=1)
    reg = (tot + per - 1) // per * per
    reg_end = jnp.cumsum(reg)
    reg_start = reg_end - reg
    seg_end = jnp.cumsum(nbt, axis=1)
    seg_start = reg_start[:, None] + seg_end - nbt
    j = jnp.arange(n_blocks, dtype=I32)
    e_j = jnp.minimum(jnp.sum(reg_end[None, :] <= j[:, None], axis=1), ne - 1).astype(I32)
    valid_j = (j - reg_start[e_j]) < tot[e_j]
    step_e = e_j[::per]
    step_valid = valid_j[::per].astype(I32)
    cum = jnp.cumsum(nb, axis=1)
    n_tile = cum[:, -1]
    k = jnp.minimum(jnp.arange(k_max, dtype=I32)[None, :], n_tile[:, None] - 1)
    e_k = jnp.sum(cum[:, None, :] <= k[:, :, None], axis=2).astype(I32)
    tile = jnp.arange(nt, dtype=I32)[:, None]
    s_k = k - (jnp.take_along_axis(cum, e_k, axis=1) - jnp.take_along_axis(nb, e_k, axis=1))
    blk_k = seg_start[e_k, tile] + s_k
    valid_k = (jnp.arange(k_max, dtype=I32)[None, :] < n_tile[:, None]).astype(I32)
    slot = jnp.arange(nt * k_max, dtype=I32)
    src = jnp.zeros((n_blocks,), I32).at[jnp.where(valid_k > 0, blk_k, n_blocks).reshape(-1)].set(slot, mode='drop')
    return (step_e, step_valid, src), (blk_k.astype(I32), e_k, s_k.astype(I32), valid_k)


def _dispatch_kernel(e_ref, s_ref, h_ref, post_ref, gate_ref, xs_ref, gs_ref, *, k_half):
    i = pl.program_id(0)
    kh = pl.program_id(1)
    lane = lax.broadcasted_iota(I32, (1, LANES), 1)
    sub = lax.broadcasted_iota(I32, (MOE_SUB, 1), 0)
    hots = []
    for u in range(k_half):
        k = kh * k_half + u
        row = post_ref[pl.ds(e_ref[i, k], 1), :]
        want = (s_ref[i, k] * MOE_SUB + sub).astype(F32)
        hots.append(jnp.where(row == want, 1.0, 0.0).astype(BF16))
    onehot = jnp.concatenate(hots, axis=0)
    xs_ref[...] = jnp.dot(onehot, h_ref[...], preferred_element_type=F32).astype(BF16)
    gates = gate_ref[...]
    g_hi = gates.astype(BF16)
    g_lo = (gates - g_hi.astype(F32)).astype(BF16)
    gs = (jnp.dot(onehot, g_hi, preferred_element_type=F32)
          + jnp.dot(onehot, g_lo, preferred_element_type=F32))
    for u in range(k_half):
        rows = slice(u * MOE_SUB, (u + 1) * MOE_SUB)
        mine = lane == e_ref[i, kh * k_half + u]
        gcol = jnp.sum(jnp.where(mine, gs[rows], 0.0), axis=-1, keepdims=True)
        gs_ref[rows, :] = jnp.broadcast_to(gcol, (MOE_SUB, LANES))


def _dispatch(h, post, gates, sched):
    t, d = h.shape
    tm = MOE_TM
    k_max = sched[0].shape[1]
    k_half = k_max // 2
    rows = (t // tm) * k_max * MOE_SUB
    grid_spec = pltpu.PrefetchScalarGridSpec(
        num_scalar_prefetch=2,
        grid=(t // tm, 2),
        in_specs=[pl.BlockSpec((tm, d), lambda i, kh, *_: (i, 0)),
                  pl.BlockSpec((None, N_EXPERTS, tm), lambda i, kh, *_: (i, 0, 0)),
                  pl.BlockSpec((tm, LANES), lambda i, kh, *_: (i, 0))],
        out_specs=(pl.BlockSpec((k_half * MOE_SUB, d), lambda i, kh, *_: (2 * i + kh, 0)),
                   pl.BlockSpec((k_half * MOE_SUB, LANES), lambda i, kh, *_: (2 * i + kh, 0))),
    )
    return pl.pallas_call(
        functools.partial(_dispatch_kernel, k_half=k_half),
        grid_spec=grid_spec,
        out_shape=(jax.ShapeDtypeStruct((rows, d), BF16), jax.ShapeDtypeStruct((rows, LANES), F32)),
        compiler_params=_cparams(("arbitrary", "arbitrary")),
        name="moe_dispatch",
    )(*sched, h, post, gates)


def _experts_kernel(e_ref, v_ref, src_ref, *refs, per):
    xs_refs, gs_refs = refs[:per], refs[per:2 * per]
    wg_ref, wu_ref, wd_ref, ys_ref, acc_ref, xcat_ref = refs[2 * per:]
    jb = pl.program_id(0)
    f = pl.program_id(1)
    last = pl.num_programs(1) - 1

    @pl.when((v_ref[jb] > 0) & (f == 0))
    def _():
        for u in range(per):
            xcat_ref[u * MOE_SUB:(u + 1) * MOE_SUB, :] = xs_refs[u][...]

    @pl.when(v_ref[jb] > 0)
    def _():
        xb = xcat_ref[...]
        a = (_silu(jnp.dot(xb, wg_ref[...].astype(BF16), preferred_element_type=F32))
             * jnp.dot(xb, wu_ref[...].astype(BF16), preferred_element_type=F32)).astype(BF16)
        part = jnp.dot(a, wd_ref[...].astype(BF16), preferred_element_type=F32)

        @pl.when(f == 0)
        def _():
            acc_ref[...] = part

        @pl.when(f > 0)
        def _():
            acc_ref[...] += part

        @pl.when(f == last)
        def _():
            for u in range(per):
                rows = slice(u * MOE_SUB, (u + 1) * MOE_SUB)
                gate = jnp.concatenate([gs_refs[u][...]] * (ys_ref.shape[1] // LANES), axis=1)
                ys_ref[rows, :] = (acc_ref[rows, :] * gate).astype(BF16)

    @pl.when((v_ref[jb] == 0) & (f == last))
    def _():
        ys_ref[...] = jnp.zeros(ys_ref.shape, BF16)


def _experts(xs, gs, wg, wu, wd, sched, n_blocks):
    d = xs.shape[1]
    n_e, _, d_ff = wg.shape
    fc = MOE_FC
    n_f = d_ff // fc
    per = MOE_BM // MOE_SUB
    fsel = lambda jb, f, e, v: jnp.where(v[jb] > 0, f, n_f - 1)

    def src_index(u, jb, f, e, v, src):
        return (src[jb * per + u], 0)

    grid_spec = pltpu.PrefetchScalarGridSpec(
        num_scalar_prefetch=3,
        grid=(n_blocks // per, n_f),
        in_specs=([pl.BlockSpec((MOE_SUB, d), functools.partial(src_index, u)) for u in range(per)]
                  + [pl.BlockSpec((MOE_SUB, LANES), functools.partial(src_index, u)) for u in range(per)]
                  + [pl.BlockSpec((None, d, fc), lambda jb, f, e, v, src: (e[jb], 0, fsel(jb, f, e, v))),
                     pl.BlockSpec((None, d, fc), lambda jb, f, e, v, src: (e[jb], 0, fsel(jb, f, e, v))),
                     pl.BlockSpec((None, fc, d), lambda jb, f, e, v, src: (e[jb], fsel(jb, f, e, v), 0))]),
        out_specs=pl.BlockSpec((MOE_BM, d), lambda jb, f, e, v, src: (jb, 0)),
        scratch_shapes=[pltpu.VMEM((MOE_BM, d), F32), pltpu.VMEM((MOE_BM, d), BF16)],
    )
    return pl.pallas_call(
        functools.partial(_experts_kernel, per=per),
        grid_spec=grid_spec,
        out_shape=jax.ShapeDtypeStruct((n_blocks * MOE_SUB, d), BF16),
        compiler_params=_cparams(("arbitrary", "arbitrary")),
        name="moe_experts",
    )(*sched, *([xs] * per), *([gs] * per), wg, wu, wd)


def _combine_kernel(blk_ref, e_ref, s_ref, v_ref, *refs, alpha, k_max):
    ys_refs = refs[:k_max]
    pos_ref, x_ref, mod_ref, g_ref, b_ref, o_ref, pc_ref = refs[k_max:]
    i = pl.program_id(0)
    th = pos_ref.shape[0]
    lane = lax.broadcasted_iota(I32, (1, LANES), 1)
    pos = pos_ref[...]
    for e in range(N_EXPERTS):
        col = jnp.sum(jnp.where(lane == e, pos, 0.0), axis=-1, keepdims=True)
        pc_ref[e] = jnp.broadcast_to(col, (th, LANES))
    hots = []
    for k in range(k_max):
        s = jnp.where(v_ref[i, k] > 0, s_ref[i, k], -2)
        want = (s * MOE_SUB + lane).astype(F32)
        hots.append(jnp.where(pc_ref[e_ref[i, k]] == want, 1.0, 0.0).astype(BF16))
    f = jnp.dot(jnp.concatenate(hots, axis=1), jnp.concatenate([r[...] for r in ys_refs], axis=0),
                preferred_element_type=F32)
    y = alpha * x_ref[...] + mod_ref[5:6, :] * f
    o_ref[...] = _standardize(y) * g_ref[...] + b_ref[...]


def _combine(ys, pos, xf, mod, g, b, sched, seq, alpha):
    t, d = xf.shape
    th = MOE_TM // 2
    per_b = seq // th
    k_max = sched[0].shape[1]
    row = lambda n: pl.BlockSpec((th, n), lambda i, hh, *_: (2 * i + hh, 0))
    full = lambda a: pl.BlockSpec(a.shape, lambda i, hh, *_: (0,) * a.ndim)

    def ys_index(k, i, hh, blk, e, s, v):
        return (blk[i, k], 0)

    grid_spec = pltpu.PrefetchScalarGridSpec(
        num_scalar_prefetch=4,
        grid=(t // MOE_TM, 2),
        in_specs=([pl.BlockSpec((MOE_SUB, d), functools.partial(ys_index, k)) for k in range(k_max)]
                  + [row(LANES), row(d),
                     pl.BlockSpec((None, 6, d), lambda i, hh, *_: ((2 * i + hh) // per_b, 0, 0)),
                     full(g), full(b)]),
        out_specs=row(d),
        scratch_shapes=[pltpu.VMEM((N_EXPERTS, th, LANES), F32)],
    )
    return pl.pallas_call(
        functools.partial(_combine_kernel, alpha=alpha, k_max=k_max),
        grid_spec=grid_spec,
        out_shape=jax.ShapeDtypeStruct((t, d), F32),
        compiler_params=_cparams(("arbitrary", "arbitrary")),
        name="moe_combine",
    )(*sched, *([ys] * k_max), pos, xf, mod, g, b)


def _moe(xf, mod, wr, br, wg, wu, wd, g, b, seq, alpha):
    t, d = xf.shape
    nt = t // MOE_TM
    per = MOE_BM // MOE_SUB
    n_blocks = TOP_K * t // MOE_SUB + nt * N_EXPERTS + N_EXPERTS * (per - 1)
    n_blocks = (n_blocks + per - 1) // per * per
    k_max = TOP_K * MOE_TM // MOE_SUB + N_EXPERTS
    utri = jnp.asarray(np.triu(np.ones((MOE_TM, MOE_TM), np.float32), 1), BF16)
    h, gates, pos, post, cnt = _route(xf, mod, wr, br, utri, seq)
    sched_x, sched_c = _moe_schedule(cnt[:, :, 0].astype(I32), n_blocks, k_max)
    xs, gs = _dispatch(h, post, gates, sched_c[1:3])
    ys = _experts(xs, gs, wg, wu, wd, sched_x, n_blocks)
    return _combine(ys, pos, xf, mod, g, b, sched_c, seq, alpha)


def _inproj_columns():
    sizes = (NSA_WIDTH, 6 * NSA_KV_HEADS * HEAD_DIM, 3 * NSA_HEADS, DSA_WIDTH, DSA_LATENT,
             IDX_HEADS * IDX_DIM, IDX_DIM, IDX_HEADS, 2 * SGU_WIDTH)
    starts = np.concatenate([[0], np.cumsum(sizes)])
    o_nq, o_kv, o_g, o_dq, o_ckv, o_iq, o_ik, o_iw, o_sgu = starts[:-1]
    perm = np.full((C_TOTAL,), -1, np.int64)
    perm[C_NQ:C_NQ + 384] = o_nq + np.arange(384)
    perm[C_KV:C_KV + 768] = o_kv + np.arange(768)
    perm[C_DQ:C_DQ + 384] = o_dq + np.arange(384)
    perm[C_CKV:C_CKV + 128] = o_ckv + np.arange(128)
    perm[C_IQ:C_IQ + 128] = o_iq + np.arange(128)
    perm[C_SGU:C_SGU + 512] = o_sgu + np.arange(512)
    perm[C_SMALL:C_SMALL + IDX_DIM] = o_ik + np.arange(IDX_DIM)
    perm[C_SMALL + SMALL_IW:C_SMALL + SMALL_IW + IDX_HEADS] = o_iw + np.arange(IDX_HEADS)
    perm[C_SMALL + SMALL_GATE:C_SMALL + SMALL_GATE + 3 * NSA_HEADS] = o_g + np.arange(3 * NSA_HEADS)
    return perm, int(starts[-1])


def _compress_weights(pos, w1, w2):
    half = CMP_LEN // 2
    eye_g = jnp.eye(NSA_KV_HEADS, dtype=F32)
    eye_j = jnp.eye(2, dtype=F32)

    def big(w1_half):
        w = jnp.einsum('jlde,jk,gh->ljgdkhe', w1_half, eye_j, eye_g)
        return w.reshape(half * 2 * NSA_KV_HEADS * HEAD_DIM, 2 * NSA_KV_HEADS * HEAD_DIM)

    def posrow(p_half):
        p = jnp.broadcast_to(p_half.transpose(1, 0, 2)[:, :, None, :], (half, 2, NSA_KV_HEADS, HEAD_DIM))
        return p.reshape(1, -1)

    w2big = jnp.einsum('jef,jk,gh->jgekhf', w2, eye_j, eye_g).reshape(256, 256)
    return (posrow(pos[:, :half]), posrow(pos[:, half:]),
            big(w1[:, :half]).astype(BF16), big(w1[:, half:]).astype(BF16), w2big.astype(BF16))


def _rope_tables(positions):
    pos = positions.astype(F32).reshape(-1, 1)

    def tab(dim):
        inv = ROPE_THETA ** (-jnp.arange(0, dim, 2, dtype=F32) / dim)
        lane = np.arange(LANES)
        ang = pos * inv[lane % (dim // 2)][None, :]
        sign = np.where(lane % dim < dim // 2, -1.0, 1.0).astype(np.float32)
        return jnp.cos(ang), jnp.sin(ang) * sign

    cos_h, sin_h = tab(HEAD_DIM)
    cos_i, sin_i = tab(IDX_DIM)
    return cos_h, sin_h, cos_i, sin_i


def kernel(x, c, positions, w_ada, b_ada, w_in, nsa_cmp_pos, nsa_cmp_w1, nsa_cmp_w2, dsa_kv_norm, dsa_w_uk, dsa_w_uv, sgu_norm_g, sgu_norm_b, sgu_w, sgu_b, w_out, ln1_g, ln1_b, ln2_g, ln2_b, ffn_w_gate, ffn_w_up, ffn_w_down, moe_w_router, moe_b_router, moe_w_gate, moe_w_up, moe_w_down):
    batch, seq, d = x.shape
    depth = w_ada.shape[0]
    t = batch * seq
    alpha = (2 * depth) ** 0.25
    assert seq % 512 == 0 and seq >= WINDOW + Q_BLK

    tabs = _rope_tables(positions)
    mod_all = _adaln(c, w_ada, b_ada).reshape(depth, batch, 6, d)
    perm, in_width = _inproj_columns()
    perm = jnp.asarray(np.where(perm < 0, in_width, perm), I32)

    n_blk = seq // SEL_BLOCK
    n_half = seq // CMP_STRIDE
    cmp_start = np.arange(n_half)[None, :] * CMP_STRIDE
    blk_start = np.arange(n_blk)[:, None] * SEL_BLOCK
    ovt = jnp.asarray((cmp_start < blk_start + SEL_BLOCK) & (cmp_start + CMP_LEN > blk_start), BF16)
    ltri = jnp.asarray(np.tril(np.ones((SEL_KC, SEL_KC), np.float32), -1), BF16)

    xf = x.reshape(t, d)
    for layer in range(depth):
        mod = mod_all[layer]
        w_pad = jnp.concatenate([w_in[layer], jnp.zeros((d, 1), F32)], axis=1)
        w_ext = jnp.take(w_pad, perm, axis=1).astype(BF16)
        wkv = jnp.concatenate([dsa_w_uk[layer], dsa_w_uv[layer]], axis=1).astype(BF16)
        (nq, nqr, kvcmp, ksel, vselT, kwin, vwinT, dq, dk, dvT, iq, small, sgu_z) = _inproj(
            xf, mod, tabs, w_ext, wkv, dsa_kv_norm[layer].reshape(1, -1), batch, seq)

        plo, phi, wlo, whi, w2big = _compress_weights(nsa_cmp_pos[layer], nsa_cmp_w1[layer], nsa_cmp_w2[layer])
        kc, vct = _compress(kvcmp.reshape(batch, n_half, CMP_STRIDE * 256), plo, phi, wlo, whi, w2big)
        o_a = _nsa(nq, nqr, small, kc, vct, ksel, vselT, kwin, vwinT, ovt, batch, seq)
        o_b = _dsa(dq, iq, small, dk, dvT, ltri, batch, seq)
        bs = jnp.repeat(sgu_b[layer].T, HEAD_DIM, axis=1)
        o_c = _sgu(sgu_z, sgu_norm_g[layer].reshape(1, -1), sgu_norm_b[layer].reshape(1, -1),
                   sgu_w[layer], bs)
        xf = _outproj(o_a, o_b, o_c, xf, mod, w_out[layer].astype(BF16),
                      ln1_g[layer].reshape(1, -1), ln1_b[layer].reshape(1, -1), seq, alpha)

        j = layer // 2
        g2, b2 = ln2_g[layer].reshape(1, -1), ln2_b[layer].reshape(1, -1)
        if layer % 2 == 0:
            xf = _ffn(xf, mod, ffn_w_gate[j].astype(BF16), ffn_w_up[j].astype(BF16),
                      ffn_w_down[j].astype(BF16), g2, b2, seq, alpha)
        else:
            wr = jnp.pad(moe_w_router[j], ((0, 0), (0, LANES - N_EXPERTS)))
            br = jnp.pad(moe_b_router[j], (0, LANES - N_EXPERTS)).reshape(1, -1)
            xf = _moe(xf, mod, wr, br, moe_w_gate[j], moe_w_up[j], moe_w_down[j], g2, b2, seq, alpha)
    return xf.reshape(batch, seq, d)
```

```python
import functools

import numpy as np
import jax
import jax.numpy as jnp
from jax import lax
from jax.experimental import pallas as pl
from jax.experimental.pallas import tpu as pltpu

F32 = jnp.float32
BF16 = jnp.bfloat16
I32 = jnp.int32

HEAD_DIM = 64
Q_BLK = 128
ROPE_THETA = 10000.0
LN_EPS = 1e-5
RMS_EPS = 1e-6

NSA_HEADS = 6
NSA_KV_HEADS = 2
NSA_GROUP = NSA_HEADS // NSA_KV_HEADS
NSA_WIDTH = NSA_HEADS * HEAD_DIM
CMP_LEN = 32
CMP_STRIDE = 16
SEL_BLOCK = 64
SEL_TOP = 16
WINDOW = 512
FORCE_BONUS = 1e4

DSA_HEADS = 6
DSA_WIDTH = DSA_HEADS * HEAD_DIM
DSA_LATENT = 128
IDX_HEADS = 4
IDX_DIM = 32
IDX_TOPK_MAX = 256

SGU_GROUPS = 4
SGU_CHUNK = 128
SGU_WIDTH = SGU_GROUPS * HEAD_DIM

N_EXPERTS = 8
TOP_K = 2

LANES = 128
VMEM_LIMIT = 56 * 1024 * 1024
NEG = -1e30
INT_MIN = -(2 ** 31)

C_NQ = 0
C_KV = 384
C_DQ = 1152
C_CKV = 1536
C_IQ = 1664
C_SGU = 1792
C_SMALL = 2304
C_TOTAL = 2432
SMALL_IW = 32
SMALL_GATE = 36

SEL_KC = 512
Q_SCALE = HEAD_DIM ** -0.5 * 1.4426950408889634


def _cparams(sem):
    return pltpu.CompilerParams(dimension_semantics=sem, vmem_limit_bytes=VMEM_LIMIT)


def _dot(a, b):
    return jnp.dot(a.astype(BF16), b.astype(BF16), preferred_element_type=F32)


def _dot_nt(a, b):
    return lax.dot_general(a.astype(BF16), b.astype(BF16), (((1,), (1,)), ((), ())),
                           preferred_element_type=F32)


def _gelu(x):
    return 0.5 * x * (1.0 + jnp.tanh(0.7978845608028654 * (x + 0.044715 * (x * x * x))))


def _silu(x):
    return x * (1.0 / (1.0 + jnp.exp(-x)))


def _sigmoid(x):
    return 1.0 / (1.0 + jnp.exp(-x))


def _standardize(x):
    mu = jnp.mean(x, axis=-1, keepdims=True)
    xc = x - mu
    var = jnp.mean(xc * xc, axis=-1, keepdims=True)
    return xc * lax.rsqrt(var + LN_EPS)


def _adaln_kernel(c_ref, w_ref, b_ref, o_ref):
    c = c_ref[...]
    o_ref[...] = jnp.dot(_silu(c), w_ref[...], preferred_element_type=F32,
                         precision=lax.Precision.HIGHEST) + b_ref[...]


def _adaln(c, w_ada, b_ada):
    depth, d, n = w_ada.shape
    b = c.shape[0]
    tn = 512
    return pl.pallas_call(
        _adaln_kernel,
        grid=(depth, n // tn),
        in_specs=[pl.BlockSpec((b, d), lambda l, j: (0, 0)),
                  pl.BlockSpec((None, d, tn), lambda l, j: (l, 0, j)),
                  pl.BlockSpec((None, 1, tn), lambda l, j: (l, 0, j))],
        out_specs=pl.BlockSpec((None, b, tn), lambda l, j: (l, 0, j)),
        out_shape=jax.ShapeDtypeStruct((depth, b, n), F32),
        compiler_params=_cparams(("arbitrary", "arbitrary")),
        name="adaln",
    )(c, w_ada, b_ada.reshape(depth, 1, n))


def _inproj_kernel(x_ref, mod_ref, cos_ref, sin_ref, cosi_ref, sini_ref, w_ref, wkv_ref, kvg_ref,
                   nq_ref, nqr_ref, kvcmp_ref, ksel_ref, vselT_ref, kwin_ref, vwinT_ref,
                   dq_ref, dk_ref, dvT_ref, iq_ref, small_ref, sgu_ref):
    tm = x_ref.shape[0]
    shift = mod_ref[0:1, :]
    scale = mod_ref[1:2, :]
    hb = (_standardize(x_ref[...]) * (1.0 + scale) + shift).astype(BF16)

    def proj(c0, n):
        return jnp.dot(hb, w_ref[:, c0:c0 + n], preferred_element_type=F32)

    cos = cos_ref[...]
    sin = sin_ref[...]
    lane = lax.broadcasted_iota(I32, (1, LANES), 1)

    def rope(z, cos_t, sin_t, dim):
        half = dim // 2
        low = (lane & (dim - 1)) < half
        outs = []
        for c in range(z.shape[1] // LANES):
            zc = z[:, c * LANES:(c + 1) * LANES]
            swapped = jnp.where(low, pltpu.roll(zc, LANES - half, 1), pltpu.roll(zc, half, 1))
            outs.append(zc * cos_t + swapped * sin_t)
        return outs[0] if len(outs) == 1 else jnp.concatenate(outs, axis=1)

    zq = proj(C_NQ, 384)
    nq_ref[...] = (zq * Q_SCALE).astype(BF16)
    nqr_ref[...] = (rope(zq, cos, sin, HEAD_DIM) * Q_SCALE).astype(BF16)

    kvcmp_ref[...] = proj(C_KV, 256)
    ksel_ref[...] = rope(proj(C_KV + 256, 128), cos, sin, HEAD_DIM).astype(BF16)
    ones_t = jnp.ones((HEAD_DIM, tm), F32)
    vsel_t = proj(C_KV + 384, 128).T
    vsel_x = jnp.concatenate([vsel_t[:HEAD_DIM], ones_t, vsel_t[HEAD_DIM:], ones_t], axis=0).astype(BF16)
    for j in range(tm // SEL_KC):
        vselT_ref[j] = vsel_x[:, j * SEL_KC:(j + 1) * SEL_KC]
    kwin_ref[...] = rope(proj(C_KV + 512, 128), cos, sin, HEAD_DIM).astype(BF16)
    vwin_t = proj(C_KV + 640, 128).T
    vwin_x = jnp.concatenate([vwin_t[:HEAD_DIM], ones_t, vwin_t[HEAD_DIM:], ones_t], axis=0).astype(BF16)
    for j in range(tm // Q_BLK):
        vwinT_ref[j] = vwin_x[:, j * Q_BLK:(j + 1) * Q_BLK]

    dq_ref[...] = (rope(proj(C_DQ, 384), cos, sin, HEAD_DIM) * Q_SCALE).astype(BF16)

    ckv = proj(C_CKV, 128)
    ckv = ckv * lax.rsqrt(jnp.mean(ckv * ckv, axis=-1, keepdims=True) + RMS_EPS) * kvg_ref[...]
    kd = jnp.dot(ckv.astype(BF16), wkv_ref[...], preferred_element_type=F32)
    first = lane < HEAD_DIM
    dkv = rope(kd, jnp.where(first, cos, 1.0), jnp.where(first, sin, 0.0), HEAD_DIM)
    dk_ref[...] = dkv.astype(BF16)
    dv_x = jnp.concatenate([dkv.T[HEAD_DIM:], ones_t], axis=0).astype(BF16)
    for j in range(tm // SEL_KC):
        dvT_ref[j] = dv_x[:, j * SEL_KC:(j + 1) * SEL_KC]

    cosi = cosi_ref[...]
    sini = sini_ref[...]
    iq_ref[...] = rope(proj(C_IQ, 128), cosi, sini, IDX_DIM).astype(BF16)
    isk = lane < IDX_DIM
    small_ref[...] = rope(proj(C_SMALL, 128), jnp.where(isk, cosi, 1.0), jnp.where(isk, sini, 0.0), IDX_DIM)
    sgu_ref[...] = proj(C_SGU, 512)


def _inproj(xf, mod, tabs, w_ext, wkv, kvg, batch, seq):
    t, d = xf.shape
    tm = 512
    per_b = seq // tm
    row = lambda n: pl.BlockSpec((tm, n), lambda i: (i, 0))
    trk = lambda r: pl.BlockSpec((None, tm // SEL_KC, r, SEL_KC), lambda i: (i // per_b, i % per_b, 0, 0))
    tr128 = pl.BlockSpec((None, tm // Q_BLK, 256, Q_BLK), lambda i: (i // per_b, i % per_b, 0, 0))
    full = lambda a: pl.BlockSpec(a.shape, lambda i: (0,) * a.ndim)
    out_shape = (
        jax.ShapeDtypeStruct((t, 384), BF16),
        jax.ShapeDtypeStruct((t, 384), BF16),
        jax.ShapeDtypeStruct((t, 256), F32),
        jax.ShapeDtypeStruct((t, 128), BF16),
        jax.ShapeDtypeStruct((batch, seq // SEL_KC, 256, SEL_KC), BF16),
        jax.ShapeDtypeStruct((t, 128), BF16),
        jax.ShapeDtypeStruct((batch, seq // Q_BLK, 256, Q_BLK), BF16),
        jax.ShapeDtypeStruct((t, 384), BF16),
        jax.ShapeDtypeStruct((t, 128), BF16),
        jax.ShapeDtypeStruct((batch, seq // SEL_KC, 128, SEL_KC), BF16),
        jax.ShapeDtypeStruct((t, 128), BF16),
        jax.ShapeDtypeStruct((t, 128), F32),
        jax.ShapeDtypeStruct((t, 512), F32),
    )
    out_specs = (row(384), row(384), row(256), row(128), trk(256), row(128), tr128,
                 row(384), row(128), trk(128), row(128), row(128), row(512))
    return pl.pallas_call(
        _inproj_kernel,
        grid=(t // tm,),
        in_specs=[row(d),
                  pl.BlockSpec((None, 6, d), lambda i: (i // per_b, 0, 0)),
                  row(128), row(128), row(128), row(128),
                  full(w_ext), full(wkv), full(kvg)],
        out_specs=out_specs,
        out_shape=out_shape,
        compiler_params=_cparams(("parallel",)),
        name="inproj",
    )(xf, mod, *tabs, w_ext, wkv, kvg)


def _compress_kernel(h_ref, plo_ref, phi_ref, wlo_ref, whi_ref, w2_ref, kc_ref, vct_ref):
    h = h_ref[...]
    a = _dot(h + plo_ref[...], wlo_ref[...])
    b = _dot(h + phi_ref[...], whi_ref[...])
    nh = h.shape[0]
    pre = a + pltpu.roll(b, nh - 1, 0)
    cmp = _dot(_gelu(pre), w2_ref[...])
    kc_ref[...] = cmp[:, :128].astype(BF16)
    v_t = cmp[:, 128:].T
    ones_t = jnp.ones((HEAD_DIM, nh), F32)
    vct_ref[...] = jnp.concatenate([v_t[:HEAD_DIM], ones_t, v_t[HEAD_DIM:], ones_t], axis=0).astype(BF16)


def _compress(kvcmp_h, plo, phi, wlo, whi, w2):
    batch, nh, width = kvcmp_h.shape
    full = lambda a: pl.BlockSpec(a.shape, lambda b: (0,) * a.ndim)
    return pl.pallas_call(
        _compress_kernel,
        grid=(batch,),
        in_specs=[pl.BlockSpec((None, nh, width), lambda b: (b, 0, 0)),
                  full(plo), full(phi), full(wlo), full(whi), full(w2)],
        out_specs=(pl.BlockSpec((None, nh, 128), lambda b: (b, 0, 0)),
                   pl.BlockSpec((None, 256, nh), lambda b: (b, 0, 0))),
        out_shape=(jax.ShapeDtypeStruct((batch, nh, 128), BF16),
                   jax.ShapeDtypeStruct((batch, 256, nh), BF16)),
        compiler_params=_cparams(("parallel",)),
        name="nsa_compress",
    )(kvcmp_h, plo, phi, wlo, whi, w2)


def _softmax_pv(s, mask, vt_ones):
    sm = jnp.where(mask, s, NEG)
    m = jnp.max(sm, axis=0, keepdims=True)
    p = jnp.exp2(sm - m).astype(BF16)
    pv = jnp.dot(vt_ones, p, preferred_element_type=F32)
    dh = vt_ones.shape[0] // 2
    inv = jnp.where(m > 0.5 * NEG, 1.0 / jnp.maximum(pv[dh:dh + 1], 1e-30), 0.0)
    return p, inv, pv[:dh]


def _flash_step(s, on, vt_ones, m_ref, acc_ref):
    m_new, acc_new = _flash_update(s, on, vt_ones, m_ref[...], acc_ref[...])
    acc_ref[...] = acc_new
    m_ref[...] = m_new


def _masked_max(s, on, m_old):
    n_grp = s.shape[1] // Q_BLK
    sm = [jnp.where(on, s[:, j * Q_BLK:(j + 1) * Q_BLK], NEG) for j in range(n_grp)]
    m_new = jnp.maximum(m_old, jnp.concatenate([jnp.max(x, axis=0, keepdims=True) for x in sm], axis=1))
    return sm, m_new


def _exp2_weights(sm, m_new):
    return jnp.concatenate([jnp.exp2(x - m_new[:, j * Q_BLK:(j + 1) * Q_BLK]).astype(BF16)
                            for j, x in enumerate(sm)], axis=1)


def _flash_update(s, on, vt_ones, m_old, acc_old):
    sm, m_new = _masked_max(s, on, m_old)
    p = _exp2_weights(sm, m_new)
    acc_new = jnp.exp2(m_old - m_new) * acc_old + jnp.dot(vt_ones, p, preferred_element_type=F32)
    return m_new, acc_new


def _nsa_kernel(nq_ref, nqr_ref, small_ref, kc_ref, vct_ref, ksel_ref, vselT_ref, kwin_ref,
                vwinT_ref, ovt_ref, o_ref, sc_ref, lim_ref, oc_ref, m_ref, acc_ref, sa_ref, sb_ref,
                *, n_blk, n_sel, n_cmp):
    i = pl.program_id(1)
    t0 = i * Q_BLK
    tq = t0 + lax.broadcasted_iota(I32, (1, Q_BLK), 1)
    tq3 = jnp.concatenate([tq, tq, tq], axis=1)
    nq = nq_ref[...]
    nqr = nqr_ref[...]
    nh = kc_ref.shape[0]

    def stack_heads(q, g):
        hs = [g * NSA_GROUP + r for r in range(NSA_GROUP)]
        return jnp.concatenate([q[:, h * HEAD_DIM:(h + 1) * HEAD_DIM] for h in hs], axis=0)

    qrs = [stack_heads(nqr, g) for g in range(NSA_KV_HEADS)]
    for g in range(NSA_KV_HEADS):
        lo, hi = g * HEAD_DIM, (g + 1) * HEAD_DIM
        s_c = _dot_nt(kc_ref[:, lo:hi], stack_heads(nq, g))
        n_io = lax.broadcasted_iota(I32, (nh, 1), 0)
        m_c = (n_io * CMP_STRIDE + (CMP_LEN - 1) <= tq3) & (n_io < n_cmp)
        p_c, inv_c, oc = _softmax_pv(s_c, m_c, vct_ref[2 * lo:2 * hi, :])
        oc_ref[g] = oc * inv_c

        imp = jnp.zeros((n_blk, Q_BLK), F32)
        for r in range(NSA_GROUP):
            cs = slice(r * Q_BLK, (r + 1) * Q_BLK)
            imp = imp + jnp.dot(ovt_ref[...], p_c[:, cs], preferred_element_type=F32) * inv_c[:, cs]
        j_io = lax.broadcasted_iota(I32, (n_blk, 1), 0)
        cur = tq >> 6
        valid = j_io <= cur
        forced = (j_io == 0) | (j_io == cur) | (j_io == cur - 1)
        score = jnp.where(valid, imp + jnp.where(forced, FORCE_BONUS, 0.0), -jnp.inf)
        sc_ref[...] = score
        rank = jnp.zeros((n_blk, Q_BLK), F32)
        for b in range(n_blk):
            row = sc_ref[b:b + 1, :]
            beats = (row > score) | ((row == score) & (j_io > b))
            rank = rank + jnp.where(beats, 1.0, 0.0)
        lim_ref[g] = jnp.where((rank < n_sel) & valid, tq, -1)

    m_ref[...] = jnp.full(m_ref.shape, NEG, F32)
    acc_ref[...] = jnp.zeros(acc_ref.shape, F32)
    bpc = SEL_KC // SEL_BLOCK

    n_ch = (t0 + Q_BLK + SEL_KC - 1) // SEL_KC

    def scores(c, g):
        k0 = pl.multiple_of(c * SEL_KC, SEL_KC)
        return _dot_nt(ksel_ref[pl.ds(k0, SEL_KC), g * HEAD_DIM:(g + 1) * HEAD_DIM], qrs[g])

    def sel_step(c, carry):
        key = c * SEL_KC + lax.broadcasted_iota(I32, (SEL_KC, 1), 0)

        def flash(s, g):
            rows = [jnp.broadcast_to(lim_ref[g, pl.ds(c * bpc + u, 1), :], (SEL_BLOCK, Q_BLK))
                    for u in range(bpc)]
            on = key <= jnp.concatenate(rows, axis=0)
            _flash_step(s, on, vselT_ref[c, 2 * g * HEAD_DIM:2 * (g + 1) * HEAD_DIM, :],
                        m_ref.at[g], acc_ref.at[g])

        sb_ref[...] = scores(c, 1)
        flash(sa_ref[...], 0)
        sa_ref[...] = scores(jnp.minimum(c + 1, n_ch - 1), 0)
        flash(sb_ref[...], 1)
        return carry

    sa_ref[...] = scores(0, 0)
    lax.fori_loop(0, n_ch, sel_step, 0)

    small_t = small_ref[...].T
    heads = []
    for g in range(NSA_KV_HEADS):
        lo, hi = g * HEAD_DIM, (g + 1) * HEAD_DIM
        acc = acc_ref[g]
        o_s = acc[:HEAD_DIM] / jnp.maximum(acc[HEAD_DIM:HEAD_DIM + 1], 1e-30)
        o_c = oc_ref[g]

        nband = WINDOW // Q_BLK + 1
        cb = jnp.maximum(i - WINDOW // Q_BLK, 0)
        b0 = pl.multiple_of(cb * Q_BLK, Q_BLK)
        s_w = _dot_nt(kwin_ref[pl.ds(b0, nband * Q_BLK), lo:hi], qrs[g])
        key = b0 + lax.broadcasted_iota(I32, (nband * Q_BLK, 1), 0)
        diff = tq3 - key
        vt_w = jnp.concatenate([vwinT_ref[cb + u, 2 * lo:2 * hi, :] for u in range(nband)], axis=1)
        _, inv_w, o_w = _softmax_pv(s_w, (diff >= 0) & (diff < WINDOW), vt_w)
        o_w = o_w * inv_w

        for r in range(NSA_GROUP):
            gi = SMALL_GATE + (g * NSA_GROUP + r) * 3
            gt = _sigmoid(small_t[gi:gi + 3, :])
            cs = slice(r * Q_BLK, (r + 1) * Q_BLK)
            heads.append(gt[0:1, :] * o_c[:, cs] + gt[1:2, :] * o_s[:, cs] + gt[2:3, :] * o_w[:, cs])
    o_ref[...] = jnp.concatenate(heads, axis=0).T.astype(BF16)


def _nsa(nq, nqr, small, kc, vct, ksel, vselT, kwin, vwinT, ovt, batch, seq):
    n_q = seq // Q_BLK
    n_blk = seq // SEL_BLOCK
    n_cmp = (seq - CMP_LEN) // CMP_STRIDE + 1
    nh = kc.shape[1]
    blk = lambda n: pl.BlockSpec((Q_BLK, n), lambda b, i: (b * n_q + i, 0))
    perb2 = lambda r, c: pl.BlockSpec((None, r, c), lambda b, i: (b, 0, 0))
    perb_rows = pl.BlockSpec((seq, 128), lambda b, i: (b, 0))
    kern = functools.partial(_nsa_kernel, n_blk=n_blk, n_sel=min(SEL_TOP, n_blk), n_cmp=n_cmp)
    n_col = NSA_GROUP * Q_BLK
    return pl.pallas_call(
        kern,
        grid=(batch, n_q),
        in_specs=[blk(384), blk(384), blk(128),
                  perb2(nh, 128), perb2(256, nh),
                  perb_rows,
                  pl.BlockSpec((None, seq // SEL_KC, 256, SEL_KC), lambda b, i: (b, 0, 0, 0)),
                  perb_rows,
                  pl.BlockSpec((None, seq // Q_BLK, 256, Q_BLK), lambda b, i: (b, 0, 0, 0)),
                  pl.BlockSpec(ovt.shape, lambda b, i: (0, 0))],
        out_specs=blk(384),
        out_shape=jax.ShapeDtypeStruct((batch * seq, 384), BF16),
        scratch_shapes=[pltpu.VMEM((n_blk, Q_BLK), F32),
                        pltpu.VMEM((NSA_KV_HEADS, n_blk, Q_BLK), I32),
                        pltpu.VMEM((NSA_KV_HEADS, HEAD_DIM, n_col), F32),
                        pltpu.VMEM((NSA_KV_HEADS, 1, n_col), F32),
                        pltpu.VMEM((NSA_KV_HEADS, 2 * HEAD_DIM, n_col), F32),
                        pltpu.VMEM((SEL_KC, n_col), F32), pltpu.VMEM((SEL_KC, n_col), F32)],
        compiler_params=_cparams(("parallel", "arbitrary")),
        name="nsa_attn",
    )(nq, nqr, small, kc, vct, ksel, vselT, kwin, vwinT, ovt)


PLANE_KEYS = 32 * 8


def _bit_transpose32(rows):
    rows = list(rows)
    j, mask = 16, 0x0000FFFF
    while j:
        m32 = np.array(mask, np.uint32).view(np.int32)
        k = 0
        while k < 32:
            t = (rows[k] ^ lax.shift_right_logical(rows[k + j], np.int32(j))) & m32
            rows[k] = rows[k] ^ t
            rows[k + j] = rows[k + j] ^ (t << j)
            k = (k + j + 1) & ~j
        j >>= 1
        mask = (mask ^ (mask << j)) & 0xFFFFFFFF
    return rows

def _dsa_kernel(dq_ref, iq_ref, small_ref, dk_ref, dvT_ref, ltri_ref, o_ref,
                ord_ref, plane_ref, m_ref, acc_ref, sa_ref, sb_ref, *, k_top):
    i = pl.program_id(1)
    t0 = i * Q_BLK
    kc = SEL_KC
    n_ch = (t0 + Q_BLK + kc - 1) // kc
    tq = t0 + lax.broadcasted_iota(I32, (1, Q_BLK), 1)

    small_t = small_ref[pl.ds(pl.multiple_of(t0, Q_BLK), Q_BLK), :].T
    w_rows = [small_t[SMALL_IW + h:SMALL_IW + h + 1, :] * (IDX_HEADS ** -0.5) for h in range(IDX_HEADS)]
    iq = iq_ref[...]
    iqs = jnp.concatenate([iq[:, h * IDX_DIM:(h + 1) * IDX_DIM] for h in range(IDX_HEADS)], axis=0)

    def score_step(c, carry):
        k0 = pl.multiple_of(c * kc, kc)
        ik = small_ref[pl.ds(k0, kc), :][:, 0:IDX_DIM]
        lg = _dot_nt(ik, iqs)
        sc = jnp.zeros((kc, Q_BLK), F32)
        for h in range(IDX_HEADS):
            sc = sc + w_rows[h] * jnp.maximum(lg[:, h * Q_BLK:(h + 1) * Q_BLK], 0.0)
        sc = jnp.where(sc == 0.0, 0.0, sc)
        bits = lax.bitcast_convert_type(sc, I32)
        ordv = bits ^ ((bits >> 31) & 0x7FFFFFFF)
        key = k0 + lax.broadcasted_iota(I32, (kc, 1), 0)
        ordv = jnp.where(key <= tq, ordv, INT_MIN)
        ord_ref[pl.ds(k0, kc), :] = ordv
        u = ordv ^ INT_MIN
        for g in range(kc // PLANE_KEYS):
            rows = [u[g * PLANE_KEYS + r * 8:g * PLANE_KEYS + (r + 1) * 8, :] for r in range(32)]
            cols = _bit_transpose32(rows)
            w0 = pl.multiple_of((c * (kc // PLANE_KEYS) + g) * 8, 8)
            for b in range(32):
                plane_ref[b, pl.ds(w0, 8), :] = cols[31 - b]
        return carry

    lax.fori_loop(0, n_ch, score_step, 0)

    def clear_step(c, carry):
        w0 = pl.multiple_of(c * (kc // PLANE_KEYS) * 8, 8)
        for b in range(32):
            plane_ref[b, pl.ds(w0, (kc // PLANE_KEYS) * 8), :] = jnp.zeros(((kc // PLANE_KEYS) * 8, Q_BLK), I32)
        return carry

    lax.fori_loop(n_ch, plane_ref.shape[1] * 32 // kc, clear_step, 0)

    n_words = plane_ref.shape[1] // 8
    alive = [jnp.broadcast_to(jnp.where(w < n_ch * (kc // PLANE_KEYS), -1, 0).astype(I32), (8, Q_BLK))
             for w in range(n_words)]
    k_rem = jnp.full((1, Q_BLK), k_top, I32)
    prefix = jnp.zeros((1, Q_BLK), I32)

    def popcount_rows(words):
        pcs = [lax.population_count(x) for x in words]
        while len(pcs) > 1:
            pcs = [a + b for a, b in zip(pcs[0::2], pcs[1::2])]
        return jnp.sum(pcs[0], axis=0, keepdims=True)

    for bit in range(31, -1, -1):
        ones = [alive[w] & plane_ref[bit, w * 8:(w + 1) * 8, :] for w in range(n_words)]
        cnt = popcount_rows(ones)
        take = cnt >= k_rem
        keep0 = jnp.broadcast_to(jnp.where(take, 0, -1).astype(I32), (8, Q_BLK))
        alive = [ones[w] ^ (alive[w] & keep0) for w in range(n_words)]
        k_rem = jnp.where(take, k_rem, k_rem - cnt)
        prefix = prefix | jnp.where(take, np.int32(INT_MIN) if bit == 31 else np.int32(1 << bit), 0)
    thr = prefix ^ np.int32(INT_MIN)
    n_eq = popcount_rows(alive)
    short = thr == INT_MIN
    need = jnp.where(short, 0, k_rem)
    thr_all = jnp.where(short, INT_MIN + 1, thr)
    no_cut = jnp.min(jnp.where(short | (n_eq == need), 1.0, 0.0)) > 0.5

    q = dq_ref[...]
    qs = jnp.concatenate([q[:, h * HEAD_DIM:(h + 1) * HEAD_DIM] for h in range(DSA_HEADS)], axis=0)
    m_ref[...] = jnp.full(m_ref.shape, NEG, F32)
    acc_ref[...] = jnp.zeros(acc_ref.shape, F32)

    def sweep(mask_fn):
        def attn_step(c, seen):
            k0 = pl.multiple_of(c * kc, kc)
            on, seen = mask_fn(ord_ref[pl.ds(k0, kc), :], seen)
            vt = dvT_ref[c]
            sb_ref[...] = scores(c, 1)
            _flash_step(sa_ref[...], on[:half], vt[:, :half], m_ref, acc_ref)
            sa_ref[...] = scores(jnp.minimum(c + 1, n_ch - 1), 0)
            _flash_step(sb_ref[...], on[half:], vt[:, half:], m_ref, acc_ref)
            return seen
        sa_ref[...] = scores(0, 0)
        lax.fori_loop(0, n_ch, attn_step, jnp.zeros((1, Q_BLK), F32))

    half = kc // 2

    def scores(c, sub):
        k0 = pl.multiple_of(c * kc + sub * half, half)
        return _dot_nt(dk_ref[pl.ds(k0, half), 0:HEAD_DIM], qs)

    @pl.when(no_cut)
    def _():
        sweep(lambda o, seen: (o >= thr_all, seen))

    @pl.when(jnp.logical_not(no_cut))
    def _():
        need_f = need.astype(F32)

        def cut_mask(o, seen):
            eq = jnp.where(o == thr, 1.0, 0.0)
            before = jnp.dot(ltri_ref[...], eq.astype(BF16), preferred_element_type=F32) + seen
            on = (o > thr) | ((eq > 0.5) & (before < need_f))
            return on, seen + jnp.sum(eq, axis=0, keepdims=True)

        sweep(cut_mask)

    acc = acc_ref[...]
    o_t = acc[:HEAD_DIM] / jnp.maximum(acc[HEAD_DIM:HEAD_DIM + 1], 1e-30)
    heads = [o_t[:, h * Q_BLK:(h + 1) * Q_BLK] for h in range(DSA_HEADS)]
    o_ref[...] = jnp.concatenate(heads, axis=0).T.astype(BF16)


def _dsa(dq, iq, small, dk, dvT, ltri, batch, seq):
    n_q = seq // Q_BLK
    blk = lambda n: pl.BlockSpec((Q_BLK, n), lambda b, i: (b * n_q + i, 0))
    perb_rows = pl.BlockSpec((seq, 128), lambda b, i: (b, 0))
    kern = functools.partial(_dsa_kernel, k_top=min(IDX_TOPK_MAX, seq // 4))
    return pl.pallas_call(
        kern,
        grid=(batch, n_q),
        in_specs=[blk(384), blk(128), perb_rows, perb_rows,
                  pl.BlockSpec((None, seq // SEL_KC, 128, SEL_KC), lambda b, i: (b, 0, 0, 0)),
                  pl.BlockSpec(ltri.shape, lambda b, i: (0, 0))],
        out_specs=blk(384),
        out_shape=jax.ShapeDtypeStruct((batch * seq, 384), BF16),
        scratch_shapes=[pltpu.VMEM((seq, Q_BLK), I32),
                        pltpu.VMEM((32, seq // 32, Q_BLK), I32),
                        pltpu.VMEM((1, DSA_HEADS * Q_BLK), F32),
                        pltpu.VMEM((2 * HEAD_DIM, DSA_HEADS * Q_BLK), F32),
                        pltpu.VMEM((SEL_KC // 2, DSA_HEADS * Q_BLK), F32),
                        pltpu.VMEM((SEL_KC // 2, DSA_HEADS * Q_BLK), F32)],
        compiler_params=_cparams(("parallel", "arbitrary")),
        name="dsa_attn",
    )(dq, iq, small, dk, dvT, ltri)


def _sgu_kernel(z_ref, g_ref, b_ref, w_ref, bs_ref, o_ref):
    row = lax.broadcasted_iota(I32, (SGU_CHUNK, SGU_CHUNK), 0)
    col = lax.broadcasted_iota(I32, (SGU_CHUNK, SGU_CHUNK), 1)
    ws = [jnp.where(row >= col, w_ref[g], 0.0).astype(BF16) for g in range(SGU_GROUPS)]
    for c in range(z_ref.shape[0] // SGU_CHUNK):
        rows = slice(c * SGU_CHUNK, (c + 1) * SGU_CHUNK)
        z = _gelu(z_ref[rows, :])
        outs = []
        for g in range(SGU_GROUPS):
            lo, hi = g * HEAD_DIM, (g + 1) * HEAD_DIM
            u = z[:, lo:hi]
            v = _standardize(z[:, SGU_WIDTH + lo:SGU_WIDTH + hi]) * g_ref[:, lo:hi] + b_ref[:, lo:hi]
            outs.append(u * (_dot(ws[g], v) + bs_ref[:, lo:hi]))
        o_ref[rows, :] = jnp.concatenate(outs, axis=1).astype(BF16)


def _sgu(z, g, b, w, bs):
    t = z.shape[0]
    tm = SGU_CHUNK
    full = lambda a: pl.BlockSpec(a.shape, lambda i: (0,) * a.ndim)
    return pl.pallas_call(
        _sgu_kernel,
        grid=(t // tm,),
        in_specs=[pl.BlockSpec((tm, 2 * SGU_WIDTH), lambda i: (i, 0)),
                  full(g), full(b), full(w), full(bs)],
        out_specs=pl.BlockSpec((tm, SGU_WIDTH), lambda i: (i, 0)),
        out_shape=jax.ShapeDtypeStruct((t, SGU_WIDTH), BF16),
        compiler_params=_cparams(("parallel",)),
        name="sgu",
    )(z, g, b, w, bs)


def _outproj_kernel(oa_ref, ob_ref, oc_ref, x_ref, mod_ref, w_ref, g_ref, b_ref, o_ref, *, alpha):
    mix = (jnp.dot(oa_ref[...], w_ref[0:384, :], preferred_element_type=F32)
           + jnp.dot(ob_ref[...], w_ref[384:768, :], preferred_element_type=F32)
           + jnp.dot(oc_ref[...], w_ref[768:1024, :], preferred_element_type=F32))
    y = alpha * x_ref[...] + mod_ref[2:3, :] * mix
    o_ref[...] = _standardize(y) * g_ref[...] + b_ref[...]


def _outproj(oa, ob, oc, xf, mod, w, g, b, seq, alpha):
    t, d = xf.shape
    tm = 512
    per_b = seq // tm
    row = lambda n: pl.BlockSpec((tm, n), lambda i: (i, 0))
    full = lambda a: pl.BlockSpec(a.shape, lambda i: (0,) * a.ndim)
    return pl.pallas_call(
        functools.partial(_outproj_kernel, alpha=alpha),
        grid=(t // tm,),
        in_specs=[row(384), row(384), row(256), row(d),
                  pl.BlockSpec((None, 6, d), lambda i: (i // per_b, 0, 0)),
                  full(w), full(g), full(b)],
        out_specs=row(d),
        out_shape=jax.ShapeDtypeStruct((t, d), F32),
        compiler_params=_cparams(("parallel",)),
        name="outproj",
    )(oa, ob, oc, xf, mod, w, g, b)


def _ffn_kernel(x_ref, mod_ref, wg_ref, wu_ref, wd_ref, g_ref, b_ref, o_ref, acc_ref, *, alpha, fc):
    x = x_ref[...]
    hb = (_standardize(x) * (1.0 + mod_ref[4:5, :]) + mod_ref[3:4, :]).astype(BF16)
    d_ff = wg_ref.shape[1]
    for j in range(d_ff // fc):
        cs = slice(j * fc, (j + 1) * fc)
        a = (_silu(jnp.dot(hb, wg_ref[:, cs], preferred_element_type=F32))
             * jnp.dot(hb, wu_ref[:, cs], preferred_element_type=F32)).astype(BF16)
        part = jnp.dot(a, wd_ref[cs, :], preferred_element_type=F32)
        if j == 0:
            acc_ref[...] = part
        else:
            acc_ref[...] += part
    y = alpha * x + mod_ref[5:6, :] * acc_ref[...]
    o_ref[...] = _standardize(y) * g_ref[...] + b_ref[...]


def _ffn(xf, mod, wg, wu, wd, g, b, seq, alpha):
    t, d = xf.shape
    tm = 512
    per_b = seq // tm
    row = pl.BlockSpec((tm, d), lambda i: (i, 0))
    once = lambda a: pl.BlockSpec(a.shape, lambda i: (0,) * a.ndim, pipeline_mode=pl.Buffered(1))
    return pl.pallas_call(
        functools.partial(_ffn_kernel, alpha=alpha, fc=256),
        grid=(t // tm,),
        in_specs=[row, pl.BlockSpec((None, 6, d), lambda i: (i // per_b, 0, 0)),
                  once(wg), once(wu), once(wd), once(g), once(b)],
        out_specs=row,
        out_shape=jax.ShapeDtypeStruct((t, d), F32),
        scratch_shapes=[pltpu.VMEM((tm, d), F32)],
        compiler_params=_cparams(("parallel",)),
        name="ffn",
    )(xf, mod, wg, wu, wd, g, b)


MOE_TM = 1024
MOE_SUB = 128
MOE_BM = 1024
MOE_FC = 896


def _route_kernel(x_ref, mod_ref, wr_ref, br_ref, utri_ref, h_ref, gate_ref, pos_ref, post_ref, cnt_ref):
    h = _standardize(x_ref[...]) * (1.0 + mod_ref[4:5, :]) + mod_ref[3:4, :]
    h_ref[...] = h.astype(BF16)
    lane = lax.broadcasted_iota(I32, (1, LANES), 1)
    logits = jnp.dot(h, wr_ref[...], preferred_element_type=F32,
                     precision=lax.Precision.HIGHEST) + br_ref[...]
    lg = jnp.where(lane < N_EXPERTS, logits, -jnp.inf)
    v0 = jnp.max(lg, axis=-1, keepdims=True)
    lane_f = lane.astype(F32)
    i0 = jnp.min(jnp.where(lg == v0, lane_f, float(LANES)), axis=-1, keepdims=True)
    lg1 = jnp.where(lane_f == i0, -jnp.inf, lg)
    v1 = jnp.max(lg1, axis=-1, keepdims=True)
    i1 = jnp.min(jnp.where(lg1 == v1, lane_f, float(LANES)), axis=-1, keepdims=True)
    e1 = jnp.exp(v1 - v0)
    den = 1.0 + e1
    gate_ref[...] = jnp.where(lane_f == i0, 1.0 / den, 0.0) + jnp.where(lane_f == i1, e1 / den, 0.0)
    sel_t = jnp.where((lane_f == i0) | (lane_f == i1), 1.0, 0.0).T
    rank_t = jnp.dot(sel_t.astype(BF16), utri_ref[...], preferred_element_type=F32)
    pos_t = jnp.where(sel_t > 0.5, rank_t, -1.0)
    post_ref[...] = pos_t[0:N_EXPERTS]
    pos_ref[...] = pos_t.T
    cnt_ref[...] = jnp.broadcast_to(jnp.sum(sel_t[0:N_EXPERTS], axis=1, keepdims=True), (N_EXPERTS, LANES))


def _route(xf, mod, wr, br, utri, seq):
    t, d = xf.shape
    tm = MOE_TM
    nt = t // tm
    per_b = seq // tm
    full = lambda a: pl.BlockSpec(a.shape, lambda i: (0,) * a.ndim)
    return pl.pallas_call(
        _route_kernel,
        grid=(nt,),
        in_specs=[pl.BlockSpec((tm, d), lambda i: (i, 0)),
                  pl.BlockSpec((None, 6, d), lambda i: (i // per_b, 0, 0)),
                  full(wr), full(br), full(utri)],
        out_specs=(pl.BlockSpec((tm, d), lambda i: (i, 0)),
                   pl.BlockSpec((tm, LANES), lambda i: (i, 0)),
                   pl.BlockSpec((tm, LANES), lambda i: (i, 0)),
                   pl.BlockSpec((None, N_EXPERTS, tm), lambda i: (i, 0, 0)),
                   pl.BlockSpec((None, N_EXPERTS, LANES), lambda i: (i, 0, 0))),
        out_shape=(jax.ShapeDtypeStruct((t, d), BF16),
                   jax.ShapeDtypeStruct((t, LANES), F32),
                   jax.ShapeDtypeStruct((t, LANES), F32),
                   jax.ShapeDtypeStruct((nt, N_EXPERTS, tm), F32),
                   jax.ShapeDtypeStruct((nt, N_EXPERTS, LANES), F32)),
        compiler_params=_cparams(("parallel",)),
        name="moe_route",
    )(xf, mod, wr, br, utri)


def _moe_schedule(cnt, n_blocks, k_max):
    nt, ne = cnt.shape
    per = MOE_BM // MOE_SUB
    nb = (cnt + MOE_SUB - 1) // MOE_SUB
    nbt = nb.T
    tot = jnp.sum(nbt, axis=1)
    reg = (tot + per - 1) // per * per
    reg_end = jnp.cumsum(reg)
    reg_start = reg_end - reg
    seg_end = jnp.cumsum(nbt, axis=1)
    seg_start = reg_start[:, None] + seg_end - nbt
    j = jnp.arange(n_blocks, dtype=I32)
    e_j = jnp.minimum(jnp.sum(reg_end[None, :] <= j[:, None], axis=1), ne - 1).astype(I32)
    valid_j = (j - reg_start[e_j]) < tot[e_j]
    step_e = e_j[::per]
    step_valid = valid_j[::per].astype(I32)
    cum = jnp.cumsum(nb, axis=1)
    n_tile = cum[:, -1]
    k = jnp.minimum(jnp.arange(k_max, dtype=I32)[None, :], n_tile[:, None] - 1)
    e_k = jnp.sum(cum[:, None, :] <= k[:, :, None], axis=2).astype(I32)
    tile = jnp.arange(nt, dtype=I32)[:, None]
    s_k = k - (jnp.take_along_axis(cum, e_k, axis=1) - jnp.take_along_axis(nb, e_k, axis=1))
    blk_k = seg_start[e_k, tile] + s_k
    valid_k = (jnp.arange(k_max, dtype=I32)[None, :] < n_tile[:, None]).astype(I32)
    slot = jnp.arange(nt * k_max, dtype=I32)
    src = jnp.zeros((n_blocks,), I32).at[jnp.where(valid_k > 0, blk_k, n_blocks).reshape(-1)].set(slot, mode='drop')
    return (step_e, step_valid, src), (blk_k.astype(I32), e_k, s_k.astype(I32), valid_k)


def _dispatch_kernel(e_ref, s_ref, h_ref, post_ref, gate_ref, xs_ref, gs_ref, *, k_half):
    i = pl.program_id(0)
    kh = pl.program_id(1)
    lane = lax.broadcasted_iota(I32, (1, LANES), 1)
    sub = lax.broadcasted_iota(I32, (MOE_SUB, 1), 0)
    hots = []
    for u in range(k_half):
        k = kh * k_half + u
        row = post_ref[pl.ds(e_ref[i, k], 1), :]
        want = (s_ref[i, k] * MOE_SUB + sub).astype(F32)
        hots.append(jnp.where(row == want, 1.0, 0.0).astype(BF16))
    onehot = jnp.concatenate(hots, axis=0)
    xs_ref[...] = jnp.dot(onehot, h_ref[...], preferred_element_type=F32).astype(BF16)
    gates = gate_ref[...]
    g_hi = gates.astype(BF16)
    g_lo = (gates - g_hi.astype(F32)).astype(BF16)
    gs = (jnp.dot(onehot, g_hi, preferred_element_type=F32)
          + jnp.dot(onehot, g_lo, preferred_element_type=F32))
    for u in range(k_half):
        rows = slice(u * MOE_SUB, (u + 1) * MOE_SUB)
        mine = lane == e_ref[i, kh * k_half + u]
        gcol = jnp.sum(jnp.where(mine, gs[rows], 0.0), axis=-1, keepdims=True)
        gs_ref[rows, :] = jnp.broadcast_to(gcol, (MOE_SUB, LANES))


def _dispatch(h, post, gates, sched):
    t, d = h.shape
    tm = MOE_TM
    k_max = sched[0].shape[1]
    k_half = k_max // 2
    rows = (t // tm) * k_max * MOE_SUB
    grid_spec = pltpu.PrefetchScalarGridSpec(
        num_scalar_prefetch=2,
        grid=(t // tm, 2),
        in_specs=[pl.BlockSpec((tm, d), lambda i, kh, *_: (i, 0)),
                  pl.BlockSpec((None, N_EXPERTS, tm), lambda i, kh, *_: (i, 0, 0)),
                  pl.BlockSpec((tm, LANES), lambda i, kh, *_: (i, 0))],
        out_specs=(pl.BlockSpec((k_half * MOE_SUB, d), lambda i, kh, *_: (2 * i + kh, 0)),
                   pl.BlockSpec((k_half * MOE_SUB, LANES), lambda i, kh, *_: (2 * i + kh, 0))),
    )
    return pl.pallas_call(
        functools.partial(_dispatch_kernel, k_half=k_half),
        grid_spec=grid_spec,
        out_shape=(jax.ShapeDtypeStruct((rows, d), BF16), jax.ShapeDtypeStruct((rows, LANES), F32)),
        compiler_params=_cparams(("arbitrary", "arbitrary")),
        name="moe_dispatch",
    )(*sched, h, post, gates)


def _experts_kernel(e_ref, v_ref, src_ref, *refs, per):
    xs_refs, gs_refs = refs[:per], refs[per:2 * per]
    wg_ref, wu_ref, wd_ref, ys_ref, acc_ref, xcat_ref = refs[2 * per:]
    jb = pl.program_id(0)
    f = pl.program_id(1)
    last = pl.num_programs(1) - 1

    @pl.when((v_ref[jb] > 0) & (f == 0))
    def _():
        for u in range(per):
            xcat_ref[u * MOE_SUB:(u + 1) * MOE_SUB, :] = xs_refs[u][...]

    @pl.when(v_ref[jb] > 0)
    def _():
        xb = xcat_ref[...]
        a = (_silu(jnp.dot(xb, wg_ref[...].astype(BF16), preferred_element_type=F32))
             * jnp.dot(xb, wu_ref[...].astype(BF16), preferred_element_type=F32)).astype(BF16)
        part = jnp.dot(a, wd_ref[...].astype(BF16), preferred_element_type=F32)

        @pl.when(f == 0)
        def _():
            acc_ref[...] = part

        @pl.when(f > 0)
        def _():
            acc_ref[...] += part

        @pl.when(f == last)
        def _():
            for u in range(per):
                rows = slice(u * MOE_SUB, (u + 1) * MOE_SUB)
                gate = jnp.concatenate([gs_refs[u][...]] * (ys_ref.shape[1] // LANES), axis=1)
                ys_ref[rows, :] = (acc_ref[rows, :] * gate).astype(BF16)

    @pl.when((v_ref[jb] == 0) & (f == last))
    def _():
        ys_ref[...] = jnp.zeros(ys_ref.shape, BF16)


def _experts(xs, gs, wg, wu, wd, sched, n_blocks):
    d = xs.shape[1]
    n_e, _, d_ff = wg.shape
    fc = MOE_FC
    n_f = d_ff // fc
    per = MOE_BM // MOE_SUB
    fsel = lambda jb, f, e, v: jnp.where(v[jb] > 0, f, n_f - 1)

    def src_index(u, jb, f, e, v, src):
        return (src[jb * per + u], 0)

    grid_spec = pltpu.PrefetchScalarGridSpec(
        num_scalar_prefetch=3,
        grid=(n_blocks // per, n_f),
        in_specs=([pl.BlockSpec((MOE_SUB, d), functools.partial(src_index, u)) for u in range(per)]
                  + [pl.BlockSpec((MOE_SUB, LANES), functools.partial(src_index, u)) for u in range(per)]
                  + [pl.BlockSpec((None, d, fc), lambda jb, f, e, v, src: (e[jb], 0, fsel(jb, f, e, v))),
                     pl.BlockSpec((None, d, fc), lambda jb, f, e, v, src: (e[jb], 0, fsel(jb, f, e, v))),
                     pl.BlockSpec((None, fc, d), lambda jb, f, e, v, src: (e[jb], fsel(jb, f, e, v), 0))]),
        out_specs=pl.BlockSpec((MOE_BM, d), lambda jb, f, e, v, src: (jb, 0)),
        scratch_shapes=[pltpu.VMEM((MOE_BM, d), F32), pltpu.VMEM((MOE_BM, d), BF16)],
    )
    return pl.pallas_call(
        functools.partial(_experts_kernel, per=per),
        grid_spec=grid_spec,
        out_shape=jax.ShapeDtypeStruct((n_blocks * MOE_SUB, d), BF16),
        compiler_params=_cparams(("arbitrary", "arbitrary")),
        name="moe_experts",
    )(*sched, *([xs] * per), *([gs] * per), wg, wu, wd)


def _combine_kernel(blk_ref, e_ref, s_ref, v_ref, *refs, alpha, k_max):
    ys_refs = refs[:k_max]
    pos_ref, x_ref, mod_ref, g_ref, b_ref, o_ref, pc_ref = refs[k_max:]
    i = pl.program_id(0)
    th = pos_ref.shape[0]
    lane = lax.broadcasted_iota(I32, (1, LANES), 1)
    pos = pos_ref[...]
    for e in range(N_EXPERTS):
        col = jnp.sum(jnp.where(lane == e, pos, 0.0), axis=-1, keepdims=True)
        pc_ref[e] = jnp.broadcast_to(col, (th, LANES))
    hots = []
    for k in range(k_max):
        s = jnp.where(v_ref[i, k] > 0, s_ref[i, k], -2)
        want = (s * MOE_SUB + lane).astype(F32)
        hots.append(jnp.where(pc_ref[e_ref[i, k]] == want, 1.0, 0.0).astype(BF16))
    f = jnp.dot(jnp.concatenate(hots, axis=1), jnp.concatenate([r[...] for r in ys_refs], axis=0),
                preferred_element_type=F32)
    y = alpha * x_ref[...] + mod_ref[5:6, :] * f
    o_ref[...] = _standardize(y) * g_ref[...] + b_ref[...]


def _combine(ys, pos, xf, mod, g, b, sched, seq, alpha):
    t, d = xf.shape
    th = MOE_TM // 2
    per_b = seq // th
    k_max = sched[0].shape[1]
    row = lambda n: pl.BlockSpec((th, n), lambda i, hh, *_: (2 * i + hh, 0))
    full = lambda a: pl.BlockSpec(a.shape, lambda i, hh, *_: (0,) * a.ndim)

    def ys_index(k, i, hh, blk, e, s, v):
        return (blk[i, k], 0)

    grid_spec = pltpu.PrefetchScalarGridSpec(
        num_scalar_prefetch=4,
        grid=(t // MOE_TM, 2),
        in_specs=([pl.BlockSpec((MOE_SUB, d), functools.partial(ys_index, k)) for k in range(k_max)]
                  + [row(LANES), row(d),
                     pl.BlockSpec((None, 6, d), lambda i, hh, *_: ((2 * i + hh) // per_b, 0, 0)),
                     full(g), full(b)]),
        out_specs=row(d),
        scratch_shapes=[pltpu.VMEM((N_EXPERTS, th, LANES), F32)],
    )
    return pl.pallas_call(
        functools.partial(_combine_kernel, alpha=alpha, k_max=k_max),
        grid_spec=grid_spec,
        out_shape=jax.ShapeDtypeStruct((t, d), F32),
        compiler_params=_cparams(("arbitrary", "arbitrary")),
        name="moe_combine",
    )(*sched, *([ys] * k_max), pos, xf, mod, g, b)


def _moe(xf, mod, wr, br, wg, wu, wd, g, b, seq, alpha):
    t, d = xf.shape
    nt = t // MOE_TM
    per = MOE_BM // MOE_SUB
    n_blocks = TOP_K * t // MOE_SUB + nt * N_EXPERTS + N_EXPERTS * (per - 1)
    n_blocks = (n_blocks + per - 1) // per * per
    k_max = TOP_K * MOE_TM // MOE_SUB + N_EXPERTS
    utri = jnp.asarray(np.triu(np.ones((MOE_TM, MOE_TM), np.float32), 1), BF16)
    h, gates, pos, post, cnt = _route(xf, mod, wr, br, utri, seq)
    sched_x, sched_c = _moe_schedule(cnt[:, :, 0].astype(I32), n_blocks, k_max)
    xs, gs = _dispatch(h, post, gates, sched_c[1:3])
    ys = _experts(xs, gs, wg, wu, wd, sched_x, n_blocks)
    return _combine(ys, pos, xf, mod, g, b, sched_c, seq, alpha)


def _inproj_columns():
    sizes = (NSA_WIDTH, 6 * NSA_KV_HEADS * HEAD_DIM, 3 * NSA_HEADS, DSA_WIDTH, DSA_LATENT,
             IDX_HEADS * IDX_DIM, IDX_DIM, IDX_HEADS, 2 * SGU_WIDTH)
    starts = np.concatenate([[0], np.cumsum(sizes)])
    o_nq, o_kv, o_g, o_dq, o_ckv, o_iq, o_ik, o_iw, o_sgu = starts[:-1]
    perm = np.full((C_TOTAL,), -1, np.int64)
    perm[C_NQ:C_NQ + 384] = o_nq + np.arange(384)
    perm[C_KV:C_KV + 768] = o_kv + np.arange(768)
    perm[C_DQ:C_DQ + 384] = o_dq + np.arange(384)
    perm[C_CKV:C_CKV + 128] = o_ckv + np.arange(128)
    perm[C_IQ:C_IQ + 128] = o_iq + np.arange(128)
    perm[C_SGU:C_SGU + 512] = o_sgu + np.arange(512)
    perm[C_SMALL:C_SMALL + IDX_DIM] = o_ik + np.arange(IDX_DIM)
    perm[C_SMALL + SMALL_IW:C_SMALL + SMALL_IW + IDX_HEADS] = o_iw + np.arange(IDX_HEADS)
    perm[C_SMALL + SMALL_GATE:C_SMALL + SMALL_GATE + 3 * NSA_HEADS] = o_g + np.arange(3 * NSA_HEADS)
    return perm, int(starts[-1])


def _compress_weights(pos, w1, w2):
    half = CMP_LEN // 2
    eye_g = jnp.eye(NSA_KV_HEADS, dtype=F32)
    eye_j = jnp.eye(2, dtype=F32)

    def big(w1_half):
        w = jnp.einsum('jlde,jk,gh->ljgdkhe', w1_half, eye_j, eye_g)
        return w.reshape(half * 2 * NSA_KV_HEADS * HEAD_DIM, 2 * NSA_KV_HEADS * HEAD_DIM)

    def posrow(p_half):
        p = jnp.broadcast_to(p_half.transpose(1, 0, 2)[:, :, None, :], (half, 2, NSA_KV_HEADS, HEAD_DIM))
        return p.reshape(1, -1)

    w2big = jnp.einsum('jef,jk,gh->jgekhf', w2, eye_j, eye_g).reshape(256, 256)
    return (posrow(pos[:, :half]), posrow(pos[:, half:]),
            big(w1[:, :half]).astype(BF16), big(w1[:, half:]).astype(BF16), w2big.astype(BF16))


def _rope_tables(positions):
    pos = positions.astype(F32).reshape(-1, 1)

    def tab(dim):
        inv = ROPE_THETA ** (-jnp.arange(0, dim, 2, dtype=F32) / dim)
        lane = np.arange(LANES)
        ang = pos * inv[lane % (dim // 2)][None, :]
        sign = np.where(lane % dim < dim // 2, -1.0, 1.0).astype(np.float32)
        return jnp.cos(ang), jnp.sin(ang) * sign

    cos_h, sin_h = tab(HEAD_DIM)
    cos_i, sin_i = tab(IDX_DIM)
    return cos_h, sin_h, cos_i, sin_i


def kernel(x, c, positions, w_ada, b_ada, w_in, nsa_cmp_pos, nsa_cmp_w1, nsa_cmp_w2, dsa_kv_norm, dsa_w_uk, dsa_w_uv, sgu_norm_g, sgu_norm_b, sgu_w, sgu_b, w_out, ln1_g, ln1_b, ln2_g, ln2_b, ffn_w_gate, ffn_w_up, ffn_w_down, moe_w_router, moe_b_router, moe_w_gate, moe_w_up, moe_w_down):
    batch, seq, d = x.shape
    depth = w_ada.shape[0]
    t = batch * seq
    alpha = (2 * depth) ** 0.25
    assert seq % 512 == 0 and seq >= WINDOW + Q_BLK

    tabs = _rope_tables(positions)
    mod_all = _adaln(c, w_ada, b_ada).reshape(depth, batch, 6, d)
    perm, in_width = _inproj_columns()
    perm = jnp.asarray(np.where(perm < 0, in_width, perm), I32)

    n_blk = seq // SEL_BLOCK
    n_half = seq // CMP_STRIDE
    cmp_start = np.arange(n_half)[None, :] * CMP_STRIDE
    blk_start = np.arange(n_blk)[:, None] * SEL_BLOCK
    ovt = jnp.asarray((cmp_start < blk_start + SEL_BLOCK) & (cmp_start + CMP_LEN > blk_start), BF16)
    ltri = jnp.asarray(np.tril(np.ones((SEL_KC, SEL_KC), np.float32), -1), BF16)

    xf = x.reshape(t, d)
    for layer in range(depth):
        mod = mod_all[layer]
        w_pad = jnp.concatenate([w_in[layer], jnp.zeros((d, 1), F32)], axis=1)
        w_ext = jnp.take(w_pad, perm, axis=1).astype(BF16)
        wkv = jnp.concatenate([dsa_w_uk[layer], dsa_w_uv[layer]], axis=1).astype(BF16)
        (nq, nqr, kvcmp, ksel, vselT, kwin, vwinT, dq, dk, dvT, iq, small, sgu_z) = _inproj(
            xf, mod, tabs, w_ext, wkv, dsa_kv_norm[layer].reshape(1, -1), batch, seq)

        plo, phi, wlo, whi, w2big = _compress_weights(nsa_cmp_pos[layer], nsa_cmp_w1[layer], nsa_cmp_w2[layer])
        kc, vct = _compress(kvcmp.reshape(batch, n_half, CMP_STRIDE * 256), plo, phi, wlo, whi, w2big)
        o_a = _nsa(nq, nqr, small, kc, vct, ksel, vselT, kwin, vwinT, ovt, batch, seq)
        o_b = _dsa(dq, iq, small, dk, dvT, ltri, batch, seq)
        bs = jnp.repeat(sgu_b[layer].T, HEAD_DIM, axis=1)
        o_c = _sgu(sgu_z, sgu_norm_g[layer].reshape(1, -1), sgu_norm_b[layer].reshape(1, -1),
                   sgu_w[layer], bs)
        xf = _outproj(o_a, o_b, o_c, xf, mod, w_out[layer].astype(BF16),
                      ln1_g[layer].reshape(1, -1), ln1_b[layer].reshape(1, -1), seq, alpha)

        j = layer // 2
        g2, b2 = ln2_g[layer].reshape(1, -1), ln2_b[layer].reshape(1, -1)
        if layer % 2 == 0:
            xf = _ffn(xf, mod, ffn_w_gate[j].astype(BF16), ffn_w_up[j].astype(BF16),
                      ffn_w_down[j].astype(BF16), g2, b2, seq, alpha)
        else:
            wr = jnp.pad(moe_w_router[j], ((0, 0), (0, LANES - N_EXPERTS)))
            br = jnp.pad(moe_b_router[j], (0, LANES - N_EXPERTS)).reshape(1, -1)
            xf = _moe(xf, mod, wr, br, moe_w_gate[j], moe_w_up[j], moe_w_down[j], g2, b2, seq, alpha)
    return xf.reshape(batch, seq, d)
```

```python
import functools

import numpy as np
import jax
import jax.numpy as jnp
from jax import lax
from jax.experimental import pallas as pl
from jax.experimental.pallas import tpu as pltpu

F32 = jnp.float32
BF16 = jnp.bfloat16
I32 = jnp.int32

HEAD_DIM = 64
Q_BLK = 128
ROPE_THETA = 10000.0
LN_EPS = 1e-5
RMS_EPS = 1e-6

NSA_HEADS = 6
NSA_KV_HEADS = 2
NSA_GROUP = NSA_HEADS // NSA_KV_HEADS
NSA_WIDTH = NSA_HEADS * HEAD_DIM
CMP_LEN = 32
CMP_STRIDE = 16
SEL_BLOCK = 64
SEL_TOP = 16
WINDOW = 512
FORCE_BONUS = 1e4

DSA_HEADS = 6
DSA_WIDTH = DSA_HEADS * HEAD_DIM
DSA_LATENT = 128
IDX_HEADS = 4
IDX_DIM = 32
IDX_TOPK_MAX = 256

SGU_GROUPS = 4
SGU_CHUNK = 128
SGU_WIDTH = SGU_GROUPS * HEAD_DIM

N_EXPERTS = 8
TOP_K = 2

LANES = 128
VMEM_LIMIT = 56 * 1024 * 1024
NEG = -1e30
INT_MIN = -(2 ** 31)

C_NQ = 0
C_KV = 384
C_DQ = 1152
C_CKV = 1536
C_IQ = 1664
C_SGU = 1792
C_SMALL = 2304
C_TOTAL = 2432
SMALL_IW = 32
SMALL_GATE = 36

SEL_KC = 512
Q_SCALE = HEAD_DIM ** -0.5 * 1.4426950408889634


def _cparams(sem):
    return pltpu.CompilerParams(dimension_semantics=sem, vmem_limit_bytes=VMEM_LIMIT)


def _dot(a, b):
    return jnp.dot(a.astype(BF16), b.astype(BF16), preferred_element_type=F32)


def _dot_nt(a, b):
    return lax.dot_general(a.astype(BF16), b.astype(BF16), (((1,), (1,)), ((), ())),
                           preferred_element_type=F32)


def _gelu(x):
    return 0.5 * x * (1.0 + jnp.tanh(0.7978845608028654 * (x + 0.044715 * (x * x * x))))


def _silu(x):
    return x * (1.0 / (1.0 + jnp.exp(-x)))


def _sigmoid(x):
    return 1.0 / (1.0 + jnp.exp(-x))


def _standardize(x):
    mu = jnp.mean(x, axis=-1, keepdims=True)
    xc = x - mu
    var = jnp.mean(xc * xc, axis=-1, keepdims=True)
    return xc * lax.rsqrt(var + LN_EPS)


def _adaln_kernel(c_ref, w_ref, b_ref, o_ref):
    c = c_ref[...]
    o_ref[...] = jnp.dot(_silu(c), w_ref[...], preferred_element_type=F32,
                         precision=lax.Precision.HIGHEST) + b_ref[...]


def _adaln(c, w_ada, b_ada):
    depth, d, n = w_ada.shape
    b = c.shape[0]
    tn = 512
    return pl.pallas_call(
        _adaln_kernel,
        grid=(depth, n // tn),
        in_specs=[pl.BlockSpec((b, d), lambda l, j: (0, 0)),
                  pl.BlockSpec((None, d, tn), lambda l, j: (l, 0, j)),
                  pl.BlockSpec((None, 1, tn), lambda l, j: (l, 0, j))],
        out_specs=pl.BlockSpec((None, b, tn), lambda l, j: (l, 0, j)),
        out_shape=jax.ShapeDtypeStruct((depth, b, n), F32),
        compiler_params=_cparams(("arbitrary", "arbitrary")),
        name="adaln",
    )(c, w_ada, b_ada.reshape(depth, 1, n))


def _inproj_kernel(x_ref, mod_ref, cos_ref, sin_ref, cosi_ref, sini_ref, w_ref, wkv_ref, kvg_ref,
                   nq_ref, nqr_ref, kvcmp_ref, ksel_ref, vselT_ref, kwin_ref, vwinT_ref,
                   dq_ref, dk_ref, dvT_ref, iq_ref, small_ref, sgu_ref):
    tm = x_ref.shape[0]
    shift = mod_ref[0:1, :]
    scale = mod_ref[1:2, :]
    hb = (_standardize(x_ref[...]) * (1.0 + scale) + shift).astype(BF16)

    def proj(c0, n):
        return jnp.dot(hb, w_ref[:, c0:c0 + n], preferred_element_type=F32)

    cos = cos_ref[...]
    sin = sin_ref[...]
    lane = lax.broadcasted_iota(I32, (1, LANES), 1)

    def rope(z, cos_t, sin_t, dim):
        half = dim // 2
        low = (lane & (dim - 1)) < half
        outs = []
        for c in range(z.shape[1] // LANES):
            zc = z[:, c * LANES:(c + 1) * LANES]
            swapped = jnp.where(low, pltpu.roll(zc, LANES - half, 1), pltpu.roll(zc, half, 1))
            outs.append(zc * cos_t + swapped * sin_t)
        return outs[0] if len(outs) == 1 else jnp.concatenate(outs, axis=1)

    zq = proj(C_NQ, 384)
    nq_ref[...] = (zq * Q_SCALE).astype(BF16)
    nqr_ref[...] = (rope(zq, cos, sin, HEAD_DIM) * Q_SCALE).astype(BF16)

    kvcmp_ref[...] = proj(C_KV, 256)
    ksel_ref[...] = rope(proj(C_KV + 256, 128), cos, sin, HEAD_DIM).astype(BF16)
    ones_t = jnp.ones((HEAD_DIM, tm), F32)
    vsel_t = proj(C_KV + 384, 128).T
    vsel_x = jnp.concatenate([vsel_t[:HEAD_DIM], ones_t, vsel_t[HEAD_DIM:], ones_t], axis=0).astype(BF16)
    for j in range(tm // SEL_KC):
        vselT_ref[j] = vsel_x[:, j * SEL_KC:(j + 1) * SEL_KC]
    kwin_ref[...] = rope(proj(C_KV + 512, 128), cos, sin, HEAD_DIM).astype(BF16)
    vwin_t = proj(C_KV + 640, 128).T
    vwin_x = jnp.concatenate([vwin_t[:HEAD_DIM], ones_t, vwin_t[HEAD_DIM:], ones_t], axis=0).astype(BF16)
    for j in range(tm // Q_BLK):
        vwinT_ref[j] = vwin_x[:, j * Q_BLK:(j + 1) * Q_BLK]

    dq_ref[...] = (rope(proj(C_DQ, 384), cos, sin, HEAD_DIM) * Q_SCALE).astype(BF16)

    ckv = proj(C_CKV, 128)
    ckv = ckv * lax.rsqrt(jnp.mean(ckv * ckv, axis=-1, keepdims=True) + RMS_EPS) * kvg_ref[...]
    kd = jnp.dot(ckv.astype(BF16), wkv_ref[...], preferred_element_type=F32)
    first = lane < HEAD_DIM
    dkv = rope(kd, jnp.where(first, cos, 1.0), jnp.where(first, sin, 0.0), HEAD_DIM)
    dk_ref[...] = dkv.astype(BF16)
    dv_x = jnp.concatenate([dkv.T[HEAD_DIM:], ones_t], axis=0).astype(BF16)
    for j in range(tm // SEL_KC):
        dvT_ref[j] = dv_x[:, j * SEL_KC:(j + 1) * SEL_KC]

    cosi = cosi_ref[...]
    sini = sini_ref[...]
    iq_ref[...] = rope(proj(C_IQ, 128), cosi, sini, IDX_DIM).astype(BF16)
    isk = lane < IDX_DIM
    small_ref[...] = rope(proj(C_SMALL, 128), jnp.where(isk, cosi, 1.0), jnp.where(isk, sini, 0.0), IDX_DIM)
    sgu_ref[...] = proj(C_SGU, 512)


def _inproj(xf, mod, tabs, w_ext, wkv, kvg, batch, seq):
    t, d = xf.shape
    tm = 1024
    per_b = seq // tm
    row = lambda n: pl.BlockSpec((tm, n), lambda i: (i, 0))
    trk = lambda r: pl.BlockSpec((None, tm // SEL_KC, r, SEL_KC), lambda i: (i // per_b, i % per_b, 0, 0))
    tr128 = pl.BlockSpec((None, tm // Q_BLK, 256, Q_BLK), lambda i: (i // per_b, i % per_b, 0, 0))
    full = lambda a: pl.BlockSpec(a.shape, lambda i: (0,) * a.ndim)
    out_shape = (
        jax.ShapeDtypeStruct((t, 384), BF16),
        jax.ShapeDtypeStruct((t, 384), BF16),
        jax.ShapeDtypeStruct((t, 256), F32),
        jax.ShapeDtypeStruct((t, 128), BF16),
        jax.ShapeDtypeStruct((batch, seq // SEL_KC, 256, SEL_KC), BF16),
        jax.ShapeDtypeStruct((t, 128), BF16),
        jax.ShapeDtypeStruct((batch, seq // Q_BLK, 256, Q_BLK), BF16),
        jax.ShapeDtypeStruct((t, 384), BF16),
        jax.ShapeDtypeStruct((t, 128), BF16),
        jax.ShapeDtypeStruct((batch, seq // SEL_KC, 128, SEL_KC), BF16),
        jax.ShapeDtypeStruct((t, 128), BF16),
        jax.ShapeDtypeStruct((t, 128), F32),
        jax.ShapeDtypeStruct((t, 512), F32),
    )
    out_specs = (row(384), row(384), row(256), row(128), trk(256), row(128), tr128,
                 row(384), row(128), trk(128), row(128), row(128), row(512))
    return pl.pallas_call(
        _inproj_kernel,
        grid=(t // tm,),
        in_specs=[row(d),
                  pl.BlockSpec((None, 6, d), lambda i: (i // per_b, 0, 0)),
                  row(128), row(128), row(128), row(128),
                  full(w_ext), full(wkv), full(kvg)],
        out_specs=out_specs,
        out_shape=out_shape,
        compiler_params=_cparams(("parallel",)),
        name="inproj",
    )(xf, mod, *tabs, w_ext, wkv, kvg)


def _compress_kernel(h_ref, plo_ref, phi_ref, wlo_ref, whi_ref, w2_ref, kc_ref, vct_ref):
    h = h_ref[...]
    a = _dot(h + plo_ref[...], wlo_ref[...])
    b = _dot(h + phi_ref[...], whi_ref[...])
    nh = h.shape[0]
    pre = a + pltpu.roll(b, nh - 1, 0)
    cmp = _dot(_gelu(pre), w2_ref[...])
    kc_ref[...] = cmp[:, :128].astype(BF16)
    v_t = cmp[:, 128:].T
    ones_t = jnp.ones((HEAD_DIM, nh), F32)
    vct_ref[...] = jnp.concatenate([v_t[:HEAD_DIM], ones_t, v_t[HEAD_DIM:], ones_t], axis=0).astype(BF16)


def _compress(kvcmp_h, plo, phi, wlo, whi, w2):
    batch, nh, width = kvcmp_h.shape
    full = lambda a: pl.BlockSpec(a.shape, lambda b: (0,) * a.ndim)
    return pl.pallas_call(
        _compress_kernel,
        grid=(batch,),
        in_specs=[pl.BlockSpec((None, nh, width), lambda b: (b, 0, 0)),
                  full(plo), full(phi), full(wlo), full(whi), full(w2)],
        out_specs=(pl.BlockSpec((None, nh, 128), lambda b: (b, 0, 0)),
                   pl.BlockSpec((None, 256, nh), lambda b: (b, 0, 0))),
        out_shape=(jax.ShapeDtypeStruct((batch, nh, 128), BF16),
                   jax.ShapeDtypeStruct((batch, 256, nh), BF16)),
        compiler_params=_cparams(("parallel",)),
        name="nsa_compress",
    )(kvcmp_h, plo, phi, wlo, whi, w2)


def _softmax_pv(s, mask, vt_ones):
    sm = jnp.where(mask, s, NEG)
    m = jnp.max(sm, axis=0, keepdims=True)
    p = jnp.exp2(sm - m).astype(BF16)
    pv = jnp.dot(vt_ones, p, preferred_element_type=F32)
    dh = vt_ones.shape[0] // 2
    inv = jnp.where(m > 0.5 * NEG, 1.0 / jnp.maximum(pv[dh:dh + 1], 1e-30), 0.0)
    return p, inv, pv[:dh]


def _flash_step(s, on, vt_ones, m_ref, acc_ref):
    m_new, acc_new = _flash_update(s, on, vt_ones, m_ref[...], acc_ref[...])
    acc_ref[...] = acc_new
    m_ref[...] = m_new


def _masked_max(s, on, m_old):
    n_grp = s.shape[1] // Q_BLK
    sm = [jnp.where(on, s[:, j * Q_BLK:(j + 1) * Q_BLK], NEG) for j in range(n_grp)]
    m_new = jnp.maximum(m_old, jnp.concatenate([jnp.max(x, axis=0, keepdims=True) for x in sm], axis=1))
    return sm, m_new


def _exp2_weights(sm, m_new):
    return jnp.concatenate([jnp.exp2(x - m_new[:, j * Q_BLK:(j + 1) * Q_BLK]).astype(BF16)
                            for j, x in enumerate(sm)], axis=1)


def _flash_update(s, on, vt_ones, m_old, acc_old):
    sm, m_new = _masked_max(s, on, m_old)
    p = _exp2_weights(sm, m_new)
    acc_new = jnp.exp2(m_old - m_new) * acc_old + jnp.dot(vt_ones, p, preferred_element_type=F32)
    return m_new, acc_new


def _nsa_kernel(nq_ref, nqr_ref, small_ref, kc_ref, vct_ref, ksel_ref, vselT_ref, kwin_ref,
                vwinT_ref, ovt_ref, o_ref, sc_ref, lim_ref, oc_ref, m_ref, acc_ref, sa_ref, sb_ref,
                *, n_blk, n_sel, n_cmp):
    i = pl.program_id(1)
    t0 = i * Q_BLK
    tq = t0 + lax.broadcasted_iota(I32, (1, Q_BLK), 1)
    tq3 = jnp.concatenate([tq, tq, tq], axis=1)
    nq = nq_ref[...]
    nqr = nqr_ref[...]
    nh = kc_ref.shape[0]

    def stack_heads(q, g):
        hs = [g * NSA_GROUP + r for r in range(NSA_GROUP)]
        return jnp.concatenate([q[:, h * HEAD_DIM:(h + 1) * HEAD_DIM] for h in hs], axis=0)

    qrs = [stack_heads(nqr, g) for g in range(NSA_KV_HEADS)]
    for g in range(NSA_KV_HEADS):
        lo, hi = g * HEAD_DIM, (g + 1) * HEAD_DIM
        s_c = _dot_nt(kc_ref[:, lo:hi], stack_heads(nq, g))
        n_io = lax.broadcasted_iota(I32, (nh, 1), 0)
        m_c = (n_io * CMP_STRIDE + (CMP_LEN - 1) <= tq3) & (n_io < n_cmp)
        p_c, inv_c, oc = _softmax_pv(s_c, m_c, vct_ref[2 * lo:2 * hi, :])
        oc_ref[g] = oc * inv_c

        imp = jnp.zeros((n_blk, Q_BLK), F32)
        for r in range(NSA_GROUP):
            cs = slice(r * Q_BLK, (r + 1) * Q_BLK)
            imp = imp + jnp.dot(ovt_ref[...], p_c[:, cs], preferred_element_type=F32) * inv_c[:, cs]
        j_io = lax.broadcasted_iota(I32, (n_blk, 1), 0)
        cur = tq >> 6
        valid = j_io <= cur
        forced = (j_io == 0) | (j_io == cur) | (j_io == cur - 1)
        score = jnp.where(valid, imp + jnp.where(forced, FORCE_BONUS, 0.0), -jnp.inf)
        sc_ref[...] = score
        rank = jnp.zeros((n_blk, Q_BLK), F32)
        for b in range(n_blk):
            row = sc_ref[b:b + 1, :]
            beats = (row > score) | ((row == score) & (j_io > b))
            rank = rank + jnp.where(beats, 1.0, 0.0)
        lim_ref[g] = jnp.where((rank < n_sel) & valid, tq, -1)

    m_ref[...] = jnp.full(m_ref.shape, NEG, F32)
    acc_ref[...] = jnp.zeros(acc_ref.shape, F32)
    bpc = SEL_KC // SEL_BLOCK

    n_ch = (t0 + Q_BLK + SEL_KC - 1) // SEL_KC

    def scores(c, g):
        k0 = pl.multiple_of(c * SEL_KC, SEL_KC)
        return _dot_nt(ksel_ref[pl.ds(k0, SEL_KC), g * HEAD_DIM:(g + 1) * HEAD_DIM], qrs[g])

    def sel_step(c, carry):
        key = c * SEL_KC + lax.broadcasted_iota(I32, (SEL_KC, 1), 0)

        def flash(s, g):
            rows = [jnp.broadcast_to(lim_ref[g, pl.ds(c * bpc + u, 1), :], (SEL_BLOCK, Q_BLK))
                    for u in range(bpc)]
            on = key <= jnp.concatenate(rows, axis=0)
            _flash_step(s, on, vselT_ref[c, 2 * g * HEAD_DIM:2 * (g + 1) * HEAD_DIM, :],
                        m_ref.at[g], acc_ref.at[g])

        sb_ref[...] = scores(c, 1)
        flash(sa_ref[...], 0)
        sa_ref[...] = scores(jnp.minimum(c + 1, n_ch - 1), 0)
        flash(sb_ref[...], 1)
        return carry

    sa_ref[...] = scores(0, 0)
    lax.fori_loop(0, n_ch, sel_step, 0)

    small_t = small_ref[...].T
    heads = []
    for g in range(NSA_KV_HEADS):
        lo, hi = g * HEAD_DIM, (g + 1) * HEAD_DIM
        acc = acc_ref[g]
        o_s = acc[:HEAD_DIM] / jnp.maximum(acc[HEAD_DIM:HEAD_DIM + 1], 1e-30)
        o_c = oc_ref[g]

        nband = WINDOW // Q_BLK + 1
        cb = jnp.maximum(i - WINDOW // Q_BLK, 0)
        b0 = pl.multiple_of(cb * Q_BLK, Q_BLK)
        s_w = _dot_nt(kwin_ref[pl.ds(b0, nband * Q_BLK), lo:hi], qrs[g])
        key = b0 + lax.broadcasted_iota(I32, (nband * Q_BLK, 1), 0)
        diff = tq3 - key
        vt_w = jnp.concatenate([vwinT_ref[cb + u, 2 * lo:2 * hi, :] for u in range(nband)], axis=1)
        _, inv_w, o_w = _softmax_pv(s_w, (diff >= 0) & (diff < WINDOW), vt_w)
        o_w = o_w * inv_w

        for r in range(NSA_GROUP):
            gi = SMALL_GATE + (g * NSA_GROUP + r) * 3
            gt = _sigmoid(small_t[gi:gi + 3, :])
            cs = slice(r * Q_BLK, (r + 1) * Q_BLK)
            heads.append(gt[0:1, :] * o_c[:, cs] + gt[1:2, :] * o_s[:, cs] + gt[2:3, :] * o_w[:, cs])
    o_ref[...] = jnp.concatenate(heads, axis=0).T.astype(BF16)


def _nsa(nq, nqr, small, kc, vct, ksel, vselT, kwin, vwinT, ovt, batch, seq):
    n_q = seq // Q_BLK
    n_blk = seq // SEL_BLOCK
    n_cmp = (seq - CMP_LEN) // CMP_STRIDE + 1
    nh = kc.shape[1]
    blk = lambda n: pl.BlockSpec((Q_BLK, n), lambda b, i: (b * n_q + i, 0))
    perb2 = lambda r, c: pl.BlockSpec((None, r, c), lambda b, i: (b, 0, 0))
    perb_rows = pl.BlockSpec((seq, 128), lambda b, i: (b, 0))
    kern = functools.partial(_nsa_kernel, n_blk=n_blk, n_sel=min(SEL_TOP, n_blk), n_cmp=n_cmp)
    n_col = NSA_GROUP * Q_BLK
    return pl.pallas_call(
        kern,
        grid=(batch, n_q),
        in_specs=[blk(384), blk(384), blk(128),
                  perb2(nh, 128), perb2(256, nh),
                  perb_rows,
                  pl.BlockSpec((None, seq // SEL_KC, 256, SEL_KC), lambda b, i: (b, 0, 0, 0)),
                  perb_rows,
                  pl.BlockSpec((None, seq // Q_BLK, 256, Q_BLK), lambda b, i: (b, 0, 0, 0)),
                  pl.BlockSpec(ovt.shape, lambda b, i: (0, 0))],
        out_specs=blk(384),
        out_shape=jax.ShapeDtypeStruct((batch * seq, 384), BF16),
        scratch_shapes=[pltpu.VMEM((n_blk, Q_BLK), F32),
                        pltpu.VMEM((NSA_KV_HEADS, n_blk, Q_BLK), I32),
                        pltpu.VMEM((NSA_KV_HEADS, HEAD_DIM, n_col), F32),
                        pltpu.VMEM((NSA_KV_HEADS, 1, n_col), F32),
                        pltpu.VMEM((NSA_KV_HEADS, 2 * HEAD_DIM, n_col), F32),
                        pltpu.VMEM((SEL_KC, n_col), F32), pltpu.VMEM((SEL_KC, n_col), F32)],
        compiler_params=_cparams(("parallel", "arbitrary")),
        name="nsa_attn",
    )(nq, nqr, small, kc, vct, ksel, vselT, kwin, vwinT, ovt)


PLANE_KEYS = 32 * 8


def _bit_transpose32(rows):
    rows = list(rows)
    j, mask = 16, 0x0000FFFF
    while j:
        m32 = np.array(mask, np.uint32).view(np.int32)
        k = 0
        while k < 32:
            t = (rows[k] ^ lax.shift_right_logical(rows[k + j], np.int32(j))) & m32
            rows[k] = rows[k] ^ t
            rows[k + j] = rows[k + j] ^ (t << j)
            k = (k + j + 1) & ~j
        j >>= 1
        mask = (mask ^ (mask << j)) & 0xFFFFFFFF
    return rows

def _dsa_kernel(dq_ref, iq_ref, small_ref, dk_ref, dvT_ref, ltri_ref, o_ref,
                ord_ref, plane_ref, m_ref, acc_ref, sa_ref, sb_ref, *, k_top):
    i = pl.program_id(1)
    t0 = i * Q_BLK
    kc = SEL_KC
    n_ch = (t0 + Q_BLK + kc - 1) // kc
    tq = t0 + lax.broadcasted_iota(I32, (1, Q_BLK), 1)

    small_t = small_ref[pl.ds(pl.multiple_of(t0, Q_BLK), Q_BLK), :].T
    w_rows = [small_t[SMALL_IW + h:SMALL_IW + h + 1, :] * (IDX_HEADS ** -0.5) for h in range(IDX_HEADS)]
    iq = iq_ref[...]
    iqs = jnp.concatenate([iq[:, h * IDX_DIM:(h + 1) * IDX_DIM] for h in range(IDX_HEADS)], axis=0)

    def score_step(c, carry):
        k0 = pl.multiple_of(c * kc, kc)
        ik = small_ref[pl.ds(k0, kc), :][:, 0:IDX_DIM]
        lg = _dot_nt(ik, iqs)
        sc = jnp.zeros((kc, Q_BLK), F32)
        for h in range(IDX_HEADS):
            sc = sc + w_rows[h] * jnp.maximum(lg[:, h * Q_BLK:(h + 1) * Q_BLK], 0.0)
        sc = jnp.where(sc == 0.0, 0.0, sc)
        bits = lax.bitcast_convert_type(sc, I32)
        ordv = bits ^ ((bits >> 31) & 0x7FFFFFFF)
        key = k0 + lax.broadcasted_iota(I32, (kc, 1), 0)
        ordv = jnp.where(key <= tq, ordv, INT_MIN)
        ord_ref[pl.ds(k0, kc), :] = ordv
        u = ordv ^ INT_MIN
        for g in range(kc // PLANE_KEYS):
            rows = [u[g * PLANE_KEYS + r * 8:g * PLANE_KEYS + (r + 1) * 8, :] for r in range(32)]
            cols = _bit_transpose32(rows)
            w0 = pl.multiple_of((c * (kc // PLANE_KEYS) + g) * 8, 8)
            for b in range(32):
                plane_ref[b, pl.ds(w0, 8), :] = cols[31 - b]
        return carry

    lax.fori_loop(0, n_ch, score_step, 0)

    def clear_step(c, carry):
        w0 = pl.multiple_of(c * (kc // PLANE_KEYS) * 8, 8)
        for b in range(32):
            plane_ref[b, pl.ds(w0, (kc // PLANE_KEYS) * 8), :] = jnp.zeros(((kc // PLANE_KEYS) * 8, Q_BLK), I32)
        return carry

    lax.fori_loop(n_ch, plane_ref.shape[1] * 32 // kc, clear_step, 0)

    n_words = plane_ref.shape[1] // 8
    alive = [jnp.broadcast_to(jnp.where(w < n_ch * (kc // PLANE_KEYS), -1, 0).astype(I32), (8, Q_BLK))
             for w in range(n_words)]
    k_rem = jnp.full((1, Q_BLK), k_top, I32)
    prefix = jnp.zeros((1, Q_BLK), I32)

    def popcount_rows(words):
        pcs = [lax.population_count(x) for x in words]
        while len(pcs) > 1:
            pcs = [a + b for a, b in zip(pcs[0::2], pcs[1::2])]
        return jnp.sum(pcs[0], axis=0, keepdims=True)

    for bit in range(31, -1, -1):
        ones = [alive[w] & plane_ref[bit, w * 8:(w + 1) * 8, :] for w in range(n_words)]
        cnt = popcount_rows(ones)
        take = cnt >= k_rem
        keep0 = jnp.broadcast_to(jnp.where(take, 0, -1).astype(I32), (8, Q_BLK))
        alive = [ones[w] ^ (alive[w] & keep0) for w in range(n_words)]
        k_rem = jnp.where(take, k_rem, k_rem - cnt)
        prefix = prefix | jnp.where(take, np.int32(INT_MIN) if bit == 31 else np.int32(1 << bit), 0)
    thr = prefix ^ np.int32(INT_MIN)
    n_eq = popcount_rows(alive)
    short = thr == INT_MIN
    need = jnp.where(short, 0, k_rem)
    thr_all = jnp.where(short, INT_MIN + 1, thr)
    no_cut = jnp.min(jnp.where(short | (n_eq == need), 1.0, 0.0)) > 0.5

    q = dq_ref[...]
    qs = jnp.concatenate([q[:, h * HEAD_DIM:(h + 1) * HEAD_DIM] for h in range(DSA_HEADS)], axis=0)
    m_ref[...] = jnp.full(m_ref.shape, NEG, F32)
    acc_ref[...] = jnp.zeros(acc_ref.shape, F32)

    def sweep(mask_fn):
        def attn_step(c, seen):
            k0 = pl.multiple_of(c * kc, kc)
            on, seen = mask_fn(ord_ref[pl.ds(k0, kc), :], seen)
            vt = dvT_ref[c]
            sb_ref[...] = scores(c, 1)
            _flash_step(sa_ref[...], on[:half], vt[:, :half], m_ref, acc_ref)
            sa_ref[...] = scores(jnp.minimum(c + 1, n_ch - 1), 0)
            _flash_step(sb_ref[...], on[half:], vt[:, half:], m_ref, acc_ref)
            return seen
        sa_ref[...] = scores(0, 0)
        lax.fori_loop(0, n_ch, attn_step, jnp.zeros((1, Q_BLK), F32))

    half = kc // 2

    def scores(c, sub):
        k0 = pl.multiple_of(c * kc + sub * half, half)
        return _dot_nt(dk_ref[pl.ds(k0, half), 0:HEAD_DIM], qs)

    @pl.when(no_cut)
    def _():
        sweep(lambda o, seen: (o >= thr_all, seen))

    @pl.when(jnp.logical_not(no_cut))
    def _():
        need_f = need.astype(F32)

        def cut_mask(o, seen):
            eq = jnp.where(o == thr, 1.0, 0.0)
            before = jnp.dot(ltri_ref[...], eq.astype(BF16), preferred_element_type=F32) + seen
            on = (o > thr) | ((eq > 0.5) & (before < need_f))
            return on, seen + jnp.sum(eq, axis=0, keepdims=True)

        sweep(cut_mask)

    acc = acc_ref[...]
    o_t = acc[:HEAD_DIM] / jnp.maximum(acc[HEAD_DIM:HEAD_DIM + 1], 1e-30)
    heads = [o_t[:, h * Q_BLK:(h + 1) * Q_BLK] for h in range(DSA_HEADS)]
    o_ref[...] = jnp.concatenate(heads, axis=0).T.astype(BF16)


def _dsa(dq, iq, small, dk, dvT, ltri, batch, seq):
    n_q = seq // Q_BLK
    blk = lambda n: pl.BlockSpec((Q_BLK, n), lambda b, i: (b * n_q + i, 0))
    perb_rows = pl.BlockSpec((seq, 128), lambda b, i: (b, 0))
    kern = functools.partial(_dsa_kernel, k_top=min(IDX_TOPK_MAX, seq // 4))
    return pl.pallas_call(
        kern,
        grid=(batch, n_q),
        in_specs=[blk(384), blk(128), perb_rows, perb_rows,
                  pl.BlockSpec((None, seq // SEL_KC, 128, SEL_KC), lambda b, i: (b, 0, 0, 0)),
                  pl.BlockSpec(ltri.shape, lambda b, i: (0, 0))],
        out_specs=blk(384),
        out_shape=jax.ShapeDtypeStruct((batch * seq, 384), BF16),
        scratch_shapes=[pltpu.VMEM((seq, Q_BLK), I32),
                        pltpu.VMEM((32, seq // 32, Q_BLK), I32),
                        pltpu.VMEM((1, DSA_HEADS * Q_BLK), F32),
                        pltpu.VMEM((2 * HEAD_DIM, DSA_HEADS * Q_BLK), F32),
                        pltpu.VMEM((SEL_KC // 2, DSA_HEADS * Q_BLK), F32),
                        pltpu.VMEM((SEL_KC // 2, DSA_HEADS * Q_BLK), F32)],
        compiler_params=_cparams(("parallel", "arbitrary")),
        name="dsa_attn",
    )(dq, iq, small, dk, dvT, ltri)


def _sgu_kernel(z_ref, g_ref, b_ref, w_ref, bs_ref, o_ref):
    row = lax.broadcasted_iota(I32, (SGU_CHUNK, SGU_CHUNK), 0)
    col = lax.broadcasted_iota(I32, (SGU_CHUNK, SGU_CHUNK), 1)
    ws = [jnp.where(row >= col, w_ref[g], 0.0).astype(BF16) for g in range(SGU_GROUPS)]
    for c in range(z_ref.shape[0] // SGU_CHUNK):
        rows = slice(c * SGU_CHUNK, (c + 1) * SGU_CHUNK)
        z = _gelu(z_ref[rows, :])
        outs = []
        for g in range(SGU_GROUPS):
            lo, hi = g * HEAD_DIM, (g + 1) * HEAD_DIM
            u = z[:, lo:hi]
            v = _standardize(z[:, SGU_WIDTH + lo:SGU_WIDTH + hi]) * g_ref[:, lo:hi] + b_ref[:, lo:hi]
            outs.append(u * (_dot(ws[g], v) + bs_ref[:, lo:hi]))
        o_ref[rows, :] = jnp.concatenate(outs, axis=1).astype(BF16)


def _sgu(z, g, b, w, bs):
    t = z.shape[0]
    tm = SGU_CHUNK
    full = lambda a: pl.BlockSpec(a.shape, lambda i: (0,) * a.ndim)
    return pl.pallas_call(
        _sgu_kernel,
        grid=(t // tm,),
        in_specs=[pl.BlockSpec((tm, 2 * SGU_WIDTH), lambda i: (i, 0)),
                  full(g), full(b), full(w), full(bs)],
        out_specs=pl.BlockSpec((tm, SGU_WIDTH), lambda i: (i, 0)),
        out_shape=jax.ShapeDtypeStruct((t, SGU_WIDTH), BF16),
        compiler_params=_cparams(("parallel",)),
        name="sgu",
    )(z, g, b, w, bs)


def _outproj_kernel(oa_ref, ob_ref, oc_ref, x_ref, mod_ref, w_ref, g_ref, b_ref, o_ref, *, alpha):
    mix = (jnp.dot(oa_ref[...], w_ref[0:384, :], preferred_element_type=F32)
           + jnp.dot(ob_ref[...], w_ref[384:768, :], preferred_element_type=F32)
           + jnp.dot(oc_ref[...], w_ref[768:1024, :], preferred_element_type=F32))
    y = alpha * x_ref[...] + mod_ref[2:3, :] * mix
    o_ref[...] = _standardize(y) * g_ref[...] + b_ref[...]


def _outproj(oa, ob, oc, xf, mod, w, g, b, seq, alpha):
    t, d = xf.shape
    tm = 512
    per_b = seq // tm
    row = lambda n: pl.BlockSpec((tm, n), lambda i: (i, 0))
    full = lambda a: pl.BlockSpec(a.shape, lambda i: (0,) * a.ndim)
    return pl.pallas_call(
        functools.partial(_outproj_kernel, alpha=alpha),
        grid=(t // tm,),
        in_specs=[row(384), row(384), row(256), row(d),
                  pl.BlockSpec((None, 6, d), lambda i: (i // per_b, 0, 0)),
                  full(w), full(g), full(b)],
        out_specs=row(d),
        out_shape=jax.ShapeDtypeStruct((t, d), F32),
        compiler_params=_cparams(("parallel",)),
        name="outproj",
    )(oa, ob, oc, xf, mod, w, g, b)


def _ffn_kernel(x_ref, mod_ref, wg_ref, wu_ref, wd_ref, g_ref, b_ref, o_ref, acc_ref, *, alpha, fc):
    x = x_ref[...]
    hb = (_standardize(x) * (1.0 + mod_ref[4:5, :]) + mod_ref[3:4, :]).astype(BF16)
    d_ff = wg_ref.shape[1]
    for j in range(d_ff // fc):
        cs = slice(j * fc, (j + 1) * fc)
        a = (_silu(jnp.dot(hb, wg_ref[:, cs], preferred_element_type=F32))
             * jnp.dot(hb, wu_ref[:, cs], preferred_element_type=F32)).astype(BF16)
        part = jnp.dot(a, wd_ref[cs, :], preferred_element_type=F32)
        if j == 0:
            acc_ref[...] = part
        else:
            acc_ref[...] += part
    y = alpha * x + mod_ref[5:6, :] * acc_ref[...]
    o_ref[...] = _standardize(y) * g_ref[...] + b_ref[...]


def _ffn(xf, mod, wg, wu, wd, g, b, seq, alpha):
    t, d = xf.shape
    tm = 1024
    per_b = seq // tm
    row = pl.BlockSpec((tm, d), lambda i: (i, 0))
    once = lambda a: pl.BlockSpec(a.shape, lambda i: (0,) * a.ndim, pipeline_mode=pl.Buffered(1))
    return pl.pallas_call(
        functools.partial(_ffn_kernel, alpha=alpha, fc=256),
        grid=(t // tm,),
        in_specs=[row, pl.BlockSpec((None, 6, d), lambda i: (i // per_b, 0, 0)),
                  once(wg), once(wu), once(wd), once(g), once(b)],
        out_specs=row,
        out_shape=jax.ShapeDtypeStruct((t, d), F32),
        scratch_shapes=[pltpu.VMEM((tm, d), F32)],
        compiler_params=_cparams(("parallel",)),
        name="ffn",
    )(xf, mod, wg, wu, wd, g, b)


MOE_TM = 1024
MOE_SUB = 128
MOE_BM = 1024
MOE_FC = 896


def _route_kernel(x_ref, mod_ref, wr_ref, br_ref, utri_ref, h_ref, gate_ref, pos_ref, post_ref, cnt_ref):
    h = _standardize(x_ref[...]) * (1.0 + mod_ref[4:5, :]) + mod_ref[3:4, :]
    h_ref[...] = h.astype(BF16)
    lane = lax.broadcasted_iota(I32, (1, LANES), 1)
    logits = jnp.dot(h, wr_ref[...], preferred_element_type=F32,
                     precision=lax.Precision.HIGHEST) + br_ref[...]
    lg = jnp.where(lane < N_EXPERTS, logits, -jnp.inf)
    v0 = jnp.max(lg, axis=-1, keepdims=True)
    lane_f = lane.astype(F32)
    i0 = jnp.min(jnp.where(lg == v0, lane_f, float(LANES)), axis=-1, keepdims=True)
    lg1 = jnp.where(lane_f == i0, -jnp.inf, lg)
    v1 = jnp.max(lg1, axis=-1, keepdims=True)
    i1 = jnp.min(jnp.where(lg1 == v1, lane_f, float(LANES)), axis=-1, keepdims=True)
    e1 = jnp.exp(v1 - v0)
    den = 1.0 + e1
    gate_ref[...] = jnp.where(lane_f == i0, 1.0 / den, 0.0) + jnp.where(lane_f == i1, e1 / den, 0.0)
    sel_t = jnp.where((lane_f == i0) | (lane_f == i1), 1.0, 0.0).T
    rank_t = jnp.dot(sel_t.astype(BF16), utri_ref[...], preferred_element_type=F32)
    pos_t = jnp.where(sel_t > 0.5, rank_t, -1.0)
    post_ref[...] = pos_t[0:N_EXPERTS]
    pos_ref[...] = pos_t.T
    cnt_ref[...] = jnp.broadcast_to(jnp.sum(sel_t[0:N_EXPERTS], axis=1, keepdims=True), (N_EXPERTS, LANES))


def _route(xf, mod, wr, br, utri, seq):
    t, d = xf.shape
    tm = MOE_TM
    nt = t // tm
    per_b = seq // tm
    full = lambda a: pl.BlockSpec(a.shape, lambda i: (0,) * a.ndim)
    return pl.pallas_call(
        _route_kernel,
        grid=(nt,),
        in_specs=[pl.BlockSpec((tm, d), lambda i: (i, 0)),
                  pl.BlockSpec((None, 6, d), lambda i: (i // per_b, 0, 0)),
                  full(wr), full(br), full(utri)],
        out_specs=(pl.BlockSpec((tm, d), lambda i: (i, 0)),
                   pl.BlockSpec((tm, LANES), lambda i: (i, 0)),
                   pl.BlockSpec((tm, LANES), lambda i: (i, 0)),
                   pl.BlockSpec((None, N_EXPERTS, tm), lambda i: (i, 0, 0)),
                   pl.BlockSpec((None, N_EXPERTS, LANES), lambda i: (i, 0, 0))),
        out_shape=(jax.ShapeDtypeStruct((t, d), BF16),
                   jax.ShapeDtypeStruct((t, LANES), F32),
                   jax.ShapeDtypeStruct((t, LANES), F32),
                   jax.ShapeDtypeStruct((nt, N_EXPERTS, tm), F32),
                   jax.ShapeDtypeStruct((nt, N_EXPERTS, LANES), F32)),
        compiler_params=_cparams(("parallel",)),
        name="moe_route",
    )(xf, mod, wr, br, utri)


def _moe_schedule(cnt, n_blocks, k_max):
    nt, ne = cnt.shape
    per = MOE_BM // MOE_SUB
    nb = (cnt + MOE_SUB - 1) // MOE_SUB
    nbt = nb.T
    tot = jnp.sum(nbt, axis=1)
    reg = (tot + per - 1) // per * per
    reg_end = jnp.cumsum(reg)
    reg_start = reg_end - reg
    seg_end = jnp.cumsum(nbt, axis=1)
    seg_start = reg_start[:, None] + seg_end - nbt
    j = jnp.arange(n_blocks, dtype=I32)
    e_j = jnp.minimum(jnp.sum(reg_end[None, :] <= j[:, None], axis=1), ne - 1).astype(I32)
    valid_j = (j - reg_start[e_j]) < tot[e_j]
    step_e = e_j[::per]
    step_valid = valid_j[::per].astype(I32)
    cum = jnp.cumsum(nb, axis=1)
    n_tile = cum[:, -1]
    k = jnp.minimum(jnp.arange(k_max, dtype=I32)[None, :], n_tile[:, None] - 1)
    e_k = jnp.sum(cum[:, None, :] <= k[:, :, None], axis=2).astype(I32)
    tile = jnp.arange(nt, dtype=I32)[:, None]
    s_k = k - (jnp.take_along_axis(cum, e_k, axis=1) - jnp.take_along_axis(nb, e_k, axis=1))
    blk_k = seg_start[e_k, tile] + s_k
    valid_k = (jnp.arange(k_max, dtype=I32)[None, :] < n_tile[:, None]).astype(I32)
    slot = jnp.arange(nt * k_max, dtype=I32)
    src = jnp.zeros((n_blocks,), I32).at[jnp.where(valid_k > 0, blk_k, n_blocks).reshape(-1)].set(slot, mode='drop')
    return (step_e, step_valid, src), (blk_k.astype(I32), e_k, s_k.astype(I32), valid_k)


def _dispatch_kernel(e_ref, s_ref, h_ref, post_ref, gate_ref, xs_ref, gs_ref, *, k_half):
    i = pl.program_id(0)
    kh = pl.program_id(1)
    lane = lax.broadcasted_iota(I32, (1, LANES), 1)
    sub = lax.broadcasted_iota(I32, (MOE_SUB, 1), 0)
    hots = []
    for u in range(k_half):
        k = kh * k_half + u
        row = post_ref[pl.ds(e_ref[i, k], 1), :]
        want = (s_ref[i, k] * MOE_SUB + sub).astype(F32)
        hots.append(jnp.where(row == want, 1.0, 0.0).astype(BF16))
    onehot = jnp.concatenate(hots, axis=0)
    xs_ref[...] = jnp.dot(onehot, h_ref[...], preferred_element_type=F32).astype(BF16)
    gates = gate_ref[...]
    g_hi = gates.astype(BF16)
    g_lo = (gates - g_hi.astype(F32)).astype(BF16)
    gs = (jnp.dot(onehot, g_hi, preferred_element_type=F32)
          + jnp.dot(onehot, g_lo, preferred_element_type=F32))
    for u in range(k_half):
        rows = slice(u * MOE_SUB, (u + 1) * MOE_SUB)
        mine = lane == e_ref[i, kh * k_half + u]
        gcol = jnp.sum(jnp.where(mine, gs[rows], 0.0), axis=-1, keepdims=True)
        gs_ref[rows, :] = jnp.broadcast_to(gcol, (MOE_SUB, LANES))


def _dispatch(h, post, gates, sched):
    t, d = h.shape
    tm = MOE_TM
    k_max = sched[0].shape[1]
    k_half = k_max // 2
    rows = (t // tm) * k_max * MOE_SUB
    grid_spec = pltpu.PrefetchScalarGridSpec(
        num_scalar_prefetch=2,
        grid=(t // tm, 2),
        in_specs=[pl.BlockSpec((tm, d), lambda i, kh, *_: (i, 0)),
                  pl.BlockSpec((None, N_EXPERTS, tm), lambda i, kh, *_: (i, 0, 0)),
                  pl.BlockSpec((tm, LANES), lambda i, kh, *_: (i, 0))],
        out_specs=(pl.BlockSpec((k_half * MOE_SUB, d), lambda i, kh, *_: (2 * i + kh, 0)),
                   pl.BlockSpec((k_half * MOE_SUB, LANES), lambda i, kh, *_: (2 * i + kh, 0))),
    )
    return pl.pallas_call(
        functools.partial(_dispatch_kernel, k_half=k_half),
        grid_spec=grid_spec,
        out_shape=(jax.ShapeDtypeStruct((rows, d), BF16), jax.ShapeDtypeStruct((rows, LANES), F32)),
        compiler_params=_cparams(("arbitrary", "arbitrary")),
        name="moe_dispatch",
    )(*sched, h, post, gates)


def _experts_kernel(e_ref, v_ref, src_ref, *refs, per):
    xs_refs, gs_refs = refs[:per], refs[per:2 * per]
    wg_ref, wu_ref, wd_ref, ys_ref, acc_ref, xcat_ref = refs[2 * per:]
    jb = pl.program_id(0)
    f = pl.program_id(1)
    last = pl.num_programs(1) - 1

    @pl.when((v_ref[jb] > 0) & (f == 0))
    def _():
        for u in range(per):
            xcat_ref[u * MOE_SUB:(u + 1) * MOE_SUB, :] = xs_refs[u][...]

    @pl.when(v_ref[jb] > 0)
    def _():
        xb = xcat_ref[...]
        a = (_silu(jnp.dot(xb, wg_ref[...].astype(BF16), preferred_element_type=F32))
             * jnp.dot(xb, wu_ref[...].astype(BF16), preferred_element_type=F32)).astype(BF16)
        part = jnp.dot(a, wd_ref[...].astype(BF16), preferred_element_type=F32)

        @pl.when(f == 0)
        def _():
            acc_ref[...] = part

        @pl.when(f > 0)
        def _():
            acc_ref[...] += part

        @pl.when(f == last)
        def _():
            for u in range(per):
                rows = slice(u * MOE_SUB, (u + 1) * MOE_SUB)
                gate = jnp.concatenate([gs_refs[u][...]] * (ys_ref.shape[1] // LANES), axis=1)
                ys_ref[rows, :] = (acc_ref[rows, :] * gate).astype(BF16)

    @pl.when((v_ref[jb] == 0) & (f == last))
    def _():
        ys_ref[...] = jnp.zeros(ys_ref.shape, BF16)


def _experts(xs, gs, wg, wu, wd, sched, n_blocks):
    d = xs.shape[1]
    n_e, _, d_ff = wg.shape
    fc = MOE_FC
    n_f = d_ff // fc
    per = MOE_BM // MOE_SUB
    fsel = lambda jb, f, e, v: jnp.where(v[jb] > 0, f, n_f - 1)

    def src_index(u, jb, f, e, v, src):
        return (src[jb * per + u], 0)

    grid_spec = pltpu.PrefetchScalarGridSpec(
        num_scalar_prefetch=3,
        grid=(n_blocks // per, n_f),
        in_specs=([pl.BlockSpec((MOE_SUB, d), functools.partial(src_index, u)) for u in range(per)]
                  + [pl.BlockSpec((MOE_SUB, LANES), functools.partial(src_index, u)) for u in range(per)]
                  + [pl.BlockSpec((None, d, fc), lambda jb, f, e, v, src: (e[jb], 0, fsel(jb, f, e, v))),
                     pl.BlockSpec((None, d, fc), lambda jb, f, e, v, src: (e[jb], 0, fsel(jb, f, e, v))),
                     pl.BlockSpec((None, fc, d), lambda jb, f, e, v, src: (e[jb], fsel(jb, f, e, v), 0))]),
        out_specs=pl.BlockSpec((MOE_BM, d), lambda jb, f, e, v, src: (jb, 0)),
        scratch_shapes=[pltpu.VMEM((MOE_BM, d), F32), pltpu.VMEM((MOE_BM, d), BF16)],
    )
    return pl.pallas_call(
        functools.partial(_experts_kernel, per=per),
        grid_spec=grid_spec,
        out_shape=jax.ShapeDtypeStruct((n_blocks * MOE_SUB, d), BF16),
        compiler_params=_cparams(("arbitrary", "arbitrary")),
        name="moe_experts",
    )(*sched, *([xs] * per), *([gs] * per), wg, wu, wd)


def _combine_kernel(blk_ref, e_ref, s_ref, v_ref, *refs, alpha, k_max):
    ys_refs = refs[:k_max]
    pos_ref, x_ref, mod_ref, g_ref, b_ref, o_ref, pc_ref = refs[k_max:]
    i = pl.program_id(0)
    th = pos_ref.shape[0]
    lane = lax.broadcasted_iota(I32, (1, LANES), 1)
    pos = pos_ref[...]
    for e in range(N_EXPERTS):
        col = jnp.sum(jnp.where(lane == e, pos, 0.0), axis=-1, keepdims=True)
        pc_ref[e] = jnp.broadcast_to(col, (th, LANES))
    hots = []
    for k in range(k_max):
        s = jnp.where(v_ref[i, k] > 0, s_ref[i, k], -2)
        want = (s * MOE_SUB + lane).astype(F32)
        hots.append(jnp.where(pc_ref[e_ref[i, k]] == want, 1.0, 0.0).astype(BF16))
    f = jnp.dot(jnp.concatenate(hots, axis=1), jnp.concatenate([r[...] for r in ys_refs], axis=0),
                preferred_element_type=F32)
    y = alpha * x_ref[...] + mod_ref[5:6, :] * f
    o_ref[...] = _standardize(y) * g_ref[...] + b_ref[...]


def _combine(ys, pos, xf, mod, g, b, sched, seq, alpha):
    t, d = xf.shape
    th = MOE_TM // 2
    per_b = seq // th
    k_max = sched[0].shape[1]
    row = lambda n: pl.BlockSpec((th, n), lambda i, hh, *_: (2 * i + hh, 0))
    full = lambda a: pl.BlockSpec(a.shape, lambda i, hh, *_: (0,) * a.ndim)

    def ys_index(k, i, hh, blk, e, s, v):
        return (blk[i, k], 0)

    grid_spec = pltpu.PrefetchScalarGridSpec(
        num_scalar_prefetch=4,
        grid=(t // MOE_TM, 2),
        in_specs=([pl.BlockSpec((MOE_SUB, d), functools.partial(ys_index, k)) for k in range(k_max)]
                  + [row(LANES), row(d),
                     pl.BlockSpec((None, 6, d), lambda i, hh, *_: ((2 * i + hh) // per_b, 0, 0)),
                     full(g), full(b)]),
        out_specs=row(d),
        scratch_shapes=[pltpu.VMEM((N_EXPERTS, th, LANES), F32)],
    )
    return pl.pallas_call(
        functools.partial(_combine_kernel, alpha=alpha, k_max=k_max),
        grid_spec=grid_spec,
        out_shape=jax.ShapeDtypeStruct((t, d), F32),
        compiler_params=_cparams(("arbitrary", "arbitrary")),
        name="moe_combine",
    )(*sched, *([ys] * k_max), pos, xf, mod, g, b)


def _moe(xf, mod, wr, br, wg, wu, wd, g, b, seq, alpha):
    t, d = xf.shape
    nt = t // MOE_TM
    per = MOE_BM // MOE_SUB
    n_blocks = TOP_K * t // MOE_SUB + nt * N_EXPERTS + N_EXPERTS * (per - 1)
    n_blocks = (n_blocks + per - 1) // per * per
    k_max = TOP_K * MOE_TM // MOE_SUB + N_EXPERTS
    utri = jnp.asarray(np.triu(np.ones((MOE_TM, MOE_TM), np.float32), 1), BF16)
    h, gates, pos, post, cnt = _route(xf, mod, wr, br, utri, seq)
    sched_x, sched_c = _moe_schedule(cnt[:, :, 0].astype(I32), n_blocks, k_max)
    xs, gs = _dispatch(h, post, gates, sched_c[1:3])
    ys = _experts(xs, gs, wg, wu, wd, sched_x, n_blocks)
    return _combine(ys, pos, xf, mod, g, b, sched_c, seq, alpha)


def _inproj_columns():
    sizes = (NSA_WIDTH, 6 * NSA_KV_HEADS * HEAD_DIM, 3 * NSA_HEADS, DSA_WIDTH, DSA_LATENT,
             IDX_HEADS * IDX_DIM, IDX_DIM, IDX_HEADS, 2 * SGU_WIDTH)
    starts = np.concatenate([[0], np.cumsum(sizes)])
    o_nq, o_kv, o_g, o_dq, o_ckv, o_iq, o_ik, o_iw, o_sgu = starts[:-1]
    perm = np.full((C_TOTAL,), -1, np.int64)
    perm[C_NQ:C_NQ + 384] = o_nq + np.arange(384)
    perm[C_KV:C_KV + 768] = o_kv + np.arange(768)
    perm[C_DQ:C_DQ + 384] = o_dq + np.arange(384)
    perm[C_CKV:C_CKV + 128] = o_ckv + np.arange(128)
    perm[C_IQ:C_IQ + 128] = o_iq + np.arange(128)
    perm[C_SGU:C_SGU + 512] = o_sgu + np.arange(512)
    perm[C_SMALL:C_SMALL + IDX_DIM] = o_ik + np.arange(IDX_DIM)
    perm[C_SMALL + SMALL_IW:C_SMALL + SMALL_IW + IDX_HEADS] = o_iw + np.arange(IDX_HEADS)
    perm[C_SMALL + SMALL_GATE:C_SMALL + SMALL_GATE + 3 * NSA_HEADS] = o_g + np.arange(3 * NSA_HEADS)
    return perm, int(starts[-1])


def _compress_weights(pos, w1, w2):
    half = CMP_LEN // 2
    eye_g = jnp.eye(NSA_KV_HEADS, dtype=F32)
    eye_j = jnp.eye(2, dtype=F32)

    def big(w1_half):
        w = jnp.einsum('jlde,jk,gh->ljgdkhe', w1_half, eye_j, eye_g)
        return w.reshape(half * 2 * NSA_KV_HEADS * HEAD_DIM, 2 * NSA_KV_HEADS * HEAD_DIM)

    def posrow(p_half):
        p = jnp.broadcast_to(p_half.transpose(1, 0, 2)[:, :, None, :], (half, 2, NSA_KV_HEADS, HEAD_DIM))
        return p.reshape(1, -1)

    w2big = jnp.einsum('jef,jk,gh->jgekhf', w2, eye_j, eye_g).reshape(256, 256)
    return (posrow(pos[:, :half]), posrow(pos[:, half:]),
            big(w1[:, :half]).astype(BF16), big(w1[:, half:]).astype(BF16), w2big.astype(BF16))


def _rope_tables(positions):
    pos = positions.astype(F32).reshape(-1, 1)

    def tab(dim):
        inv = ROPE_THETA ** (-jnp.arange(0, dim, 2, dtype=F32) / dim)
        lane = np.arange(LANES)
        ang = pos * inv[lane % (dim // 2)][None, :]
        sign = np.where(lane % dim < dim // 2, -1.0, 1.0).astype(np.float32)
        return jnp.cos(ang), jnp.sin(ang) * sign

    cos_h, sin_h = tab(HEAD_DIM)
    cos_i, sin_i = tab(IDX_DIM)
    return cos_h, sin_h, cos_i, sin_i


def kernel(x, c, positions, w_ada, b_ada, w_in, nsa_cmp_pos, nsa_cmp_w1, nsa_cmp_w2, dsa_kv_norm, dsa_w_uk, dsa_w_uv, sgu_norm_g, sgu_norm_b, sgu_w, sgu_b, w_out, ln1_g, ln1_b, ln2_g, ln2_b, ffn_w_gate, ffn_w_up, ffn_w_down, moe_w_router, moe_b_router, moe_w_gate, moe_w_up, moe_w_down):
    batch, seq, d = x.shape
    depth = w_ada.shape[0]
    t = batch * seq
    alpha = (2 * depth) ** 0.25
    assert seq % MOE_TM == 0 and seq >= WINDOW + Q_BLK

    tabs = _rope_tables(positions)
    mod_all = _adaln(c, w_ada, b_ada).reshape(depth, batch, 6, d)
    perm, in_width = _inproj_columns()
    perm = jnp.asarray(np.where(perm < 0, in_width, perm), I32)

    n_blk = seq // SEL_BLOCK
    n_half = seq // CMP_STRIDE
    cmp_start = np.arange(n_half)[None, :] * CMP_STRIDE
    blk_start = np.arange(n_blk)[:, None] * SEL_BLOCK
    ovt = jnp.asarray((cmp_start < blk_start + SEL_BLOCK) & (cmp_start + CMP_LEN > blk_start), BF16)
    ltri = jnp.asarray(np.tril(np.ones((SEL_KC, SEL_KC), np.float32), -1), BF16)

    xf = x.reshape(t, d)
    for layer in range(depth):
        mod = mod_all[layer]
        w_pad = jnp.concatenate([w_in[layer], jnp.zeros((d, 1), F32)], axis=1)
        w_ext = jnp.take(w_pad, perm, axis=1).astype(BF16)
        wkv = jnp.concatenate([dsa_w_uk[layer], dsa_w_uv[layer]], axis=1).astype(BF16)
        (nq, nqr, kvcmp, ksel, vselT, kwin, vwinT, dq, dk, dvT, iq, small, sgu_z) = _inproj(
            xf, mod, tabs, w_ext, wkv, dsa_kv_norm[layer].reshape(1, -1), batch, seq)

        plo, phi, wlo, whi, w2big = _compress_weights(nsa_cmp_pos[layer], nsa_cmp_w1[layer], nsa_cmp_w2[layer])
        kc, vct = _compress(kvcmp.reshape(batch, n_half, CMP_STRIDE * 256), plo, phi, wlo, whi, w2big)
        o_a = _nsa(nq, nqr, small, kc, vct, ksel, vselT, kwin, vwinT, ovt, batch, seq)
        o_b = _dsa(dq, iq, small, dk, dvT, ltri, batch, seq)
        bs = jnp.repeat(sgu_b[layer].T, HEAD_DIM, axis=1)
        o_c = _sgu(sgu_z, sgu_norm_g[layer].reshape(1, -1), sgu_norm_b[layer].reshape(1, -1),
                   sgu_w[layer], bs)
        xf = _outproj(o_a, o_b, o_c, xf, mod, w_out[layer].astype(BF16),
                      ln1_g[layer].reshape(1, -1), ln1_b[layer].reshape(1, -1), seq, alpha)

        j = layer // 2
        g2, b2 = ln2_g[layer].reshape(1, -1), ln2_b[layer].reshape(1, -1)
        if layer % 2 == 0:
            xf = _ffn(xf, mod, ffn_w_gate[j].astype(BF16), ffn_w_up[j].astype(BF16),
                      ffn_w_down[j].astype(BF16), g2, b2, seq, alpha)
        else:
            wr = jnp.pad(moe_w_router[j], ((0, 0), (0, LANES - N_EXPERTS)))
            br = jnp.pad(moe_b_router[j], (0, LANES - N_EXPERTS)).reshape(1, -1)
            xf = _moe(xf, mod, wr, br, moe_w_gate[j], moe_w_up[j], moe_w_down[j], g2, b2, seq, alpha)
    return xf.reshape(batch, seq, d)
```

```python
import functools

import numpy as np
import jax
import jax.numpy as jnp
from jax import lax
from jax.experimental import pallas as pl
from jax.experimental.pallas import tpu as pltpu

F32 = jnp.float32
BF16 = jnp.bfloat16
I32 = jnp.int32

HEAD_DIM = 64
Q_BLK = 128
ROPE_THETA = 10000.0
LN_EPS = 1e-5
RMS_EPS = 1e-6

NSA_HEADS = 6
NSA_KV_HEADS = 2
NSA_GROUP = NSA_HEADS // NSA_KV_HEADS
NSA_WIDTH = NSA_HEADS * HEAD_DIM
CMP_LEN = 32
CMP_STRIDE = 16
SEL_BLOCK = 64
SEL_TOP = 16
WINDOW = 512
FORCE_BONUS = 1e4

DSA_HEADS = 6
DSA_WIDTH = DSA_HEADS * HEAD_DIM
DSA_LATENT = 128
IDX_HEADS = 4
IDX_DIM = 32
IDX_TOPK_MAX = 256

SGU_GROUPS = 4
SGU_CHUNK = 128
SGU_WIDTH = SGU_GROUPS * HEAD_DIM

N_EXPERTS = 8
TOP_K = 2

LANES = 128
VMEM_LIMIT = 56 * 1024 * 1024
NEG = -1e30
INT_MIN = -(2 ** 31)

C_NQ = 0
C_KV = 384
C_DQ = 1152
C_CKV = 1536
C_IQ = 1664
C_SGU = 1792
C_SMALL = 2304
C_TOTAL = 2432
SMALL_IW = 32
SMALL_GATE = 36

SEL_KC = 512
Q_SCALE = HEAD_DIM ** -0.5 * 1.4426950408889634


def _cparams(sem):
    return pltpu.CompilerParams(dimension_semantics=sem, vmem_limit_bytes=VMEM_LIMIT)


def _dot(a, b):
    return jnp.dot(a.astype(BF16), b.astype(BF16), preferred_element_type=F32)


def _dot_nt(a, b):
    return lax.dot_general(a.astype(BF16), b.astype(BF16), (((1,), (1,)), ((), ())),
                           preferred_element_type=F32)


def _gelu(x):
    return 0.5 * x * (1.0 + jnp.tanh(0.7978845608028654 * (x + 0.044715 * (x * x * x))))


def _silu(x):
    return x * (1.0 / (1.0 + jnp.exp(-x)))


def _sigmoid(x):
    return 1.0 / (1.0 + jnp.exp(-x))


def _standardize(x):
    mu = jnp.mean(x, axis=-1, keepdims=True)
    xc = x - mu
    var = jnp.mean(xc * xc, axis=-1, keepdims=True)
    return xc * lax.rsqrt(var + LN_EPS)


def _adaln_kernel(c_ref, w_ref, b_ref, o_ref):
    c = c_ref[...]
    o_ref[...] = jnp.dot(_silu(c), w_ref[...], preferred_element_type=F32,
                         precision=lax.Precision.HIGHEST) + b_ref[...]


def _adaln(c, w_ada, b_ada):
    depth, d, n = w_ada.shape
    b = c.shape[0]
    tn = 512
    return pl.pallas_call(
        _adaln_kernel,
        grid=(depth, n // tn),
        in_specs=[pl.BlockSpec((b, d), lambda l, j: (0, 0)),
                  pl.BlockSpec((None, d, tn), lambda l, j: (l, 0, j)),
                  pl.BlockSpec((None, 1, tn), lambda l, j: (l, 0, j))],
        out_specs=pl.BlockSpec((None, b, tn), lambda l, j: (l, 0, j)),
        out_shape=jax.ShapeDtypeStruct((depth, b, n), F32),
        compiler_params=_cparams(("arbitrary", "arbitrary")),
        name="adaln",
    )(c, w_ada, b_ada.reshape(depth, 1, n))


def _inproj_kernel(x_ref, mod_ref, cos_ref, sin_ref, cosi_ref, sini_ref, w_ref, wkv_ref, kvg_ref,
                   nq_ref, nqr_ref, kvcmp_ref, ksel_ref, vselT_ref, kwin_ref, vwinT_ref,
                   dq_ref, dk_ref, dvT_ref, iq_ref, small_ref, sgu_ref):
    tm = x_ref.shape[0]
    shift = mod_ref[0:1, :]
    scale = mod_ref[1:2, :]
    hb = (_standardize(x_ref[...]) * (1.0 + scale) + shift).astype(BF16)

    def proj(c0, n):
        return jnp.dot(hb, w_ref[:, c0:c0 + n], preferred_element_type=F32)

    cos = cos_ref[...]
    sin = sin_ref[...]
    lane = lax.broadcasted_iota(I32, (1, LANES), 1)

    def rope(z, cos_t, sin_t, dim):
        half = dim // 2
        low = (lane & (dim - 1)) < half
        outs = []
        for c in range(z.shape[1] // LANES):
            zc = z[:, c * LANES:(c + 1) * LANES]
            swapped = jnp.where(low, pltpu.roll(zc, LANES - half, 1), pltpu.roll(zc, half, 1))
            outs.append(zc * cos_t + swapped * sin_t)
        return outs[0] if len(outs) == 1 else jnp.concatenate(outs, axis=1)

    zq = proj(C_NQ, 384)
    nq_ref[...] = (zq * Q_SCALE).astype(BF16)
    nqr_ref[...] = (rope(zq, cos, sin, HEAD_DIM) * Q_SCALE).astype(BF16)

    kvcmp_ref[...] = proj(C_KV, 256)
    ksel_ref[...] = rope(proj(C_KV + 256, 128), cos, sin, HEAD_DIM).astype(BF16)
    ones_t = jnp.ones((HEAD_DIM, tm), F32)
    vsel_t = proj(C_KV + 384, 128).T
    vsel_x = jnp.concatenate([vsel_t[:HEAD_DIM], ones_t, vsel_t[HEAD_DIM:], ones_t], axis=0).astype(BF16)
    for j in range(tm // SEL_KC):
        vselT_ref[j] = vsel_x[:, j * SEL_KC:(j + 1) * SEL_KC]
    kwin_ref[...] = rope(proj(C_KV + 512, 128), cos, sin, HEAD_DIM).astype(BF16)
    vwin_t = proj(C_KV + 640, 128).T
    vwin_x = jnp.concatenate([vwin_t[:HEAD_DIM], ones_t, vwin_t[HEAD_DIM:], ones_t], axis=0).astype(BF16)
    for j in range(tm // Q_BLK):
        vwinT_ref[j] = vwin_x[:, j * Q_BLK:(j + 1) * Q_BLK]

    dq_ref[...] = (rope(proj(C_DQ, 384), cos, sin, HEAD_DIM) * Q_SCALE).astype(BF16)

    ckv = proj(C_CKV, 128)
    ckv = ckv * lax.rsqrt(jnp.mean(ckv * ckv, axis=-1, keepdims=True) + RMS_EPS) * kvg_ref[...]
    kd = jnp.dot(ckv.astype(BF16), wkv_ref[...], preferred_element_type=F32)
    first = lane < HEAD_DIM
    dkv = rope(kd, jnp.where(first, cos, 1.0), jnp.where(first, sin, 0.0), HEAD_DIM)
    dk_ref[...] = dkv.astype(BF16)
    dv_x = jnp.concatenate([dkv.T[HEAD_DIM:], ones_t], axis=0).astype(BF16)
    for j in range(tm // SEL_KC):
        dvT_ref[j] = dv_x[:, j * SEL_KC:(j + 1) * SEL_KC]

    cosi = cosi_ref[...]
    sini = sini_ref[...]
    iq_ref[...] = rope(proj(C_IQ, 128), cosi, sini, IDX_DIM).astype(BF16)
    isk = lane < IDX_DIM
    small_ref[...] = rope(proj(C_SMALL, 128), jnp.where(isk, cosi, 1.0), jnp.where(isk, sini, 0.0), IDX_DIM)
    sgu_ref[...] = proj(C_SGU, 512)


def _inproj(xf, mod, tabs, w_ext, wkv, kvg, batch, seq):
    t, d = xf.shape
    tm = 1024
    per_b = seq // tm
    row = lambda n: pl.BlockSpec((tm, n), lambda i: (i, 0))
    trk = lambda r: pl.BlockSpec((None, tm // SEL_KC, r, SEL_KC), lambda i: (i // per_b, i % per_b, 0, 0))
    tr128 = pl.BlockSpec((None, tm // Q_BLK, 256, Q_BLK), lambda i: (i // per_b, i % per_b, 0, 0))
    full = lambda a: pl.BlockSpec(a.shape, lambda i: (0,) * a.ndim)
    out_shape = (
        jax.ShapeDtypeStruct((t, 384), BF16),
        jax.ShapeDtypeStruct((t, 384), BF16),
        jax.ShapeDtypeStruct((t, 256), F32),
        jax.ShapeDtypeStruct((t, 128), BF16),
        jax.ShapeDtypeStruct((batch, seq // SEL_KC, 256, SEL_KC), BF16),
        jax.ShapeDtypeStruct((t, 128), BF16),
        jax.ShapeDtypeStruct((batch, seq // Q_BLK, 256, Q_BLK), BF16),
        jax.ShapeDtypeStruct((t, 384), BF16),
        jax.ShapeDtypeStruct((t, 128), BF16),
        jax.ShapeDtypeStruct((batch, seq // SEL_KC, 128, SEL_KC), BF16),
        jax.ShapeDtypeStruct((t, 128), BF16),
        jax.ShapeDtypeStruct((t, 128), F32),
        jax.ShapeDtypeStruct((t, 512), F32),
    )
    out_specs = (row(384), row(384), row(256), row(128), trk(256), row(128), tr128,
                 row(384), row(128), trk(128), row(128), row(128), row(512))
    return pl.pallas_call(
        _inproj_kernel,
        grid=(t // tm,),
        in_specs=[row(d),
                  pl.BlockSpec((None, 6, d), lambda i: (i // per_b, 0, 0)),
                  row(128), row(128), row(128), row(128),
                  full(w_ext), full(wkv), full(kvg)],
        out_specs=out_specs,
        out_shape=out_shape,
        compiler_params=_cparams(("parallel",)),
        name="inproj",
    )(xf, mod, *tabs, w_ext, wkv, kvg)


def _compress_kernel(h_ref, plo_ref, phi_ref, wlo_ref, whi_ref, w2_ref, kc_ref, vct_ref):
    h = h_ref[...]
    a = _dot(h + plo_ref[...], wlo_ref[...])
    b = _dot(h + phi_ref[...], whi_ref[...])
    nh = h.shape[0]
    pre = a + pltpu.roll(b, nh - 1, 0)
    cmp = _dot(_gelu(pre), w2_ref[...])
    kc_ref[...] = cmp[:, :128].astype(BF16)
    v_t = cmp[:, 128:].T
    ones_t = jnp.ones((HEAD_DIM, nh), F32)
    vct_ref[...] = jnp.concatenate([v_t[:HEAD_DIM], ones_t, v_t[HEAD_DIM:], ones_t], axis=0).astype(BF16)


def _compress(kvcmp_h, plo, phi, wlo, whi, w2):
    batch, nh, width = kvcmp_h.shape
    full = lambda a: pl.BlockSpec(a.shape, lambda b: (0,) * a.ndim)
    return pl.pallas_call(
        _compress_kernel,
        grid=(batch,),
        in_specs=[pl.BlockSpec((None, nh, width), lambda b: (b, 0, 0)),
                  full(plo), full(phi), full(wlo), full(whi), full(w2)],
        out_specs=(pl.BlockSpec((None, nh, 128), lambda b: (b, 0, 0)),
                   pl.BlockSpec((None, 256, nh), lambda b: (b, 0, 0))),
        out_shape=(jax.ShapeDtypeStruct((batch, nh, 128), BF16),
                   jax.ShapeDtypeStruct((batch, 256, nh), BF16)),
        compiler_params=_cparams(("parallel",)),
        name="nsa_compress",
    )(kvcmp_h, plo, phi, wlo, whi, w2)


def _softmax_pv(s, mask, vt_ones):
    sm = jnp.where(mask, s, NEG)
    m = jnp.max(sm, axis=0, keepdims=True)
    p = jnp.exp2(sm - m).astype(BF16)
    pv = jnp.dot(vt_ones, p, preferred_element_type=F32)
    dh = vt_ones.shape[0] // 2
    inv = jnp.where(m > 0.5 * NEG, 1.0 / jnp.maximum(pv[dh:dh + 1], 1e-30), 0.0)
    return p, inv, pv[:dh]


def _flash_step(s, on, vt_ones, m_ref, acc_ref):
    m_new, acc_new = _flash_update(s, on, vt_ones, m_ref[...], acc_ref[...])
    acc_ref[...] = acc_new
    m_ref[...] = m_new


def _masked_max(s, on, m_old):
    n_grp = s.shape[1] // Q_BLK
    sm = [jnp.where(on, s[:, j * Q_BLK:(j + 1) * Q_BLK], NEG) for j in range(n_grp)]
    m_new = jnp.maximum(m_old, jnp.concatenate([jnp.max(x, axis=0, keepdims=True) for x in sm], axis=1))
    return sm, m_new


def _exp2_weights(sm, m_new):
    return jnp.concatenate([jnp.exp2(x - m_new[:, j * Q_BLK:(j + 1) * Q_BLK]).astype(BF16)
                            for j, x in enumerate(sm)], axis=1)


def _flash_update(s, on, vt_ones, m_old, acc_old):
    sm, m_new = _masked_max(s, on, m_old)
    p = _exp2_weights(sm, m_new)
    acc_new = jnp.exp2(m_old - m_new) * acc_old + jnp.dot(vt_ones, p, preferred_element_type=F32)
    return m_new, acc_new


def _nsa_kernel(nq_ref, nqr_ref, small_ref, kc_ref, vct_ref, ksel_ref, vselT_ref, kwin_ref,
                vwinT_ref, ovt_ref, o_ref, sc_ref, lim_ref, oc_ref, m_ref, acc_ref, sa_ref, sb_ref,
                *, n_blk, n_sel, n_cmp):
    i = pl.program_id(1)
    t0 = i * Q_BLK
    tq = t0 + lax.broadcasted_iota(I32, (1, Q_BLK), 1)
    tq3 = jnp.concatenate([tq, tq, tq], axis=1)
    nq = nq_ref[...]
    nqr = nqr_ref[...]
    nh = kc_ref.shape[0]

    def stack_heads(q, g):
        hs = [g * NSA_GROUP + r for r in range(NSA_GROUP)]
        return jnp.concatenate([q[:, h * HEAD_DIM:(h + 1) * HEAD_DIM] for h in hs], axis=0)

    qrs = [stack_heads(nqr, g) for g in range(NSA_KV_HEADS)]
    for g in range(NSA_KV_HEADS):
        lo, hi = g * HEAD_DIM, (g + 1) * HEAD_DIM
        s_c = _dot_nt(kc_ref[:, lo:hi], stack_heads(nq, g))
        n_io = lax.broadcasted_iota(I32, (nh, 1), 0)
        m_c = (n_io * CMP_STRIDE + (CMP_LEN - 1) <= tq3) & (n_io < n_cmp)
        p_c, inv_c, oc = _softmax_pv(s_c, m_c, vct_ref[2 * lo:2 * hi, :])
        oc_ref[g] = oc * inv_c

        imp = jnp.zeros((n_blk, Q_BLK), F32)
        for r in range(NSA_GROUP):
            cs = slice(r * Q_BLK, (r + 1) * Q_BLK)
            imp = imp + jnp.dot(ovt_ref[...], p_c[:, cs], preferred_element_type=F32) * inv_c[:, cs]
        j_io = lax.broadcasted_iota(I32, (n_blk, 1), 0)
        cur = tq >> 6
        valid = j_io <= cur
        forced = (j_io == 0) | (j_io == cur) | (j_io == cur - 1)
        score = jnp.where(valid, imp + jnp.where(forced, FORCE_BONUS, 0.0), -jnp.inf)
        sc_ref[...] = score
        rank = jnp.zeros((n_blk, Q_BLK), F32)
        for b in range(n_blk):
            row = sc_ref[b:b + 1, :]
            beats = (row > score) | ((row == score) & (j_io > b))
            rank = rank + jnp.where(beats, 1.0, 0.0)
        lim_ref[g] = jnp.where((rank < n_sel) & valid, tq, -1)

    m_ref[...] = jnp.full(m_ref.shape, NEG, F32)
    acc_ref[...] = jnp.zeros(acc_ref.shape, F32)
    bpc = SEL_KC // SEL_BLOCK

    n_ch = (t0 + Q_BLK + SEL_KC - 1) // SEL_KC

    def scores(c, g):
        k0 = pl.multiple_of(c * SEL_KC, SEL_KC)
        return _dot_nt(ksel_ref[pl.ds(k0, SEL_KC), g * HEAD_DIM:(g + 1) * HEAD_DIM], qrs[g])

    def sel_step(c, carry):
        key = c * SEL_KC + lax.broadcasted_iota(I32, (SEL_KC, 1), 0)

        def flash(s, g):
            rows = [jnp.broadcast_to(lim_ref[g, pl.ds(c * bpc + u, 1), :], (SEL_BLOCK, Q_BLK))
                    for u in range(bpc)]
            on = key <= jnp.concatenate(rows, axis=0)
            _flash_step(s, on, vselT_ref[c, 2 * g * HEAD_DIM:2 * (g + 1) * HEAD_DIM, :],
                        m_ref.at[g], acc_ref.at[g])

        sb_ref[...] = scores(c, 1)
        flash(sa_ref[...], 0)
        sa_ref[...] = scores(jnp.minimum(c + 1, n_ch - 1), 0)
        flash(sb_ref[...], 1)
        return carry

    sa_ref[...] = scores(0, 0)
    lax.fori_loop(0, n_ch, sel_step, 0)

    small_t = small_ref[...].T
    heads = []
    for g in range(NSA_KV_HEADS):
        lo, hi = g * HEAD_DIM, (g + 1) * HEAD_DIM
        acc = acc_ref[g]
        o_s = acc[:HEAD_DIM] / jnp.maximum(acc[HEAD_DIM:HEAD_DIM + 1], 1e-30)
        o_c = oc_ref[g]

        nband = WINDOW // Q_BLK + 1
        cb = jnp.maximum(i - WINDOW // Q_BLK, 0)
        b0 = pl.multiple_of(cb * Q_BLK, Q_BLK)
        s_w = _dot_nt(kwin_ref[pl.ds(b0, nband * Q_BLK), lo:hi], qrs[g])
        key = b0 + lax.broadcasted_iota(I32, (nband * Q_BLK, 1), 0)
        diff = tq3 - key
        vt_w = jnp.concatenate([vwinT_ref[cb + u, 2 * lo:2 * hi, :] for u in range(nband)], axis=1)
        _, inv_w, o_w = _softmax_pv(s_w, (diff >= 0) & (diff < WINDOW), vt_w)
        o_w = o_w * inv_w

        for r in range(NSA_GROUP):
            gi = SMALL_GATE + (g * NSA_GROUP + r) * 3
            gt = _sigmoid(small_t[gi:gi + 3, :])
            cs = slice(r * Q_BLK, (r + 1) * Q_BLK)
            heads.append(gt[0:1, :] * o_c[:, cs] + gt[1:2, :] * o_s[:, cs] + gt[2:3, :] * o_w[:, cs])
    o_ref[...] = jnp.concatenate(heads, axis=0).T.astype(BF16)


def _nsa(nq, nqr, small, kc, vct, ksel, vselT, kwin, vwinT, ovt, batch, seq):
    n_q = seq // Q_BLK
    n_blk = seq // SEL_BLOCK
    n_cmp = (seq - CMP_LEN) // CMP_STRIDE + 1
    nh = kc.shape[1]
    blk = lambda n: pl.BlockSpec((Q_BLK, n), lambda b, i: (b * n_q + i, 0))
    perb2 = lambda r, c: pl.BlockSpec((None, r, c), lambda b, i: (b, 0, 0))
    perb_rows = pl.BlockSpec((seq, 128), lambda b, i: (b, 0))
    kern = functools.partial(_nsa_kernel, n_blk=n_blk, n_sel=min(SEL_TOP, n_blk), n_cmp=n_cmp)
    n_col = NSA_GROUP * Q_BLK
    return pl.pallas_call(
        kern,
        grid=(batch, n_q),
        in_specs=[blk(384), blk(384), blk(128),
                  perb2(nh, 128), perb2(256, nh),
                  perb_rows,
                  pl.BlockSpec((None, seq // SEL_KC, 256, SEL_KC), lambda b, i: (b, 0, 0, 0)),
                  perb_rows,
                  pl.BlockSpec((None, seq // Q_BLK, 256, Q_BLK), lambda b, i: (b, 0, 0, 0)),
                  pl.BlockSpec(ovt.shape, lambda b, i: (0, 0))],
        out_specs=blk(384),
        out_shape=jax.ShapeDtypeStruct((batch * seq, 384), BF16),
        scratch_shapes=[pltpu.VMEM((n_blk, Q_BLK), F32),
                        pltpu.VMEM((NSA_KV_HEADS, n_blk, Q_BLK), I32),
                        pltpu.VMEM((NSA_KV_HEADS, HEAD_DIM, n_col), F32),
                        pltpu.VMEM((NSA_KV_HEADS, 1, n_col), F32),
                        pltpu.VMEM((NSA_KV_HEADS, 2 * HEAD_DIM, n_col), F32),
                        pltpu.VMEM((SEL_KC, n_col), F32), pltpu.VMEM((SEL_KC, n_col), F32)],
        compiler_params=_cparams(("parallel", "arbitrary")),
        name="nsa_attn",
    )(nq, nqr, small, kc, vct, ksel, vselT, kwin, vwinT, ovt)


PLANE_KEYS = 32 * 8


def _bit_transpose32(rows):
    rows = list(rows)
    j, mask = 16, 0x0000FFFF
    while j:
        m32 = np.array(mask, np.uint32).view(np.int32)
        k = 0
        while k < 32:
            t = (rows[k] ^ lax.shift_right_logical(rows[k + j], np.int32(j))) & m32
            rows[k] = rows[k] ^ t
            rows[k + j] = rows[k + j] ^ (t << j)
            k = (k + j + 1) & ~j
        j >>= 1
        mask = (mask ^ (mask << j)) & 0xFFFFFFFF
    return rows

def _dsa_kernel(dq_ref, iq_ref, small_ref, dk_ref, dvT_ref, ltri_ref, o_ref,
                ord_ref, plane_ref, m_ref, acc_ref, sa_ref, sb_ref, *, k_top):
    i = pl.program_id(1)
    t0 = i * Q_BLK
    kc = SEL_KC
    n_ch = (t0 + Q_BLK + kc - 1) // kc
    tq = t0 + lax.broadcasted_iota(I32, (1, Q_BLK), 1)

    small_t = small_ref[pl.ds(pl.multiple_of(t0, Q_BLK), Q_BLK), :].T
    w_rows = [small_t[SMALL_IW + h:SMALL_IW + h + 1, :] * (IDX_HEADS ** -0.5) for h in range(IDX_HEADS)]
    iq = iq_ref[...]
    iqs = jnp.concatenate([iq[:, h * IDX_DIM:(h + 1) * IDX_DIM] for h in range(IDX_HEADS)], axis=0)

    def score_step(c, carry):
        k0 = pl.multiple_of(c * kc, kc)
        ik = small_ref[pl.ds(k0, kc), :][:, 0:IDX_DIM]
        lg = _dot_nt(ik, iqs)
        sc = jnp.zeros((kc, Q_BLK), F32)
        for h in range(IDX_HEADS):
            sc = sc + w_rows[h] * jnp.maximum(lg[:, h * Q_BLK:(h + 1) * Q_BLK], 0.0)
        sc = jnp.where(sc == 0.0, 0.0, sc)
        bits = lax.bitcast_convert_type(sc, I32)
        ordv = bits ^ ((bits >> 31) & 0x7FFFFFFF)
        key = k0 + lax.broadcasted_iota(I32, (kc, 1), 0)
        ordv = jnp.where(key <= tq, ordv, INT_MIN)
        ord_ref[pl.ds(k0, kc), :] = ordv
        u = ordv ^ INT_MIN
        for g in range(kc // PLANE_KEYS):
            rows = [u[g * PLANE_KEYS + r * 8:g * PLANE_KEYS + (r + 1) * 8, :] for r in range(32)]
            cols = _bit_transpose32(rows)
            w0 = pl.multiple_of((c * (kc // PLANE_KEYS) + g) * 8, 8)
            for b in range(32):
                plane_ref[b, pl.ds(w0, 8), :] = cols[31 - b]
        return carry

    lax.fori_loop(0, n_ch, score_step, 0)

    def clear_step(c, carry):
        w0 = pl.multiple_of(c * (kc // PLANE_KEYS) * 8, 8)
        for b in range(32):
            plane_ref[b, pl.ds(w0, (kc // PLANE_KEYS) * 8), :] = jnp.zeros(((kc // PLANE_KEYS) * 8, Q_BLK), I32)
        return carry

    lax.fori_loop(n_ch, plane_ref.shape[1] * 32 // kc, clear_step, 0)

    n_words = plane_ref.shape[1] // 8
    alive = [jnp.broadcast_to(jnp.where(w < n_ch * (kc // PLANE_KEYS), -1, 0).astype(I32), (8, Q_BLK))
             for w in range(n_words)]
    k_rem = jnp.full((1, Q_BLK), k_top, I32)
    prefix = jnp.zeros((1, Q_BLK), I32)

    def popcount_rows(words):
        pcs = [lax.population_count(x) for x in words]
        while len(pcs) > 1:
            pcs = [a + b for a, b in zip(pcs[0::2], pcs[1::2])]
        return jnp.sum(pcs[0], axis=0, keepdims=True)

    for bit in range(31, -1, -1):
        ones = [alive[w] & plane_ref[bit, w * 8:(w + 1) * 8, :] for w in range(n_words)]
        cnt = popcount_rows(ones)
        take = cnt >= k_rem
        keep0 = jnp.broadcast_to(jnp.where(take, 0, -1).astype(I32), (8, Q_BLK))
        alive = [ones[w] ^ (alive[w] & keep0) for w in range(n_words)]
        k_rem = jnp.where(take, k_rem, k_rem - cnt)
        prefix = prefix | jnp.where(take, np.int32(INT_MIN) if bit == 31 else np.int32(1 << bit), 0)
    thr = prefix ^ np.int32(INT_MIN)
    n_eq = popcount_rows(alive)
    short = thr == INT_MIN
    need = jnp.where(short, 0, k_rem)
    thr_all = jnp.where(short, INT_MIN + 1, thr)
    no_cut = jnp.min(jnp.where(short | (n_eq == need), 1.0, 0.0)) > 0.5

    q = dq_ref[...]
    qs = jnp.concatenate([q[:, h * HEAD_DIM:(h + 1) * HEAD_DIM] for h in range(DSA_HEADS)], axis=0)
    m_ref[...] = jnp.full(m_ref.shape, NEG, F32)
    acc_ref[...] = jnp.zeros(acc_ref.shape, F32)

    def sweep(mask_fn):
        def attn_step(c, seen):
            k0 = pl.multiple_of(c * kc, kc)
            on, seen = mask_fn(ord_ref[pl.ds(k0, kc), :], seen)
            vt = dvT_ref[c]
            sb_ref[...] = scores(c, 1)
            _flash_step(sa_ref[...], on[:half], vt[:, :half], m_ref, acc_ref)
            sa_ref[...] = scores(jnp.minimum(c + 1, n_ch - 1), 0)
            _flash_step(sb_ref[...], on[half:], vt[:, half:], m_ref, acc_ref)
            return seen
        sa_ref[...] = scores(0, 0)
        lax.fori_loop(0, n_ch, attn_step, jnp.zeros((1, Q_BLK), F32))

    half = kc // 2

    def scores(c, sub):
        k0 = pl.multiple_of(c * kc + sub * half, half)
        return _dot_nt(dk_ref[pl.ds(k0, half), 0:HEAD_DIM], qs)

    @pl.when(no_cut)
    def _():
        sweep(lambda o, seen: (o >= thr_all, seen))

    @pl.when(jnp.logical_not(no_cut))
    def _():
        need_f = need.astype(F32)

        def cut_mask(o, seen):
            eq = jnp.where(o == thr, 1.0, 0.0)
            before = jnp.dot(ltri_ref[...], eq.astype(BF16), preferred_element_type=F32) + seen
            on = (o > thr) | ((eq > 0.5) & (before < need_f))
            return on, seen + jnp.sum(eq, axis=0, keepdims=True)

        sweep(cut_mask)

    acc = acc_ref[...]
    o_t = acc[:HEAD_DIM] / jnp.maximum(acc[HEAD_DIM:HEAD_DIM + 1], 1e-30)
    heads = [o_t[:, h * Q_BLK:(h + 1) * Q_BLK] for h in range(DSA_HEADS)]
    o_ref[...] = jnp.concatenate(heads, axis=0).T.astype(BF16)


def _dsa(dq, iq, small, dk, dvT, ltri, batch, seq):
    n_q = seq // Q_BLK
    blk = lambda n: pl.BlockSpec((Q_BLK, n), lambda b, i: (b * n_q + i, 0))
    perb_rows = pl.BlockSpec((seq, 128), lambda b, i: (b, 0))
    kern = functools.partial(_dsa_kernel, k_top=min(IDX_TOPK_MAX, seq // 4))
    return pl.pallas_call(
        kern,
        grid=(batch, n_q),
        in_specs=[blk(384), blk(128), perb_rows, perb_rows,
                  pl.BlockSpec((None, seq // SEL_KC, 128, SEL_KC), lambda b, i: (b, 0, 0, 0)),
                  pl.BlockSpec(ltri.shape, lambda b, i: (0, 0))],
        out_specs=blk(384),
        out_shape=jax.ShapeDtypeStruct((batch * seq, 384), BF16),
        scratch_shapes=[pltpu.VMEM((seq, Q_BLK), I32),
                        pltpu.VMEM((32, seq // 32, Q_BLK), I32),
                        pltpu.VMEM((1, DSA_HEADS * Q_BLK), F32),
                        pltpu.VMEM((2 * HEAD_DIM, DSA_HEADS * Q_BLK), F32),
                        pltpu.VMEM((SEL_KC // 2, DSA_HEADS * Q_BLK), F32),
                        pltpu.VMEM((SEL_KC // 2, DSA_HEADS * Q_BLK), F32)],
        compiler_params=_cparams(("parallel", "arbitrary")),
        name="dsa_attn",
    )(dq, iq, small, dk, dvT, ltri)


def _sgu_kernel(z_ref, g_ref, b_ref, w_ref, bs_ref, o_ref):
    row = lax.broadcasted_iota(I32, (SGU_CHUNK, SGU_CHUNK), 0)
    col = lax.broadcasted_iota(I32, (SGU_CHUNK, SGU_CHUNK), 1)
    ws = [jnp.where(row >= col, w_ref[g], 0.0).astype(BF16) for g in range(SGU_GROUPS)]
    for c in range(z_ref.shape[0] // SGU_CHUNK):
        rows = slice(c * SGU_CHUNK, (c + 1) * SGU_CHUNK)
        z = _gelu(z_ref[rows, :])
        outs = []
        for g in range(SGU_GROUPS):
            lo, hi = g * HEAD_DIM, (g + 1) * HEAD_DIM
            u = z[:, lo:hi]
            v = _standardize(z[:, SGU_WIDTH + lo:SGU_WIDTH + hi]) * g_ref[:, lo:hi] + b_ref[:, lo:hi]
            outs.append(u * (_dot(ws[g], v) + bs_ref[:, lo:hi]))
        o_ref[rows, :] = jnp.concatenate(outs, axis=1).astype(BF16)


def _sgu(z, g, b, w, bs):
    t = z.shape[0]
    tm = SGU_CHUNK
    full = lambda a: pl.BlockSpec(a.shape, lambda i: (0,) * a.ndim)
    return pl.pallas_call(
        _sgu_kernel,
        grid=(t // tm,),
        in_specs=[pl.BlockSpec((tm, 2 * SGU_WIDTH), lambda i: (i, 0)),
                  full(g), full(b), full(w), full(bs)],
        out_specs=pl.BlockSpec((tm, SGU_WIDTH), lambda i: (i, 0)),
        out_shape=jax.ShapeDtypeStruct((t, SGU_WIDTH), BF16),
        compiler_params=_cparams(("parallel",)),
        name="sgu",
    )(z, g, b, w, bs)


def _outproj_kernel(oa_ref, ob_ref, oc_ref, x_ref, mod_ref, w_ref, g_ref, b_ref, o_ref, *, alpha):
    mix = (jnp.dot(oa_ref[...], w_ref[0:384, :], preferred_element_type=F32)
           + jnp.dot(ob_ref[...], w_ref[384:768, :], preferred_element_type=F32)
           + jnp.dot(oc_ref[...], w_ref[768:1024, :], preferred_element_type=F32))
    y = alpha * x_ref[...] + mod_ref[2:3, :] * mix
    o_ref[...] = _standardize(y) * g_ref[...] + b_ref[...]


def _outproj(oa, ob, oc, xf, mod, w, g, b, seq, alpha):
    t, d = xf.shape
    tm = 512
    per_b = seq // tm
    row = lambda n: pl.BlockSpec((tm, n), lambda i: (i, 0))
    full = lambda a: pl.BlockSpec(a.shape, lambda i: (0,) * a.ndim)
    return pl.pallas_call(
        functools.partial(_outproj_kernel, alpha=alpha),
        grid=(t // tm,),
        in_specs=[row(384), row(384), row(256), row(d),
                  pl.BlockSpec((None, 6, d), lambda i: (i // per_b, 0, 0)),
                  full(w), full(g), full(b)],
        out_specs=row(d),
        out_shape=jax.ShapeDtypeStruct((t, d), F32),
        compiler_params=_cparams(("parallel",)),
        name="outproj",
    )(oa, ob, oc, xf, mod, w, g, b)


def _ffn_kernel(x_ref, mod_ref, wg_ref, wu_ref, wd_ref, g_ref, b_ref, o_ref, acc_ref, *, alpha, fc):
    x = x_ref[...]
    hb = (_standardize(x) * (1.0 + mod_ref[4:5, :]) + mod_ref[3:4, :]).astype(BF16)
    d_ff = wg_ref.shape[1]
    for j in range(d_ff // fc):
        cs = slice(j * fc, (j + 1) * fc)
        a = (_silu(jnp.dot(hb, wg_ref[:, cs], preferred_element_type=F32))
             * jnp.dot(hb, wu_ref[:, cs], preferred_element_type=F32)).astype(BF16)
        part = jnp.dot(a, wd_ref[cs, :], preferred_element_type=F32)
        if j == 0:
            acc_ref[...] = part
        else:
            acc_ref[...] += part
    y = alpha * x + mod_ref[5:6, :] * acc_ref[...]
    o_ref[...] = _standardize(y) * g_ref[...] + b_ref[...]


def _ffn(xf, mod, wg, wu, wd, g, b, seq, alpha):
    t, d = xf.shape
    tm = 1024
    per_b = seq // tm
    row = pl.BlockSpec((tm, d), lambda i: (i, 0))
    once = lambda a: pl.BlockSpec(a.shape, lambda i: (0,) * a.ndim, pipeline_mode=pl.Buffered(1))
    return pl.pallas_call(
        functools.partial(_ffn_kernel, alpha=alpha, fc=256),
        grid=(t // tm,),
        in_specs=[row, pl.BlockSpec((None, 6, d), lambda i: (i // per_b, 0, 0)),
                  once(wg), once(wu), once(wd), once(g), once(b)],
        out_specs=row,
        out_shape=jax.ShapeDtypeStruct((t, d), F32),
        scratch_shapes=[pltpu.VMEM((tm, d), F32)],
        compiler_params=_cparams(("parallel",)),
        name="ffn",
    )(xf, mod, wg, wu, wd, g, b)


MOE_TM = 1024
MOE_SUB = 64
MOE_BM = 1024
MOE_FC = 896


def _route_kernel(x_ref, mod_ref, wr_ref, br_ref, utri_ref, h_ref, gate_ref, pos_ref, post_ref, cnt_ref):
    h = _standardize(x_ref[...]) * (1.0 + mod_ref[4:5, :]) + mod_ref[3:4, :]
    h_ref[...] = h.astype(BF16)
    lane = lax.broadcasted_iota(I32, (1, LANES), 1)
    logits = jnp.dot(h, wr_ref[...], preferred_element_type=F32,
                     precision=lax.Precision.HIGHEST) + br_ref[...]
    lg = jnp.where(lane < N_EXPERTS, logits, -jnp.inf)
    v0 = jnp.max(lg, axis=-1, keepdims=True)
    lane_f = lane.astype(F32)
    i0 = jnp.min(jnp.where(lg == v0, lane_f, float(LANES)), axis=-1, keepdims=True)
    lg1 = jnp.where(lane_f == i0, -jnp.inf, lg)
    v1 = jnp.max(lg1, axis=-1, keepdims=True)
    i1 = jnp.min(jnp.where(lg1 == v1, lane_f, float(LANES)), axis=-1, keepdims=True)
    e1 = jnp.exp(v1 - v0)
    den = 1.0 + e1
    gate_ref[...] = jnp.where(lane_f == i0, 1.0 / den, 0.0) + jnp.where(lane_f == i1, e1 / den, 0.0)
    sel_t = jnp.where((lane_f == i0) | (lane_f == i1), 1.0, 0.0).T
    rank_t = jnp.dot(sel_t.astype(BF16), utri_ref[...], preferred_element_type=F32)
    pos_t = jnp.where(sel_t > 0.5, rank_t, -1.0)
    post_ref[...] = pos_t[0:N_EXPERTS]
    pos_ref[...] = pos_t.T
    cnt_ref[...] = jnp.broadcast_to(jnp.sum(sel_t[0:N_EXPERTS], axis=1, keepdims=True), (N_EXPERTS, LANES))


def _route(xf, mod, wr, br, utri, seq):
    t, d = xf.shape
    tm = MOE_TM
    nt = t // tm
    per_b = seq // tm
    full = lambda a: pl.BlockSpec(a.shape, lambda i: (0,) * a.ndim)
    return pl.pallas_call(
        _route_kernel,
        grid=(nt,),
        in_specs=[pl.BlockSpec((tm, d), lambda i: (i, 0)),
                  pl.BlockSpec((None, 6, d), lambda i: (i // per_b, 0, 0)),
                  full(wr), full(br), full(utri)],
        out_specs=(pl.BlockSpec((tm, d), lambda i: (i, 0)),
                   pl.BlockSpec((tm, LANES), lambda i: (i, 0)),
                   pl.BlockSpec((tm, LANES), lambda i: (i, 0)),
                   pl.BlockSpec((None, N_EXPERTS, tm), lambda i: (i, 0, 0)),
                   pl.BlockSpec((None, N_EXPERTS, LANES), lambda i: (i, 0, 0))),
        out_shape=(jax.ShapeDtypeStruct((t, d), BF16),
                   jax.ShapeDtypeStruct((t, LANES), F32),
                   jax.ShapeDtypeStruct((t, LANES), F32),
                   jax.ShapeDtypeStruct((nt, N_EXPERTS, tm), F32),
                   jax.ShapeDtypeStruct((nt, N_EXPERTS, LANES), F32)),
        compiler_params=_cparams(("parallel",)),
        name="moe_route",
    )(xf, mod, wr, br, utri)


def _moe_schedule(cnt, n_blocks, k_max):
    nt, ne = cnt.shape
    per = MOE_BM // MOE_SUB
    nb = (cnt + MOE_SUB - 1) // MOE_SUB
    nbt = nb.T
    tot = jnp.sum(nbt, axis=1)
    reg = (tot + per - 1) // per * per
    reg_end = jnp.cumsum(reg)
    reg_start = reg_end - reg
    seg_end = jnp.cumsum(nbt, axis=1)
    seg_start = reg_start[:, None] + seg_end - nbt
    j = jnp.arange(n_blocks, dtype=I32)
    e_j = jnp.minimum(jnp.sum(reg_end[None, :] <= j[:, None], axis=1), ne - 1).astype(I32)
    valid_j = (j - reg_start[e_j]) < tot[e_j]
    step_e = e_j[::per]
    step_valid = valid_j[::per].astype(I32)
    cum = jnp.cumsum(nb, axis=1)
    n_tile = cum[:, -1]
    k = jnp.minimum(jnp.arange(k_max, dtype=I32)[None, :], n_tile[:, None] - 1)
    e_k = jnp.sum(cum[:, None, :] <= k[:, :, None], axis=2).astype(I32)
    tile = jnp.arange(nt, dtype=I32)[:, None]
    s_k = k - (jnp.take_along_axis(cum, e_k, axis=1) - jnp.take_along_axis(nb, e_k, axis=1))
    blk_k = seg_start[e_k, tile] + s_k
    valid_k = (jnp.arange(k_max, dtype=I32)[None, :] < n_tile[:, None]).astype(I32)
    slot = jnp.arange(nt * k_max, dtype=I32)
    src = jnp.zeros((n_blocks,), I32).at[jnp.where(valid_k > 0, blk_k, n_blocks).reshape(-1)].set(slot, mode='drop')
    return (step_e, step_valid, src), (blk_k.astype(I32), e_k, s_k.astype(I32), valid_k)


def _dispatch_kernel(e_ref, s_ref, h_ref, post_ref, gate_ref, xs_ref, gs_ref, *, k_half):
    i = pl.program_id(0)
    kh = pl.program_id(1)
    lane = lax.broadcasted_iota(I32, (1, LANES), 1)
    sub = lax.broadcasted_iota(I32, (MOE_SUB, 1), 0)
    hots = []
    for u in range(k_half):
        k = kh * k_half + u
        row = post_ref[pl.ds(e_ref[i, k], 1), :]
        want = (s_ref[i, k] * MOE_SUB + sub).astype(F32)
        hots.append(jnp.where(row == want, 1.0, 0.0).astype(BF16))
    onehot = jnp.concatenate(hots, axis=0)
    xs_ref[...] = jnp.dot(onehot, h_ref[...], preferred_element_type=F32).astype(BF16)
    gates = gate_ref[...]
    g_hi = gates.astype(BF16)
    g_lo = (gates - g_hi.astype(F32)).astype(BF16)
    gs = (jnp.dot(onehot, g_hi, preferred_element_type=F32)
          + jnp.dot(onehot, g_lo, preferred_element_type=F32))
    for u in range(k_half):
        rows = slice(u * MOE_SUB, (u + 1) * MOE_SUB)
        mine = lane == e_ref[i, kh * k_half + u]
        gcol = jnp.sum(jnp.where(mine, gs[rows], 0.0), axis=-1, keepdims=True)
        gs_ref[rows, :] = jnp.broadcast_to(gcol, (MOE_SUB, LANES))


def _dispatch(h, post, gates, sched):
    t, d = h.shape
    tm = MOE_TM
    k_max = sched[0].shape[1]
    k_half = k_max // 2
    rows = (t // tm) * k_max * MOE_SUB
    grid_spec = pltpu.PrefetchScalarGridSpec(
        num_scalar_prefetch=2,
        grid=(t // tm, 2),
        in_specs=[pl.BlockSpec((tm, d), lambda i, kh, *_: (i, 0)),
                  pl.BlockSpec((None, N_EXPERTS, tm), lambda i, kh, *_: (i, 0, 0)),
                  pl.BlockSpec((tm, LANES), lambda i, kh, *_: (i, 0))],
        out_specs=(pl.BlockSpec((k_half * MOE_SUB, d), lambda i, kh, *_: (2 * i + kh, 0)),
                   pl.BlockSpec((k_half * MOE_SUB, LANES), lambda i, kh, *_: (2 * i + kh, 0))),
    )
    return pl.pallas_call(
        functools.partial(_dispatch_kernel, k_half=k_half),
        grid_spec=grid_spec,
        out_shape=(jax.ShapeDtypeStruct((rows, d), BF16), jax.ShapeDtypeStruct((rows, LANES), F32)),
        compiler_params=_cparams(("arbitrary", "arbitrary")),
        name="moe_dispatch",
    )(*sched, h, post, gates)


def _experts_kernel(e_ref, v_ref, src_ref, *refs, per):
    xs_refs, gs_refs = refs[:per], refs[per:2 * per]
    wg_ref, wu_ref, wd_ref, ys_ref, acc_ref, xcat_ref = refs[2 * per:]
    jb = pl.program_id(0)
    f = pl.program_id(1)
    last = pl.num_programs(1) - 1

    @pl.when((v_ref[jb] > 0) & (f == 0))
    def _():
        for u in range(per):
            xcat_ref[u * MOE_SUB:(u + 1) * MOE_SUB, :] = xs_refs[u][...]

    @pl.when(v_ref[jb] > 0)
    def _():
        xb = xcat_ref[...]
        a = (_silu(jnp.dot(xb, wg_ref[...].astype(BF16), preferred_element_type=F32))
             * jnp.dot(xb, wu_ref[...].astype(BF16), preferred_element_type=F32)).astype(BF16)
        part = jnp.dot(a, wd_ref[...].astype(BF16), preferred_element_type=F32)

        @pl.when(f == 0)
        def _():
            acc_ref[...] = part

        @pl.when(f > 0)
        def _():
            acc_ref[...] += part

        @pl.when(f == last)
        def _():
            for u in range(per):
                rows = slice(u * MOE_SUB, (u + 1) * MOE_SUB)
                gate = jnp.concatenate([gs_refs[u][...]] * (ys_ref.shape[1] // LANES), axis=1)
                ys_ref[rows, :] = (acc_ref[rows, :] * gate).astype(BF16)

    @pl.when((v_ref[jb] == 0) & (f == last))
    def _():
        ys_ref[...] = jnp.zeros(ys_ref.shape, BF16)


def _experts(xs, gs, wg, wu, wd, sched, n_blocks):
    d = xs.shape[1]
    n_e, _, d_ff = wg.shape
    fc = MOE_FC
    n_f = d_ff // fc
    per = MOE_BM // MOE_SUB
    fsel = lambda jb, f, e, v: jnp.where(v[jb] > 0, f, n_f - 1)

    def src_index(u, jb, f, e, v, src):
        return (src[jb * per + u], 0)

    grid_spec = pltpu.PrefetchScalarGridSpec(
        num_scalar_prefetch=3,
        grid=(n_blocks // per, n_f),
        in_specs=([pl.BlockSpec((MOE_SUB, d), functools.partial(src_index, u)) for u in range(per)]
                  + [pl.BlockSpec((MOE_SUB, LANES), functools.partial(src_index, u)) for u in range(per)]
                  + [pl.BlockSpec((None, d, fc), lambda jb, f, e, v, src: (e[jb], 0, fsel(jb, f, e, v))),
                     pl.BlockSpec((None, d, fc), lambda jb, f, e, v, src: (e[jb], 0, fsel(jb, f, e, v))),
                     pl.BlockSpec((None, fc, d), lambda jb, f, e, v, src: (e[jb], fsel(jb, f, e, v), 0))]),
        out_specs=pl.BlockSpec((MOE_BM, d), lambda jb, f, e, v, src: (jb, 0)),
        scratch_shapes=[pltpu.VMEM((MOE_BM, d), F32), pltpu.VMEM((MOE_BM, d), BF16)],
    )
    return pl.pallas_call(
        functools.partial(_experts_kernel, per=per),
        grid_spec=grid_spec,
        out_shape=jax.ShapeDtypeStruct((n_blocks * MOE_SUB, d), BF16),
        compiler_params=_cparams(("arbitrary", "arbitrary")),
        name="moe_experts",
    )(*sched, *([xs] * per), *([gs] * per), wg, wu, wd)


def _combine_kernel(blk_ref, e_ref, s_ref, v_ref, *refs, alpha, k_max):
    ys_refs = refs[:k_max]
    pos_ref, x_ref, mod_ref, g_ref, b_ref, o_ref, pc_ref = refs[k_max:]
    i = pl.program_id(0)
    th = pos_ref.shape[0]
    lane = lax.broadcasted_iota(I32, (1, LANES), 1)
    pos = pos_ref[...]
    for e in range(N_EXPERTS):
        col = jnp.sum(jnp.where(lane == e, pos, 0.0), axis=-1, keepdims=True)
        pc_ref[e] = jnp.broadcast_to(col, (th, LANES))
    hots = []
    per_lane = LANES // MOE_SUB
    for k0 in range(0, k_max, per_lane):
        want, pc = None, None
        for j in range(per_lane):
            k = k0 + j
            s = jnp.where(v_ref[i, k] > 0, s_ref[i, k], -2)
            w_j = (s * MOE_SUB + lane - j * MOE_SUB).astype(F32)
            p_j = pc_ref[e_ref[i, k]]
            here = lane >= j * MOE_SUB
            want = w_j if want is None else jnp.where(here, w_j, want)
            pc = p_j if pc is None else jnp.where(here, p_j, pc)
        hots.append(jnp.where(pc == want, 1.0, 0.0).astype(BF16))
    f = jnp.dot(jnp.concatenate(hots, axis=1), jnp.concatenate([r[...] for r in ys_refs], axis=0),
                preferred_element_type=F32)
    y = alpha * x_ref[...] + mod_ref[5:6, :] * f
    o_ref[...] = _standardize(y) * g_ref[...] + b_ref[...]


def _combine(ys, pos, xf, mod, g, b, sched, seq, alpha):
    t, d = xf.shape
    th = MOE_TM // 2
    per_b = seq // th
    k_max = sched[0].shape[1]
    row = lambda n: pl.BlockSpec((th, n), lambda i, hh, *_: (2 * i + hh, 0))
    full = lambda a: pl.BlockSpec(a.shape, lambda i, hh, *_: (0,) * a.ndim)

    def ys_index(k, i, hh, blk, e, s, v):
        return (blk[i, k], 0)

    grid_spec = pltpu.PrefetchScalarGridSpec(
        num_scalar_prefetch=4,
        grid=(t // MOE_TM, 2),
        in_specs=([pl.BlockSpec((MOE_SUB, d), functools.partial(ys_index, k)) for k in range(k_max)]
                  + [row(LANES), row(d),
                     pl.BlockSpec((None, 6, d), lambda i, hh, *_: ((2 * i + hh) // per_b, 0, 0)),
                     full(g), full(b)]),
        out_specs=row(d),
        scratch_shapes=[pltpu.VMEM((N_EXPERTS, th, LANES), F32)],
    )
    return pl.pallas_call(
        functools.partial(_combine_kernel, alpha=alpha, k_max=k_max),
        grid_spec=grid_spec,
        out_shape=jax.ShapeDtypeStruct((t, d), F32),
        compiler_params=_cparams(("arbitrary", "arbitrary")),
        name="moe_combine",
    )(*sched, *([ys] * k_max), pos, xf, mod, g, b)


def _moe(xf, mod, wr, br, wg, wu, wd, g, b, seq, alpha):
    t, d = xf.shape
    nt = t // MOE_TM
    per = MOE_BM // MOE_SUB
    n_blocks = TOP_K * t // MOE_SUB + nt * N_EXPERTS + N_EXPERTS * (per - 1)
    n_blocks = (n_blocks + per - 1) // per * per
    k_max = TOP_K * MOE_TM // MOE_SUB + N_EXPERTS
    utri = jnp.asarray(np.triu(np.ones((MOE_TM, MOE_TM), np.float32), 1), BF16)
    h, gates, pos, post, cnt = _route(xf, mod, wr, br, utri, seq)
    sched_x, sched_c = _moe_schedule(cnt[:, :, 0].astype(I32), n_blocks, k_max)
    xs, gs = _dispatch(h, post, gates, sched_c[1:3])
    ys = _experts(xs, gs, wg, wu, wd, sched_x, n_blocks)
    return _combine(ys, pos, xf, mod, g, b, sched_c, seq, alpha)


def _inproj_columns():
    sizes = (NSA_WIDTH, 6 * NSA_KV_HEADS * HEAD_DIM, 3 * NSA_HEADS, DSA_WIDTH, DSA_LATENT,
             IDX_HEADS * IDX_DIM, IDX_DIM, IDX_HEADS, 2 * SGU_WIDTH)
    starts = np.concatenate([[0], np.cumsum(sizes)])
    o_nq, o_kv, o_g, o_dq, o_ckv, o_iq, o_ik, o_iw, o_sgu = starts[:-1]
    perm = np.full((C_TOTAL,), -1, np.int64)
    perm[C_NQ:C_NQ + 384] = o_nq + np.arange(384)
    perm[C_KV:C_KV + 768] = o_kv + np.arange(768)
    perm[C_DQ:C_DQ + 384] = o_dq + np.arange(384)
    perm[C_CKV:C_CKV + 128] = o_ckv + np.arange(128)
    perm[C_IQ:C_IQ + 128] = o_iq + np.arange(128)
    perm[C_SGU:C_SGU + 512] = o_sgu + np.arange(512)
    perm[C_SMALL:C_SMALL + IDX_DIM] = o_ik + np.arange(IDX_DIM)
    perm[C_SMALL + SMALL_IW:C_SMALL + SMALL_IW + IDX_HEADS] = o_iw + np.arange(IDX_HEADS)
    perm[C_SMALL + SMALL_GATE:C_SMALL + SMALL_GATE + 3 * NSA_HEADS] = o_g + np.arange(3 * NSA_HEADS)
    return perm, int(starts[-1])


def _compress_weights(pos, w1, w2):
    half = CMP_LEN // 2
    eye_g = jnp.eye(NSA_KV_HEADS, dtype=F32)
    eye_j = jnp.eye(2, dtype=F32)

    def big(w1_half):
        w = jnp.einsum('jlde,jk,gh->ljgdkhe', w1_half, eye_j, eye_g)
        return w.reshape(half * 2 * NSA_KV_HEADS * HEAD_DIM, 2 * NSA_KV_HEADS * HEAD_DIM)

    def posrow(p_half):
        p = jnp.broadcast_to(p_half.transpose(1, 0, 2)[:, :, None, :], (half, 2, NSA_KV_HEADS, HEAD_DIM))
        return p.reshape(1, -1)

    w2big = jnp.einsum('jef,jk,gh->jgekhf', w2, eye_j, eye_g).reshape(256, 256)
    return (posrow(pos[:, :half]), posrow(pos[:, half:]),
            big(w1[:, :half]).astype(BF16), big(w1[:, half:]).astype(BF16), w2big.astype(BF16))


def _rope_tables(positions):
    pos = positions.astype(F32).reshape(-1, 1)

    def tab(dim):
        inv = ROPE_THETA ** (-jnp.arange(0, dim, 2, dtype=F32) / dim)
        lane = np.arange(LANES)
        ang = pos * inv[lane % (dim // 2)][None, :]
        sign = np.where(lane % dim < dim // 2, -1.0, 1.0).astype(np.float32)
        return jnp.cos(ang), jnp.sin(ang) * sign

    cos_h, sin_h = tab(HEAD_DIM)
    cos_i, sin_i = tab(IDX_DIM)
    return cos_h, sin_h, cos_i, sin_i


def kernel(x, c, positions, w_ada, b_ada, w_in, nsa_cmp_pos, nsa_cmp_w1, nsa_cmp_w2, dsa_kv_norm, dsa_w_uk, dsa_w_uv, sgu_norm_g, sgu_norm_b, sgu_w, sgu_b, w_out, ln1_g, ln1_b, ln2_g, ln2_b, ffn_w_gate, ffn_w_up, ffn_w_down, moe_w_router, moe_b_router, moe_w_gate, moe_w_up, moe_w_down):
    batch, seq, d = x.shape
    depth = w_ada.shape[0]
    t = batch * seq
    alpha = (2 * depth) ** 0.25
    assert seq % MOE_TM == 0 and seq >= WINDOW + Q_BLK

    tabs = _rope_tables(positions)
    mod_all = _adaln(c, w_ada, b_ada).reshape(depth, batch, 6, d)
    perm, in_width = _inproj_columns()
    perm = jnp.asarray(np.where(perm < 0, in_width, perm), I32)

    n_blk = seq // SEL_BLOCK
    n_half = seq // CMP_STRIDE
    cmp_start = np.arange(n_half)[None, :] * CMP_STRIDE
    blk_start = np.arange(n_blk)[:, None] * SEL_BLOCK
    ovt = jnp.asarray((cmp_start < blk_start + SEL_BLOCK) & (cmp_start + CMP_LEN > blk_start), BF16)
    ltri = jnp.asarray(np.tril(np.ones((SEL_KC, SEL_KC), np.float32), -1), BF16)

    xf = x.reshape(t, d)
    for layer in range(depth):
        mod = mod_all[layer]
        w_pad = jnp.concatenate([w_in[layer], jnp.zeros((d, 1), F32)], axis=1)
        w_ext = jnp.take(w_pad, perm, axis=1).astype(BF16)
        wkv = jnp.concatenate([dsa_w_uk[layer], dsa_w_uv[layer]], axis=1).astype(BF16)
        (nq, nqr, kvcmp, ksel, vselT, kwin, vwinT, dq, dk, dvT, iq, small, sgu_z) = _inproj(
            xf, mod, tabs, w_ext, wkv, dsa_kv_norm[layer].reshape(1, -1), batch, seq)

        plo, phi, wlo, whi, w2big = _compress_weights(nsa_cmp_pos[layer], nsa_cmp_w1[layer], nsa_cmp_w2[layer])
        kc, vct = _compress(kvcmp.reshape(batch, n_half, CMP_STRIDE * 256), plo, phi, wlo, whi, w2big)
        o_a = _nsa(nq, nqr, small, kc, vct, ksel, vselT, kwin, vwinT, ovt, batch, seq)
        o_b = _dsa(dq, iq, small, dk, dvT, ltri, batch, seq)
        bs = jnp.repeat(sgu_b[layer].T, HEAD_DIM, axis=1)
        o_c = _sgu(sgu_z, sgu_norm_g[layer].reshape(1, -1), sgu_norm_b[layer].reshape(1, -1),
                   sgu_w[layer], bs)
        xf = _outproj(o_a, o_b, o_c, xf, mod, w_out[layer].astype(BF16),
                      ln1_g[layer].reshape(1, -1), ln1_b[layer].reshape(1, -1), seq, alpha)

        j = layer // 2
        g2, b2 = ln2_g[layer].reshape(1, -1), ln2_b[layer].reshape(1, -1)
        if layer % 2 == 0:
            xf = _ffn(xf, mod, ffn_w_gate[j].astype(BF16), ffn_w_up[j].astype(BF16),
                      ffn_w_down[j].astype(BF16), g2, b2, seq, alpha)
        else:
            wr = jnp.pad(moe_w_router[j], ((0, 0), (0, LANES - N_EXPERTS)))
            br = jnp.pad(moe_b_router[j], (0, LANES - N_EXPERTS)).reshape(1, -1)
            xf = _moe(xf, mod, wr, br, moe_w_gate[j], moe_w_up[j], moe_w_down[j], g2, b2, seq, alpha)
    return xf.reshape(batch, seq, d)
```

```python
import functools

import numpy as np
import jax
import jax.numpy as jnp
from jax import lax
from jax.experimental import pallas as pl
from jax.experimental.pallas import tpu as pltpu

F32 = jnp.float32
BF16 = jnp.bfloat16
I32 = jnp.int32

HEAD_DIM = 64
Q_BLK = 128
ROPE_THETA = 10000.0
LN_EPS = 1e-5
RMS_EPS = 1e-6

NSA_HEADS = 6
NSA_KV_HEADS = 2
NSA_GROUP = NSA_HEADS // NSA_KV_HEADS
NSA_WIDTH = NSA_HEADS * HEAD_DIM
CMP_LEN = 32
CMP_STRIDE = 16
SEL_BLOCK = 64
SEL_TOP = 16
WINDOW = 512
FORCE_BONUS = 1e4

DSA_HEADS = 6
DSA_WIDTH = DSA_HEADS * HEAD_DIM
DSA_LATENT = 128
IDX_HEADS = 4
IDX_DIM = 32
IDX_TOPK_MAX = 256

SGU_GROUPS = 4
SGU_CHUNK = 128
SGU_WIDTH = SGU_GROUPS * HEAD_DIM

N_EXPERTS = 8
TOP_K = 2

LANES = 128
VMEM_LIMIT = 56 * 1024 * 1024
NEG = -1e30
INT_MIN = -(2 ** 31)

C_NQ = 0
C_KV = 384
C_DQ = 1152
C_CKV = 1536
C_IQ = 1664
C_SGU = 1792
C_SMALL = 2304
C_TOTAL = 2432
SMALL_IW = 32
SMALL_GATE = 36

SEL_KC = 512
Q_SCALE = HEAD_DIM ** -0.5 * 1.4426950408889634


def _cparams(sem):
    return pltpu.CompilerParams(dimension_semantics=sem, vmem_limit_bytes=VMEM_LIMIT)


def _dot(a, b):
    return jnp.dot(a.astype(BF16), b.astype(BF16), preferred_element_type=F32)


def _dot_nt(a, b):
    return lax.dot_general(a.astype(BF16), b.astype(BF16), (((1,), (1,)), ((), ())),
                           preferred_element_type=F32)


def _gelu(x):
    return 0.5 * x * (1.0 + jnp.tanh(0.7978845608028654 * (x + 0.044715 * (x * x * x))))


def _silu(x):
    return x * (1.0 / (1.0 + jnp.exp(-x)))


def _sigmoid(x):
    return 1.0 / (1.0 + jnp.exp(-x))


def _standardize(x):
    mu = jnp.mean(x, axis=-1, keepdims=True)
    xc = x - mu
    var = jnp.mean(xc * xc, axis=-1, keepdims=True)
    return xc * lax.rsqrt(var + LN_EPS)


def _adaln_kernel(c_ref, w_ref, b_ref, o_ref):
    c = c_ref[...]
    o_ref[...] = jnp.dot(_silu(c), w_ref[...], preferred_element_type=F32,
                         precision=lax.Precision.HIGHEST) + b_ref[...]


def _adaln(c, w_ada, b_ada):
    depth, d, n = w_ada.shape
    b = c.shape[0]
    tn = 512
    return pl.pallas_call(
        _adaln_kernel,
        grid=(depth, n // tn),
        in_specs=[pl.BlockSpec((b, d), lambda l, j: (0, 0)),
                  pl.BlockSpec((None, d, tn), lambda l, j: (l, 0, j)),
                  pl.BlockSpec((None, 1, tn), lambda l, j: (l, 0, j))],
        out_specs=pl.BlockSpec((None, b, tn), lambda l, j: (l, 0, j)),
        out_shape=jax.ShapeDtypeStruct((depth, b, n), F32),
        compiler_params=_cparams(("arbitrary", "arbitrary")),
        name="adaln",
    )(c, w_ada, b_ada.reshape(depth, 1, n))


def _inproj_kernel(x_ref, mod_ref, cos_ref, sin_ref, cosi_ref, sini_ref, w_ref, wkv_ref, kvg_ref,
                   nq_ref, nqr_ref, kvcmp_ref, ksel_ref, vselT_ref, kwin_ref, vwinT_ref,
                   dq_ref, dk_ref, dvT_ref, iq_ref, small_ref, sgu_ref):
    tm = x_ref.shape[0]
    shift = mod_ref[0:1, :]
    scale = mod_ref[1:2, :]
    hb = (_standardize(x_ref[...]) * (1.0 + scale) + shift).astype(BF16)

    def proj(c0, n):
        return jnp.dot(hb, w_ref[:, c0:c0 + n], preferred_element_type=F32)

    cos = cos_ref[...]
    sin = sin_ref[...]
    lane = lax.broadcasted_iota(I32, (1, LANES), 1)

    def rope(z, cos_t, sin_t, dim):
        half = dim // 2
        low = (lane & (dim - 1)) < half
        outs = []
        for c in range(z.shape[1] // LANES):
            zc = z[:, c * LANES:(c + 1) * LANES]
            swapped = jnp.where(low, pltpu.roll(zc, LANES - half, 1), pltpu.roll(zc, half, 1))
            outs.append(zc * cos_t + swapped * sin_t)
        return outs[0] if len(outs) == 1 else jnp.concatenate(outs, axis=1)

    zq = proj(C_NQ, 384)
    nq_ref[...] = (zq * Q_SCALE).astype(BF16)
    nqr_ref[...] = (rope(zq, cos, sin, HEAD_DIM) * Q_SCALE).astype(BF16)

    kvcmp_ref[...] = proj(C_KV, 256)
    ksel_ref[...] = rope(proj(C_KV + 256, 128), cos, sin, HEAD_DIM).astype(BF16)
    ones_t = jnp.ones((HEAD_DIM, tm), F32)
    vsel_t = proj(C_KV + 384, 128).T
    vsel_x = jnp.concatenate([vsel_t[:HEAD_DIM], ones_t, vsel_t[HEAD_DIM:], ones_t], axis=0).astype(BF16)
    for j in range(tm // SEL_KC):
        vselT_ref[j] = vsel_x[:, j * SEL_KC:(j + 1) * SEL_KC]
    kwin_ref[...] = rope(proj(C_KV + 512, 128), cos, sin, HEAD_DIM).astype(BF16)
    vwin_t = proj(C_KV + 640, 128).T
    vwin_x = jnp.concatenate([vwin_t[:HEAD_DIM], ones_t, vwin_t[HEAD_DIM:], ones_t], axis=0).astype(BF16)
    for j in range(tm // Q_BLK):
        vwinT_ref[j] = vwin_x[:, j * Q_BLK:(j + 1) * Q_BLK]

    dq_ref[...] = (rope(proj(C_DQ, 384), cos, sin, HEAD_DIM) * Q_SCALE).astype(BF16)

    ckv = proj(C_CKV, 128)
    ckv = ckv * lax.rsqrt(jnp.mean(ckv * ckv, axis=-1, keepdims=True) + RMS_EPS) * kvg_ref[...]
    kd = jnp.dot(ckv.astype(BF16), wkv_ref[...], preferred_element_type=F32)
    first = lane < HEAD_DIM
    dkv = rope(kd, jnp.where(first, cos, 1.0), jnp.where(first, sin, 0.0), HEAD_DIM)
    dk_ref[...] = dkv.astype(BF16)
    dv_x = jnp.concatenate([dkv.T[HEAD_DIM:], ones_t], axis=0).astype(BF16)
    for j in range(tm // SEL_KC):
        dvT_ref[j] = dv_x[:, j * SEL_KC:(j + 1) * SEL_KC]

    cosi = cosi_ref[...]
    sini = sini_ref[...]
    iq_ref[...] = rope(proj(C_IQ, 128), cosi, sini, IDX_DIM).astype(BF16)
    isk = lane < IDX_DIM
    small_ref[...] = rope(proj(C_SMALL, 128), jnp.where(isk, cosi, 1.0), jnp.where(isk, sini, 0.0), IDX_DIM)
    sgu_ref[...] = proj(C_SGU, 512)


def _inproj(xf, mod, tabs, w_ext, wkv, kvg, batch, seq):
    t, d = xf.shape
    tm = 1024
    per_b = seq // tm
    row = lambda n: pl.BlockSpec((tm, n), lambda i: (i, 0))
    trk = lambda r: pl.BlockSpec((None, tm // SEL_KC, r, SEL_KC), lambda i: (i // per_b, i % per_b, 0, 0))
    tr128 = pl.BlockSpec((None, tm // Q_BLK, 256, Q_BLK), lambda i: (i // per_b, i % per_b, 0, 0))
    full = lambda a: pl.BlockSpec(a.shape, lambda i: (0,) * a.ndim)
    out_shape = (
        jax.ShapeDtypeStruct((t, 384), BF16),
        jax.ShapeDtypeStruct((t, 384), BF16),
        jax.ShapeDtypeStruct((t, 256), F32),
        jax.ShapeDtypeStruct((t, 128), BF16),
        jax.ShapeDtypeStruct((batch, seq // SEL_KC, 256, SEL_KC), BF16),
        jax.ShapeDtypeStruct((t, 128), BF16),
        jax.ShapeDtypeStruct((batch, seq // Q_BLK, 256, Q_BLK), BF16),
        jax.ShapeDtypeStruct((t, 384), BF16),
        jax.ShapeDtypeStruct((t, 128), BF16),
        jax.ShapeDtypeStruct((batch, seq // SEL_KC, 128, SEL_KC), BF16),
        jax.ShapeDtypeStruct((t, 128), BF16),
        jax.ShapeDtypeStruct((t, 128), F32),
        jax.ShapeDtypeStruct((t, 512), F32),
    )
    out_specs = (row(384), row(384), row(256), row(128), trk(256), row(128), tr128,
                 row(384), row(128), trk(128), row(128), row(128), row(512))
    return pl.pallas_call(
        _inproj_kernel,
        grid=(t // tm,),
        in_specs=[row(d),
                  pl.BlockSpec((None, 6, d), lambda i: (i // per_b, 0, 0)),
                  row(128), row(128), row(128), row(128),
                  full(w_ext), full(wkv), full(kvg)],
        out_specs=out_specs,
        out_shape=out_shape,
        compiler_params=_cparams(("parallel",)),
        name="inproj",
    )(xf, mod, *tabs, w_ext, wkv, kvg)


def _compress_kernel(h_ref, plo_ref, phi_ref, wlo_ref, whi_ref, w2_ref, kc_ref, vct_ref):
    h = h_ref[...]
    a = _dot(h + plo_ref[...], wlo_ref[...])
    b = _dot(h + phi_ref[...], whi_ref[...])
    nh = h.shape[0]
    pre = a + pltpu.roll(b, nh - 1, 0)
    cmp = _dot(_gelu(pre), w2_ref[...])
    kc_ref[...] = cmp[:, :128].astype(BF16)
    v_t = cmp[:, 128:].T
    ones_t = jnp.ones((HEAD_DIM, nh), F32)
    vct_ref[...] = jnp.concatenate([v_t[:HEAD_DIM], ones_t, v_t[HEAD_DIM:], ones_t], axis=0).astype(BF16)


def _compress(kvcmp_h, plo, phi, wlo, whi, w2):
    batch, nh, width = kvcmp_h.shape
    full = lambda a: pl.BlockSpec(a.shape, lambda b: (0,) * a.ndim)
    return pl.pallas_call(
        _compress_kernel,
        grid=(batch,),
        in_specs=[pl.BlockSpec((None, nh, width), lambda b: (b, 0, 0)),
                  full(plo), full(phi), full(wlo), full(whi), full(w2)],
        out_specs=(pl.BlockSpec((None, nh, 128), lambda b: (b, 0, 0)),
                   pl.BlockSpec((None, 256, nh), lambda b: (b, 0, 0))),
        out_shape=(jax.ShapeDtypeStruct((batch, nh, 128), BF16),
                   jax.ShapeDtypeStruct((batch, 256, nh), BF16)),
        compiler_params=_cparams(("parallel",)),
        name="nsa_compress",
    )(kvcmp_h, plo, phi, wlo, whi, w2)


def _softmax_pv(s, mask, vt_ones):
    sm = jnp.where(mask, s, NEG)
    m = jnp.max(sm, axis=0, keepdims=True)
    p = jnp.exp2(sm - m).astype(BF16)
    pv = jnp.dot(vt_ones, p, preferred_element_type=F32)
    dh = vt_ones.shape[0] // 2
    inv = jnp.where(m > 0.5 * NEG, 1.0 / jnp.maximum(pv[dh:dh + 1], 1e-30), 0.0)
    return p, inv, pv[:dh]


def _flash_step(s, on, vt_ones, m_ref, acc_ref):
    m_new, acc_new = _flash_update(s, on, vt_ones, m_ref[...], acc_ref[...])
    acc_ref[...] = acc_new
    m_ref[...] = m_new


def _masked_max(s, on, m_old):
    n_grp = s.shape[1] // Q_BLK
    sm = [jnp.where(on, s[:, j * Q_BLK:(j + 1) * Q_BLK], NEG) for j in range(n_grp)]
    m_new = jnp.maximum(m_old, jnp.concatenate([jnp.max(x, axis=0, keepdims=True) for x in sm], axis=1))
    return sm, m_new


def _exp2_weights(sm, m_new):
    return jnp.concatenate([jnp.exp2(x - m_new[:, j * Q_BLK:(j + 1) * Q_BLK]).astype(BF16)
                            for j, x in enumerate(sm)], axis=1)


def _flash_update(s, on, vt_ones, m_old, acc_old):
    sm, m_new = _masked_max(s, on, m_old)
    p = _exp2_weights(sm, m_new)
    acc_new = jnp.exp2(m_old - m_new) * acc_old + jnp.dot(vt_ones, p, preferred_element_type=F32)
    return m_new, acc_new


def _nsa_kernel(nq_ref, nqr_ref, small_ref, kc_ref, vct_ref, ksel_ref, vselT_ref, kwin_ref,
                vwinT_ref, ovt_ref, o_ref, sc_ref, lim_ref, oc_ref, m_ref, acc_ref, sa_ref, sb_ref,
                *, n_blk, n_sel, n_cmp):
    i = pl.program_id(1)
    t0 = i * Q_BLK
    tq = t0 + lax.broadcasted_iota(I32, (1, Q_BLK), 1)
    tq3 = jnp.concatenate([tq, tq, tq], axis=1)
    nq = nq_ref[...]
    nqr = nqr_ref[...]
    nh = kc_ref.shape[0]

    def stack_heads(q, g):
        hs = [g * NSA_GROUP + r for r in range(NSA_GROUP)]
        return jnp.concatenate([q[:, h * HEAD_DIM:(h + 1) * HEAD_DIM] for h in hs], axis=0)

    qrs = [stack_heads(nqr, g) for g in range(NSA_KV_HEADS)]
    for g in range(NSA_KV_HEADS):
        lo, hi = g * HEAD_DIM, (g + 1) * HEAD_DIM
        s_c = _dot_nt(kc_ref[:, lo:hi], stack_heads(nq, g))
        n_io = lax.broadcasted_iota(I32, (nh, 1), 0)
        m_c = (n_io * CMP_STRIDE + (CMP_LEN - 1) <= tq3) & (n_io < n_cmp)
        p_c, inv_c, oc = _softmax_pv(s_c, m_c, vct_ref[2 * lo:2 * hi, :])
        oc_ref[g] = oc * inv_c

        imp = jnp.zeros((n_blk, Q_BLK), F32)
        for r in range(NSA_GROUP):
            cs = slice(r * Q_BLK, (r + 1) * Q_BLK)
            imp = imp + jnp.dot(ovt_ref[...], p_c[:, cs], preferred_element_type=F32) * inv_c[:, cs]
        j_io = lax.broadcasted_iota(I32, (n_blk, 1), 0)
        cur = tq >> 6
        valid = j_io <= cur
        forced = (j_io == 0) | (j_io == cur) | (j_io == cur - 1)
        score = jnp.where(valid, imp + jnp.where(forced, FORCE_BONUS, 0.0), -jnp.inf)
        sc_ref[...] = score
        rank = jnp.zeros((n_blk, Q_BLK), F32)
        for b in range(n_blk):
            row = sc_ref[b:b + 1, :]
            beats = (row > score) | ((row == score) & (j_io > b))
            rank = rank + jnp.where(beats, 1.0, 0.0)
        lim_ref[g] = jnp.where((rank < n_sel) & valid, tq, -1)

    m_ref[...] = jnp.full(m_ref.shape, NEG, F32)
    acc_ref[...] = jnp.zeros(acc_ref.shape, F32)
    bpc = SEL_KC // SEL_BLOCK

    n_ch = (t0 + Q_BLK + SEL_KC - 1) // SEL_KC

    def scores(c, g):
        k0 = pl.multiple_of(c * SEL_KC, SEL_KC)
        return _dot_nt(ksel_ref[pl.ds(k0, SEL_KC), g * HEAD_DIM:(g + 1) * HEAD_DIM], qrs[g])

    def sel_step(c, carry):
        key = c * SEL_KC + lax.broadcasted_iota(I32, (SEL_KC, 1), 0)

        def flash(s, g):
            rows = [jnp.broadcast_to(lim_ref[g, pl.ds(c * bpc + u, 1), :], (SEL_BLOCK, Q_BLK))
                    for u in range(bpc)]
            on = key <= jnp.concatenate(rows, axis=0)
            _flash_step(s, on, vselT_ref[c, 2 * g * HEAD_DIM:2 * (g + 1) * HEAD_DIM, :],
                        m_ref.at[g], acc_ref.at[g])

        sb_ref[...] = scores(c, 1)
        flash(sa_ref[...], 0)
        sa_ref[...] = scores(jnp.minimum(c + 1, n_ch - 1), 0)
        flash(sb_ref[...], 1)
        return carry

    sa_ref[...] = scores(0, 0)
    lax.fori_loop(0, n_ch, sel_step, 0)

    small_t = small_ref[...].T
    heads = []
    for g in range(NSA_KV_HEADS):
        lo, hi = g * HEAD_DIM, (g + 1) * HEAD_DIM
        acc = acc_ref[g]
        o_s = acc[:HEAD_DIM] / jnp.maximum(acc[HEAD_DIM:HEAD_DIM + 1], 1e-30)
        o_c = oc_ref[g]

        nband = WINDOW // Q_BLK + 1
        cb = jnp.maximum(i - WINDOW // Q_BLK, 0)
        b0 = pl.multiple_of(cb * Q_BLK, Q_BLK)
        s_w = _dot_nt(kwin_ref[pl.ds(b0, nband * Q_BLK), lo:hi], qrs[g])
        key = b0 + lax.broadcasted_iota(I32, (nband * Q_BLK, 1), 0)
        diff = tq3 - key
        vt_w = jnp.concatenate([vwinT_ref[cb + u, 2 * lo:2 * hi, :] for u in range(nband)], axis=1)
        _, inv_w, o_w = _softmax_pv(s_w, (diff >= 0) & (diff < WINDOW), vt_w)
        o_w = o_w * inv_w

        for r in range(NSA_GROUP):
            gi = SMALL_GATE + (g * NSA_GROUP + r) * 3
            gt = _sigmoid(small_t[gi:gi + 3, :])
            cs = slice(r * Q_BLK, (r + 1) * Q_BLK)
            heads.append(gt[0:1, :] * o_c[:, cs] + gt[1:2, :] * o_s[:, cs] + gt[2:3, :] * o_w[:, cs])
    o_ref[...] = jnp.concatenate(heads, axis=0).T.astype(BF16)


def _nsa(nq, nqr, small, kc, vct, ksel, vselT, kwin, vwinT, ovt, batch, seq):
    n_q = seq // Q_BLK
    n_blk = seq // SEL_BLOCK
    n_cmp = (seq - CMP_LEN) // CMP_STRIDE + 1
    nh = kc.shape[1]
    blk = lambda n: pl.BlockSpec((Q_BLK, n), lambda b, i: (b * n_q + i, 0))
    perb2 = lambda r, c: pl.BlockSpec((None, r, c), lambda b, i: (b, 0, 0))
    perb_rows = pl.BlockSpec((seq, 128), lambda b, i: (b, 0))
    kern = functools.partial(_nsa_kernel, n_blk=n_blk, n_sel=min(SEL_TOP, n_blk), n_cmp=n_cmp)
    n_col = NSA_GROUP * Q_BLK
    return pl.pallas_call(
        kern,
        grid=(batch, n_q),
        in_specs=[blk(384), blk(384), blk(128),
                  perb2(nh, 128), perb2(256, nh),
                  perb_rows,
                  pl.BlockSpec((None, seq // SEL_KC, 256, SEL_KC), lambda b, i: (b, 0, 0, 0)),
                  perb_rows,
                  pl.BlockSpec((None, seq // Q_BLK, 256, Q_BLK), lambda b, i: (b, 0, 0, 0)),
                  pl.BlockSpec(ovt.shape, lambda b, i: (0, 0))],
        out_specs=blk(384),
        out_shape=jax.ShapeDtypeStruct((batch * seq, 384), BF16),
        scratch_shapes=[pltpu.VMEM((n_blk, Q_BLK), F32),
                        pltpu.VMEM((NSA_KV_HEADS, n_blk, Q_BLK), I32),
                        pltpu.VMEM((NSA_KV_HEADS, HEAD_DIM, n_col), F32),
                        pltpu.VMEM((NSA_KV_HEADS, 1, n_col), F32),
                        pltpu.VMEM((NSA_KV_HEADS, 2 * HEAD_DIM, n_col), F32),
                        pltpu.VMEM((SEL_KC, n_col), F32), pltpu.VMEM((SEL_KC, n_col), F32)],
        compiler_params=_cparams(("parallel", "arbitrary")),
        name="nsa_attn",
    )(nq, nqr, small, kc, vct, ksel, vselT, kwin, vwinT, ovt)


PLANE_KEYS = 32 * 8


def _bit_transpose32(rows):
    rows = list(rows)
    j, mask = 16, 0x0000FFFF
    while j:
        m32 = np.array(mask, np.uint32).view(np.int32)
        k = 0
        while k < 32:
            t = (rows[k] ^ lax.shift_right_logical(rows[k + j], np.int32(j))) & m32
            rows[k] = rows[k] ^ t
            rows[k + j] = rows[k + j] ^ (t << j)
            k = (k + j + 1) & ~j
        j >>= 1
        mask = (mask ^ (mask << j)) & 0xFFFFFFFF
    return rows

def _dsa_kernel(dq_ref, iq_ref, small_ref, dk_ref, dvT_ref, ltri_ref, o_ref,
                ord_ref, plane_ref, m_ref, acc_ref, sa_ref, sb_ref, *, k_top):
    i = pl.program_id(1)
    t0 = i * Q_BLK
    kc = SEL_KC
    n_ch = (t0 + Q_BLK + kc - 1) // kc
    tq = t0 + lax.broadcasted_iota(I32, (1, Q_BLK), 1)

    small_t = small_ref[pl.ds(pl.multiple_of(t0, Q_BLK), Q_BLK), :].T
    w_rows = [small_t[SMALL_IW + h:SMALL_IW + h + 1, :] * (IDX_HEADS ** -0.5) for h in range(IDX_HEADS)]
    iq = iq_ref[...]
    iqs = jnp.concatenate([iq[:, h * IDX_DIM:(h + 1) * IDX_DIM] for h in range(IDX_HEADS)], axis=0)

    def score_step(c, carry):
        k0 = pl.multiple_of(c * kc, kc)
        ik = small_ref[pl.ds(k0, kc), :][:, 0:IDX_DIM]
        lg = _dot_nt(ik, iqs)
        sc = jnp.zeros((kc, Q_BLK), F32)
        for h in range(IDX_HEADS):
            sc = sc + w_rows[h] * jnp.maximum(lg[:, h * Q_BLK:(h + 1) * Q_BLK], 0.0)
        sc = jnp.where(sc == 0.0, 0.0, sc)
        bits = lax.bitcast_convert_type(sc, I32)
        ordv = bits ^ ((bits >> 31) & 0x7FFFFFFF)
        key = k0 + lax.broadcasted_iota(I32, (kc, 1), 0)
        ordv = jnp.where(key <= tq, ordv, INT_MIN)
        ord_ref[pl.ds(k0, kc), :] = ordv
        u = ordv ^ INT_MIN
        for g in range(kc // PLANE_KEYS):
            rows = [u[g * PLANE_KEYS + r * 8:g * PLANE_KEYS + (r + 1) * 8, :] for r in range(32)]
            cols = _bit_transpose32(rows)
            w0 = pl.multiple_of((c * (kc // PLANE_KEYS) + g) * 8, 8)
            for b in range(32):
                plane_ref[b, pl.ds(w0, 8), :] = cols[31 - b]
        return carry

    lax.fori_loop(0, n_ch, score_step, 0)

    def clear_step(c, carry):
        w0 = pl.multiple_of(c * (kc // PLANE_KEYS) * 8, 8)
        for b in range(32):
            plane_ref[b, pl.ds(w0, (kc // PLANE_KEYS) * 8), :] = jnp.zeros(((kc // PLANE_KEYS) * 8, Q_BLK), I32)
        return carry

    lax.fori_loop(n_ch, plane_ref.shape[1] * 32 // kc, clear_step, 0)

    n_words = plane_ref.shape[1] // 8
    alive = [jnp.broadcast_to(jnp.where(w < n_ch * (kc // PLANE_KEYS), -1, 0).astype(I32), (8, Q_BLK))
             for w in range(n_words)]
    k_rem = jnp.full((1, Q_BLK), k_top, I32)
    prefix = jnp.zeros((1, Q_BLK), I32)

    def popcount_rows(words):
        pcs = [lax.population_count(x) for x in words]
        while len(pcs) > 1:
            pcs = [a + b for a, b in zip(pcs[0::2], pcs[1::2])]
        return jnp.sum(pcs[0], axis=0, keepdims=True)

    for bit in range(31, -1, -1):
        ones = [alive[w] & plane_ref[bit, w * 8:(w + 1) * 8, :] for w in range(n_words)]
        cnt = popcount_rows(ones)
        take = cnt >= k_rem
        keep0 = jnp.broadcast_to(jnp.where(take, 0, -1).astype(I32), (8, Q_BLK))
        alive = [ones[w] ^ (alive[w] & keep0) for w in range(n_words)]
        k_rem = jnp.where(take, k_rem, k_rem - cnt)
        prefix = prefix | jnp.where(take, np.int32(INT_MIN) if bit == 31 else np.int32(1 << bit), 0)
    thr = prefix ^ np.int32(INT_MIN)
    n_eq = popcount_rows(alive)
    short = thr == INT_MIN
    need = jnp.where(short, 0, k_rem)
    thr_all = jnp.where(short, INT_MIN + 1, thr)
    no_cut = jnp.min(jnp.where(short | (n_eq == need), 1.0, 0.0)) > 0.5

    q = dq_ref[...]
    qs = jnp.concatenate([q[:, h * HEAD_DIM:(h + 1) * HEAD_DIM] for h in range(DSA_HEADS)], axis=0)
    m_ref[...] = jnp.full(m_ref.shape, NEG, F32)
    acc_ref[...] = jnp.zeros(acc_ref.shape, F32)

    def sweep(mask_fn):
        def attn_step(c, seen):
            k0 = pl.multiple_of(c * kc, kc)
            on, seen = mask_fn(ord_ref[pl.ds(k0, kc), :], seen)
            vt = dvT_ref[c]
            sb_ref[...] = scores(c, 1)
            _flash_step(sa_ref[...], on[:half], vt[:, :half], m_ref, acc_ref)
            sa_ref[...] = scores(jnp.minimum(c + 1, n_ch - 1), 0)
            _flash_step(sb_ref[...], on[half:], vt[:, half:], m_ref, acc_ref)
            return seen
        sa_ref[...] = scores(0, 0)
        lax.fori_loop(0, n_ch, attn_step, jnp.zeros((1, Q_BLK), F32))

    half = kc // 2

    def scores(c, sub):
        k0 = pl.multiple_of(c * kc + sub * half, half)
        return _dot_nt(dk_ref[pl.ds(k0, half), 0:HEAD_DIM], qs)

    @pl.when(no_cut)
    def _():
        sweep(lambda o, seen: (o >= thr_all, seen))

    @pl.when(jnp.logical_not(no_cut))
    def _():
        need_f = need.astype(F32)

        def cut_mask(o, seen):
            eq = jnp.where(o == thr, 1.0, 0.0)
            before = jnp.dot(ltri_ref[...], eq.astype(BF16), preferred_element_type=F32) + seen
            on = (o > thr) | ((eq > 0.5) & (before < need_f))
            return on, seen + jnp.sum(eq, axis=0, keepdims=True)

        sweep(cut_mask)

    acc = acc_ref[...]
    o_t = acc[:HEAD_DIM] / jnp.maximum(acc[HEAD_DIM:HEAD_DIM + 1], 1e-30)
    heads = [o_t[:, h * Q_BLK:(h + 1) * Q_BLK] for h in range(DSA_HEADS)]
    o_ref[...] = jnp.concatenate(heads, axis=0).T.astype(BF16)


def _dsa(dq, iq, small, dk, dvT, ltri, batch, seq):
    n_q = seq // Q_BLK
    blk = lambda n: pl.BlockSpec((Q_BLK, n), lambda b, i: (b * n_q + i, 0))
    perb_rows = pl.BlockSpec((seq, 128), lambda b, i: (b, 0))
    kern = functools.partial(_dsa_kernel, k_top=min(IDX_TOPK_MAX, seq // 4))
    return pl.pallas_call(
        kern,
        grid=(batch, n_q),
        in_specs=[blk(384), blk(128), perb_rows, perb_rows,
                  pl.BlockSpec((None, seq // SEL_KC, 128, SEL_KC), lambda b, i: (b, 0, 0, 0)),
                  pl.BlockSpec(ltri.shape, lambda b, i: (0, 0))],
        out_specs=blk(384),
        out_shape=jax.ShapeDtypeStruct((batch * seq, 384), BF16),
        scratch_shapes=[pltpu.VMEM((seq, Q_BLK), I32),
                        pltpu.VMEM((32, seq // 32, Q_BLK), I32),
                        pltpu.VMEM((1, DSA_HEADS * Q_BLK), F32),
                        pltpu.VMEM((2 * HEAD_DIM, DSA_HEADS * Q_BLK), F32),
                        pltpu.VMEM((SEL_KC // 2, DSA_HEADS * Q_BLK), F32),
                        pltpu.VMEM((SEL_KC // 2, DSA_HEADS * Q_BLK), F32)],
        compiler_params=_cparams(("parallel", "arbitrary")),
        name="dsa_attn",
    )(dq, iq, small, dk, dvT, ltri)


def _sgu_kernel(z_ref, g_ref, b_ref, w_ref, bs_ref, o_ref):
    row = lax.broadcasted_iota(I32, (SGU_CHUNK, SGU_CHUNK), 0)
    col = lax.broadcasted_iota(I32, (SGU_CHUNK, SGU_CHUNK), 1)
    ws = [jnp.where(row >= col, w_ref[g], 0.0).astype(BF16) for g in range(SGU_GROUPS)]
    for c in range(z_ref.shape[0] // SGU_CHUNK):
        rows = slice(c * SGU_CHUNK, (c + 1) * SGU_CHUNK)
        z = _gelu(z_ref[rows, :])
        outs = []
        for g in range(SGU_GROUPS):
            lo, hi = g * HEAD_DIM, (g + 1) * HEAD_DIM
            u = z[:, lo:hi]
            v = _standardize(z[:, SGU_WIDTH + lo:SGU_WIDTH + hi]) * g_ref[:, lo:hi] + b_ref[:, lo:hi]
            outs.append(u * (_dot(ws[g], v) + bs_ref[:, lo:hi]))
        o_ref[rows, :] = jnp.concatenate(outs, axis=1).astype(BF16)


def _sgu(z, g, b, w, bs):
    t = z.shape[0]
    tm = SGU_CHUNK
    full = lambda a: pl.BlockSpec(a.shape, lambda i: (0,) * a.ndim)
    return pl.pallas_call(
        _sgu_kernel,
        grid=(t // tm,),
        in_specs=[pl.BlockSpec((tm, 2 * SGU_WIDTH), lambda i: (i, 0)),
                  full(g), full(b), full(w), full(bs)],
        out_specs=pl.BlockSpec((tm, SGU_WIDTH), lambda i: (i, 0)),
        out_shape=jax.ShapeDtypeStruct((t, SGU_WIDTH), BF16),
        compiler_params=_cparams(("parallel",)),
        name="sgu",
    )(z, g, b, w, bs)


def _outproj_kernel(oa_ref, ob_ref, oc_ref, x_ref, mod_ref, w_ref, g_ref, b_ref, o_ref, *, alpha):
    mix = (jnp.dot(oa_ref[...], w_ref[0:384, :], preferred_element_type=F32)
           + jnp.dot(ob_ref[...], w_ref[384:768, :], preferred_element_type=F32)
           + jnp.dot(oc_ref[...], w_ref[768:1024, :], preferred_element_type=F32))
    y = alpha * x_ref[...] + mod_ref[2:3, :] * mix
    o_ref[...] = _standardize(y) * g_ref[...] + b_ref[...]


def _outproj(oa, ob, oc, xf, mod, w, g, b, seq, alpha):
    t, d = xf.shape
    tm = 512
    per_b = seq // tm
    row = lambda n: pl.BlockSpec((tm, n), lambda i: (i, 0))
    full = lambda a: pl.BlockSpec(a.shape, lambda i: (0,) * a.ndim)
    return pl.pallas_call(
        functools.partial(_outproj_kernel, alpha=alpha),
        grid=(t // tm,),
        in_specs=[row(384), row(384), row(256), row(d),
                  pl.BlockSpec((None, 6, d), lambda i: (i // per_b, 0, 0)),
                  full(w), full(g), full(b)],
        out_specs=row(d),
        out_shape=jax.ShapeDtypeStruct((t, d), F32),
        compiler_params=_cparams(("parallel",)),
        name="outproj",
    )(oa, ob, oc, xf, mod, w, g, b)


def _ffn_kernel(x_ref, mod_ref, wg_ref, wu_ref, wd_ref, g_ref, b_ref, o_ref, acc_ref, *, alpha, fc):
    x = x_ref[...]
    hb = (_standardize(x) * (1.0 + mod_ref[4:5, :]) + mod_ref[3:4, :]).astype(BF16)
    d_ff = wg_ref.shape[1]
    for j in range(d_ff // fc):
        cs = slice(j * fc, (j + 1) * fc)
        a = (_silu(jnp.dot(hb, wg_ref[:, cs], preferred_element_type=F32))
             * jnp.dot(hb, wu_ref[:, cs], preferred_element_type=F32)).astype(BF16)
        part = jnp.dot(a, wd_ref[cs, :], preferred_element_type=F32)
        if j == 0:
            acc_ref[...] = part
        else:
            acc_ref[...] += part
    y = alpha * x + mod_ref[5:6, :] * acc_ref[...]
    o_ref[...] = _standardize(y) * g_ref[...] + b_ref[...]


def _ffn(xf, mod, wg, wu, wd, g, b, seq, alpha):
    t, d = xf.shape
    tm = 1024
    per_b = seq // tm
    row = pl.BlockSpec((tm, d), lambda i: (i, 0))
    once = lambda a: pl.BlockSpec(a.shape, lambda i: (0,) * a.ndim, pipeline_mode=pl.Buffered(1))
    return pl.pallas_call(
        functools.partial(_ffn_kernel, alpha=alpha, fc=256),
        grid=(t // tm,),
        in_specs=[row, pl.BlockSpec((None, 6, d), lambda i: (i // per_b, 0, 0)),
                  once(wg), once(wu), once(wd), once(g), once(b)],
        out_specs=row,
        out_shape=jax.ShapeDtypeStruct((t, d), F32),
        scratch_shapes=[pltpu.VMEM((tm, d), F32)],
        compiler_params=_cparams(("parallel",)),
        name="ffn",
    )(xf, mod, wg, wu, wd, g, b)


MOE_TM = 1024
MOE_SUB = 32
MOE_BM = 1024
MOE_FC = 896


def _route_kernel(x_ref, mod_ref, wr_ref, br_ref, utri_ref, h_ref, gate_ref, pos_ref, post_ref, cnt_ref):
    h = _standardize(x_ref[...]) * (1.0 + mod_ref[4:5, :]) + mod_ref[3:4, :]
    h_ref[...] = h.astype(BF16)
    lane = lax.broadcasted_iota(I32, (1, LANES), 1)
    logits = jnp.dot(h, wr_ref[...], preferred_element_type=F32,
                     precision=lax.Precision.HIGHEST) + br_ref[...]
    lg = jnp.where(lane < N_EXPERTS, logits, -jnp.inf)
    v0 = jnp.max(lg, axis=-1, keepdims=True)
    lane_f = lane.astype(F32)
    i0 = jnp.min(jnp.where(lg == v0, lane_f, float(LANES)), axis=-1, keepdims=True)
    lg1 = jnp.where(lane_f == i0, -jnp.inf, lg)
    v1 = jnp.max(lg1, axis=-1, keepdims=True)
    i1 = jnp.min(jnp.where(lg1 == v1, lane_f, float(LANES)), axis=-1, keepdims=True)
    e1 = jnp.exp(v1 - v0)
    den = 1.0 + e1
    gate_ref[...] = jnp.where(lane_f == i0, 1.0 / den, 0.0) + jnp.where(lane_f == i1, e1 / den, 0.0)
    sel_t = jnp.where((lane_f == i0) | (lane_f == i1), 1.0, 0.0).T
    rank_t = jnp.dot(sel_t.astype(BF16), utri_ref[...], preferred_element_type=F32)
    pos_t = jnp.where(sel_t > 0.5, rank_t, -1.0)
    post_ref[...] = pos_t[0:N_EXPERTS]
    pos_ref[...] = pos_t.T
    cnt_ref[...] = jnp.broadcast_to(jnp.sum(sel_t[0:N_EXPERTS], axis=1, keepdims=True), (N_EXPERTS, LANES))


def _route(xf, mod, wr, br, utri, seq):
    t, d = xf.shape
    tm = MOE_TM
    nt = t // tm
    per_b = seq // tm
    full = lambda a: pl.BlockSpec(a.shape, lambda i: (0,) * a.ndim)
    return pl.pallas_call(
        _route_kernel,
        grid=(nt,),
        in_specs=[pl.BlockSpec((tm, d), lambda i: (i, 0)),
                  pl.BlockSpec((None, 6, d), lambda i: (i // per_b, 0, 0)),
                  full(wr), full(br), full(utri)],
        out_specs=(pl.BlockSpec((tm, d), lambda i: (i, 0)),
                   pl.BlockSpec((tm, LANES), lambda i: (i, 0)),
                   pl.BlockSpec((tm, LANES), lambda i: (i, 0)),
                   pl.BlockSpec((None, N_EXPERTS, tm), lambda i: (i, 0, 0)),
                   pl.BlockSpec((None, N_EXPERTS, LANES), lambda i: (i, 0, 0))),
        out_shape=(jax.ShapeDtypeStruct((t, d), BF16),
                   jax.ShapeDtypeStruct((t, LANES), F32),
                   jax.ShapeDtypeStruct((t, LANES), F32),
                   jax.ShapeDtypeStruct((nt, N_EXPERTS, tm), F32),
                   jax.ShapeDtypeStruct((nt, N_EXPERTS, LANES), F32)),
        compiler_params=_cparams(("parallel",)),
        name="moe_route",
    )(xf, mod, wr, br, utri)


def _moe_schedule(cnt, n_blocks, k_max):
    nt, ne = cnt.shape
    per = MOE_BM // MOE_SUB
    nb = (cnt + MOE_SUB - 1) // MOE_SUB
    nbt = nb.T
    tot = jnp.sum(nbt, axis=1)
    reg = (tot + per - 1) // per * per
    reg_end = jnp.cumsum(reg)
    reg_start = reg_end - reg
    seg_end = jnp.cumsum(nbt, axis=1)
    seg_start = reg_start[:, None] + seg_end - nbt
    j = jnp.arange(n_blocks, dtype=I32)
    e_j = jnp.minimum(jnp.sum(reg_end[None, :] <= j[:, None], axis=1), ne - 1).astype(I32)
    valid_j = (j - reg_start[e_j]) < tot[e_j]
    step_e = e_j[::per]
    step_valid = valid_j[::per].astype(I32)
    cum = jnp.cumsum(nb, axis=1)
    n_tile = cum[:, -1]
    k = jnp.minimum(jnp.arange(k_max, dtype=I32)[None, :], n_tile[:, None] - 1)
    e_k = jnp.sum(cum[:, None, :] <= k[:, :, None], axis=2).astype(I32)
    tile = jnp.arange(nt, dtype=I32)[:, None]
    s_k = k - (jnp.take_along_axis(cum, e_k, axis=1) - jnp.take_along_axis(nb, e_k, axis=1))
    blk_k = seg_start[e_k, tile] + s_k
    valid_k = (jnp.arange(k_max, dtype=I32)[None, :] < n_tile[:, None]).astype(I32)
    slot = jnp.arange(nt * k_max, dtype=I32)
    src = jnp.zeros((n_blocks,), I32).at[jnp.where(valid_k > 0, blk_k, n_blocks).reshape(-1)].set(slot, mode='drop')
    return (step_e, step_valid, src), (blk_k.astype(I32), e_k, s_k.astype(I32), valid_k)


def _dispatch_kernel(e_ref, s_ref, h_ref, post_ref, gate_ref, xs_ref, gs_ref, *, k_half):
    i = pl.program_id(0)
    kh = pl.program_id(1)
    lane = lax.broadcasted_iota(I32, (1, LANES), 1)
    sub = lax.broadcasted_iota(I32, (MOE_SUB, 1), 0)
    hots = []
    for u in range(k_half):
        k = kh * k_half + u
        row = post_ref[pl.ds(e_ref[i, k], 1), :]
        want = (s_ref[i, k] * MOE_SUB + sub).astype(F32)
        hots.append(jnp.where(row == want, 1.0, 0.0).astype(BF16))
    onehot = jnp.concatenate(hots, axis=0)
    xs_ref[...] = jnp.dot(onehot, h_ref[...], preferred_element_type=F32).astype(BF16)
    gates = gate_ref[...]
    g_hi = gates.astype(BF16)
    g_lo = (gates - g_hi.astype(F32)).astype(BF16)
    gs = (jnp.dot(onehot, g_hi, preferred_element_type=F32)
          + jnp.dot(onehot, g_lo, preferred_element_type=F32))
    for u in range(k_half):
        rows = slice(u * MOE_SUB, (u + 1) * MOE_SUB)
        mine = lane == e_ref[i, kh * k_half + u]
        gcol = jnp.sum(jnp.where(mine, gs[rows], 0.0), axis=-1, keepdims=True)
        gs_ref[rows, :] = jnp.broadcast_to(gcol, (MOE_SUB, LANES))


def _dispatch(h, post, gates, sched):
    t, d = h.shape
    tm = MOE_TM
    k_max = sched[0].shape[1]
    k_half = k_max // 2
    rows = (t // tm) * k_max * MOE_SUB
    grid_spec = pltpu.PrefetchScalarGridSpec(
        num_scalar_prefetch=2,
        grid=(t // tm, 2),
        in_specs=[pl.BlockSpec((tm, d), lambda i, kh, *_: (i, 0)),
                  pl.BlockSpec((None, N_EXPERTS, tm), lambda i, kh, *_: (i, 0, 0)),
                  pl.BlockSpec((tm, LANES), lambda i, kh, *_: (i, 0))],
        out_specs=(pl.BlockSpec((k_half * MOE_SUB, d), lambda i, kh, *_: (2 * i + kh, 0)),
                   pl.BlockSpec((k_half * MOE_SUB, LANES), lambda i, kh, *_: (2 * i + kh, 0))),
    )
    return pl.pallas_call(
        functools.partial(_dispatch_kernel, k_half=k_half),
        grid_spec=grid_spec,
        out_shape=(jax.ShapeDtypeStruct((rows, d), BF16), jax.ShapeDtypeStruct((rows, LANES), F32)),
        compiler_params=_cparams(("arbitrary", "arbitrary")),
        name="moe_dispatch",
    )(*sched, h, post, gates)


def _experts_kernel(e_ref, v_ref, src_ref, *refs, per):
    xs_refs, gs_refs = refs[:per], refs[per:2 * per]
    wg_ref, wu_ref, wd_ref, ys_ref, acc_ref, xcat_ref = refs[2 * per:]
    jb = pl.program_id(0)
    f = pl.program_id(1)
    last = pl.num_programs(1) - 1

    @pl.when((v_ref[jb] > 0) & (f == 0))
    def _():
        for u in range(per):
            xcat_ref[u * MOE_SUB:(u + 1) * MOE_SUB, :] = xs_refs[u][...]

    @pl.when(v_ref[jb] > 0)
    def _():
        xb = xcat_ref[...]
        a = (_silu(jnp.dot(xb, wg_ref[...].astype(BF16), preferred_element_type=F32))
             * jnp.dot(xb, wu_ref[...].astype(BF16), preferred_element_type=F32)).astype(BF16)
        part = jnp.dot(a, wd_ref[...].astype(BF16), preferred_element_type=F32)

        @pl.when(f == 0)
        def _():
            acc_ref[...] = part

        @pl.when(f > 0)
        def _():
            acc_ref[...] += part

        @pl.when(f == last)
        def _():
            for u in range(per):
                rows = slice(u * MOE_SUB, (u + 1) * MOE_SUB)
                gate = jnp.concatenate([gs_refs[u][...]] * (ys_ref.shape[1] // LANES), axis=1)
                ys_ref[rows, :] = (acc_ref[rows, :] * gate).astype(BF16)

    @pl.when((v_ref[jb] == 0) & (f == last))
    def _():
        ys_ref[...] = jnp.zeros(ys_ref.shape, BF16)


def _experts(xs, gs, wg, wu, wd, sched, n_blocks):
    d = xs.shape[1]
    n_e, _, d_ff = wg.shape
    fc = MOE_FC
    n_f = d_ff // fc
    per = MOE_BM // MOE_SUB
    fsel = lambda jb, f, e, v: jnp.where(v[jb] > 0, f, n_f - 1)

    def src_index(u, jb, f, e, v, src):
        return (src[jb * per + u], 0)

    grid_spec = pltpu.PrefetchScalarGridSpec(
        num_scalar_prefetch=3,
        grid=(n_blocks // per, n_f),
        in_specs=([pl.BlockSpec((MOE_SUB, d), functools.partial(src_index, u)) for u in range(per)]
                  + [pl.BlockSpec((MOE_SUB, LANES), functools.partial(src_index, u)) for u in range(per)]
                  + [pl.BlockSpec((None, d, fc), lambda jb, f, e, v, src: (e[jb], 0, fsel(jb, f, e, v))),
                     pl.BlockSpec((None, d, fc), lambda jb, f, e, v, src: (e[jb], 0, fsel(jb, f, e, v))),
                     pl.BlockSpec((None, fc, d), lambda jb, f, e, v, src: (e[jb], fsel(jb, f, e, v), 0))]),
        out_specs=pl.BlockSpec((MOE_BM, d), lambda jb, f, e, v, src: (jb, 0)),
        scratch_shapes=[pltpu.VMEM((MOE_BM, d), F32), pltpu.VMEM((MOE_BM, d), BF16)],
    )
    return pl.pallas_call(
        functools.partial(_experts_kernel, per=per),
        grid_spec=grid_spec,
        out_shape=jax.ShapeDtypeStruct((n_blocks * MOE_SUB, d), BF16),
        compiler_params=_cparams(("arbitrary", "arbitrary")),
        name="moe_experts",
    )(*sched, *([xs] * per), *([gs] * per), wg, wu, wd)


def _combine_kernel(blk_ref, e_ref, s_ref, v_ref, *refs, alpha, k_max):
    ys_refs = refs[:k_max]
    pos_ref, x_ref, mod_ref, g_ref, b_ref, o_ref, pc_ref = refs[k_max:]
    i = pl.program_id(0)
    th = pos_ref.shape[0]
    lane = lax.broadcasted_iota(I32, (1, LANES), 1)
    pos = pos_ref[...]
    for e in range(N_EXPERTS):
        col = jnp.sum(jnp.where(lane == e, pos, 0.0), axis=-1, keepdims=True)
        pc_ref[e] = jnp.broadcast_to(col, (th, LANES))
    hots = []
    per_lane = LANES // MOE_SUB
    for k0 in range(0, k_max, per_lane):
        want, pc = None, None
        for j in range(per_lane):
            k = k0 + j
            s = jnp.where(v_ref[i, k] > 0, s_ref[i, k], -2)
            w_j = (s * MOE_SUB + lane - j * MOE_SUB).astype(F32)
            p_j = pc_ref[e_ref[i, k]]
            here = lane >= j * MOE_SUB
            want = w_j if want is None else jnp.where(here, w_j, want)
            pc = p_j if pc is None else jnp.where(here, p_j, pc)
        hots.append(jnp.where(pc == want, 1.0, 0.0).astype(BF16))
    f = jnp.dot(jnp.concatenate(hots, axis=1), jnp.concatenate([r[...] for r in ys_refs], axis=0),
                preferred_element_type=F32)
    y = alpha * x_ref[...] + mod_ref[5:6, :] * f
    o_ref[...] = _standardize(y) * g_ref[...] + b_ref[...]


def _combine(ys, pos, xf, mod, g, b, sched, seq, alpha):
    t, d = xf.shape
    th = MOE_TM // 2
    per_b = seq // th
    k_max = sched[0].shape[1]
    row = lambda n: pl.BlockSpec((th, n), lambda i, hh, *_: (2 * i + hh, 0))
    full = lambda a: pl.BlockSpec(a.shape, lambda i, hh, *_: (0,) * a.ndim)

    def ys_index(k, i, hh, blk, e, s, v):
        return (blk[i, k], 0)

    grid_spec = pltpu.PrefetchScalarGridSpec(
        num_scalar_prefetch=4,
        grid=(t // MOE_TM, 2),
        in_specs=([pl.BlockSpec((MOE_SUB, d), functools.partial(ys_index, k)) for k in range(k_max)]
                  + [row(LANES), row(d),
                     pl.BlockSpec((None, 6, d), lambda i, hh, *_: ((2 * i + hh) // per_b, 0, 0)),
                     full(g), full(b)]),
        out_specs=row(d),
        scratch_shapes=[pltpu.VMEM((N_EXPERTS, th, LANES), F32)],
    )
    return pl.pallas_call(
        functools.partial(_combine_kernel, alpha=alpha, k_max=k_max),
        grid_spec=grid_spec,
        out_shape=jax.ShapeDtypeStruct((t, d), F32),
        compiler_params=_cparams(("arbitrary", "arbitrary")),
        name="moe_combine",
    )(*sched, *([ys] * k_max), pos, xf, mod, g, b)


def _moe(xf, mod, wr, br, wg, wu, wd, g, b, seq, alpha):
    t, d = xf.shape
    nt = t // MOE_TM
    per = MOE_BM // MOE_SUB
    n_blocks = TOP_K * t // MOE_SUB + nt * N_EXPERTS + N_EXPERTS * (per - 1)
    n_blocks = (n_blocks + per - 1) // per * per
    k_max = TOP_K * MOE_TM // MOE_SUB + N_EXPERTS
    utri = jnp.asarray(np.triu(np.ones((MOE_TM, MOE_TM), np.float32), 1), BF16)
    h, gates, pos, post, cnt = _route(xf, mod, wr, br, utri, seq)
    sched_x, sched_c = _moe_schedule(cnt[:, :, 0].astype(I32), n_blocks, k_max)
    xs, gs = _dispatch(h, post, gates, sched_c[1:3])
    ys = _experts(xs, gs, wg, wu, wd, sched_x, n_blocks)
    return _combine(ys, pos, xf, mod, g, b, sched_c, seq, alpha)


def _inproj_columns():
    sizes = (NSA_WIDTH, 6 * NSA_KV_HEADS * HEAD_DIM, 3 * NSA_HEADS, DSA_WIDTH, DSA_LATENT,
             IDX_HEADS * IDX_DIM, IDX_DIM, IDX_HEADS, 2 * SGU_WIDTH)
    starts = np.concatenate([[0], np.cumsum(sizes)])
    o_nq, o_kv, o_g, o_dq, o_ckv, o_iq, o_ik, o_iw, o_sgu = starts[:-1]
    perm = np.full((C_TOTAL,), -1, np.int64)
    perm[C_NQ:C_NQ + 384] = o_nq + np.arange(384)
    perm[C_KV:C_KV + 768] = o_kv + np.arange(768)
    perm[C_DQ:C_DQ + 384] = o_dq + np.arange(384)
    perm[C_CKV:C_CKV + 128] = o_ckv + np.arange(128)
    perm[C_IQ:C_IQ + 128] = o_iq + np.arange(128)
    perm[C_SGU:C_SGU + 512] = o_sgu + np.arange(512)
    perm[C_SMALL:C_SMALL + IDX_DIM] = o_ik + np.arange(IDX_DIM)
    perm[C_SMALL + SMALL_IW:C_SMALL + SMALL_IW + IDX_HEADS] = o_iw + np.arange(IDX_HEADS)
    perm[C_SMALL + SMALL_GATE:C_SMALL + SMALL_GATE + 3 * NSA_HEADS] = o_g + np.arange(3 * NSA_HEADS)
    return perm, int(starts[-1])


def _compress_weights(pos, w1, w2):
    half = CMP_LEN // 2
    eye_g = jnp.eye(NSA_KV_HEADS, dtype=F32)
    eye_j = jnp.eye(2, dtype=F32)

    def big(w1_half):
        w = jnp.einsum('jlde,jk,gh->ljgdkhe', w1_half, eye_j, eye_g)
        return w.reshape(half * 2 * NSA_KV_HEADS * HEAD_DIM, 2 * NSA_KV_HEADS * HEAD_DIM)

    def posrow(p_half):
        p = jnp.broadcast_to(p_half.transpose(1, 0, 2)[:, :, None, :], (half, 2, NSA_KV_HEADS, HEAD_DIM))
        return p.reshape(1, -1)

    w2big = jnp.einsum('jef,jk,gh->jgekhf', w2, eye_j, eye_g).reshape(256, 256)
    return (posrow(pos[:, :half]), posrow(pos[:, half:]),
            big(w1[:, :half]).astype(BF16), big(w1[:, half:]).astype(BF16), w2big.astype(BF16))


def _rope_tables(positions):
    pos = positions.astype(F32).reshape(-1, 1)

    def tab(dim):
        inv = ROPE_THETA ** (-jnp.arange(0, dim, 2, dtype=F32) / dim)
        lane = np.arange(LANES)
        ang = pos * inv[lane % (dim // 2)][None, :]
        sign = np.where(lane % dim < dim // 2, -1.0, 1.0).astype(np.float32)
        return jnp.cos(ang), jnp.sin(ang) * sign

    cos_h, sin_h = tab(HEAD_DIM)
    cos_i, sin_i = tab(IDX_DIM)
    return cos_h, sin_h, cos_i, sin_i


def kernel(x, c, positions, w_ada, b_ada, w_in, nsa_cmp_pos, nsa_cmp_w1, nsa_cmp_w2, dsa_kv_norm, dsa_w_uk, dsa_w_uv, sgu_norm_g, sgu_norm_b, sgu_w, sgu_b, w_out, ln1_g, ln1_b, ln2_g, ln2_b, ffn_w_gate, ffn_w_up, ffn_w_down, moe_w_router, moe_b_router, moe_w_gate, moe_w_up, moe_w_down):
    batch, seq, d = x.shape
    depth = w_ada.shape[0]
    t = batch * seq
    alpha = (2 * depth) ** 0.25
    assert seq % MOE_TM == 0 and seq >= WINDOW + Q_BLK

    tabs = _rope_tables(positions)
    mod_all = _adaln(c, w_ada, b_ada).reshape(depth, batch, 6, d)
    perm, in_width = _inproj_columns()
    perm = jnp.asarray(np.where(perm < 0, in_width, perm), I32)

    n_blk = seq // SEL_BLOCK
    n_half = seq // CMP_STRIDE
    cmp_start = np.arange(n_half)[None, :] * CMP_STRIDE
    blk_start = np.arange(n_blk)[:, None] * SEL_BLOCK
    ovt = jnp.asarray((cmp_start < blk_start + SEL_BLOCK) & (cmp_start + CMP_LEN > blk_start), BF16)
    ltri = jnp.asarray(np.tril(np.ones((SEL_KC, SEL_KC), np.float32), -1), BF16)

    xf = x.reshape(t, d)
    for layer in range(depth):
        mod = mod_all[layer]
        w_pad = jnp.concatenate([w_in[layer], jnp.zeros((d, 1), F32)], axis=1)
        w_ext = jnp.take(w_pad, perm, axis=1).astype(BF16)
        wkv = jnp.concatenate([dsa_w_uk[layer], dsa_w_uv[layer]], axis=1).astype(BF16)
        (nq, nqr, kvcmp, ksel, vselT, kwin, vwinT, dq, dk, dvT, iq, small, sgu_z) = _inproj(
            xf, mod, tabs, w_ext, wkv, dsa_kv_norm[layer].reshape(1, -1), batch, seq)

        plo, phi, wlo, whi, w2big = _compress_weights(nsa_cmp_pos[layer], nsa_cmp_w1[layer], nsa_cmp_w2[layer])
        kc, vct = _compress(kvcmp.reshape(batch, n_half, CMP_STRIDE * 256), plo, phi, wlo, whi, w2big)
        o_a = _nsa(nq, nqr, small, kc, vct, ksel, vselT, kwin, vwinT, ovt, batch, seq)
        o_b = _dsa(dq, iq, small, dk, dvT, ltri, batch, seq)
        bs = jnp.repeat(sgu_b[layer].T, HEAD_DIM, axis=1)
        o_c = _sgu(sgu_z, sgu_norm_g[layer].reshape(1, -1), sgu_norm_b[layer].reshape(1, -1),
                   sgu_w[layer], bs)
        xf = _outproj(o_a, o_b, o_c, xf, mod, w_out[layer].astype(BF16),
                      ln1_g[layer].reshape(1, -1), ln1_b[layer].reshape(1, -1), seq, alpha)

        j = layer // 2
        g2, b2 = ln2_g[layer].reshape(1, -1), ln2_b[layer].reshape(1, -1)
        if layer % 2 == 0:
            xf = _ffn(xf, mod, ffn_w_gate[j].astype(BF16), ffn_w_up[j].astype(BF16),
                      ffn_w_down[j].astype(BF16), g2, b2, seq, alpha)
        else:
            wr = jnp.pad(moe_w_router[j], ((0, 0), (0, LANES - N_EXPERTS)))
            br = jnp.pad(moe_b_router[j], (0, LANES - N_EXPERTS)).reshape(1, -1)
            xf = _moe(xf, mod, wr, br, moe_w_gate[j], moe_w_up[j], moe_w_down[j], g2, b2, seq, alpha)
    return xf.reshape(batch, seq, d)
```
